```python
import math
import jax
import jax.numpy as jnp
from jax import lax
import numpy as np

D_MODEL = 1024
BATCH = 2
SEQ = 8192
DEPTH = 2

GRID_W = 64
CTX_LEN = 256
HEAD_DIM = 64
BLOCK = 128
WINDOW = 128
ROPE_THETA = 10000.0
EPS = 1e-6
NEG_INF = -1e30
N_MOD = 6

MIX_WIDTH = D_MODEL
GROUP_WIDTH = MIX_WIDTH // 4
N_OUT_HEADS = MIX_WIDTH // HEAD_DIM
DIFF_HEADS = GROUP_WIDTH // HEAD_DIM
DIFF_QK_DIM = HEAD_DIM // 2
WIN_HEADS = GROUP_WIDTH // HEAD_DIM
WIN_KV_HEADS = WIN_HEADS // 2
GLOB_HEADS = GROUP_WIDTH // HEAD_DIM
GLOB_KV_HEADS = GLOB_HEADS // 2
HY_CH = GROUP_WIDTH
HY_ORDER = 2
HY_SHORT = 3
HY_BANDS = 16
HY_EMB = 1 + 2 * HY_BANDS
HY_HIDDEN = 64
HY_TARGET = 1e-2
HY_FAST = 0.3
HY_SLOW = 1.5

W_DIFF = 3 * DIFF_HEADS * HEAD_DIM
W_WIN = (WIN_HEADS + 2 * WIN_KV_HEADS) * HEAD_DIM
W_GLOB = (GLOB_HEADS + 2 * GLOB_KV_HEADS) * HEAD_DIM
W_HY = (HY_ORDER + 1) * HY_CH
D_IN = W_DIFF + W_WIN + W_GLOB + W_HY
SPLITS = (W_DIFF, W_DIFF + W_WIN, W_DIFF + W_WIN + W_GLOB)

PEER_HEADS = 8
N_KEYS = 128
N_EXPERTS = N_KEYS * N_KEYS
PEER_TOPK = 16
PEER_DK = 256
PEER_CHUNK = 128

kernel_name = 'hybrid_flow_backbone_peer'


def rmsnorm(x, g):
    xf = x.astype(jnp.float32)
    y = xf * lax.rsqrt(jnp.mean(xf * xf, axis=-1, keepdims=True) + EPS)
    return (y * g).astype(x.dtype)


def modulate(h, shift, scale):
    return h * (1.0 + scale) + shift


def axial_rope(n_tok, dim):
    rows = n_tok // GRID_W
    row = jnp.repeat(jnp.arange(rows, dtype=jnp.float32), GRID_W)
    col = jnp.tile(jnp.arange(GRID_W, dtype=jnp.float32), rows)
    axis_dim = dim // 2
    inv_freq = ROPE_THETA ** (-jnp.arange(0, axis_dim, 2, dtype=jnp.float32) / axis_dim)
    ang = jnp.concatenate([row[:, None] * inv_freq, col[:, None] * inv_freq], axis=-1)
    return jnp.cos(ang), jnp.sin(ang)


def apply_rope(x, cos, sin):
    shape = (1, x.shape[1]) + (1,) * (x.ndim - 3) + (cos.shape[-1],)
    c, s = cos.reshape(shape), sin.reshape(shape)
    x1, x2 = jnp.split(x.astype(jnp.float32), 2, axis=-1)
    return jnp.concatenate([x1 * c - x2 * s, x1 * s + x2 * c], axis=-1).astype(x.dtype)


def sink_softmax(s, sink):
    sink = sink.astype(jnp.float32)
    m = jnp.maximum(jnp.max(s, axis=-1, keepdims=True), sink)
    e = jnp.exp(s - m)
    return e / (jnp.sum(e, axis=-1, keepdims=True) + jnp.exp(sink - m))


def dense_attention(qs, ks, coefs, v, sink=None):
    B_, Lq, H, d = qs[0].shape
    G = v.shape[2]
    R = H // G
    dv = v.shape[-1]
    nb = Lq // BLOCK
    scale = d ** -0.5
    qb = tuple(q.reshape(B_, nb, BLOCK, G, R, d).swapaxes(0, 1) for q in qs)
    sink_b = None if sink is None else sink.reshape(1, G, R, 1, 1)

    def one_block(qblk):
        acc = None
        for q_, k_, coef in zip(qblk, ks, coefs):
            s = jnp.einsum('bqgrd,bkgd->bgrqk', q_, k_, preferred_element_type=jnp.float32) * scale
            p = jax.nn.softmax(s, axis=-1) if sink_b is None else sink_softmax(s, sink_b)
            acc = coef * p if acc is None else acc + coef * p
        return jnp.einsum('bgrqk,bkgd->bqgrd', acc.astype(v.dtype), v)

    o = lax.map(one_block, qb)
    return o.swapaxes(0, 1).reshape(B_, Lq, H, dv)


def window_attention(q, k, v, kc, vc, sink):
    B_, L_, H, d = q.shape
    G = k.shape[2]
    R = H // G
    nb = L_ // BLOCK
    scale = d ** -0.5

    def band(t):
        tb = t.reshape(B_, nb, BLOCK, G, t.shape[-1])
        pad = jnp.zeros_like(tb[:, :1])
        prev = jnp.concatenate([pad, tb[:, :-1]], axis=1)
        nxt = jnp.concatenate([tb[:, 1:], pad], axis=1)
        return jnp.concatenate([prev, tb, nxt], axis=2)

    kb, vb = band(k), band(v)
    qb = q.reshape(B_, nb, BLOCK, G, R, d)
    s_loc = jnp.einsum('bnqgrd,bnkgd->bgrnqk', qb, kb, preferred_element_type=jnp.float32) * scale
    qi = jnp.arange(BLOCK)[:, None]
    kj = jnp.arange(3 * BLOCK)[None, :]
    kpos = (jnp.arange(nb)[:, None, None] - 1) * BLOCK + kj
    valid = (jnp.abs(kj - BLOCK - qi) <= WINDOW) & (kpos >= 0) & (kpos < L_)
    s_loc = jnp.where(valid, s_loc, NEG_INF)
    s_ctx = jnp.einsum('bnqgrd,bkgd->bgrnqk', qb, kc, preferred_element_type=jnp.float32) * scale
    p = sink_softmax(jnp.concatenate([s_ctx, s_loc], axis=-1), sink.reshape(1, G, R, 1, 1, 1))
    p = p.astype(v.dtype)
    n_ctx = kc.shape[1]
    o = (jnp.einsum('bgrnqk,bkgd->bnqgrd', p[..., :n_ctx], vc)
         + jnp.einsum('bgrnqk,bnkgd->bnqgrd', p[..., n_ctx:], vb))
    return o.reshape(B_, L_, H, d)


def gqa_heads(t, n_q, n_kv, g_qk):
    B_, L_, _ = t.shape
    q, k, v = jnp.split(t, [n_q * HEAD_DIM, (n_q + n_kv) * HEAD_DIM], axis=-1)
    q = rmsnorm(q.reshape(B_, L_, n_q, HEAD_DIM), g_qk[0])
    k = rmsnorm(k.reshape(B_, L_, n_kv, HEAD_DIM), g_qk[1])
    return q, k, v.reshape(B_, L_, n_kv, HEAD_DIM)


def diff_attention(p, pc, g_qk, lam_vec, lambda_init, rope, want_ctx):
    def heads(t):
        B_, L_, _ = t.shape
        q, k, v = jnp.split(t, 3, axis=-1)
        q = rmsnorm(q.reshape(B_, L_, DIFF_HEADS, 2, DIFF_QK_DIM), g_qk[0])
        k = rmsnorm(k.reshape(B_, L_, DIFF_HEADS, 2, DIFF_QK_DIM), g_qk[1])
        return q, k, v.reshape(B_, L_, DIFF_HEADS, HEAD_DIM)

    q, k, v = heads(p)
    qc, kc, vc = heads(pc)
    q, k = apply_rope(q, *rope), apply_rope(k, *rope)
    lam_vec = lam_vec.astype(jnp.float32)
    lam = jnp.exp(jnp.sum(lam_vec[0] * lam_vec[1])) - jnp.exp(jnp.sum(lam_vec[2] * lam_vec[3])) + lambda_init
    coefs = (1.0, -lam)
    kk = jnp.concatenate([kc, k], axis=1)
    vv = jnp.concatenate([vc, v], axis=1)
    o = dense_attention((q[..., 0, :], q[..., 1, :]), (kk[..., 0, :], kk[..., 1, :]), coefs, vv)
    if not want_ctx:
        return o, None
    oc = dense_attention((qc[..., 0, :], qc[..., 1, :]), (kc[..., 0, :], kc[..., 1, :]), coefs, vc)
    return o, oc


def short_conv(u, w, b):
    up = jnp.pad(u, ((0, 0), (1, 1), (0, 0)))
    return up[:, :-2] * w[0] + up[:, 1:-1] * w[1] + up[:, 2:] * w[2] + b


def hyena_filters(n_tok, w1, b1, w2, b2, w3, b3, freq):
    t = jnp.linspace(0.0, 1.0, n_tok, dtype=jnp.float32)[:, None]
    w = (2.0 * math.pi / n_tok) * jnp.arange(n_tok, dtype=jnp.float32)[:, None]
    f = jnp.linspace(1e-4, HY_BANDS - 1, HY_BANDS, dtype=jnp.float32)[None, :]
    feat = jnp.concatenate([t, jnp.cos(f * w), -jnp.sin(f * w)], axis=-1)
    hdn = jnp.sin(freq * (feat @ w1 + b1))
    hdn = jnp.sin(freq * (hdn @ w2 + b2))
    filt = (hdn @ w3 + b3).astype(jnp.float32).reshape(n_tok, HY_ORDER, 2, HY_CH)
    deltas = jnp.abs(jnp.linspace(math.log(HY_TARGET) / HY_SLOW, math.log(HY_TARGET) / HY_FAST, HY_CH,
                                  dtype=jnp.float32))
    filt = filt * jnp.exp(-t[:, :, None, None] * deltas)
    fwd, bwd = filt[:, :, 0], filt[:, :, 1]
    kfull = jnp.concatenate([fwd, jnp.zeros_like(fwd[:1]), bwd[1:][::-1]], axis=0)
    return kfull / jnp.sum(jnp.abs(kfull), axis=0, keepdims=True)


def long_conv(z, kf):
    n = z.shape[1]
    zf = jnp.fft.rfft(z.astype(jnp.float32), n=2 * n, axis=1)
    hf = jnp.fft.rfft(kf, n=2 * n, axis=0)
    return jnp.fft.irfft(zf * hf[None], n=2 * n, axis=1)[:, :n].astype(z.dtype)


def hyena(u, conv_w, conv_b, filter_params, bias):
    n = u.shape[1]
    u = short_conv(u, conv_w, conv_b)
    v, x1, x2 = jnp.split(u, 3, axis=-1)
    kfull = hyena_filters(n, *filter_params)
    z = x1 * (long_conv(v, kfull[:, 0]) + bias[0] * v)
    return x2 * (long_conv(z, kfull[:, 1]) + bias[1] * z)


def merge_heads(o, g, w_out, lambda_init):
    B_, L_, _ = o.shape
    oh = rmsnorm(o.reshape(B_, L_, N_OUT_HEADS, HEAD_DIM), g.reshape(N_OUT_HEADS, HEAD_DIM))
    head_scale = jnp.where(jnp.arange(N_OUT_HEADS) < DIFF_HEADS, 1.0 - lambda_init, 1.0)[:, None]
    return (oh * head_scale.astype(oh.dtype)).reshape(B_, L_, MIX_WIDTH) @ w_out


def flat_heads(o):
    return o.reshape(o.shape[0], o.shape[1], -1)


def token_mixing(h, hc, rope_half, rope_full, lambda_init, want_ctx, w_in, g_qk_diff, lambda_diff,
                 g_qk_win, sink_win, g_qk_glob, hy_conv_w, hy_conv_b, hy_filter_params, hy_bias,
                 g_mix_out, w_out):
    pa, pw, pg, ph = jnp.split(h @ w_in, SPLITS, axis=-1)
    pac, pwc, pgc, phc = jnp.split(hc @ w_in, SPLITS, axis=-1)
    oa, oac = diff_attention(pa, pac, g_qk_diff, lambda_diff, lambda_init, rope_half, want_ctx)
    qw, kw, vw = gqa_heads(pw, WIN_HEADS, WIN_KV_HEADS, g_qk_win)
    qwc, kwc, vwc = gqa_heads(pwc, WIN_HEADS, WIN_KV_HEADS, g_qk_win)
    qw, kw = apply_rope(qw, *rope_full), apply_rope(kw, *rope_full)
    ob = window_attention(qw, kw, vw, kwc, vwc, sink_win)
    qg, kg, vg = gqa_heads(pg, GLOB_HEADS, GLOB_KV_HEADS, g_qk_glob)
    qgc, kgc, vgc = gqa_heads(pgc, GLOB_HEADS, GLOB_KV_HEADS, g_qk_glob)
    qg, kg = apply_rope(qg, *rope_full), apply_rope(kg, *rope_full)
    og = dense_attention((qg,), (jnp.concatenate([kgc, kg], axis=1),), (1.0,),
                         jnp.concatenate([vgc, vg], axis=1))
    oh = hyena(ph, hy_conv_w, hy_conv_b, hy_filter_params, hy_bias)
    lat = merge_heads(jnp.concatenate([flat_heads(oa), flat_heads(ob), flat_heads(og), oh], axis=-1),
                      g_mix_out, w_out, lambda_init)
    if not want_ctx:
        return lat, None
    obc = dense_attention((qwc,), (kwc,), (1.0,), vwc, sink=sink_win)
    ogc = dense_attention((qgc,), (kgc,), (1.0,), vgc)
    ohc = hyena(phc, hy_conv_w, hy_conv_b, hy_filter_params, hy_bias)
    ctx_out = merge_heads(jnp.concatenate([flat_heads(oac), flat_heads(obc), flat_heads(ogc), ohc], axis=-1),
                          g_mix_out, w_out, lambda_init)
    return lat, ctx_out


def peer(h, wq, keys, u_tab, v_tab):
    B_, L_, D_ = h.shape
    T = B_ * L_
    xt = h.reshape(T, D_)
    q = (xt @ wq).reshape(T, PEER_HEADS, 2, PEER_DK // 2)
    s = jnp.einsum('thpd,hpkd->thpk', q, keys, preferred_element_type=jnp.float32)
    s_top, i_top = lax.top_k(s, PEER_TOPK)
    cand = (s_top[..., 0, :, None] + s_top[..., 1, None, :]).reshape(T, PEER_HEADS, PEER_TOPK * PEER_TOPK)
    cidx = (i_top[..., 0, :, None] * N_KEYS + i_top[..., 1, None, :]).reshape(T, PEER_HEADS, PEER_TOPK * PEER_TOPK)
    best, pos = lax.top_k(cand, PEER_TOPK)
    experts = jnp.take_along_axis(cidx, pos, axis=-1)
    gate = jax.nn.softmax(best, axis=-1).astype(h.dtype)
    nc = T // PEER_CHUNK

    def one_chunk(args):
        xch, ech, gch = args
        act = jax.nn.gelu(jnp.einsum('chkd,cd->chk', u_tab[ech], xch))
        return jnp.einsum('chk,chkd->cd', gch * act, v_tab[ech])

    out = lax.map(one_chunk, (xt.reshape(nc, PEER_CHUNK, D_),
                              experts.reshape(nc, PEER_CHUNK, PEER_HEADS, PEER_TOPK),
                              gate.reshape(nc, PEER_CHUNK, PEER_HEADS, PEER_TOPK)))
    return out.reshape(B_, L_, D_)


def setup_inputs(seed: int = 0) -> dict:
    key = jax.random.key(seed)
    ks = iter(jax.random.split(key, 40))

    def nrm(shape, s):
        return jax.random.normal(next(ks), shape, jnp.float32) * s

    def gain(shape):
        return 1.0 + nrm(shape, 0.02)

    Ld = DEPTH
    return {
        'x': nrm((BATCH, SEQ, D_MODEL), 1.0),
        'c': nrm((BATCH, D_MODEL), 1.0),
        'ctx': nrm((BATCH, CTX_LEN, D_MODEL), 1.0),
        'c_ctx': nrm((D_MODEL,), 1.0),
        'w_mod': nrm((Ld, D_MODEL, N_MOD * D_MODEL), 0.5 * D_MODEL ** -0.5),
        'b_mod': nrm((Ld, N_MOD * D_MODEL), 0.01),
        'g_norm_mix': gain((Ld, D_MODEL)),
        'w_in': nrm((Ld, D_MODEL, D_IN), D_MODEL ** -0.5),
        'g_qk_diff': gain((Ld, 2, DIFF_QK_DIM)),
        'lambda_diff': nrm((Ld, 4, DIFF_QK_DIM), 0.1),
        'g_qk_win': gain((Ld, 2, HEAD_DIM)),
        'sink_win': nrm((Ld, WIN_HEADS), 0.5),
        'g_qk_glob': gain((Ld, 2, HEAD_DIM)),
        'hy_conv_w': nrm((Ld, HY_SHORT, W_HY), 0.5),
        'hy_conv_b': nrm((Ld, W_HY), 0.02),
        'hy_w1': nrm((Ld, HY_EMB, HY_HIDDEN), HY_EMB ** -0.5),
        'hy_b1': nrm((Ld, HY_HIDDEN), 0.1),
        'hy_w2': nrm((Ld, HY_HIDDEN, HY_HIDDEN), HY_HIDDEN ** -0.5),
        'hy_b2': nrm((Ld, HY_HIDDEN), 0.1),
        'hy_w3': nrm((Ld, HY_HIDDEN, 2 * HY_ORDER * HY_CH), HY_HIDDEN ** -0.5),
        'hy_b3': nrm((Ld, 2 * HY_ORDER * HY_CH), 0.02),
        'hy_freq': 1.0 + nrm((Ld, HY_HIDDEN), 0.1),
        'hy_bias': nrm((Ld, HY_ORDER, HY_CH), 0.5),
        'g_mix_out': gain((Ld, MIX_WIDTH)),
        'w_out': nrm((Ld, MIX_WIDTH, D_MODEL), MIX_WIDTH ** -0.5),
        'g_norm_ffn': gain((Ld, D_MODEL)),
        'peer_wq': nrm((Ld, D_MODEL, PEER_HEADS * PEER_DK), D_MODEL ** -0.5),
        'peer_keys': nrm((Ld, PEER_HEADS, 2, N_KEYS, PEER_DK // 2), (PEER_DK // 2) ** -0.5),
        'peer_u': nrm((Ld, N_EXPERTS, D_MODEL), D_MODEL ** -0.5),
        'peer_v': nrm((Ld, N_EXPERTS, D_MODEL), PEER_HEADS ** -0.5),
    }


def reference(x, c, ctx, c_ctx, w_mod, b_mod, g_norm_mix, w_in, g_qk_diff, lambda_diff, g_qk_win,
              sink_win, g_qk_glob, hy_conv_w, hy_conv_b, hy_w1, hy_b1, hy_w2, hy_b2, hy_w3, hy_b3,
              hy_freq, hy_bias, g_mix_out, w_out, g_norm_ffn, peer_wq, peer_keys, peer_u, peer_v):
    n_tok = x.shape[1]
    rope_half = axial_rope(n_tok, DIFF_QK_DIM)
    rope_full = axial_rope(n_tok, HEAD_DIM)
    xc = ctx
    sc = jax.nn.silu(c)
    scc = jax.nn.silu(c_ctx)
    for i in range(DEPTH):
        want_ctx = i < DEPTH - 1
        lambda_init = 0.8 - 0.6 * math.exp(-0.3 * i)
        m = (sc @ w_mod[i] + b_mod[i])[:, None, :]
        mc = scc @ w_mod[i] + b_mod[i]
        sh1, s1, g1, sh2, s2, g2 = jnp.split(m, N_MOD, axis=-1)
        sh1c, s1c, g1c, sh2c, s2c, g2c = jnp.split(mc, N_MOD, axis=-1)
        h = modulate(rmsnorm(x, g_norm_mix[i]), sh1, s1)
        hc = modulate(rmsnorm(xc, g_norm_mix[i]), sh1c, s1c)
        filt = (hy_w1[i], hy_b1[i], hy_w2[i], hy_b2[i], hy_w3[i], hy_b3[i], hy_freq[i])
        o, oc = token_mixing(h, hc, rope_half, rope_full, lambda_init, want_ctx, w_in[i], g_qk_diff[i],
                             lambda_diff[i], g_qk_win[i], sink_win[i], g_qk_glob[i], hy_conv_w[i],
                             hy_conv_b[i], filt, hy_bias[i], g_mix_out[i], w_out[i])
        x = x + g1 * o
        h2 = modulate(rmsnorm(x, g_norm_ffn[i]), sh2, s2)
        x = x + g2 * peer(h2, peer_wq[i], peer_keys[i], peer_u[i], peer_v[i])
        if want_ctx:
            xc = xc + g1c * oc
            h2c = modulate(rmsnorm(xc, g_norm_ffn[i]), sh2c, s2c)
            xc = xc + g2c * peer(h2c, peer_wq[i], peer_keys[i], peer_u[i], peer_v[i])
    return x
```

```python
import functools
import math

import jax
import jax.numpy as jnp
from jax import lax
from jax.experimental import pallas as pl
from jax.experimental.pallas import tpu as pltpu

F32 = jnp.float32
BF16 = jnp.bfloat16

GRID_W = 64
HEAD_DIM = 64
BLOCK = 128
WINDOW = 128
ROPE_THETA = 10000.0
EPS = 1e-6
NEG_INF = -1e30
N_MOD = 6
DIFF_HEADS = 4
DIFF_QK_DIM = 32
WIN_HEADS = 4
WIN_KV_HEADS = 2
GLOB_HEADS = 4
GLOB_KV_HEADS = 2
N_OUT_HEADS = 16
HY_CH = 256
HY_ORDER = 2
HY_BANDS = 16
HY_TARGET = 1e-2
HY_FAST = 0.3
HY_SLOW = 1.5
W_DIFF = 768
W_WIN = 512
W_GLOB = 512
SPLITS = (W_DIFF, W_DIFF + W_WIN, W_DIFF + W_WIN + W_GLOB)
PEER_HEADS = 8
N_KEYS = 128
PEER_TOPK = 16

VMEM_LIMIT = 56 * 1024 * 1024


def _params(*sem):
    return pltpu.CompilerParams(dimension_semantics=sem, vmem_limit_bytes=VMEM_LIMIT)


def _mm_kernel(a_ref, b_ref, o_ref):
    o_ref[...] = jnp.dot(a_ref[...], b_ref[...], preferred_element_type=F32)


def mm(a, b, tm, tn):
    M, K = a.shape
    N = b.shape[1]
    return pl.pallas_call(
        _mm_kernel,
        grid=(M // tm, N // tn),
        in_specs=[pl.BlockSpec((tm, K), lambda i, j: (i, 0)),
                  pl.BlockSpec((K, tn), lambda i, j: (0, j))],
        out_specs=pl.BlockSpec((tm, tn), lambda i, j: (i, j)),
        out_shape=jax.ShapeDtypeStruct((M, N), F32),
        compiler_params=_params("parallel", "arbitrary"),
        name="mm",
    )(a, b)


def _normmod_mm_kernel(x_ref, g_ref, sh_ref, sc_ref, w_ref, o_ref, h_ref):
    @pl.when(pl.program_id(1) == 0)
    def _():
        x = x_ref[...]
        y = x * lax.rsqrt(jnp.mean(x * x, axis=-1, keepdims=True) + EPS) * g_ref[...]
        h_ref[...] = (y * (1.0 + sc_ref[0]) + sh_ref[0]).astype(BF16)

    o_ref[...] = jnp.dot(h_ref[...], w_ref[...], preferred_element_type=F32)


def normmod_mm(x, g, shift, scale, w, rows_per_seg, tm, tn):
    M, D = x.shape
    N = w.shape[1]
    nseg = shift.shape[0]
    blocks_per_seg = rows_per_seg // tm
    seg = lambda i, j: (jnp.minimum(i // blocks_per_seg, nseg - 1), 0, 0)
    return pl.pallas_call(
        _normmod_mm_kernel,
        grid=(M // tm, N // tn),
        in_specs=[pl.BlockSpec((tm, D), lambda i, j: (i, 0)),
                  pl.BlockSpec((1, D), lambda i, j: (0, 0)),
                  pl.BlockSpec((1, 1, D), seg),
                  pl.BlockSpec((1, 1, D), seg),
                  pl.BlockSpec((D, tn), lambda i, j: (0, j))],
        out_specs=pl.BlockSpec((tm, tn), lambda i, j: (i, j)),
        out_shape=jax.ShapeDtypeStruct((M, N), F32),
        scratch_shapes=[pltpu.VMEM((tm, D), BF16)],
        compiler_params=_params("parallel", "arbitrary"),
        name="normmod_mm",
    )(x, g.reshape(1, D), shift.reshape(nseg, 1, D), scale.reshape(nseg, 1, D), w)


def _nt_kernel(a_ref, b_ref, o_ref):
    o_ref[0] = lax.dot_general(a_ref[0], b_ref[...], (((1,), (1,)), ((), ())),
                               preferred_element_type=F32)


def keys_times_qT(keys, q, tn):
    P, n, d = keys.shape
    T = q.shape[0]
    return pl.pallas_call(
        _nt_kernel,
        grid=(P, T // tn),
        in_specs=[pl.BlockSpec((1, n, d), lambda p, j: (p, 0, 0)),
                  pl.BlockSpec((tn, d), lambda p, j: (j, p))],
        out_specs=pl.BlockSpec((1, n, tn), lambda p, j: (p, 0, j)),
        out_shape=jax.ShapeDtypeStruct((P, n, T), F32),
        compiler_params=_params("parallel", "arbitrary"),
        name="peer_scores_t",
    )(keys, q)


def _dense_attn_kernel(coef_ref, sink_ref, q_ref, k_ref, v_ref, o_ref, *, nbr, R, G, tq, tk, nk,
                       has_sink):
    rows = R * tq
    d = q_ref.shape[-1]
    dv = v_ref.shape[-1]
    g = pl.program_id(0) % G
    out = jnp.zeros((rows, dv), F32)
    for br in range(nbr):
        q = q_ref[br, 0].reshape(rows, d)
        if has_sink:
            m0 = jnp.concatenate([jnp.full((tq, 1), sink_ref[g * R + r], F32) for r in range(R)],
                                 axis=0)
            l0 = jnp.ones((rows, 1), F32)
        else:
            m0 = jnp.full((rows, 1), NEG_INF, F32)
            l0 = jnp.zeros((rows, 1), F32)

        def body(j, carry, br=br, q=q):
            m, l, acc = carry
            start = pl.multiple_of(j * tk, tk)
            kblk = k_ref[br, 0, pl.ds(start, tk), :]
            vblk = v_ref[0, pl.ds(start, tk), :]
            s = lax.dot_general(q, kblk, (((1,), (1,)), ((), ())), preferred_element_type=F32)
            m_new = jnp.maximum(m, jnp.max(s, axis=-1, keepdims=True))
            alpha = jnp.exp(m - m_new)
            p = jnp.exp(s - m_new)
            l = alpha * l + jnp.sum(p, axis=-1, keepdims=True)
            acc = alpha * acc + jnp.dot(p.astype(BF16), vblk, preferred_element_type=F32)
            return m_new, l, acc

        m, l, acc = lax.fori_loop(0, nk, body, (m0, l0, jnp.zeros((rows, dv), F32)))
        out = out + coef_ref[br] * (acc / l)
    o_ref[0] = out.reshape(R, tq, dv)


def dense_attention(q, k, v, coefs, sink, G, tq, tk):
    nbr, BG, R, Lq, d = q.shape
    Lk = k.shape[2]
    dv = v.shape[-1]
    has_sink = sink is not None
    if sink is None:
        sink = jnp.zeros((1,), F32)
    kern = functools.partial(_dense_attn_kernel, nbr=nbr, R=R, G=G, tq=tq, tk=tk, nk=Lk // tk,
                             has_sink=has_sink)
    return pl.pallas_call(
        kern,
        grid=(BG, Lq // tq),
        in_specs=[pl.BlockSpec(memory_space=pltpu.SMEM),
                  pl.BlockSpec(memory_space=pltpu.SMEM),
                  pl.BlockSpec((nbr, 1, R, tq, d), lambda b, i: (0, b, 0, i, 0)),
                  pl.BlockSpec((nbr, 1, Lk, d), lambda b, i: (0, b, 0, 0)),
                  pl.BlockSpec((1, Lk, dv), lambda b, i: (b, 0, 0))],
        out_specs=pl.BlockSpec((1, R, tq, dv), lambda b, i: (b, 0, i, 0)),
        out_shape=jax.ShapeDtypeStruct((BG, R, Lq, dv), F32),
        compiler_params=_params("parallel", "arbitrary"),
        name="dense_attn",
    )(coefs.astype(F32), sink.astype(F32), q, k, v)


def _window_attn_kernel(sink_ref, q_ref, k_ref, v_ref, o_ref, *, R, G, C, L):
    rows = R * BLOCK
    d = q_ref.shape[-1]
    g = pl.program_id(0) % G
    n = pl.program_id(1)
    q = q_ref[0].reshape(rows, d)
    start = pl.multiple_of(C + n * BLOCK, BLOCK)
    kc = k_ref[0, 0:C, :]
    vc = v_ref[0, 0:C, :]
    kl = k_ref[0, pl.ds(start, 3 * BLOCK), :]
    vl = v_ref[0, pl.ds(start, 3 * BLOCK), :]
    nt = (((1,), (1,)), ((), ()))
    s_ctx = lax.dot_general(q, kc, nt, preferred_element_type=F32)
    s_loc = lax.dot_general(q, kl, nt, preferred_element_type=F32)
    qi = lax.broadcasted_iota(jnp.int32, (rows, 3 * BLOCK), 0) & (BLOCK - 1)
    kj = lax.broadcasted_iota(jnp.int32, (rows, 3 * BLOCK), 1)
    kpos = (n - 1) * BLOCK + kj
    valid = (jnp.abs(kj - BLOCK - qi) <= WINDOW) & (kpos >= 0) & (kpos < L)
    s_loc = jnp.where(valid, s_loc, NEG_INF)
    sink = jnp.concatenate([jnp.full((BLOCK, 1), sink_ref[g * R + r], F32) for r in range(R)], axis=0)
    m = jnp.maximum(jnp.maximum(jnp.max(s_ctx, axis=-1, keepdims=True),
                                jnp.max(s_loc, axis=-1, keepdims=True)), sink)
    e_ctx = jnp.exp(s_ctx - m)
    e_loc = jnp.exp(s_loc - m)
    den = (jnp.sum(e_ctx, axis=-1, keepdims=True) + jnp.sum(e_loc, axis=-1, keepdims=True)
           + jnp.exp(sink - m))
    inv = 1.0 / den
    o = (jnp.dot((e_ctx * inv).astype(BF16), vc, preferred_element_type=F32)
         + jnp.dot((e_loc * inv).astype(BF16), vl, preferred_element_type=F32))
    o_ref[0] = o.reshape(R, BLOCK, d)


def window_attention(q, kpad, vpad, sink, G, C, L):
    BG, R, _, d = q.shape
    Lp = kpad.shape[1]
    kern = functools.partial(_window_attn_kernel, R=R, G=G, C=C, L=L)
    return pl.pallas_call(
        kern,
        grid=(BG, L // BLOCK),
        in_specs=[pl.BlockSpec(memory_space=pltpu.SMEM),
                  pl.BlockSpec((1, R, BLOCK, d), lambda b, i: (b, 0, i, 0)),
                  pl.BlockSpec((1, Lp, d), lambda b, i: (b, 0, 0)),
                  pl.BlockSpec((1, Lp, d), lambda b, i: (b, 0, 0))],
        out_specs=pl.BlockSpec((1, R, BLOCK, d), lambda b, i: (b, 0, i, 0)),
        out_shape=jax.ShapeDtypeStruct((BG, R, L, d), F32),
        compiler_params=_params("parallel", "arbitrary"),
        name="window_attn",
    )(sink.astype(F32), q, kpad, vpad)


def _peer_kernel(x_ref, u_ref, vt_ref, s1_ref, s2_ref, ea_ref, eb_ref, tau_ref, o_ref,
                 acc_ref, act_ref, g_ref, *, ni, tm):
    c = pl.program_id(1)

    @pl.when(c == 0)
    def _():
        acc_ref[...] = jnp.zeros_like(acc_ref)

    act_ref[...] = jax.nn.gelu(lax.dot_general(u_ref[...], x_ref[...], (((1,), (1,)), ((), ())),
                                               preferred_element_type=F32))

    def one_lane_tile(ts, carry):
        lanes = pl.ds(pl.multiple_of(ts * 128, 128), 128)
        for ii in range(ni):
            rows = slice(ii * N_KEYS, (ii + 1) * N_KEYS)
            w = jnp.zeros((N_KEYS, 128), F32)
            for h in range(PEER_HEADS):
                z = s1_ref[h, ii:ii + 1, lanes] + s2_ref[h, :, lanes]
                gate = ea_ref[h, ii:ii + 1, lanes] * eb_ref[h, :, lanes]
                w = w + jnp.where(z >= tau_ref[h:h + 1, lanes], gate, 0.0)
            g_ref[rows, lanes] = (w * act_ref[rows, lanes]).astype(BF16)
        return carry

    lax.fori_loop(0, tm // 128, one_lane_tile, 0)
    acc_ref[...] += jnp.dot(vt_ref[...], g_ref[...], preferred_element_type=F32)

    @pl.when(c == pl.num_programs(1) - 1)
    def _():
        o_ref[...] = acc_ref[...].T


def peer_dense(x, u, vt, s1t, s2t, eat, ebt, taut, tm, ni):
    T, D = x.shape
    E = u.shape[0]
    ec = ni * N_KEYS
    kern = functools.partial(_peer_kernel, ni=ni, tm=tm)
    tok = lambda t, c: (0, 0, t)
    chunk = lambda t, c: (0, c, t)
    return pl.pallas_call(
        kern,
        grid=(T // tm, E // ec),
        in_specs=[pl.BlockSpec((tm, D), lambda t, c: (t, 0)),
                  pl.BlockSpec((ec, D), lambda t, c: (c, 0)),
                  pl.BlockSpec((D, ec), lambda t, c: (0, c)),
                  pl.BlockSpec((PEER_HEADS, ni, tm), chunk),
                  pl.BlockSpec((PEER_HEADS, N_KEYS, tm), tok),
                  pl.BlockSpec((PEER_HEADS, ni, tm), chunk),
                  pl.BlockSpec((PEER_HEADS, N_KEYS, tm), tok),
                  pl.BlockSpec((PEER_HEADS, tm), lambda t, c: (0, t))],
        out_specs=pl.BlockSpec((tm, D), lambda t, c: (t, 0)),
        out_shape=jax.ShapeDtypeStruct((T, D), F32),
        scratch_shapes=[pltpu.VMEM((D, tm), F32),
                        pltpu.VMEM((ec, tm), F32),
                        pltpu.VMEM((ec, tm), BF16)],
        compiler_params=_params("parallel", "arbitrary"),
        name="peer_dense",
    )(x, u, vt, s1t, s2t, eat, ebt, taut)


def peer(qp, h_bf, keys, u_bf, vt_bf, tm):
    T = qp.shape[0]
    dk = keys.shape[-1]
    kflat = keys.reshape(PEER_HEADS * 2, N_KEYS, dk).astype(BF16)
    st = keys_times_qT(kflat, qp.astype(BF16), tm)
    st = st.reshape(PEER_HEADS, 2, N_KEYS, T)
    s = jnp.moveaxis(st, -1, 0)
    s_top, _ = lax.top_k(s, PEER_TOPK)
    cand = (s_top[..., 0, :, None] + s_top[..., 1, None, :]).reshape(T, PEER_HEADS, PEER_TOPK * PEER_TOPK)
    best, _ = lax.top_k(cand, PEER_TOPK)
    tau = best[..., PEER_TOPK - 1]
    zsum = jnp.sum(jnp.exp(best - best[..., :1]), axis=-1)
    max1 = s_top[..., 0, 0]
    max2 = s_top[..., 1, 0]
    tr = lambda a: jnp.transpose(a)[:, None, :]
    s1t, s2t = st[:, 0], st[:, 1]
    eat = jnp.exp(s1t - tr(max1)) / tr(zsum)
    ebt = jnp.exp(s2t - tr(max2))
    return peer_dense(h_bf, u_bf, vt_bf, s1t, s2t, eat, ebt, jnp.transpose(tau), tm, 8)


def rmsnorm(x, g):
    return x * lax.rsqrt(jnp.mean(x * x, axis=-1, keepdims=True) + EPS) * g


def axial_rope(n_tok, dim):
    rows = n_tok // GRID_W
    row = jnp.repeat(jnp.arange(rows, dtype=F32), GRID_W)
    col = jnp.tile(jnp.arange(GRID_W, dtype=F32), rows)
    axis_dim = dim // 2
    inv_freq = ROPE_THETA ** (-jnp.arange(0, axis_dim, 2, dtype=F32) / axis_dim)
    ang = jnp.concatenate([row[:, None] * inv_freq, col[:, None] * inv_freq], axis=-1)
    return jnp.cos(ang), jnp.sin(ang)


def apply_rope(x, cos, sin):
    shape = (1, x.shape[1]) + (1,) * (x.ndim - 3) + (cos.shape[-1],)
    c, s = cos.reshape(shape), sin.reshape(shape)
    x1, x2 = jnp.split(x, 2, axis=-1)
    return jnp.concatenate([x1 * c - x2 * s, x1 * s + x2 * c], axis=-1)


def short_conv(u, w, b):
    up = jnp.pad(u, ((0, 0), (1, 1), (0, 0)))
    return up[:, :-2] * w[0] + up[:, 1:-1] * w[1] + up[:, 2:] * w[2] + b


def hyena_filters(n_tok, w1, b1, w2, b2, w3, b3, freq):
    hp = lax.Precision.HIGHEST
    t = jnp.linspace(0.0, 1.0, n_tok, dtype=F32)[:, None]
    w = (2.0 * math.pi / n_tok) * jnp.arange(n_tok, dtype=F32)[:, None]
    f = jnp.linspace(1e-4, HY_BANDS - 1, HY_BANDS, dtype=F32)[None, :]
    feat = jnp.concatenate([t, jnp.cos(f * w), -jnp.sin(f * w)], axis=-1)
    hdn = jnp.sin(freq * (jnp.dot(feat, w1, precision=hp) + b1))
    hdn = jnp.sin(freq * (jnp.dot(hdn, w2, precision=hp) + b2))
    filt = (jnp.dot(hdn, w3, precision=hp) + b3).reshape(n_tok, HY_ORDER, 2, HY_CH)
    deltas = jnp.abs(jnp.linspace(math.log(HY_TARGET) / HY_SLOW, math.log(HY_TARGET) / HY_FAST, HY_CH,
                                  dtype=F32))
    filt = filt * jnp.exp(-t[:, :, None, None] * deltas)
    fwd, bwd = filt[:, :, 0], filt[:, :, 1]
    kfull = jnp.concatenate([fwd, jnp.zeros_like(fwd[:1]), bwd[1:][::-1]], axis=0)
    return kfull / jnp.sum(jnp.abs(kfull), axis=0, keepdims=True)


def long_conv(z, kf):
    n = z.shape[1]
    zf = jnp.fft.rfft(z, n=2 * n, axis=1)
    hf = jnp.fft.rfft(kf, n=2 * n, axis=0)
    return jnp.fft.irfft(zf * hf[None], n=2 * n, axis=1)[:, :n]


def hyena(u, conv_w, conv_b, filter_params, bias):
    n = u.shape[1]
    u = short_conv(u, conv_w, conv_b)
    v, x1, x2 = jnp.split(u, 3, axis=-1)
    kfull = hyena_filters(n, *filter_params)
    z = x1 * (long_conv(v, kfull[:, 0]) + bias[0] * v)
    return x2 * (long_conv(z, kfull[:, 1]) + bias[1] * z)


def _head_major(t):
    B, L, H, d = t.shape
    return jnp.transpose(t, (0, 2, 1, 3)).reshape(B * H, L, d)


def _q_layout(t, G, R):
    B, L, _, d = t.shape
    return jnp.transpose(t.reshape(B, L, G, R, d), (0, 2, 3, 1, 4)).reshape(B * G, R, L, d)


def _from_q_layout(o, B, G, R):
    _, _, L, d = o.shape
    return jnp.transpose(o.reshape(B, G, R, L, d), (0, 3, 1, 2, 4)).reshape(B, L, G * R * d)


def _pick(n, cands):
    for c in cands:
        if n % c == 0:
            return c
    raise ValueError(f"no tile for {n}")


def kernel(x, c, ctx, c_ctx, w_mod, b_mod, g_norm_mix, w_in, g_qk_diff, lambda_diff, g_qk_win, sink_win, g_qk_glob, hy_conv_w, hy_conv_b, hy_w1, hy_b1, hy_w2, hy_b2, hy_w3, hy_b3, hy_freq, hy_bias, g_mix_out, w_out, g_norm_ffn, peer_wq, peer_keys, peer_u, peer_v):
    B, L, D = x.shape
    C = ctx.shape[1]
    depth = w_mod.shape[0]
    TM = 512
    n_lat = B * L
    n_ctx = B * C
    assert L % TM == 0 and n_ctx % TM == 0
    nseg = B + 1

    rope_half = axial_rope(L, DIFF_QK_DIM)
    rope_full = axial_rope(L, HEAD_DIM)
    sc = jnp.concatenate([jax.nn.silu(c), jax.nn.silu(c_ctx)[None]], axis=0)
    sc = jnp.pad(sc, ((0, 8 - nseg), (0, 0))).astype(BF16)

    xl = x.reshape(n_lat, D)
    xc = ctx.reshape(n_ctx, D)

    for i in range(depth):
        want_ctx = i < depth - 1
        lambda_init = 0.8 - 0.6 * math.exp(-0.3 * i)
        mod = mm(sc, w_mod[i].astype(BF16), 8, 1024)[:nseg] + b_mod[i]
        sh1, s1, g1, sh2, s2, g2 = jnp.split(mod, N_MOD, axis=-1)

        xall = jnp.concatenate([xl, xc], axis=0)
        p = normmod_mm(xall, g_norm_mix[i], sh1, s1, w_in[i].astype(BF16), L, TM, 640)
        pl_lat = p[:n_lat].reshape(B, L, -1)
        pc_ctx = p[n_lat:].reshape(B, C, -1)
        pa, pw, pg, ph = jnp.split(pl_lat, SPLITS, axis=-1)
        pac, pwc, pgc, phc = jnp.split(pc_ctx, SPLITS, axis=-1)

        def diff_heads(t):
            B_, L_, _ = t.shape
            q, k, v = jnp.split(t, 3, axis=-1)
            q = rmsnorm(q.reshape(B_, L_, DIFF_HEADS, 2, DIFF_QK_DIM), g_qk_diff[i, 0])
            k = rmsnorm(k.reshape(B_, L_, DIFF_HEADS, 2, DIFF_QK_DIM), g_qk_diff[i, 1])
            return q, k, v.reshape(B_, L_, DIFF_HEADS, HEAD_DIM)

        qa, ka, va = diff_heads(pa)
        qac, kac, vac = diff_heads(pac)
        qa, ka = apply_rope(qa, *rope_half), apply_rope(ka, *rope_half)
        lam_vec = lambda_diff[i]
        lam = (jnp.exp(jnp.sum(lam_vec[0] * lam_vec[1])) - jnp.exp(jnp.sum(lam_vec[2] * lam_vec[3]))
               + lambda_init)
        coefs = jnp.stack([jnp.ones((), F32), -lam])
        dscale = DIFF_QK_DIM ** -0.5

        def diff_q(q):
            return jnp.stack([_head_major(q[..., b_, :] * dscale) for b_ in range(2)])[:, :, None].astype(BF16)

        def diff_k(k):
            return jnp.stack([_head_major(k[..., b_, :]) for b_ in range(2)]).astype(BF16)

        kka = jnp.concatenate([kac, ka], axis=1)
        vva = jnp.concatenate([vac, va], axis=1)
        Lk = C + L
        tk = _pick(Lk, (768, 512, 256, 128))
        oa = dense_attention(diff_q(qa), diff_k(kka), _head_major(vva).astype(BF16), coefs, None,
                             DIFF_HEADS, 256, tk)
        oa = _from_q_layout(oa, B, DIFF_HEADS, 1)

        def gqa_heads(t, n_q, n_kv, g_qk):
            B_, L_, _ = t.shape
            q, k, v = jnp.split(t, [n_q * HEAD_DIM, (n_q + n_kv) * HEAD_DIM], axis=-1)
            q = rmsnorm(q.reshape(B_, L_, n_q, HEAD_DIM), g_qk[0])
            k = rmsnorm(k.reshape(B_, L_, n_kv, HEAD_DIM), g_qk[1])
            return q, k, v.reshape(B_, L_, n_kv, HEAD_DIM)

        hscale = HEAD_DIM ** -0.5
        qw, kw, vw = gqa_heads(pw, WIN_HEADS, WIN_KV_HEADS, g_qk_win[i])
        qwc, kwc, vwc = gqa_heads(pwc, WIN_HEADS, WIN_KV_HEADS, g_qk_win[i])
        qw, kw = apply_rope(qw, *rope_full), apply_rope(kw, *rope_full)
        zblk = jnp.zeros((B, BLOCK, WIN_KV_HEADS, HEAD_DIM), F32)
        kpad = _head_major(jnp.concatenate([kwc, zblk, kw, zblk], axis=1)).astype(BF16)
        vpad = _head_major(jnp.concatenate([vwc, zblk, vw, zblk], axis=1)).astype(BF16)
        Rw = WIN_HEADS // WIN_KV_HEADS
        ob = window_attention(_q_layout(qw * hscale, WIN_KV_HEADS, Rw).astype(BF16), kpad, vpad,
                              sink_win[i], WIN_KV_HEADS, C, L)
        ob = _from_q_layout(ob, B, WIN_KV_HEADS, Rw)

        qg, kg, vg = gqa_heads(pg, GLOB_HEADS, GLOB_KV_HEADS, g_qk_glob[i])
        qgc, kgc, vgc = gqa_heads(pgc, GLOB_HEADS, GLOB_KV_HEADS, g_qk_glob[i])
        qg, kg = apply_rope(qg, *rope_full), apply_rope(kg, *rope_full)
        Rg = GLOB_HEADS // GLOB_KV_HEADS
        one = jnp.ones((1,), F32)
        kkg = _head_major(jnp.concatenate([kgc, kg], axis=1)).astype(BF16)[None]
        vvg = _head_major(jnp.concatenate([vgc, vg], axis=1)).astype(BF16)
        og = dense_attention(_q_layout(qg * hscale, GLOB_KV_HEADS, Rg).astype(BF16)[None], kkg, vvg,
                             one, None, GLOB_KV_HEADS, 128, tk)
        og = _from_q_layout(og, B, GLOB_KV_HEADS, Rg)

        filt = (hy_w1[i], hy_b1[i], hy_w2[i], hy_b2[i], hy_w3[i], hy_b3[i], hy_freq[i])
        oh = hyena(ph, hy_conv_w[i], hy_conv_b[i], filt, hy_bias[i])

        mixed = [jnp.concatenate([oa, ob, og, oh], axis=-1).reshape(n_lat, D)]
        if want_ctx:
            oac = dense_attention(diff_q(qac), diff_k(kac), _head_major(vac).astype(BF16), coefs, None,
                                  DIFF_HEADS, C, C)
            oac = _from_q_layout(oac, B, DIFF_HEADS, 1)
            obc = dense_attention(_q_layout(qwc * hscale, WIN_KV_HEADS, Rw).astype(BF16)[None],
                                  _head_major(kwc).astype(BF16)[None], _head_major(vwc).astype(BF16),
                                  one, sink_win[i], WIN_KV_HEADS, C, C)
            obc = _from_q_layout(obc, B, WIN_KV_HEADS, Rw)
            ogc = dense_attention(_q_layout(qgc * hscale, GLOB_KV_HEADS, Rg).astype(BF16)[None],
                                  _head_major(kgc).astype(BF16)[None], _head_major(vgc).astype(BF16),
                                  one, None, GLOB_KV_HEADS, C, C)
            ogc = _from_q_layout(ogc, B, GLOB_KV_HEADS, Rg)
            ohc = hyena(phc, hy_conv_w[i], hy_conv_b[i], filt, hy_bias[i])
            mixed.append(jnp.concatenate([oac, obc, ogc, ohc], axis=-1).reshape(n_ctx, D))
        o = jnp.concatenate(mixed, axis=0)
        n_rows = o.shape[0]

        oh_ = rmsnorm(o.reshape(n_rows, N_OUT_HEADS, HEAD_DIM), g_mix_out[i].reshape(N_OUT_HEADS, HEAD_DIM))
        head_scale = jnp.where(jnp.arange(N_OUT_HEADS) < DIFF_HEADS, 1.0 - lambda_init, 1.0)[:, None]
        om = (oh_ * head_scale.astype(F32)).reshape(n_rows, D)
        proj = mm(om.astype(BF16), w_out[i].astype(BF16), TM, 512)

        def seg_rows(v, n):
            lat = jnp.repeat(v[:B], L, axis=0)
            if n == n_lat:
                return lat
            return jnp.concatenate([lat, jnp.broadcast_to(v[B:], (n_ctx, v.shape[-1]))], axis=0)

        xcur = xall[:n_rows] + seg_rows(g1, n_rows) * proj

        qp = normmod_mm(xcur, g_norm_ffn[i], sh2, s2, peer_wq[i].astype(BF16), L, TM, 512)
        h2 = (rmsnorm(xcur, g_norm_ffn[i]) * (1.0 + seg_rows(s2, n_rows)) + seg_rows(sh2, n_rows)).astype(BF16)
        ff = peer(qp, h2, peer_keys[i], peer_u[i].astype(BF16), jnp.transpose(peer_v[i].astype(BF16)), TM)
        xcur = xcur + seg_rows(g2, n_rows) * ff
        xl = xcur[:n_lat]
        if want_ctx:
            xc = xcur[n_lat:]

    return xl.reshape(B, L, D)
```

```python
import functools
import math

import jax
import jax.numpy as jnp
from jax import lax
from jax.experimental import pallas as pl
from jax.experimental.pallas import tpu as pltpu

F32 = jnp.float32
BF16 = jnp.bfloat16

GRID_W = 64
HEAD_DIM = 64
BLOCK = 128
WINDOW = 128
ROPE_THETA = 10000.0
EPS = 1e-6
NEG_INF = -1e30
N_MOD = 6
DIFF_HEADS = 4
DIFF_QK_DIM = 32
WIN_HEADS = 4
WIN_KV_HEADS = 2
GLOB_HEADS = 4
GLOB_KV_HEADS = 2
N_OUT_HEADS = 16
HY_CH = 256
HY_ORDER = 2
HY_BANDS = 16
HY_TARGET = 1e-2
HY_FAST = 0.3
HY_SLOW = 1.5
W_DIFF = 768
W_WIN = 512
W_GLOB = 512
SPLITS = (W_DIFF, W_DIFF + W_WIN, W_DIFF + W_WIN + W_GLOB)
PEER_HEADS = 8
N_KEYS = 128
PEER_TOPK = 16

VMEM_LIMIT = 56 * 1024 * 1024


def _params(*sem):
    return pltpu.CompilerParams(dimension_semantics=sem, vmem_limit_bytes=VMEM_LIMIT)


def _mm_kernel(a_ref, b_ref, o_ref):
    o_ref[...] = jnp.dot(a_ref[...], b_ref[...], preferred_element_type=F32)


def mm(a, b, tm, tn):
    M, K = a.shape
    N = b.shape[1]
    return pl.pallas_call(
        _mm_kernel,
        grid=(M // tm, N // tn),
        in_specs=[pl.BlockSpec((tm, K), lambda i, j: (i, 0)),
                  pl.BlockSpec((K, tn), lambda i, j: (0, j))],
        out_specs=pl.BlockSpec((tm, tn), lambda i, j: (i, j)),
        out_shape=jax.ShapeDtypeStruct((M, N), F32),
        compiler_params=_params("parallel", "arbitrary"),
        name="mm",
    )(a, b)


def _normmod_mm_kernel(x_ref, g_ref, sh_ref, sc_ref, w_ref, o_ref, h_ref):
    @pl.when(pl.program_id(1) == 0)
    def _():
        x = x_ref[...]
        y = x * lax.rsqrt(jnp.mean(x * x, axis=-1, keepdims=True) + EPS) * g_ref[...]
        h_ref[...] = (y * (1.0 + sc_ref[0]) + sh_ref[0]).astype(BF16)

    o_ref[...] = jnp.dot(h_ref[...], w_ref[...], preferred_element_type=F32)


def normmod_mm(x, g, shift, scale, w, rows_per_seg, tm, tn):
    M, D = x.shape
    N = w.shape[1]
    nseg = shift.shape[0]
    blocks_per_seg = rows_per_seg // tm
    seg = lambda i, j: (jnp.minimum(i // blocks_per_seg, nseg - 1), 0, 0)
    return pl.pallas_call(
        _normmod_mm_kernel,
        grid=(M // tm, N // tn),
        in_specs=[pl.BlockSpec((tm, D), lambda i, j: (i, 0)),
                  pl.BlockSpec((1, D), lambda i, j: (0, 0)),
                  pl.BlockSpec((1, 1, D), seg),
                  pl.BlockSpec((1, 1, D), seg),
                  pl.BlockSpec((D, tn), lambda i, j: (0, j))],
        out_specs=pl.BlockSpec((tm, tn), lambda i, j: (i, j)),
        out_shape=jax.ShapeDtypeStruct((M, N), F32),
        scratch_shapes=[pltpu.VMEM((tm, D), BF16)],
        compiler_params=_params("parallel", "arbitrary"),
        name="normmod_mm",
    )(x, g.reshape(1, D), shift.reshape(nseg, 1, D), scale.reshape(nseg, 1, D), w)


def _nt_kernel(a_ref, b_ref, o_ref):
    o_ref[0] = lax.dot_general(a_ref[0], b_ref[...], (((1,), (1,)), ((), ())),
                               preferred_element_type=F32)


def keys_times_qT(keys, q, tn):
    P, n, d = keys.shape
    T = q.shape[0]
    return pl.pallas_call(
        _nt_kernel,
        grid=(P, T // tn),
        in_specs=[pl.BlockSpec((1, n, d), lambda p, j: (p, 0, 0)),
                  pl.BlockSpec((tn, d), lambda p, j: (j, p))],
        out_specs=pl.BlockSpec((1, n, tn), lambda p, j: (p, 0, j)),
        out_shape=jax.ShapeDtypeStruct((P, n, T), F32),
        compiler_params=_params("parallel", "arbitrary"),
        name="peer_scores_t",
    )(keys, q)


def _dense_attn_kernel(coef_ref, sink_ref, q_ref, k_ref, v_ref, o_ref, *, nbr, R, G, tq, tk, nk,
                       has_sink):
    rows = R * tq
    d = q_ref.shape[-1]
    dv = v_ref.shape[-1]
    g = pl.program_id(0) % G
    out = jnp.zeros((rows, dv), F32)
    for br in range(nbr):
        q = q_ref[br, 0].reshape(rows, d)
        if has_sink:
            m0 = jnp.concatenate([jnp.full((tq, 1), sink_ref[g * R + r], F32) for r in range(R)],
                                 axis=0)
            l0 = jnp.ones((rows, 1), F32)
        else:
            m0 = jnp.full((rows, 1), NEG_INF, F32)
            l0 = jnp.zeros((rows, 1), F32)

        def body(j, carry, br=br, q=q):
            m, l, acc = carry
            start = pl.multiple_of(j * tk, tk)
            kblk = k_ref[br, 0, pl.ds(start, tk), :]
            vblk = v_ref[0, pl.ds(start, tk), :]
            s = lax.dot_general(q, kblk, (((1,), (1,)), ((), ())), preferred_element_type=F32)
            m_new = jnp.maximum(m, jnp.max(s, axis=-1, keepdims=True))
            alpha = jnp.exp(m - m_new)
            p = jnp.exp(s - m_new)
            l = alpha * l + jnp.sum(p, axis=-1, keepdims=True)
            acc = alpha * acc + jnp.dot(p.astype(BF16), vblk, preferred_element_type=F32)
            return m_new, l, acc

        m, l, acc = lax.fori_loop(0, nk, body, (m0, l0, jnp.zeros((rows, dv), F32)))
        out = out + coef_ref[br] * (acc / l)
    o_ref[0] = out.reshape(R, tq, dv)


def dense_attention(q, k, v, coefs, sink, G, tq, tk):
    nbr, BG, R, Lq, d = q.shape
    Lk = k.shape[2]
    dv = v.shape[-1]
    has_sink = sink is not None
    if sink is None:
        sink = jnp.zeros((1,), F32)
    kern = functools.partial(_dense_attn_kernel, nbr=nbr, R=R, G=G, tq=tq, tk=tk, nk=Lk // tk,
                             has_sink=has_sink)
    return pl.pallas_call(
        kern,
        grid=(BG, Lq // tq),
        in_specs=[pl.BlockSpec(memory_space=pltpu.SMEM),
                  pl.BlockSpec(memory_space=pltpu.SMEM),
                  pl.BlockSpec((nbr, 1, R, tq, d), lambda b, i: (0, b, 0, i, 0)),
                  pl.BlockSpec((nbr, 1, Lk, d), lambda b, i: (0, b, 0, 0)),
                  pl.BlockSpec((1, Lk, dv), lambda b, i: (b, 0, 0))],
        out_specs=pl.BlockSpec((1, R, tq, dv), lambda b, i: (b, 0, i, 0)),
        out_shape=jax.ShapeDtypeStruct((BG, R, Lq, dv), F32),
        compiler_params=_params("parallel", "arbitrary"),
        name="dense_attn",
    )(coefs.astype(F32), sink.astype(F32), q, k, v)


def _window_attn_kernel(sink_ref, q_ref, k_ref, v_ref, o_ref, *, R, G, C, L):
    rows = R * BLOCK
    d = q_ref.shape[-1]
    g = pl.program_id(0) % G
    n = pl.program_id(1)
    q = q_ref[0].reshape(rows, d)
    start = pl.multiple_of(C + n * BLOCK, BLOCK)
    kc = k_ref[0, 0:C, :]
    vc = v_ref[0, 0:C, :]
    kl = k_ref[0, pl.ds(start, 3 * BLOCK), :]
    vl = v_ref[0, pl.ds(start, 3 * BLOCK), :]
    nt = (((1,), (1,)), ((), ()))
    s_ctx = lax.dot_general(q, kc, nt, preferred_element_type=F32)
    s_loc = lax.dot_general(q, kl, nt, preferred_element_type=F32)
    qi = lax.broadcasted_iota(jnp.int32, (rows, 3 * BLOCK), 0) & (BLOCK - 1)
    kj = lax.broadcasted_iota(jnp.int32, (rows, 3 * BLOCK), 1)
    kpos = (n - 1) * BLOCK + kj
    valid = (jnp.abs(kj - BLOCK - qi) <= WINDOW) & (kpos >= 0) & (kpos < L)
    s_loc = jnp.where(valid, s_loc, NEG_INF)
    sink = jnp.concatenate([jnp.full((BLOCK, 1), sink_ref[g * R + r], F32) for r in range(R)], axis=0)
    m = jnp.maximum(jnp.maximum(jnp.max(s_ctx, axis=-1, keepdims=True),
                                jnp.max(s_loc, axis=-1, keepdims=True)), sink)
    e_ctx = jnp.exp(s_ctx - m)
    e_loc = jnp.exp(s_loc - m)
    den = (jnp.sum(e_ctx, axis=-1, keepdims=True) + jnp.sum(e_loc, axis=-1, keepdims=True)
           + jnp.exp(sink - m))
    inv = 1.0 / den
    o = (jnp.dot((e_ctx * inv).astype(BF16), vc, preferred_element_type=F32)
         + jnp.dot((e_loc * inv).astype(BF16), vl, preferred_element_type=F32))
    o_ref[0] = o.reshape(R, BLOCK, d)


def window_attention(q, kpad, vpad, sink, G, C, L):
    BG, R, _, d = q.shape
    Lp = kpad.shape[1]
    kern = functools.partial(_window_attn_kernel, R=R, G=G, C=C, L=L)
    return pl.pallas_call(
        kern,
        grid=(BG, L // BLOCK),
        in_specs=[pl.BlockSpec(memory_space=pltpu.SMEM),
                  pl.BlockSpec((1, R, BLOCK, d), lambda b, i: (b, 0, i, 0)),
                  pl.BlockSpec((1, Lp, d), lambda b, i: (b, 0, 0)),
                  pl.BlockSpec((1, Lp, d), lambda b, i: (b, 0, 0))],
        out_specs=pl.BlockSpec((1, R, BLOCK, d), lambda b, i: (b, 0, i, 0)),
        out_shape=jax.ShapeDtypeStruct((BG, R, L, d), F32),
        compiler_params=_params("parallel", "arbitrary"),
        name="window_attn",
    )(sink.astype(F32), q, kpad, vpad)


_CAND_PAIRS = tuple((r, s) for r in range(PEER_TOPK) for s in range(PEER_TOPK)
                    if (r + 1) * (s + 1) <= PEER_TOPK)
_CAND_ROWS = -(-len(_CAND_PAIRS) // 8) * 8


def _top_rows(v, n):
    iota = lax.broadcasted_iota(jnp.int32, v.shape, 0)
    rows = []
    for r in range(n):
        m = jnp.max(v, axis=0, keepdims=True)
        rows.append(m)
        if r + 1 < n:
            first = jnp.min(jnp.where(v == m, iota, v.shape[0]), axis=0, keepdims=True)
            v = jnp.where(iota == first, -jnp.inf, v)
    return rows


def _peer_select_kernel(s_ref, ea_ref, eb_ref, tau_ref, c_ref, *, tm):
    n_lane = tm // 128

    def body(it, carry):
        h = it // n_lane
        lanes = pl.ds(pl.multiple_of((it % n_lane) * 128, 128), 128)
        a = s_ref[h, 0, :, lanes]
        b = s_ref[h, 1, :, lanes]
        ta = _top_rows(a, PEER_TOPK)
        tb = _top_rows(b, PEER_TOPK)
        c_ref[...] = jnp.full(c_ref.shape, -jnp.inf, F32)
        for k, (r, s) in enumerate(_CAND_PAIRS):
            c_ref[k:k + 1, :] = ta[r] + tb[s]
        best = _top_rows(c_ref[...], PEER_TOPK)
        zsum = jnp.zeros_like(best[0])
        for bk in best:
            zsum = zsum + jnp.exp(bk - best[0])
        tau_ref[h, :, lanes] = best[PEER_TOPK - 1]
        ea_ref[h, :, lanes] = jnp.exp(a - ta[0]) / zsum
        eb_ref[h, :, lanes] = jnp.exp(b - tb[0])
        return carry

    lax.fori_loop(0, PEER_HEADS * n_lane, body, 0)


def peer_select(st, tm):
    H, _, n, T = st.shape
    kern = functools.partial(_peer_select_kernel, tm=tm)
    return pl.pallas_call(
        kern,
        grid=(T // tm,),
        in_specs=[pl.BlockSpec((H, 2, n, tm), lambda t: (0, 0, 0, t))],
        out_specs=[pl.BlockSpec((H, n, tm), lambda t: (0, 0, t)),
                   pl.BlockSpec((H, n, tm), lambda t: (0, 0, t)),
                   pl.BlockSpec((H, 1, tm), lambda t: (0, 0, t))],
        out_shape=[jax.ShapeDtypeStruct((H, n, T), F32),
                   jax.ShapeDtypeStruct((H, n, T), F32),
                   jax.ShapeDtypeStruct((H, 1, T), F32)],
        scratch_shapes=[pltpu.VMEM((_CAND_ROWS, 128), F32)],
        compiler_params=_params("parallel"),
        name="peer_select",
    )(st)


def _peer_kernel(x_ref, u_ref, vt_ref, s1_ref, s2_ref, ea_ref, eb_ref, tau_ref, o_ref,
                 acc_ref, act_ref, g_ref, *, ni, tm):
    c = pl.program_id(1)

    @pl.when(c == 0)
    def _():
        acc_ref[...] = jnp.zeros_like(acc_ref)

    act_ref[...] = jax.nn.gelu(lax.dot_general(u_ref[...], x_ref[...], (((1,), (1,)), ((), ())),
                                               preferred_element_type=F32))

    def one_lane_tile(ts, carry):
        lanes = pl.ds(pl.multiple_of(ts * 128, 128), 128)
        for ii in range(ni):
            rows = slice(ii * N_KEYS, (ii + 1) * N_KEYS)
            w = jnp.zeros((N_KEYS, 128), F32)
            for h in range(PEER_HEADS):
                z = s1_ref[h, ii:ii + 1, lanes] + s2_ref[h, :, lanes]
                gate = ea_ref[h, ii:ii + 1, lanes] * eb_ref[h, :, lanes]
                w = w + jnp.where(z >= tau_ref[h, :, lanes], gate, 0.0)
            g_ref[rows, lanes] = (w * act_ref[rows, lanes]).astype(BF16)
        return carry

    lax.fori_loop(0, tm // 128, one_lane_tile, 0)
    acc_ref[...] += jnp.dot(vt_ref[...], g_ref[...], preferred_element_type=F32)

    @pl.when(c == pl.num_programs(1) - 1)
    def _():
        o_ref[...] = acc_ref[...].T


def peer_dense(x, u, vt, st, eat, ebt, taut, tm, ni):
    T, D = x.shape
    E = u.shape[0]
    ec = ni * N_KEYS
    kern = functools.partial(_peer_kernel, ni=ni, tm=tm)
    return pl.pallas_call(
        kern,
        grid=(T // tm, E // ec),
        in_specs=[pl.BlockSpec((tm, D), lambda t, c: (t, 0)),
                  pl.BlockSpec((ec, D), lambda t, c: (c, 0)),
                  pl.BlockSpec((D, ec), lambda t, c: (0, c)),
                  pl.BlockSpec((PEER_HEADS, None, ni, tm), lambda t, c: (0, 0, c, t)),
                  pl.BlockSpec((PEER_HEADS, None, N_KEYS, tm), lambda t, c: (0, 1, 0, t)),
                  pl.BlockSpec((PEER_HEADS, ni, tm), lambda t, c: (0, c, t)),
                  pl.BlockSpec((PEER_HEADS, N_KEYS, tm), lambda t, c: (0, 0, t)),
                  pl.BlockSpec((PEER_HEADS, 1, tm), lambda t, c: (0, 0, t))],
        out_specs=pl.BlockSpec((tm, D), lambda t, c: (t, 0)),
        out_shape=jax.ShapeDtypeStruct((T, D), F32),
        scratch_shapes=[pltpu.VMEM((D, tm), F32),
                        pltpu.VMEM((ec, tm), F32),
                        pltpu.VMEM((ec, tm), BF16)],
        compiler_params=_params("parallel", "arbitrary"),
        name="peer_dense",
    )(x, u, vt, st, st, eat, ebt, taut)


def peer(qp, h_bf, keys, u_bf, vt_bf, tm):
    T = qp.shape[0]
    dk = keys.shape[-1]
    kflat = keys.reshape(PEER_HEADS * 2, N_KEYS, dk).astype(BF16)
    st = keys_times_qT(kflat, qp.astype(BF16), tm).reshape(PEER_HEADS, 2, N_KEYS, T)
    eat, ebt, taut = peer_select(st, 256)
    return peer_dense(h_bf, u_bf, vt_bf, st, eat, ebt, taut, tm, 8)


def rmsnorm(x, g):
    return x * lax.rsqrt(jnp.mean(x * x, axis=-1, keepdims=True) + EPS) * g


def axial_rope(n_tok, dim):
    rows = n_tok // GRID_W
    row = jnp.repeat(jnp.arange(rows, dtype=F32), GRID_W)
    col = jnp.tile(jnp.arange(GRID_W, dtype=F32), rows)
    axis_dim = dim // 2
    inv_freq = ROPE_THETA ** (-jnp.arange(0, axis_dim, 2, dtype=F32) / axis_dim)
    ang = jnp.concatenate([row[:, None] * inv_freq, col[:, None] * inv_freq], axis=-1)
    return jnp.cos(ang), jnp.sin(ang)


def apply_rope(x, cos, sin):
    shape = (1, x.shape[1]) + (1,) * (x.ndim - 3) + (cos.shape[-1],)
    c, s = cos.reshape(shape), sin.reshape(shape)
    x1, x2 = jnp.split(x, 2, axis=-1)
    return jnp.concatenate([x1 * c - x2 * s, x1 * s + x2 * c], axis=-1)


def short_conv(u, w, b):
    up = jnp.pad(u, ((0, 0), (1, 1), (0, 0)))
    return up[:, :-2] * w[0] + up[:, 1:-1] * w[1] + up[:, 2:] * w[2] + b


def hyena_filters(n_tok, w1, b1, w2, b2, w3, b3, freq):
    hp = lax.Precision.HIGHEST
    t = jnp.linspace(0.0, 1.0, n_tok, dtype=F32)[:, None]
    w = (2.0 * math.pi / n_tok) * jnp.arange(n_tok, dtype=F32)[:, None]
    f = jnp.linspace(1e-4, HY_BANDS - 1, HY_BANDS, dtype=F32)[None, :]
    feat = jnp.concatenate([t, jnp.cos(f * w), -jnp.sin(f * w)], axis=-1)
    hdn = jnp.sin(freq * (jnp.dot(feat, w1, precision=hp) + b1))
    hdn = jnp.sin(freq * (jnp.dot(hdn, w2, precision=hp) + b2))
    filt = (jnp.dot(hdn, w3, precision=hp) + b3).reshape(n_tok, HY_ORDER, 2, HY_CH)
    deltas = jnp.abs(jnp.linspace(math.log(HY_TARGET) / HY_SLOW, math.log(HY_TARGET) / HY_FAST, HY_CH,
                                  dtype=F32))
    filt = filt * jnp.exp(-t[:, :, None, None] * deltas)
    fwd, bwd = filt[:, :, 0], filt[:, :, 1]
    kfull = jnp.concatenate([fwd, jnp.zeros_like(fwd[:1]), bwd[1:][::-1]], axis=0)
    return kfull / jnp.sum(jnp.abs(kfull), axis=0, keepdims=True)


def long_conv(z, kf):
    n = z.shape[1]
    zf = jnp.fft.rfft(z, n=2 * n, axis=1)
    hf = jnp.fft.rfft(kf, n=2 * n, axis=0)
    return jnp.fft.irfft(zf * hf[None], n=2 * n, axis=1)[:, :n]


def hyena(u, conv_w, conv_b, filter_params, bias):
    n = u.shape[1]
    u = short_conv(u, conv_w, conv_b)
    v, x1, x2 = jnp.split(u, 3, axis=-1)
    kfull = hyena_filters(n, *filter_params)
    z = x1 * (long_conv(v, kfull[:, 0]) + bias[0] * v)
    return x2 * (long_conv(z, kfull[:, 1]) + bias[1] * z)


def _head_major(t):
    B, L, H, d = t.shape
    return jnp.transpose(t, (0, 2, 1, 3)).reshape(B * H, L, d)


def _q_layout(t, G, R):
    B, L, _, d = t.shape
    return jnp.transpose(t.reshape(B, L, G, R, d), (0, 2, 3, 1, 4)).reshape(B * G, R, L, d)


def _from_q_layout(o, B, G, R):
    _, _, L, d = o.shape
    return jnp.transpose(o.reshape(B, G, R, L, d), (0, 3, 1, 2, 4)).reshape(B, L, G * R * d)


def _pick(n, cands):
    for c in cands:
        if n % c == 0:
            return c
    raise ValueError(f"no tile for {n}")


def kernel(x, c, ctx, c_ctx, w_mod, b_mod, g_norm_mix, w_in, g_qk_diff, lambda_diff, g_qk_win, sink_win, g_qk_glob, hy_conv_w, hy_conv_b, hy_w1, hy_b1, hy_w2, hy_b2, hy_w3, hy_b3, hy_freq, hy_bias, g_mix_out, w_out, g_norm_ffn, peer_wq, peer_keys, peer_u, peer_v):
    B, L, D = x.shape
    C = ctx.shape[1]
    depth = w_mod.shape[0]
    TM = 512
    n_lat = B * L
    n_ctx = B * C
    assert L % TM == 0 and n_ctx % TM == 0
    nseg = B + 1

    rope_half = axial_rope(L, DIFF_QK_DIM)
    rope_full = axial_rope(L, HEAD_DIM)
    sc = jnp.concatenate([jax.nn.silu(c), jax.nn.silu(c_ctx)[None]], axis=0)
    sc = jnp.pad(sc, ((0, 8 - nseg), (0, 0))).astype(BF16)

    xl = x.reshape(n_lat, D)
    xc = ctx.reshape(n_ctx, D)

    for i in range(depth):
        want_ctx = i < depth - 1
        lambda_init = 0.8 - 0.6 * math.exp(-0.3 * i)
        mod = mm(sc, w_mod[i].astype(BF16), 8, 1024)[:nseg] + b_mod[i]
        sh1, s1, g1, sh2, s2, g2 = jnp.split(mod, N_MOD, axis=-1)

        xall = jnp.concatenate([xl, xc], axis=0)
        p = normmod_mm(xall, g_norm_mix[i], sh1, s1, w_in[i].astype(BF16), L, TM, 640)
        pl_lat = p[:n_lat].reshape(B, L, -1)
        pc_ctx = p[n_lat:].reshape(B, C, -1)
        pa, pw, pg, ph = jnp.split(pl_lat, SPLITS, axis=-1)
        pac, pwc, pgc, phc = jnp.split(pc_ctx, SPLITS, axis=-1)

        def diff_heads(t):
            B_, L_, _ = t.shape
            q, k, v = jnp.split(t, 3, axis=-1)
            q = rmsnorm(q.reshape(B_, L_, DIFF_HEADS, 2, DIFF_QK_DIM), g_qk_diff[i, 0])
            k = rmsnorm(k.reshape(B_, L_, DIFF_HEADS, 2, DIFF_QK_DIM), g_qk_diff[i, 1])
            return q, k, v.reshape(B_, L_, DIFF_HEADS, HEAD_DIM)

        qa, ka, va = diff_heads(pa)
        qac, kac, vac = diff_heads(pac)
        qa, ka = apply_rope(qa, *rope_half), apply_rope(ka, *rope_half)
        lam_vec = lambda_diff[i]
        lam = (jnp.exp(jnp.sum(lam_vec[0] * lam_vec[1])) - jnp.exp(jnp.sum(lam_vec[2] * lam_vec[3]))
               + lambda_init)
        coefs = jnp.stack([jnp.ones((), F32), -lam])
        dscale = DIFF_QK_DIM ** -0.5

        def diff_q(q):
            return jnp.stack([_head_major(q[..., b_, :] * dscale) for b_ in range(2)])[:, :, None].astype(BF16)

        def diff_k(k):
            return jnp.stack([_head_major(k[..., b_, :]) for b_ in range(2)]).astype(BF16)

        kka = jnp.concatenate([kac, ka], axis=1)
        vva = jnp.concatenate([vac, va], axis=1)
        Lk = C + L
        tk = _pick(Lk, (768, 512, 256, 128))
        oa = dense_attention(diff_q(qa), diff_k(kka), _head_major(vva).astype(BF16), coefs, None,
                             DIFF_HEADS, 256, tk)
        oa = _from_q_layout(oa, B, DIFF_HEADS, 1)

        def gqa_heads(t, n_q, n_kv, g_qk):
            B_, L_, _ = t.shape
            q, k, v = jnp.split(t, [n_q * HEAD_DIM, (n_q + n_kv) * HEAD_DIM], axis=-1)
            q = rmsnorm(q.reshape(B_, L_, n_q, HEAD_DIM), g_qk[0])
            k = rmsnorm(k.reshape(B_, L_, n_kv, HEAD_DIM), g_qk[1])
            return q, k, v.reshape(B_, L_, n_kv, HEAD_DIM)

        hscale = HEAD_DIM ** -0.5
        qw, kw, vw = gqa_heads(pw, WIN_HEADS, WIN_KV_HEADS, g_qk_win[i])
        qwc, kwc, vwc = gqa_heads(pwc, WIN_HEADS, WIN_KV_HEADS, g_qk_win[i])
        qw, kw = apply_rope(qw, *rope_full), apply_rope(kw, *rope_full)
        zblk = jnp.zeros((B, BLOCK, WIN_KV_HEADS, HEAD_DIM), F32)
        kpad = _head_major(jnp.concatenate([kwc, zblk, kw, zblk], axis=1)).astype(BF16)
        vpad = _head_major(jnp.concatenate([vwc, zblk, vw, zblk], axis=1)).astype(BF16)
        Rw = WIN_HEADS // WIN_KV_HEADS
        ob = window_attention(_q_layout(qw * hscale, WIN_KV_HEADS, Rw).astype(BF16), kpad, vpad,
                              sink_win[i], WIN_KV_HEADS, C, L)
        ob = _from_q_layout(ob, B, WIN_KV_HEADS, Rw)

        qg, kg, vg = gqa_heads(pg, GLOB_HEADS, GLOB_KV_HEADS, g_qk_glob[i])
        qgc, kgc, vgc = gqa_heads(pgc, GLOB_HEADS, GLOB_KV_HEADS, g_qk_glob[i])
        qg, kg = apply_rope(qg, *rope_full), apply_rope(kg, *rope_full)
        Rg = GLOB_HEADS // GLOB_KV_HEADS
        one = jnp.ones((1,), F32)
        kkg = _head_major(jnp.concatenate([kgc, kg], axis=1)).astype(BF16)[None]
        vvg = _head_major(jnp.concatenate([vgc, vg], axis=1)).astype(BF16)
        og = dense_attention(_q_layout(qg * hscale, GLOB_KV_HEADS, Rg).astype(BF16)[None], kkg, vvg,
                             one, None, GLOB_KV_HEADS, 128, tk)
        og = _from_q_layout(og, B, GLOB_KV_HEADS, Rg)

        filt = (hy_w1[i], hy_b1[i], hy_w2[i], hy_b2[i], hy_w3[i], hy_b3[i], hy_freq[i])
        oh = hyena(ph, hy_conv_w[i], hy_conv_b[i], filt, hy_bias[i])

        mixed = [jnp.concatenate([oa, ob, og, oh], axis=-1).reshape(n_lat, D)]
        if want_ctx:
            oac = dense_attention(diff_q(qac), diff_k(kac), _head_major(vac).astype(BF16), coefs, None,
                                  DIFF_HEADS, C, C)
            oac = _from_q_layout(oac, B, DIFF_HEADS, 1)
            obc = dense_attention(_q_layout(qwc * hscale, WIN_KV_HEADS, Rw).astype(BF16)[None],
                                  _head_major(kwc).astype(BF16)[None], _head_major(vwc).astype(BF16),
                                  one, sink_win[i], WIN_KV_HEADS, C, C)
            obc = _from_q_layout(obc, B, WIN_KV_HEADS, Rw)
            ogc = dense_attention(_q_layout(qgc * hscale, GLOB_KV_HEADS, Rg).astype(BF16)[None],
                                  _head_major(kgc).astype(BF16)[None], _head_major(vgc).astype(BF16),
                                  one, None, GLOB_KV_HEADS, C, C)
            ogc = _from_q_layout(ogc, B, GLOB_KV_HEADS, Rg)
            ohc = hyena(phc, hy_conv_w[i], hy_conv_b[i], filt, hy_bias[i])
            mixed.append(jnp.concatenate([oac, obc, ogc, ohc], axis=-1).reshape(n_ctx, D))
        o = jnp.concatenate(mixed, axis=0)
        n_rows = o.shape[0]

        oh_ = rmsnorm(o.reshape(n_rows, N_OUT_HEADS, HEAD_DIM), g_mix_out[i].reshape(N_OUT_HEADS, HEAD_DIM))
        head_scale = jnp.where(jnp.arange(N_OUT_HEADS) < DIFF_HEADS, 1.0 - lambda_init, 1.0)[:, None]
        om = (oh_ * head_scale.astype(F32)).reshape(n_rows, D)
        proj = mm(om.astype(BF16), w_out[i].astype(BF16), TM, 512)

        def seg_rows(v, n):
            lat = jnp.repeat(v[:B], L, axis=0)
            if n == n_lat:
                return lat
            return jnp.concatenate([lat, jnp.broadcast_to(v[B:], (n_ctx, v.shape[-1]))], axis=0)

        xcur = xall[:n_rows] + seg_rows(g1, n_rows) * proj

        qp = normmod_mm(xcur, g_norm_ffn[i], sh2, s2, peer_wq[i].astype(BF16), L, TM, 512)
        h2 = (rmsnorm(xcur, g_norm_ffn[i]) * (1.0 + seg_rows(s2, n_rows)) + seg_rows(sh2, n_rows)).astype(BF16)
        ff = peer(qp, h2, peer_keys[i], peer_u[i].astype(BF16), jnp.transpose(peer_v[i].astype(BF16)), TM)
        xcur = xcur + seg_rows(g2, n_rows) * ff
        xl = xcur[:n_lat]
        if want_ctx:
            xc = xcur[n_lat:]

    return xl.reshape(B, L, D)
```

```python
import functools
import math

import jax
import jax.numpy as jnp
from jax import lax
from jax.experimental import pallas as pl
from jax.experimental.pallas import tpu as pltpu

F32 = jnp.float32
BF16 = jnp.bfloat16

GRID_W = 64
HEAD_DIM = 64
BLOCK = 128
WINDOW = 128
ROPE_THETA = 10000.0
EPS = 1e-6
NEG_INF = -1e30
N_MOD = 6
DIFF_HEADS = 4
DIFF_QK_DIM = 32
WIN_HEADS = 4
WIN_KV_HEADS = 2
GLOB_HEADS = 4
GLOB_KV_HEADS = 2
N_OUT_HEADS = 16
HY_CH = 256
HY_ORDER = 2
HY_BANDS = 16
HY_TARGET = 1e-2
HY_FAST = 0.3
HY_SLOW = 1.5
W_DIFF = 768
W_WIN = 512
W_GLOB = 512
SPLITS = (W_DIFF, W_DIFF + W_WIN, W_DIFF + W_WIN + W_GLOB)
PEER_HEADS = 8
N_KEYS = 128
PEER_TOPK = 16

VMEM_LIMIT = 56 * 1024 * 1024
ATTN_TQ_DIFF = 512
ATTN_TQ_GLOB = 256
KEY_CHUNK_CAP = 1408


def _params(*sem):
    return pltpu.CompilerParams(dimension_semantics=sem, vmem_limit_bytes=VMEM_LIMIT)


def _mm_kernel(a_ref, b_ref, o_ref):
    o_ref[...] = jnp.dot(a_ref[...], b_ref[...], preferred_element_type=F32)


def mm(a, b, tm, tn):
    M, K = a.shape
    N = b.shape[1]
    return pl.pallas_call(
        _mm_kernel,
        grid=(M // tm, N // tn),
        in_specs=[pl.BlockSpec((tm, K), lambda i, j: (i, 0)),
                  pl.BlockSpec((K, tn), lambda i, j: (0, j))],
        out_specs=pl.BlockSpec((tm, tn), lambda i, j: (i, j)),
        out_shape=jax.ShapeDtypeStruct((M, N), F32),
        compiler_params=_params("parallel", "arbitrary"),
        name="mm",
    )(a, b)


def _normmod_mm_kernel(x_ref, g_ref, sh_ref, sc_ref, w_ref, o_ref, h_ref):
    @pl.when(pl.program_id(1) == 0)
    def _():
        x = x_ref[...]
        y = x * lax.rsqrt(jnp.mean(x * x, axis=-1, keepdims=True) + EPS) * g_ref[...]
        h_ref[...] = (y * (1.0 + sc_ref[0]) + sh_ref[0]).astype(BF16)

    o_ref[...] = jnp.dot(h_ref[...], w_ref[...], preferred_element_type=F32)


def normmod_mm(x, g, shift, scale, w, rows_per_seg, tm, tn):
    M, D = x.shape
    N = w.shape[1]
    nseg = shift.shape[0]
    blocks_per_seg = rows_per_seg // tm
    seg = lambda i, j: (jnp.minimum(i // blocks_per_seg, nseg - 1), 0, 0)
    return pl.pallas_call(
        _normmod_mm_kernel,
        grid=(M // tm, N // tn),
        in_specs=[pl.BlockSpec((tm, D), lambda i, j: (i, 0)),
                  pl.BlockSpec((1, D), lambda i, j: (0, 0)),
                  pl.BlockSpec((1, 1, D), seg),
                  pl.BlockSpec((1, 1, D), seg),
                  pl.BlockSpec((D, tn), lambda i, j: (0, j))],
        out_specs=pl.BlockSpec((tm, tn), lambda i, j: (i, j)),
        out_shape=jax.ShapeDtypeStruct((M, N), F32),
        scratch_shapes=[pltpu.VMEM((tm, D), BF16)],
        compiler_params=_params("parallel", "arbitrary"),
        name="normmod_mm",
    )(x, g.reshape(1, D), shift.reshape(nseg, 1, D), scale.reshape(nseg, 1, D), w)


def _nt_kernel(a_ref, b_ref, o_ref):
    o_ref[0] = lax.dot_general(a_ref[0], b_ref[...], (((1,), (1,)), ((), ())),
                               preferred_element_type=F32)


def keys_times_qT(keys, q, tn):
    P, n, d = keys.shape
    T = q.shape[0]
    return pl.pallas_call(
        _nt_kernel,
        grid=(P, T // tn),
        in_specs=[pl.BlockSpec((1, n, d), lambda p, j: (p, 0, 0)),
                  pl.BlockSpec((tn, d), lambda p, j: (j, p))],
        out_specs=pl.BlockSpec((1, n, tn), lambda p, j: (p, 0, j)),
        out_shape=jax.ShapeDtypeStruct((P, n, T), F32),
        compiler_params=_params("parallel", "arbitrary"),
        name="peer_scores_t",
    )(keys, q)


LOG2E = math.log2(math.e)


def _dense_attn_kernel(coef_ref, sink_ref, q_ref, k_ref, vt_ref, o_ref, sa_ref, sb_ref, *, nbr, R, G,
                       tq, tk, nk, has_sink):
    cols = R * tq
    d = q_ref.shape[-1]
    dv = vt_ref.shape[1]
    g = pl.program_id(0) % G
    nt = (((1,), (1,)), ((), ()))
    bufs = (sa_ref, sb_ref)
    out = jnp.zeros((dv, cols), F32)
    for br in range(nbr):
        q = q_ref[br, 0].reshape(cols, d)
        if has_sink:
            m = jnp.concatenate([jnp.full((1, tq), sink_ref[g * R + r] * LOG2E, F32)
                                 for r in range(R)], axis=1)
            l = jnp.ones((1, cols), F32)
        else:
            m = jnp.full((1, cols), NEG_INF, F32)
            l = jnp.zeros((1, cols), F32)
        acc = jnp.zeros((dv, cols), F32)
        bufs[0][...] = lax.dot_general(k_ref[br, 0, 0:tk, :], q, nt, preferred_element_type=F32)
        for j in range(nk):
            if j + 1 < nk:
                bufs[(j + 1) % 2][...] = lax.dot_general(k_ref[br, 0, (j + 1) * tk:(j + 2) * tk, :], q, nt,
                                                         preferred_element_type=F32)
            s = bufs[j % 2][...]
            m_new = jnp.maximum(m, jnp.max(s, axis=0, keepdims=True))
            alpha = jnp.exp2(m - m_new)
            p = jnp.exp2(s - m_new)
            l = alpha * l + jnp.sum(p, axis=0, keepdims=True)
            acc = alpha * acc + jnp.dot(vt_ref[0, :, j * tk:(j + 1) * tk], p.astype(BF16),
                                        preferred_element_type=F32)
            m = m_new
        out = out + coef_ref[br] * (acc / l)
    o_ref[0] = out.T.reshape(R, tq, dv)


def dense_attention(q, k, vt, coefs, sink, G, tq, tk):
    nbr, BG, R, Lq, d = q.shape
    Lk = k.shape[2]
    dv = vt.shape[1]
    has_sink = sink is not None
    if sink is None:
        sink = jnp.zeros((1,), F32)
    kern = functools.partial(_dense_attn_kernel, nbr=nbr, R=R, G=G, tq=tq, tk=tk, nk=Lk // tk,
                             has_sink=has_sink)
    return pl.pallas_call(
        kern,
        grid=(BG, Lq // tq),
        in_specs=[pl.BlockSpec(memory_space=pltpu.SMEM),
                  pl.BlockSpec(memory_space=pltpu.SMEM),
                  pl.BlockSpec((nbr, 1, R, tq, d), lambda b, i: (0, b, 0, i, 0)),
                  pl.BlockSpec((nbr, 1, Lk, d), lambda b, i: (0, b, 0, 0)),
                  pl.BlockSpec((1, dv, Lk), lambda b, i: (b, 0, 0))],
        out_specs=pl.BlockSpec((1, R, tq, dv), lambda b, i: (b, 0, i, 0)),
        out_shape=jax.ShapeDtypeStruct((BG, R, Lq, dv), F32),
        scratch_shapes=[pltpu.VMEM((tk, R * tq), F32), pltpu.VMEM((tk, R * tq), F32)],
        compiler_params=_params("parallel", "arbitrary"),
        name="dense_attn",
    )(coefs.astype(F32), sink.astype(F32), q, k, vt)


def _window_attn_kernel(sink_ref, q_ref, k_ref, v_ref, o_ref, *, R, G, C, L):
    rows = R * BLOCK
    d = q_ref.shape[-1]
    g = pl.program_id(0) % G
    n = pl.program_id(1)
    q = q_ref[0].reshape(rows, d)
    start = pl.multiple_of(C + n * BLOCK, BLOCK)
    kc = k_ref[0, 0:C, :]
    vc = v_ref[0, 0:C, :]
    kl = k_ref[0, pl.ds(start, 3 * BLOCK), :]
    vl = v_ref[0, pl.ds(start, 3 * BLOCK), :]
    nt = (((1,), (1,)), ((), ()))
    s_ctx = lax.dot_general(q, kc, nt, preferred_element_type=F32)
    s_loc = lax.dot_general(q, kl, nt, preferred_element_type=F32)
    qi = lax.broadcasted_iota(jnp.int32, (rows, 3 * BLOCK), 0) & (BLOCK - 1)
    kj = lax.broadcasted_iota(jnp.int32, (rows, 3 * BLOCK), 1)
    kpos = (n - 1) * BLOCK + kj
    valid = (jnp.abs(kj - BLOCK - qi) <= WINDOW) & (kpos >= 0) & (kpos < L)
    s_loc = jnp.where(valid, s_loc, NEG_INF)
    sink = jnp.concatenate([jnp.full((BLOCK, 1), sink_ref[g * R + r], F32) for r in range(R)], axis=0)
    m = jnp.maximum(jnp.maximum(jnp.max(s_ctx, axis=-1, keepdims=True),
                                jnp.max(s_loc, axis=-1, keepdims=True)), sink)
    e_ctx = jnp.exp(s_ctx - m)
    e_loc = jnp.exp(s_loc - m)
    den = (jnp.sum(e_ctx, axis=-1, keepdims=True) + jnp.sum(e_loc, axis=-1, keepdims=True)
           + jnp.exp(sink - m))
    inv = 1.0 / den
    o = (jnp.dot((e_ctx * inv).astype(BF16), vc, preferred_element_type=F32)
         + jnp.dot((e_loc * inv).astype(BF16), vl, preferred_element_type=F32))
    o_ref[0] = o.reshape(R, BLOCK, d)


def window_attention(q, kpad, vpad, sink, G, C, L):
    BG, R, _, d = q.shape
    Lp = kpad.shape[1]
    kern = functools.partial(_window_attn_kernel, R=R, G=G, C=C, L=L)
    return pl.pallas_call(
        kern,
        grid=(BG, L // BLOCK),
        in_specs=[pl.BlockSpec(memory_space=pltpu.SMEM),
                  pl.BlockSpec((1, R, BLOCK, d), lambda b, i: (b, 0, i, 0)),
                  pl.BlockSpec((1, Lp, d), lambda b, i: (b, 0, 0)),
                  pl.BlockSpec((1, Lp, d), lambda b, i: (b, 0, 0))],
        out_specs=pl.BlockSpec((1, R, BLOCK, d), lambda b, i: (b, 0, i, 0)),
        out_shape=jax.ShapeDtypeStruct((BG, R, L, d), F32),
        compiler_params=_params("parallel", "arbitrary"),
        name="window_attn",
    )(sink.astype(F32), q, kpad, vpad)


_CAND_PAIRS = tuple((r, s) for r in range(PEER_TOPK) for s in range(PEER_TOPK)
                    if (r + 1) * (s + 1) <= PEER_TOPK)
_CAND_ROWS = -(-len(_CAND_PAIRS) // 8) * 8


def _top_rows(v, n):
    iota = lax.broadcasted_iota(jnp.int32, v.shape, 0)
    rows = []
    for r in range(n):
        m = jnp.max(v, axis=0, keepdims=True)
        rows.append(m)
        if r + 1 < n:
            first = jnp.min(jnp.where(v == m, iota, v.shape[0]), axis=0, keepdims=True)
            v = jnp.where(iota == first, -jnp.inf, v)
    return rows


def _peer_select_kernel(s_ref, ea_ref, eb_ref, th_ref, c_ref, *, tm):
    n_lane = tm // 128

    def body(it, carry):
        h = it // n_lane
        lanes = pl.ds(pl.multiple_of((it % n_lane) * 128, 128), 128)
        a = s_ref[h, 0, :, lanes]
        b = s_ref[h, 1, :, lanes]
        ta = _top_rows(a, PEER_TOPK)
        tb = _top_rows(b, PEER_TOPK)
        c_ref[...] = jnp.full(c_ref.shape, -jnp.inf, F32)
        for k, (r, s) in enumerate(_CAND_PAIRS):
            c_ref[k:k + 1, :] = ta[r] + tb[s]
        best = _top_rows(c_ref[...], PEER_TOPK)
        tau = best[PEER_TOPK - 1]
        zsum = jnp.zeros_like(best[0])
        for bk in best:
            zsum = zsum + jnp.exp(bk - best[0])
        theta = jnp.full(a.shape, jnp.inf, F32)
        for r in reversed(range(PEER_TOPK)):
            th_r = jnp.full(tau.shape, jnp.inf, F32)
            for s in range(PEER_TOPK // (r + 1)):
                th_r = jnp.minimum(th_r, jnp.where(ta[r] + tb[s] >= tau, tb[s], jnp.inf))
            theta = jnp.where(a == ta[r], th_r, theta)
        th_ref[h, :, lanes] = theta
        ea_ref[h, :, lanes] = jnp.exp(a - ta[0]) / zsum
        eb_ref[h, :, lanes] = jnp.exp(b - tb[0])
        return carry

    lax.fori_loop(0, PEER_HEADS * n_lane, body, 0)


def peer_select(st, tm):
    H, _, n, T = st.shape
    kern = functools.partial(_peer_select_kernel, tm=tm)
    return pl.pallas_call(
        kern,
        grid=(T // tm,),
        in_specs=[pl.BlockSpec((H, 2, n, tm), lambda t: (0, 0, 0, t))],
        out_specs=[pl.BlockSpec((H, n, tm), lambda t: (0, 0, t))] * 3,
        out_shape=[jax.ShapeDtypeStruct((H, n, T), F32)] * 3,
        scratch_shapes=[pltpu.VMEM((_CAND_ROWS, 128), F32)],
        compiler_params=_params("parallel"),
        name="peer_select",
    )(st)


def _peer_kernel(x_ref, u_ref, vt_ref, s2_ref, th_ref, ea_ref, eb_ref, o_ref, acc_ref, act_ref, g_ref,
                 *, ni, tm):
    c = pl.program_id(1)

    @pl.when(c == 0)
    def _():
        acc_ref[...] = jnp.zeros_like(acc_ref)

    act_ref[...] = jax.nn.gelu(lax.dot_general(u_ref[...], x_ref[...], (((1,), (1,)), ((), ())),
                                               preferred_element_type=F32))

    for ts in range(tm // 128):
        lanes = slice(ts * 128, (ts + 1) * 128)
        for ii in range(ni):
            rows = slice(ii * N_KEYS, (ii + 1) * N_KEYS)
            w = jnp.zeros((N_KEYS, 128), F32)
            for h in range(PEER_HEADS):
                sel = s2_ref[h, :, lanes] >= th_ref[h, ii:ii + 1, lanes]
                w = w + jnp.where(sel, ea_ref[h, ii:ii + 1, lanes] * eb_ref[h, :, lanes], 0.0)
            g_ref[rows, lanes] = (w * act_ref[rows, lanes]).astype(BF16)

    acc_ref[...] += jnp.dot(vt_ref[...], g_ref[...], preferred_element_type=F32)

    @pl.when(c == pl.num_programs(1) - 1)
    def _():
        o_ref[...] = acc_ref[...].T


def peer_dense(x, u, vt, st, tht, eat, ebt, tm, ni):
    T, D = x.shape
    E = u.shape[0]
    ec = ni * N_KEYS
    kern = functools.partial(_peer_kernel, ni=ni, tm=tm)
    rows_of_chunk = pl.BlockSpec((PEER_HEADS, ni, tm), lambda t, c: (0, c, t))
    return pl.pallas_call(
        kern,
        grid=(T // tm, E // ec),
        in_specs=[pl.BlockSpec((tm, D), lambda t, c: (t, 0)),
                  pl.BlockSpec((ec, D), lambda t, c: (c, 0)),
                  pl.BlockSpec((D, ec), lambda t, c: (0, c)),
                  pl.BlockSpec((PEER_HEADS, None, N_KEYS, tm), lambda t, c: (0, 1, 0, t)),
                  rows_of_chunk,
                  rows_of_chunk,
                  pl.BlockSpec((PEER_HEADS, N_KEYS, tm), lambda t, c: (0, 0, t))],
        out_specs=pl.BlockSpec((tm, D), lambda t, c: (t, 0)),
        out_shape=jax.ShapeDtypeStruct((T, D), F32),
        scratch_shapes=[pltpu.VMEM((D, tm), F32),
                        pltpu.VMEM((ec, tm), F32),
                        pltpu.VMEM((ec, tm), BF16)],
        compiler_params=_params("parallel", "arbitrary"),
        name="peer_dense",
    )(x, u, vt, st, tht, eat, ebt)


def peer(qp, h_bf, keys, u_bf, vt_bf, tm):
    T = qp.shape[0]
    dk = keys.shape[-1]
    kflat = keys.reshape(PEER_HEADS * 2, N_KEYS, dk).astype(BF16)
    st = keys_times_qT(kflat, qp.astype(BF16), tm).reshape(PEER_HEADS, 2, N_KEYS, T)
    eat, ebt, tht = peer_select(st, 256)
    return peer_dense(h_bf, u_bf, vt_bf, st, tht, eat, ebt, tm, 8)


def rmsnorm(x, g):
    return x * lax.rsqrt(jnp.mean(x * x, axis=-1, keepdims=True) + EPS) * g


def axial_rope(n_tok, dim):
    rows = n_tok // GRID_W
    row = jnp.repeat(jnp.arange(rows, dtype=F32), GRID_W)
    col = jnp.tile(jnp.arange(GRID_W, dtype=F32), rows)
    axis_dim = dim // 2
    inv_freq = ROPE_THETA ** (-jnp.arange(0, axis_dim, 2, dtype=F32) / axis_dim)
    ang = jnp.concatenate([row[:, None] * inv_freq, col[:, None] * inv_freq], axis=-1)
    return jnp.cos(ang), jnp.sin(ang)


def apply_rope(x, cos, sin):
    shape = (1, x.shape[1]) + (1,) * (x.ndim - 3) + (cos.shape[-1],)
    c, s = cos.reshape(shape), sin.reshape(shape)
    x1, x2 = jnp.split(x, 2, axis=-1)
    return jnp.concatenate([x1 * c - x2 * s, x1 * s + x2 * c], axis=-1)


def short_conv(u, w, b):
    up = jnp.pad(u, ((0, 0), (1, 1), (0, 0)))
    return up[:, :-2] * w[0] + up[:, 1:-1] * w[1] + up[:, 2:] * w[2] + b


def hyena_filters(n_tok, w1, b1, w2, b2, w3, b3, freq):
    hp = lax.Precision.HIGHEST
    t = jnp.linspace(0.0, 1.0, n_tok, dtype=F32)[:, None]
    w = (2.0 * math.pi / n_tok) * jnp.arange(n_tok, dtype=F32)[:, None]
    f = jnp.linspace(1e-4, HY_BANDS - 1, HY_BANDS, dtype=F32)[None, :]
    feat = jnp.concatenate([t, jnp.cos(f * w), -jnp.sin(f * w)], axis=-1)
    hdn = jnp.sin(freq * (jnp.dot(feat, w1, precision=hp) + b1))
    hdn = jnp.sin(freq * (jnp.dot(hdn, w2, precision=hp) + b2))
    filt = (jnp.dot(hdn, w3, precision=hp) + b3).reshape(n_tok, HY_ORDER, 2, HY_CH)
    deltas = jnp.abs(jnp.linspace(math.log(HY_TARGET) / HY_SLOW, math.log(HY_TARGET) / HY_FAST, HY_CH,
                                  dtype=F32))
    filt = filt * jnp.exp(-t[:, :, None, None] * deltas)
    fwd, bwd = filt[:, :, 0], filt[:, :, 1]
    kfull = jnp.concatenate([fwd, jnp.zeros_like(fwd[:1]), bwd[1:][::-1]], axis=0)
    return kfull / jnp.sum(jnp.abs(kfull), axis=0, keepdims=True)


def long_conv(z, kf):
    n = z.shape[1]
    zf = jnp.fft.rfft(z, n=2 * n, axis=1)
    hf = jnp.fft.rfft(kf, n=2 * n, axis=0)
    return jnp.fft.irfft(zf * hf[None], n=2 * n, axis=1)[:, :n]


def hyena(u, conv_w, conv_b, filter_params, bias):
    n = u.shape[1]
    u = short_conv(u, conv_w, conv_b)
    v, x1, x2 = jnp.split(u, 3, axis=-1)
    kfull = hyena_filters(n, *filter_params)
    z = x1 * (long_conv(v, kfull[:, 0]) + bias[0] * v)
    return x2 * (long_conv(z, kfull[:, 1]) + bias[1] * z)


def _head_major(t):
    B, L, H, d = t.shape
    return jnp.transpose(t, (0, 2, 1, 3)).reshape(B * H, L, d)


def _head_major_t(t):
    B, L, H, d = t.shape
    return jnp.transpose(t, (0, 2, 3, 1)).reshape(B * H, d, L)


def _q_layout(t, G, R):
    B, L, _, d = t.shape
    return jnp.transpose(t.reshape(B, L, G, R, d), (0, 2, 3, 1, 4)).reshape(B * G, R, L, d)


def _from_q_layout(o, B, G, R):
    _, _, L, d = o.shape
    return jnp.transpose(o.reshape(B, G, R, L, d), (0, 3, 1, 2, 4)).reshape(B, L, G * R * d)


def _key_chunk(n_keys):
    for c in range(KEY_CHUNK_CAP, 0, -128):
        if n_keys % c == 0:
            return c
    raise ValueError(f"no key chunk for {n_keys}")


def kernel(x, c, ctx, c_ctx, w_mod, b_mod, g_norm_mix, w_in, g_qk_diff, lambda_diff, g_qk_win, sink_win, g_qk_glob, hy_conv_w, hy_conv_b, hy_w1, hy_b1, hy_w2, hy_b2, hy_w3, hy_b3, hy_freq, hy_bias, g_mix_out, w_out, g_norm_ffn, peer_wq, peer_keys, peer_u, peer_v):
    B, L, D = x.shape
    C = ctx.shape[1]
    depth = w_mod.shape[0]
    TM = 512
    n_lat = B * L
    n_ctx = B * C
    assert L % TM == 0 and n_ctx % TM == 0
    nseg = B + 1

    rope_half = axial_rope(L, DIFF_QK_DIM)
    rope_full = axial_rope(L, HEAD_DIM)
    sc = jnp.concatenate([jax.nn.silu(c), jax.nn.silu(c_ctx)[None]], axis=0)
    sc = jnp.pad(sc, ((0, 8 - nseg), (0, 0))).astype(BF16)

    xl = x.reshape(n_lat, D)
    xc = ctx.reshape(n_ctx, D)

    for i in range(depth):
        want_ctx = i < depth - 1
        lambda_init = 0.8 - 0.6 * math.exp(-0.3 * i)
        mod = mm(sc, w_mod[i].astype(BF16), 8, 1024)[:nseg] + b_mod[i]
        sh1, s1, g1, sh2, s2, g2 = jnp.split(mod, N_MOD, axis=-1)

        xall = jnp.concatenate([xl, xc], axis=0)
        p = normmod_mm(xall, g_norm_mix[i], sh1, s1, w_in[i].astype(BF16), L, TM, 640)
        pl_lat = p[:n_lat].reshape(B, L, -1)
        pc_ctx = p[n_lat:].reshape(B, C, -1)
        pa, pw, pg, ph = jnp.split(pl_lat, SPLITS, axis=-1)
        pac, pwc, pgc, phc = jnp.split(pc_ctx, SPLITS, axis=-1)

        def diff_heads(t):
            B_, L_, _ = t.shape
            q, k, v = jnp.split(t, 3, axis=-1)
            q = rmsnorm(q.reshape(B_, L_, DIFF_HEADS, 2, DIFF_QK_DIM), g_qk_diff[i, 0])
            k = rmsnorm(k.reshape(B_, L_, DIFF_HEADS, 2, DIFF_QK_DIM), g_qk_diff[i, 1])
            return q, k, v.reshape(B_, L_, DIFF_HEADS, HEAD_DIM)

        qa, ka, va = diff_heads(pa)
        qac, kac, vac = diff_heads(pac)
        qa, ka = apply_rope(qa, *rope_half), apply_rope(ka, *rope_half)
        lam_vec = lambda_diff[i]
        lam = (jnp.exp(jnp.sum(lam_vec[0] * lam_vec[1])) - jnp.exp(jnp.sum(lam_vec[2] * lam_vec[3]))
               + lambda_init)
        coefs = jnp.stack([jnp.ones((), F32), -lam])
        dscale = DIFF_QK_DIM ** -0.5 * LOG2E

        def diff_q(q):
            return jnp.stack([_head_major(q[..., b_, :] * dscale) for b_ in range(2)])[:, :, None].astype(BF16)

        def diff_k(k):
            return jnp.stack([_head_major(k[..., b_, :]) for b_ in range(2)]).astype(BF16)

        kka = jnp.concatenate([kac, ka], axis=1)
        vva = jnp.concatenate([vac, va], axis=1)
        oa = dense_attention(diff_q(qa), diff_k(kka), _head_major_t(vva).astype(BF16), coefs, None,
                             DIFF_HEADS, ATTN_TQ_DIFF, _key_chunk(C + L))
        oa = _from_q_layout(oa, B, DIFF_HEADS, 1)

        def gqa_heads(t, n_q, n_kv, g_qk):
            B_, L_, _ = t.shape
            q, k, v = jnp.split(t, [n_q * HEAD_DIM, (n_q + n_kv) * HEAD_DIM], axis=-1)
            q = rmsnorm(q.reshape(B_, L_, n_q, HEAD_DIM), g_qk[0])
            k = rmsnorm(k.reshape(B_, L_, n_kv, HEAD_DIM), g_qk[1])
            return q, k, v.reshape(B_, L_, n_kv, HEAD_DIM)

        hscale = HEAD_DIM ** -0.5
        hscale2 = hscale * LOG2E
        qw, kw, vw = gqa_heads(pw, WIN_HEADS, WIN_KV_HEADS, g_qk_win[i])
        qwc, kwc, vwc = gqa_heads(pwc, WIN_HEADS, WIN_KV_HEADS, g_qk_win[i])
        qw, kw = apply_rope(qw, *rope_full), apply_rope(kw, *rope_full)
        zblk = jnp.zeros((B, BLOCK, WIN_KV_HEADS, HEAD_DIM), F32)
        kpad = _head_major(jnp.concatenate([kwc, zblk, kw, zblk], axis=1)).astype(BF16)
        vpad = _head_major(jnp.concatenate([vwc, zblk, vw, zblk], axis=1)).astype(BF16)
        Rw = WIN_HEADS // WIN_KV_HEADS
        ob = window_attention(_q_layout(qw * hscale, WIN_KV_HEADS, Rw).astype(BF16), kpad, vpad,
                              sink_win[i], WIN_KV_HEADS, C, L)
        ob = _from_q_layout(ob, B, WIN_KV_HEADS, Rw)

        qg, kg, vg = gqa_heads(pg, GLOB_HEADS, GLOB_KV_HEADS, g_qk_glob[i])
        qgc, kgc, vgc = gqa_heads(pgc, GLOB_HEADS, GLOB_KV_HEADS, g_qk_glob[i])
        qg, kg = apply_rope(qg, *rope_full), apply_rope(kg, *rope_full)
        Rg = GLOB_HEADS // GLOB_KV_HEADS
        one = jnp.ones((1,), F32)
        kkg = _head_major(jnp.concatenate([kgc, kg], axis=1)).astype(BF16)[None]
        vvg = _head_major_t(jnp.concatenate([vgc, vg], axis=1)).astype(BF16)
        og = dense_attention(_q_layout(qg * hscale2, GLOB_KV_HEADS, Rg).astype(BF16)[None], kkg, vvg,
                             one, None, GLOB_KV_HEADS, ATTN_TQ_GLOB, _key_chunk(C + L))
        og = _from_q_layout(og, B, GLOB_KV_HEADS, Rg)

        filt = (hy_w1[i], hy_b1[i], hy_w2[i], hy_b2[i], hy_w3[i], hy_b3[i], hy_freq[i])
        oh = hyena(ph, hy_conv_w[i], hy_conv_b[i], filt, hy_bias[i])

        mixed = [jnp.concatenate([oa, ob, og, oh], axis=-1).reshape(n_lat, D)]
        if want_ctx:
            oac = dense_attention(diff_q(qac), diff_k(kac), _head_major_t(vac).astype(BF16), coefs, None,
                                  DIFF_HEADS, C, C)
            oac = _from_q_layout(oac, B, DIFF_HEADS, 1)
            obc = dense_attention(_q_layout(qwc * hscale2, WIN_KV_HEADS, Rw).astype(BF16)[None],
                                  _head_major(kwc).astype(BF16)[None], _head_major_t(vwc).astype(BF16),
                                  one, sink_win[i], WIN_KV_HEADS, C, C)
            obc = _from_q_layout(obc, B, WIN_KV_HEADS, Rw)
            ogc = dense_attention(_q_layout(qgc * hscale2, GLOB_KV_HEADS, Rg).astype(BF16)[None],
                                  _head_major(kgc).astype(BF16)[None], _head_major_t(vgc).astype(BF16),
                                  one, None, GLOB_KV_HEADS, C, C)
            ogc = _from_q_layout(ogc, B, GLOB_KV_HEADS, Rg)
            ohc = hyena(phc, hy_conv_w[i], hy_conv_b[i], filt, hy_bias[i])
            mixed.append(jnp.concatenate([oac, obc, ogc, ohc], axis=-1).reshape(n_ctx, D))
        o = jnp.concatenate(mixed, axis=0)
        n_rows = o.shape[0]

        oh_ = rmsnorm(o.reshape(n_rows, N_OUT_HEADS, HEAD_DIM), g_mix_out[i].reshape(N_OUT_HEADS, HEAD_DIM))
        head_scale = jnp.where(jnp.arange(N_OUT_HEADS) < DIFF_HEADS, 1.0 - lambda_init, 1.0)[:, None]
        om = (oh_ * head_scale.astype(F32)).reshape(n_rows, D)
        proj = mm(om.astype(BF16), w_out[i].astype(BF16), TM, 512)

        def seg_rows(v, n):
            lat = jnp.repeat(v[:B], L, axis=0)
            if n == n_lat:
                return lat
            return jnp.concatenate([lat, jnp.broadcast_to(v[B:], (n_ctx, v.shape[-1]))], axis=0)

        xcur = xall[:n_rows] + seg_rows(g1, n_rows) * proj

        qp = normmod_mm(xcur, g_norm_ffn[i], sh2, s2, peer_wq[i].astype(BF16), L, TM, 512)
        h2 = (rmsnorm(xcur, g_norm_ffn[i]) * (1.0 + seg_rows(s2, n_rows)) + seg_rows(sh2, n_rows)).astype(BF16)
        ff = peer(qp, h2, peer_keys[i], peer_u[i].astype(BF16), jnp.transpose(peer_v[i].astype(BF16)), TM)
        xcur = xcur + seg_rows(g2, n_rows) * ff
        xl = xcur[:n_lat]
        if want_ctx:
            xc = xcur[n_lat:]

    return xl.reshape(B, L, D)
```

```python
import functools
import math

import jax
import jax.numpy as jnp
import numpy as np
from jax import lax
from jax.experimental import pallas as pl
from jax.experimental.pallas import tpu as pltpu

F32 = jnp.float32
BF16 = jnp.bfloat16

GRID_W = 64
HEAD_DIM = 64
BLOCK = 128
WINDOW = 128
ROPE_THETA = 10000.0
EPS = 1e-6
NEG_INF = -1e30
N_MOD = 6
DIFF_HEADS = 4
DIFF_QK_DIM = 32
WIN_HEADS = 4
WIN_KV_HEADS = 2
GLOB_HEADS = 4
GLOB_KV_HEADS = 2
N_OUT_HEADS = 16
HY_CH = 256
HY_ORDER = 2
HY_BANDS = 16
HY_TARGET = 1e-2
HY_FAST = 0.3
HY_SLOW = 1.5
W_DIFF = 768
W_WIN = 512
W_GLOB = 512
SPLITS = (W_DIFF, W_DIFF + W_WIN, W_DIFF + W_WIN + W_GLOB)
PEER_HEADS = 8
N_KEYS = 128
PEER_TOPK = 16

VMEM_LIMIT = 56 * 1024 * 1024
ATTN_TQ_DIFF = 512
ATTN_TQ_GLOB = 256
KEY_CHUNK_CAP = 1408
HY_CH_BLOCK = 16


def _params(*sem):
    return pltpu.CompilerParams(dimension_semantics=sem, vmem_limit_bytes=VMEM_LIMIT)


def _mm_kernel(a_ref, b_ref, o_ref):
    o_ref[...] = jnp.dot(a_ref[...], b_ref[...], preferred_element_type=F32)


def mm(a, b, tm, tn):
    M, K = a.shape
    N = b.shape[1]
    return pl.pallas_call(
        _mm_kernel,
        grid=(M // tm, N // tn),
        in_specs=[pl.BlockSpec((tm, K), lambda i, j: (i, 0)),
                  pl.BlockSpec((K, tn), lambda i, j: (0, j))],
        out_specs=pl.BlockSpec((tm, tn), lambda i, j: (i, j)),
        out_shape=jax.ShapeDtypeStruct((M, N), F32),
        compiler_params=_params("parallel", "arbitrary"),
        name="mm",
    )(a, b)


def _normmod_mm_kernel(x_ref, g_ref, sh_ref, sc_ref, w_ref, o_ref, h_ref):
    @pl.when(pl.program_id(1) == 0)
    def _():
        x = x_ref[...]
        y = x * lax.rsqrt(jnp.mean(x * x, axis=-1, keepdims=True) + EPS) * g_ref[...]
        h_ref[...] = (y * (1.0 + sc_ref[0]) + sh_ref[0]).astype(BF16)

    o_ref[...] = jnp.dot(h_ref[...], w_ref[...], preferred_element_type=F32)


def normmod_mm(x, g, shift, scale, w, rows_per_seg, tm, tn):
    M, D = x.shape
    N = w.shape[1]
    nseg = shift.shape[0]
    blocks_per_seg = rows_per_seg // tm
    seg = lambda i, j: (jnp.minimum(i // blocks_per_seg, nseg - 1), 0, 0)
    return pl.pallas_call(
        _normmod_mm_kernel,
        grid=(M // tm, N // tn),
        in_specs=[pl.BlockSpec((tm, D), lambda i, j: (i, 0)),
                  pl.BlockSpec((1, D), lambda i, j: (0, 0)),
                  pl.BlockSpec((1, 1, D), seg),
                  pl.BlockSpec((1, 1, D), seg),
                  pl.BlockSpec((D, tn), lambda i, j: (0, j))],
        out_specs=pl.BlockSpec((tm, tn), lambda i, j: (i, j)),
        out_shape=jax.ShapeDtypeStruct((M, N), F32),
        scratch_shapes=[pltpu.VMEM((tm, D), BF16)],
        compiler_params=_params("parallel", "arbitrary"),
        name="normmod_mm",
    )(x, g.reshape(1, D), shift.reshape(nseg, 1, D), scale.reshape(nseg, 1, D), w)


def _nt_kernel(a_ref, b_ref, o_ref):
    o_ref[0] = lax.dot_general(a_ref[0], b_ref[...], (((1,), (1,)), ((), ())),
                               preferred_element_type=F32)


def keys_times_qT(keys, q, tn):
    P, n, d = keys.shape
    T = q.shape[0]
    return pl.pallas_call(
        _nt_kernel,
        grid=(P, T // tn),
        in_specs=[pl.BlockSpec((1, n, d), lambda p, j: (p, 0, 0)),
                  pl.BlockSpec((tn, d), lambda p, j: (j, p))],
        out_specs=pl.BlockSpec((1, n, tn), lambda p, j: (p, 0, j)),
        out_shape=jax.ShapeDtypeStruct((P, n, T), F32),
        compiler_params=_params("parallel", "arbitrary"),
        name="peer_scores_t",
    )(keys, q)


LOG2E = math.log2(math.e)


def _dense_attn_kernel(coef_ref, sink_ref, q_ref, k_ref, vt_ref, o_ref, sa_ref, sb_ref, *, nbr, R, G,
                       tq, tk, nk, has_sink):
    cols = R * tq
    d = q_ref.shape[-1]
    dv = vt_ref.shape[1]
    g = pl.program_id(0) % G
    nt = (((1,), (1,)), ((), ()))
    bufs = (sa_ref, sb_ref)
    out = jnp.zeros((dv, cols), F32)
    for br in range(nbr):
        q = q_ref[br, 0].reshape(cols, d)
        if has_sink:
            m = jnp.concatenate([jnp.full((1, tq), sink_ref[g * R + r] * LOG2E, F32)
                                 for r in range(R)], axis=1)
            l = jnp.ones((1, cols), F32)
        else:
            m = jnp.full((1, cols), NEG_INF, F32)
            l = jnp.zeros((1, cols), F32)
        acc = jnp.zeros((dv, cols), F32)
        bufs[0][...] = lax.dot_general(k_ref[br, 0, 0:tk, :], q, nt, preferred_element_type=F32)
        for j in range(nk):
            if j + 1 < nk:
                bufs[(j + 1) % 2][...] = lax.dot_general(k_ref[br, 0, (j + 1) * tk:(j + 2) * tk, :], q, nt,
                                                         preferred_element_type=F32)
            s = bufs[j % 2][...]
            m_new = jnp.maximum(m, jnp.max(s, axis=0, keepdims=True))
            alpha = jnp.exp2(m - m_new)
            p = jnp.exp2(s - m_new)
            l = alpha * l + jnp.sum(p, axis=0, keepdims=True)
            acc = alpha * acc + jnp.dot(vt_ref[0, :, j * tk:(j + 1) * tk], p.astype(BF16),
                                        preferred_element_type=F32)
            m = m_new
        out = out + coef_ref[br] * (acc / l)
    o_ref[0] = out.T.reshape(R, tq, dv)


def dense_attention(q, k, vt, coefs, sink, G, tq, tk):
    nbr, BG, R, Lq, d = q.shape
    Lk = k.shape[2]
    dv = vt.shape[1]
    has_sink = sink is not None
    if sink is None:
        sink = jnp.zeros((1,), F32)
    kern = functools.partial(_dense_attn_kernel, nbr=nbr, R=R, G=G, tq=tq, tk=tk, nk=Lk // tk,
                             has_sink=has_sink)
    return pl.pallas_call(
        kern,
        grid=(BG, Lq // tq),
        in_specs=[pl.BlockSpec(memory_space=pltpu.SMEM),
                  pl.BlockSpec(memory_space=pltpu.SMEM),
                  pl.BlockSpec((nbr, 1, R, tq, d), lambda b, i: (0, b, 0, i, 0)),
                  pl.BlockSpec((nbr, 1, Lk, d), lambda b, i: (0, b, 0, 0)),
                  pl.BlockSpec((1, dv, Lk), lambda b, i: (b, 0, 0))],
        out_specs=pl.BlockSpec((1, R, tq, dv), lambda b, i: (b, 0, i, 0)),
        out_shape=jax.ShapeDtypeStruct((BG, R, Lq, dv), F32),
        scratch_shapes=[pltpu.VMEM((tk, R * tq), F32), pltpu.VMEM((tk, R * tq), F32)],
        compiler_params=_params("parallel", "arbitrary"),
        name="dense_attn",
    )(coefs.astype(F32), sink.astype(F32), q, k, vt)


def _window_attn_kernel(sink_ref, q_ref, k_ref, v_ref, o_ref, *, R, G, C, L):
    rows = R * BLOCK
    d = q_ref.shape[-1]
    g = pl.program_id(0) % G
    n = pl.program_id(1)
    q = q_ref[0].reshape(rows, d)
    start = pl.multiple_of(C + n * BLOCK, BLOCK)
    kc = k_ref[0, 0:C, :]
    vc = v_ref[0, 0:C, :]
    kl = k_ref[0, pl.ds(start, 3 * BLOCK), :]
    vl = v_ref[0, pl.ds(start, 3 * BLOCK), :]
    nt = (((1,), (1,)), ((), ()))
    s_ctx = lax.dot_general(q, kc, nt, preferred_element_type=F32)
    s_loc = lax.dot_general(q, kl, nt, preferred_element_type=F32)
    qi = lax.broadcasted_iota(jnp.int32, (rows, 3 * BLOCK), 0) & (BLOCK - 1)
    kj = lax.broadcasted_iota(jnp.int32, (rows, 3 * BLOCK), 1)
    kpos = (n - 1) * BLOCK + kj
    valid = (jnp.abs(kj - BLOCK - qi) <= WINDOW) & (kpos >= 0) & (kpos < L)
    s_loc = jnp.where(valid, s_loc, NEG_INF)
    sink = jnp.concatenate([jnp.full((BLOCK, 1), sink_ref[g * R + r], F32) for r in range(R)], axis=0)
    m = jnp.maximum(jnp.maximum(jnp.max(s_ctx, axis=-1, keepdims=True),
                                jnp.max(s_loc, axis=-1, keepdims=True)), sink)
    e_ctx = jnp.exp(s_ctx - m)
    e_loc = jnp.exp(s_loc - m)
    den = (jnp.sum(e_ctx, axis=-1, keepdims=True) + jnp.sum(e_loc, axis=-1, keepdims=True)
           + jnp.exp(sink - m))
    inv = 1.0 / den
    o = (jnp.dot((e_ctx * inv).astype(BF16), vc, preferred_element_type=F32)
         + jnp.dot((e_loc * inv).astype(BF16), vl, preferred_element_type=F32))
    o_ref[0] = o.reshape(R, BLOCK, d)


def window_attention(q, kpad, vpad, sink, G, C, L):
    BG, R, _, d = q.shape
    Lp = kpad.shape[1]
    kern = functools.partial(_window_attn_kernel, R=R, G=G, C=C, L=L)
    return pl.pallas_call(
        kern,
        grid=(BG, L // BLOCK),
        in_specs=[pl.BlockSpec(memory_space=pltpu.SMEM),
                  pl.BlockSpec((1, R, BLOCK, d), lambda b, i: (b, 0, i, 0)),
                  pl.BlockSpec((1, Lp, d), lambda b, i: (b, 0, 0)),
                  pl.BlockSpec((1, Lp, d), lambda b, i: (b, 0, 0))],
        out_specs=pl.BlockSpec((1, R, BLOCK, d), lambda b, i: (b, 0, i, 0)),
        out_shape=jax.ShapeDtypeStruct((BG, R, L, d), F32),
        compiler_params=_params("parallel", "arbitrary"),
        name="window_attn",
    )(sink.astype(F32), q, kpad, vpad)


_CAND_PAIRS = tuple((r, s) for r in range(PEER_TOPK) for s in range(PEER_TOPK)
                    if (r + 1) * (s + 1) <= PEER_TOPK)
_CAND_ROWS = -(-len(_CAND_PAIRS) // 8) * 8


def _top_rows(v, n):
    iota = lax.broadcasted_iota(jnp.int32, v.shape, 0)
    rows = []
    for r in range(n):
        m = jnp.max(v, axis=0, keepdims=True)
        rows.append(m)
        if r + 1 < n:
            first = jnp.min(jnp.where(v == m, iota, v.shape[0]), axis=0, keepdims=True)
            v = jnp.where(iota == first, -jnp.inf, v)
    return rows


def _peer_select_kernel(s_ref, ea_ref, eb_ref, th_ref, c_ref, *, tm):
    n_lane = tm // 128

    def body(it, carry):
        h = it // n_lane
        lanes = pl.ds(pl.multiple_of((it % n_lane) * 128, 128), 128)
        a = s_ref[h, 0, :, lanes]
        b = s_ref[h, 1, :, lanes]
        ta = _top_rows(a, PEER_TOPK)
        tb = _top_rows(b, PEER_TOPK)
        c_ref[...] = jnp.full(c_ref.shape, -jnp.inf, F32)
        for k, (r, s) in enumerate(_CAND_PAIRS):
            c_ref[k:k + 1, :] = ta[r] + tb[s]
        best = _top_rows(c_ref[...], PEER_TOPK)
        tau = best[PEER_TOPK - 1]
        zsum = jnp.zeros_like(best[0])
        for bk in best:
            zsum = zsum + jnp.exp(bk - best[0])
        theta = jnp.full(a.shape, jnp.inf, F32)
        for r in reversed(range(PEER_TOPK)):
            th_r = jnp.full(tau.shape, jnp.inf, F32)
            for s in range(PEER_TOPK // (r + 1)):
                th_r = jnp.minimum(th_r, jnp.where(ta[r] + tb[s] >= tau, tb[s], jnp.inf))
            theta = jnp.where(a == ta[r], th_r, theta)
        th_ref[h, :, lanes] = theta
        ea_ref[h, :, lanes] = jnp.exp(a - ta[0]) / zsum
        eb_ref[h, :, lanes] = jnp.exp(b - tb[0])
        return carry

    lax.fori_loop(0, PEER_HEADS * n_lane, body, 0)


def peer_select(st, tm):
    H, _, n, T = st.shape
    kern = functools.partial(_peer_select_kernel, tm=tm)
    return pl.pallas_call(
        kern,
        grid=(T // tm,),
        in_specs=[pl.BlockSpec((H, 2, n, tm), lambda t: (0, 0, 0, t))],
        out_specs=[pl.BlockSpec((H, n, tm), lambda t: (0, 0, t))] * 3,
        out_shape=[jax.ShapeDtypeStruct((H, n, T), F32)] * 3,
        scratch_shapes=[pltpu.VMEM((_CAND_ROWS, 128), F32)],
        compiler_params=_params("parallel"),
        name="peer_select",
    )(st)


def _peer_kernel(x_ref, u_ref, vt_ref, s2_ref, th_ref, ea_ref, eb_ref, o_ref, acc_ref, act_ref, g_ref,
                 *, ni, tm):
    c = pl.program_id(1)

    @pl.when(c == 0)
    def _():
        acc_ref[...] = jnp.zeros_like(acc_ref)

    act_ref[...] = jax.nn.gelu(lax.dot_general(u_ref[...], x_ref[...], (((1,), (1,)), ((), ())),
                                               preferred_element_type=F32))

    for ts in range(tm // 128):
        lanes = slice(ts * 128, (ts + 1) * 128)
        for ii in range(ni):
            rows = slice(ii * N_KEYS, (ii + 1) * N_KEYS)
            w = jnp.zeros((N_KEYS, 128), F32)
            for h in range(PEER_HEADS):
                sel = s2_ref[h, :, lanes] >= th_ref[h, ii:ii + 1, lanes]
                w = w + jnp.where(sel, ea_ref[h, ii:ii + 1, lanes] * eb_ref[h, :, lanes], 0.0)
            g_ref[rows, lanes] = (w * act_ref[rows, lanes]).astype(BF16)

    acc_ref[...] += jnp.dot(vt_ref[...], g_ref[...], preferred_element_type=F32)

    @pl.when(c == pl.num_programs(1) - 1)
    def _():
        o_ref[...] = acc_ref[...].T


def peer_dense(x, u, vt, st, tht, eat, ebt, tm, ni):
    T, D = x.shape
    E = u.shape[0]
    ec = ni * N_KEYS
    kern = functools.partial(_peer_kernel, ni=ni, tm=tm)
    rows_of_chunk = pl.BlockSpec((PEER_HEADS, ni, tm), lambda t, c: (0, c, t))
    return pl.pallas_call(
        kern,
        grid=(T // tm, E // ec),
        in_specs=[pl.BlockSpec((tm, D), lambda t, c: (t, 0)),
                  pl.BlockSpec((ec, D), lambda t, c: (c, 0)),
                  pl.BlockSpec((D, ec), lambda t, c: (0, c)),
                  pl.BlockSpec((PEER_HEADS, None, N_KEYS, tm), lambda t, c: (0, 1, 0, t)),
                  rows_of_chunk,
                  rows_of_chunk,
                  pl.BlockSpec((PEER_HEADS, N_KEYS, tm), lambda t, c: (0, 0, t))],
        out_specs=pl.BlockSpec((tm, D), lambda t, c: (t, 0)),
        out_shape=jax.ShapeDtypeStruct((T, D), F32),
        scratch_shapes=[pltpu.VMEM((D, tm), F32),
                        pltpu.VMEM((ec, tm), F32),
                        pltpu.VMEM((ec, tm), BF16)],
        compiler_params=_params("parallel", "arbitrary"),
        name="peer_dense",
    )(x, u, vt, st, tht, eat, ebt)


def peer(qp, h_bf, keys, u_bf, vt_bf, tm):
    T = qp.shape[0]
    dk = keys.shape[-1]
    kflat = keys.reshape(PEER_HEADS * 2, N_KEYS, dk).astype(BF16)
    st = keys_times_qT(kflat, qp.astype(BF16), tm).reshape(PEER_HEADS, 2, N_KEYS, T)
    eat, ebt, tht = peer_select(st, 256)
    return peer_dense(h_bf, u_bf, vt_bf, st, tht, eat, ebt, tm, 8)


def rmsnorm(x, g):
    return x * lax.rsqrt(jnp.mean(x * x, axis=-1, keepdims=True) + EPS) * g


def axial_rope(n_tok, dim):
    rows = n_tok // GRID_W
    row = jnp.repeat(jnp.arange(rows, dtype=F32), GRID_W)
    col = jnp.tile(jnp.arange(GRID_W, dtype=F32), rows)
    axis_dim = dim // 2
    inv_freq = ROPE_THETA ** (-jnp.arange(0, axis_dim, 2, dtype=F32) / axis_dim)
    ang = jnp.concatenate([row[:, None] * inv_freq, col[:, None] * inv_freq], axis=-1)
    return jnp.cos(ang), jnp.sin(ang)


def apply_rope(x, cos, sin):
    shape = (1, x.shape[1]) + (1,) * (x.ndim - 3) + (cos.shape[-1],)
    c, s = cos.reshape(shape), sin.reshape(shape)
    x1, x2 = jnp.split(x, 2, axis=-1)
    return jnp.concatenate([x1 * c - x2 * s, x1 * s + x2 * c], axis=-1)


def short_conv(u, w, b):
    up = jnp.pad(u, ((0, 0), (1, 1), (0, 0)))
    return up[:, :-2] * w[0] + up[:, 1:-1] * w[1] + up[:, 2:] * w[2] + b


def hyena_filters(n_tok, w1, b1, w2, b2, w3, b3, freq):
    hp = lax.Precision.HIGHEST
    t = jnp.linspace(0.0, 1.0, n_tok, dtype=F32)[:, None]
    w = (2.0 * math.pi / n_tok) * jnp.arange(n_tok, dtype=F32)[:, None]
    f = jnp.linspace(1e-4, HY_BANDS - 1, HY_BANDS, dtype=F32)[None, :]
    feat = jnp.concatenate([t, jnp.cos(f * w), -jnp.sin(f * w)], axis=-1)
    hdn = jnp.sin(freq * (jnp.dot(feat, w1, precision=hp) + b1))
    hdn = jnp.sin(freq * (jnp.dot(hdn, w2, precision=hp) + b2))
    filt = (jnp.dot(hdn, w3, precision=hp) + b3).reshape(n_tok, HY_ORDER, 2, HY_CH)
    deltas = jnp.abs(jnp.linspace(math.log(HY_TARGET) / HY_SLOW, math.log(HY_TARGET) / HY_FAST, HY_CH,
                                  dtype=F32))
    filt = filt * jnp.exp(-t[:, :, None, None] * deltas)
    fwd, bwd = filt[:, :, 0], filt[:, :, 1]
    kfull = jnp.concatenate([fwd, jnp.zeros_like(fwd[:1]), bwd[1:][::-1]], axis=0)
    return kfull / jnp.sum(jnp.abs(kfull), axis=0, keepdims=True)


def long_conv(z, kf):
    n = z.shape[1]
    zf = jnp.fft.rfft(z, n=2 * n, axis=1)
    hf = jnp.fft.rfft(kf, n=2 * n, axis=0)
    return jnp.fft.irfft(zf * hf[None], n=2 * n, axis=1)[:, :n]


def hyena_small(u, conv_w, conv_b, filter_params, bias):
    n = u.shape[1]
    u = short_conv(u, conv_w, conv_b)
    v, x1, x2 = jnp.split(u, 3, axis=-1)
    kfull = hyena_filters(n, *filter_params)
    z = x1 * (long_conv(v, kfull[:, 0]) + bias[0] * v)
    return x2 * (long_conv(z, kfull[:, 1]) + bias[1] * z)


def _hi_lo(x):
    hi = x.astype(BF16)
    return hi, (x - hi.astype(F32)).astype(BF16)


def _dot3(a, b):
    d = lambda x, y: jnp.dot(x, y, preferred_element_type=F32)
    return d(a[0], b[0]) + d(a[1], b[0]) + d(a[0], b[1])


def _dft_constants(n_tok):
    n = 2 * n_tok
    n1 = n // 128
    a1 = 2.0 * np.pi * np.outer(np.arange(n1), np.arange(n1)) / n1
    a2 = 2.0 * np.pi * np.outer(np.arange(128), np.arange(128)) / 128
    at = 2.0 * np.pi * np.outer(np.arange(n1), np.arange(128)) / n
    c1, s1, c2, s2 = np.cos(a1), np.sin(a1), np.cos(a2), np.sin(a2)
    pair = lambda m: _hi_lo(jnp.asarray(m, F32))
    return dict(
        f1_half=pair(np.concatenate([c1[:, :n1 // 2], -s1[:, :n1 // 2]], axis=0)),
        f1_full=pair(np.concatenate([c1, -s1], axis=0)),
        m_fwd=pair(np.block([[c2, -s2], [s2, c2]])),
        m_inv=pair(np.block([[c2, s2], [-s2, c2]])),
        g_half=pair(np.concatenate([c1[:n1 // 2], -s1[:n1 // 2]], axis=1)),
        tr=jnp.asarray(np.cos(at), F32), ti=jnp.asarray(-np.sin(at), F32))


def _dft_fwd(seqs, f1, tr, ti, m_fwd):
    n1 = tr.shape[0]
    y = _dot3(f1, _hi_lo(jnp.concatenate(seqs, axis=1)))
    rows = []
    for k in range(len(seqs)):
        yr, yi = y[:n1, k * 128:(k + 1) * 128], y[n1:, k * 128:(k + 1) * 128]
        rows.append(jnp.concatenate([yr * tr - yi * ti, yr * ti + yi * tr], axis=1))
    return _dot3(_hi_lo(jnp.concatenate(rows, axis=0)), m_fwd)


def _dft_inv_half(p, g_half, tr, ti, m_inv):
    n1 = tr.shape[0]
    u = _dot3(_hi_lo(p), m_inv)
    cols = []
    for k in range(p.shape[0] // n1):
        ur, ui = u[k * n1:(k + 1) * n1, :128], u[k * n1:(k + 1) * n1, 128:]
        cols.append(jnp.concatenate([ur * tr + ui * ti, ui * tr - ur * ti], axis=0))
    return _dot3(g_half, _hi_lo(jnp.concatenate(cols, axis=1))) * (1.0 / (n1 * 128))


def _pairs(refs):
    return (refs[0][...], refs[1][...])


def _spectrum_kernel(a_ref, f1h, f1l, tr_ref, ti_ref, mh, ml, o_ref):
    cb, n1 = a_ref.shape[0], tr_ref.shape[0]
    x = _dft_fwd([a_ref[k] for k in range(cb)], _pairs((f1h, f1l)), tr_ref[...], ti_ref[...],
                 _pairs((mh, ml)))
    o_ref[...] = x.reshape(cb, n1, 256)


def filter_spectrum(kf, consts, cb):
    items, n1, _ = kf.shape
    full = lambda shape: pl.BlockSpec(shape, lambda i: (0,) * len(shape))
    f1, m = consts["f1_full"], consts["m_fwd"]
    return pl.pallas_call(
        _spectrum_kernel,
        grid=(items // cb,),
        in_specs=[pl.BlockSpec((cb, n1, 128), lambda i: (i, 0, 0)),
                  full(f1[0].shape), full(f1[1].shape), full((n1, 128)), full((n1, 128)),
                  full(m[0].shape), full(m[1].shape)],
        out_specs=pl.BlockSpec((cb, n1, 256), lambda i: (i, 0, 0)),
        out_shape=jax.ShapeDtypeStruct((items, n1, 256), F32),
        compiler_params=_params("parallel"),
        name="filter_spectrum",
    )(kf, f1[0], f1[1], consts["tr"], consts["ti"], m[0], m[1])


def _conv_gate_kernel(bias_ref, u_ref, g_ref, h_ref, f1h, f1l, tr_ref, ti_ref, mfh, mfl, mih, mil, gh, gl,
                      o_ref, *, cb):
    tr, ti = tr_ref[...], ti_ref[...]
    n1 = tr.shape[0]
    c0 = pl.program_id(1) * cb
    u = [u_ref[0, k] for k in range(cb)]
    x = _dft_fwd(u, _pairs((f1h, f1l)), tr, ti, _pairs((mfh, mfl)))
    prod = []
    for k in range(cb):
        xr, xi = x[k * n1:(k + 1) * n1, :128], x[k * n1:(k + 1) * n1, 128:]
        hr, hi = h_ref[k, :, :128], h_ref[k, :, 128:]
        prod.append(jnp.concatenate([xr * hr - xi * hi, xr * hi + xi * hr], axis=1))
    y = _dft_inv_half(jnp.concatenate(prod, axis=0), _pairs((gh, gl)), tr, ti, _pairs((mih, mil)))
    for k in range(cb):
        o_ref[0, k] = g_ref[0, k] * (y[:, k * 128:(k + 1) * 128] + bias_ref[c0 + k] * u[k])


def conv_gate(u, gate, spec, bias, consts, cb):
    B, C, half, _ = u.shape
    n1 = 2 * half
    full = lambda shape: pl.BlockSpec(shape, lambda b, c: (0,) * len(shape))
    seq = pl.BlockSpec((1, cb, half, 128), lambda b, c: (b, c, 0, 0))
    mats = [*consts["f1_half"], consts["tr"], consts["ti"], *consts["m_fwd"], *consts["m_inv"],
            *consts["g_half"]]
    return pl.pallas_call(
        functools.partial(_conv_gate_kernel, cb=cb),
        grid=(B, C // cb),
        in_specs=[pl.BlockSpec(memory_space=pltpu.SMEM), seq, seq,
                  pl.BlockSpec((cb, n1, 256), lambda b, c: (c, 0, 0))] + [full(m.shape) for m in mats],
        out_specs=seq,
        out_shape=jax.ShapeDtypeStruct(u.shape, F32),
        compiler_params=_params("parallel", "arbitrary"),
        name="conv_gate",
    )(bias.astype(F32), u, gate, spec, *mats)


def hyena(u, conv_w, conv_b, filter_params, bias):
    B, n, _ = u.shape
    half = n // 128
    consts = _dft_constants(n)
    u = short_conv(u, conv_w, conv_b)
    seqs = jnp.transpose(u, (0, 2, 1)).reshape(B, 3, HY_CH, half, 128)
    v, x1, x2 = seqs[:, 0], seqs[:, 1], seqs[:, 2]
    kfull = hyena_filters(n, *filter_params)
    kf = jnp.transpose(kfull, (1, 2, 0)).reshape(HY_ORDER * HY_CH, 2 * half, 128)
    spec = filter_spectrum(kf, consts, HY_CH_BLOCK).reshape(HY_ORDER, HY_CH, 2 * half, 256)
    z = conv_gate(v, x1, spec[0], bias[0], consts, HY_CH_BLOCK)
    o = conv_gate(z, x2, spec[1], bias[1], consts, HY_CH_BLOCK)
    return jnp.transpose(o.reshape(B, HY_CH, n), (0, 2, 1))


def _head_major(t):
    B, L, H, d = t.shape
    return jnp.transpose(t, (0, 2, 1, 3)).reshape(B * H, L, d)


def _head_major_t(t):
    B, L, H, d = t.shape
    return jnp.transpose(t, (0, 2, 3, 1)).reshape(B * H, d, L)


def _q_layout(t, G, R):
    B, L, _, d = t.shape
    return jnp.transpose(t.reshape(B, L, G, R, d), (0, 2, 3, 1, 4)).reshape(B * G, R, L, d)


def _from_q_layout(o, B, G, R):
    _, _, L, d = o.shape
    return jnp.transpose(o.reshape(B, G, R, L, d), (0, 3, 1, 2, 4)).reshape(B, L, G * R * d)


def _key_chunk(n_keys):
    for c in range(KEY_CHUNK_CAP, 0, -128):
        if n_keys % c == 0:
            return c
    raise ValueError(f"no key chunk for {n_keys}")


def kernel(x, c, ctx, c_ctx, w_mod, b_mod, g_norm_mix, w_in, g_qk_diff, lambda_diff, g_qk_win, sink_win, g_qk_glob, hy_conv_w, hy_conv_b, hy_w1, hy_b1, hy_w2, hy_b2, hy_w3, hy_b3, hy_freq, hy_bias, g_mix_out, w_out, g_norm_ffn, peer_wq, peer_keys, peer_u, peer_v):
    B, L, D = x.shape
    C = ctx.shape[1]
    depth = w_mod.shape[0]
    TM = 512
    n_lat = B * L
    n_ctx = B * C
    assert L % TM == 0 and n_ctx % TM == 0
    nseg = B + 1

    rope_half = axial_rope(L, DIFF_QK_DIM)
    rope_full = axial_rope(L, HEAD_DIM)
    sc = jnp.concatenate([jax.nn.silu(c), jax.nn.silu(c_ctx)[None]], axis=0)
    sc = jnp.pad(sc, ((0, 8 - nseg), (0, 0))).astype(BF16)

    xl = x.reshape(n_lat, D)
    xc = ctx.reshape(n_ctx, D)

    for i in range(depth):
        want_ctx = i < depth - 1
        lambda_init = 0.8 - 0.6 * math.exp(-0.3 * i)
        mod = mm(sc, w_mod[i].astype(BF16), 8, 1024)[:nseg] + b_mod[i]
        sh1, s1, g1, sh2, s2, g2 = jnp.split(mod, N_MOD, axis=-1)

        xall = jnp.concatenate([xl, xc], axis=0)
        p = normmod_mm(xall, g_norm_mix[i], sh1, s1, w_in[i].astype(BF16), L, TM, 640)
        pl_lat = p[:n_lat].reshape(B, L, -1)
        pc_ctx = p[n_lat:].reshape(B, C, -1)
        pa, pw, pg, ph = jnp.split(pl_lat, SPLITS, axis=-1)
        pac, pwc, pgc, phc = jnp.split(pc_ctx, SPLITS, axis=-1)

        def diff_heads(t):
            B_, L_, _ = t.shape
            q, k, v = jnp.split(t, 3, axis=-1)
            q = rmsnorm(q.reshape(B_, L_, DIFF_HEADS, 2, DIFF_QK_DIM), g_qk_diff[i, 0])
            k = rmsnorm(k.reshape(B_, L_, DIFF_HEADS, 2, DIFF_QK_DIM), g_qk_diff[i, 1])
            return q, k, v.reshape(B_, L_, DIFF_HEADS, HEAD_DIM)

        qa, ka, va = diff_heads(pa)
        qac, kac, vac = diff_heads(pac)
        qa, ka = apply_rope(qa, *rope_half), apply_rope(ka, *rope_half)
        lam_vec = lambda_diff[i]
        lam = (jnp.exp(jnp.sum(lam_vec[0] * lam_vec[1])) - jnp.exp(jnp.sum(lam_vec[2] * lam_vec[3]))
               + lambda_init)
        coefs = jnp.stack([jnp.ones((), F32), -lam])
        dscale = DIFF_QK_DIM ** -0.5 * LOG2E

        def diff_q(q):
            return jnp.stack([_head_major(q[..., b_, :] * dscale) for b_ in range(2)])[:, :, None].astype(BF16)

        def diff_k(k):
            return jnp.stack([_head_major(k[..., b_, :]) for b_ in range(2)]).astype(BF16)

        kka = jnp.concatenate([kac, ka], axis=1)
        vva = jnp.concatenate([vac, va], axis=1)
        oa = dense_attention(diff_q(qa), diff_k(kka), _head_major_t(vva).astype(BF16), coefs, None,
                             DIFF_HEADS, ATTN_TQ_DIFF, _key_chunk(C + L))
        oa = _from_q_layout(oa, B, DIFF_HEADS, 1)

        def gqa_heads(t, n_q, n_kv, g_qk):
            B_, L_, _ = t.shape
            q, k, v = jnp.split(t, [n_q * HEAD_DIM, (n_q + n_kv) * HEAD_DIM], axis=-1)
            q = rmsnorm(q.reshape(B_, L_, n_q, HEAD_DIM), g_qk[0])
            k = rmsnorm(k.reshape(B_, L_, n_kv, HEAD_DIM), g_qk[1])
            return q, k, v.reshape(B_, L_, n_kv, HEAD_DIM)

        hscale = HEAD_DIM ** -0.5
        hscale2 = hscale * LOG2E
        qw, kw, vw = gqa_heads(pw, WIN_HEADS, WIN_KV_HEADS, g_qk_win[i])
        qwc, kwc, vwc = gqa_heads(pwc, WIN_HEADS, WIN_KV_HEADS, g_qk_win[i])
        qw, kw = apply_rope(qw, *rope_full), apply_rope(kw, *rope_full)
        zblk = jnp.zeros((B, BLOCK, WIN_KV_HEADS, HEAD_DIM), F32)
        kpad = _head_major(jnp.concatenate([kwc, zblk, kw, zblk], axis=1)).astype(BF16)
        vpad = _head_major(jnp.concatenate([vwc, zblk, vw, zblk], axis=1)).astype(BF16)
        Rw = WIN_HEADS // WIN_KV_HEADS
        ob = window_attention(_q_layout(qw * hscale, WIN_KV_HEADS, Rw).astype(BF16), kpad, vpad,
                              sink_win[i], WIN_KV_HEADS, C, L)
        ob = _from_q_layout(ob, B, WIN_KV_HEADS, Rw)

        qg, kg, vg = gqa_heads(pg, GLOB_HEADS, GLOB_KV_HEADS, g_qk_glob[i])
        qgc, kgc, vgc = gqa_heads(pgc, GLOB_HEADS, GLOB_KV_HEADS, g_qk_glob[i])
        qg, kg = apply_rope(qg, *rope_full), apply_rope(kg, *rope_full)
        Rg = GLOB_HEADS // GLOB_KV_HEADS
        one = jnp.ones((1,), F32)
        kkg = _head_major(jnp.concatenate([kgc, kg], axis=1)).astype(BF16)[None]
        vvg = _head_major_t(jnp.concatenate([vgc, vg], axis=1)).astype(BF16)
        og = dense_attention(_q_layout(qg * hscale2, GLOB_KV_HEADS, Rg).astype(BF16)[None], kkg, vvg,
                             one, None, GLOB_KV_HEADS, ATTN_TQ_GLOB, _key_chunk(C + L))
        og = _from_q_layout(og, B, GLOB_KV_HEADS, Rg)

        filt = (hy_w1[i], hy_b1[i], hy_w2[i], hy_b2[i], hy_w3[i], hy_b3[i], hy_freq[i])
        oh = hyena(ph, hy_conv_w[i], hy_conv_b[i], filt, hy_bias[i])

        mixed = [jnp.concatenate([oa, ob, og, oh], axis=-1).reshape(n_lat, D)]
        if want_ctx:
            oac = dense_attention(diff_q(qac), diff_k(kac), _head_major_t(vac).astype(BF16), coefs, None,
                                  DIFF_HEADS, C, C)
            oac = _from_q_layout(oac, B, DIFF_HEADS, 1)
            obc = dense_attention(_q_layout(qwc * hscale2, WIN_KV_HEADS, Rw).astype(BF16)[None],
                                  _head_major(kwc).astype(BF16)[None], _head_major_t(vwc).astype(BF16),
                                  one, sink_win[i], WIN_KV_HEADS, C, C)
            obc = _from_q_layout(obc, B, WIN_KV_HEADS, Rw)
            ogc = dense_attention(_q_layout(qgc * hscale2, GLOB_KV_HEADS, Rg).astype(BF16)[None],
                                  _head_major(kgc).astype(BF16)[None], _head_major_t(vgc).astype(BF16),
                                  one, None, GLOB_KV_HEADS, C, C)
            ogc = _from_q_layout(ogc, B, GLOB_KV_HEADS, Rg)
            ohc = hyena_small(phc, hy_conv_w[i], hy_conv_b[i], filt, hy_bias[i])
            mixed.append(jnp.concatenate([oac, obc, ogc, ohc], axis=-1).reshape(n_ctx, D))
        o = jnp.concatenate(mixed, axis=0)
        n_rows = o.shape[0]

        oh_ = rmsnorm(o.reshape(n_rows, N_OUT_HEADS, HEAD_DIM), g_mix_out[i].reshape(N_OUT_HEADS, HEAD_DIM))
        head_scale = jnp.where(jnp.arange(N_OUT_HEADS) < DIFF_HEADS, 1.0 - lambda_init, 1.0)[:, None]
        om = (oh_ * head_scale.astype(F32)).reshape(n_rows, D)
        proj = mm(om.astype(BF16), w_out[i].astype(BF16), TM, 512)

        def seg_rows(v, n):
            lat = jnp.repeat(v[:B], L, axis=0)
            if n == n_lat:
                return lat
            return jnp.concatenate([lat, jnp.broadcast_to(v[B:], (n_ctx, v.shape[-1]))], axis=0)

        xcur = xall[:n_rows] + seg_rows(g1, n_rows) * proj

        qp = normmod_mm(xcur, g_norm_ffn[i], sh2, s2, peer_wq[i].astype(BF16), L, TM, 512)
        h2 = (rmsnorm(xcur, g_norm_ffn[i]) * (1.0 + seg_rows(s2, n_rows)) + seg_rows(sh2, n_rows)).astype(BF16)
        ff = peer(qp, h2, peer_keys[i], peer_u[i].astype(BF16), jnp.transpose(peer_v[i].astype(BF16)), TM)
        xcur = xcur + seg_rows(g2, n_rows) * ff
        xl = xcur[:n_lat]
        if want_ctx:
            xc = xcur[n_lat:]

    return xl.reshape(B, L, D)
```

```python
import functools
import math

import jax
import jax.numpy as jnp
import numpy as np
from jax import lax
from jax.experimental import pallas as pl
from jax.experimental.pallas import tpu as pltpu

F32 = jnp.float32
BF16 = jnp.bfloat16

GRID_W = 64
HEAD_DIM = 64
BLOCK = 128
WINDOW = 128
ROPE_THETA = 10000.0
EPS = 1e-6
NEG_INF = -1e30
N_MOD = 6
DIFF_HEADS = 4
DIFF_QK_DIM = 32
WIN_HEADS = 4
WIN_KV_HEADS = 2
GLOB_HEADS = 4
GLOB_KV_HEADS = 2
N_OUT_HEADS = 16
HY_CH = 256
HY_ORDER = 2
HY_BANDS = 16
HY_TARGET = 1e-2
HY_FAST = 0.3
HY_SLOW = 1.5
W_DIFF = 768
W_WIN = 512
W_GLOB = 512
SPLITS = (W_DIFF, W_DIFF + W_WIN, W_DIFF + W_WIN + W_GLOB)
PEER_HEADS = 8
N_KEYS = 128
PEER_TOPK = 16

VMEM_LIMIT = 56 * 1024 * 1024
ATTN_TQ_DIFF = 512
ATTN_TQ_GLOB = 256
KEY_CHUNK_CAP = 1408
HY_CH_BLOCK = 16


def _params(*sem):
    return pltpu.CompilerParams(dimension_semantics=sem, vmem_limit_bytes=VMEM_LIMIT)


def _mm_kernel(a_ref, b_ref, o_ref):
    o_ref[...] = jnp.dot(a_ref[...], b_ref[...], preferred_element_type=F32)


def mm(a, b, tm, tn):
    M, K = a.shape
    N = b.shape[1]
    return pl.pallas_call(
        _mm_kernel,
        grid=(M // tm, N // tn),
        in_specs=[pl.BlockSpec((tm, K), lambda i, j: (i, 0)),
                  pl.BlockSpec((K, tn), lambda i, j: (0, j))],
        out_specs=pl.BlockSpec((tm, tn), lambda i, j: (i, j)),
        out_shape=jax.ShapeDtypeStruct((M, N), F32),
        compiler_params=_params("parallel", "arbitrary"),
        name="mm",
    )(a, b)


def _seg_spec(rows_per_seg, tm, nseg, D):
    blocks_per_seg = rows_per_seg // tm
    return pl.BlockSpec((1, 1, D), lambda i: (jnp.minimum(i // blocks_per_seg, nseg - 1), 0, 0))


def _normmod_mm_kernel(x_ref, g_ref, sh_ref, sc_ref, w_ref, o_ref, *h_out):
    x = x_ref[...]
    y = x * lax.rsqrt(jnp.mean(x * x, axis=-1, keepdims=True) + EPS) * g_ref[...]
    h = (y * (1.0 + sc_ref[0]) + sh_ref[0]).astype(BF16)
    if h_out:
        h_out[0][...] = h
    o_ref[...] = jnp.dot(h, w_ref[...], preferred_element_type=F32).astype(o_ref.dtype)


def normmod_mm(x, g, shift, scale, w, rows_per_seg, tm, out_dtype, with_h):
    M, D = x.shape
    N = w.shape[1]
    nseg = shift.shape[0]
    seg = _seg_spec(rows_per_seg, tm, nseg, D)
    row_block = lambda n: pl.BlockSpec((tm, n), lambda i: (i, 0))
    out_specs, out_shape = [row_block(N)], [jax.ShapeDtypeStruct((M, N), out_dtype)]
    if with_h:
        out_specs.append(row_block(D))
        out_shape.append(jax.ShapeDtypeStruct((M, D), BF16))
    return pl.pallas_call(
        _normmod_mm_kernel,
        grid=(M // tm,),
        in_specs=[row_block(D), pl.BlockSpec((1, D), lambda i: (0, 0)), seg, seg,
                  pl.BlockSpec((D, N), lambda i: (0, 0))],
        out_specs=out_specs,
        out_shape=out_shape,
        compiler_params=_params("parallel"),
        name="normmod_mm",
    )(x, g.reshape(1, D), shift.reshape(nseg, 1, D), scale.reshape(nseg, 1, D), w)


def _mm_residual_kernel(a_ref, w_ref, x_ref, gate_ref, o_ref):
    o_ref[...] = x_ref[...] + gate_ref[0] * jnp.dot(a_ref[...], w_ref[...], preferred_element_type=F32)


def mm_residual(a, w, x, gate, rows_per_seg, tm):
    M, K = a.shape
    N = w.shape[1]
    nseg = gate.shape[0]
    row_block = lambda n: pl.BlockSpec((tm, n), lambda i: (i, 0))
    return pl.pallas_call(
        _mm_residual_kernel,
        grid=(M // tm,),
        in_specs=[row_block(K), pl.BlockSpec((K, N), lambda i: (0, 0)), row_block(N),
                  _seg_spec(rows_per_seg, tm, nseg, N)],
        out_specs=row_block(N),
        out_shape=jax.ShapeDtypeStruct((M, N), F32),
        compiler_params=_params("parallel"),
        name="mm_residual",
    )(a, w, x, gate.reshape(nseg, 1, N))


def _scores_kernel(k_ref, q_ref, o_ref):
    d = k_ref.shape[-1]
    for p in range(k_ref.shape[0]):
        o_ref[p] = lax.dot_general(k_ref[p], q_ref[:, p * d:(p + 1) * d], (((1,), (1,)), ((), ())),
                                   preferred_element_type=F32)


def keys_times_qT(keys, q, tn):
    P, n, d = keys.shape
    T = q.shape[0]
    return pl.pallas_call(
        _scores_kernel,
        grid=(T // tn,),
        in_specs=[pl.BlockSpec((P, n, d), lambda j: (0, 0, 0)),
                  pl.BlockSpec((tn, P * d), lambda j: (j, 0))],
        out_specs=pl.BlockSpec((P, n, tn), lambda j: (0, 0, j)),
        out_shape=jax.ShapeDtypeStruct((P, n, T), F32),
        compiler_params=_params("parallel"),
        name="peer_scores_t",
    )(keys, q)


LOG2E = math.log2(math.e)


def _dense_attn_kernel(coef_ref, sink_ref, q_ref, k_ref, vt_ref, o_ref, sa_ref, sb_ref, *, nbr, R, G,
                       tq, tk, nk, has_sink):
    cols = R * tq
    d = q_ref.shape[-1]
    dv = vt_ref.shape[1]
    g = pl.program_id(0) % G
    nt = (((1,), (1,)), ((), ()))
    bufs = (sa_ref, sb_ref)
    out = jnp.zeros((dv, cols), F32)
    for br in range(nbr):
        q = q_ref[br, 0].reshape(cols, d)
        if has_sink:
            m = jnp.concatenate([jnp.full((1, tq), sink_ref[g * R + r] * LOG2E, F32)
                                 for r in range(R)], axis=1)
            l = jnp.ones((1, cols), F32)
        else:
            m = jnp.full((1, cols), NEG_INF, F32)
            l = jnp.zeros((1, cols), F32)
        acc = jnp.zeros((dv, cols), F32)
        bufs[0][...] = lax.dot_general(k_ref[br, 0, 0:tk, :], q, nt, preferred_element_type=F32)
        for j in range(nk):
            if j + 1 < nk:
                bufs[(j + 1) % 2][...] = lax.dot_general(k_ref[br, 0, (j + 1) * tk:(j + 2) * tk, :], q, nt,
                                                         preferred_element_type=F32)
            s = bufs[j % 2][...]
            m_new = jnp.maximum(m, jnp.max(s, axis=0, keepdims=True))
            alpha = jnp.exp2(m - m_new)
            p = jnp.exp2(s - m_new)
            l = alpha * l + jnp.sum(p, axis=0, keepdims=True)
            acc = alpha * acc + jnp.dot(vt_ref[0, :, j * tk:(j + 1) * tk], p.astype(BF16),
                                        preferred_element_type=F32)
            m = m_new
        out = out + coef_ref[br] * (acc / l)
    o_ref[0] = out.T.reshape(R, tq, dv)


def dense_attention(q, k, vt, coefs, sink, G, tq, tk):
    nbr, BG, R, Lq, d = q.shape
    Lk = k.shape[2]
    dv = vt.shape[1]
    has_sink = sink is not None
    if sink is None:
        sink = jnp.zeros((1,), F32)
    kern = functools.partial(_dense_attn_kernel, nbr=nbr, R=R, G=G, tq=tq, tk=tk, nk=Lk // tk,
                             has_sink=has_sink)
    return pl.pallas_call(
        kern,
        grid=(BG, Lq // tq),
        in_specs=[pl.BlockSpec(memory_space=pltpu.SMEM),
                  pl.BlockSpec(memory_space=pltpu.SMEM),
                  pl.BlockSpec((nbr, 1, R, tq, d), lambda b, i: (0, b, 0, i, 0)),
                  pl.BlockSpec((nbr, 1, Lk, d), lambda b, i: (0, b, 0, 0)),
                  pl.BlockSpec((1, dv, Lk), lambda b, i: (b, 0, 0))],
        out_specs=pl.BlockSpec((1, R, tq, dv), lambda b, i: (b, 0, i, 0)),
        out_shape=jax.ShapeDtypeStruct((BG, R, Lq, dv), F32),
        scratch_shapes=[pltpu.VMEM((tk, R * tq), F32), pltpu.VMEM((tk, R * tq), F32)],
        compiler_params=_params("parallel", "arbitrary"),
        name="dense_attn",
    )(coefs.astype(F32), sink.astype(F32), q, k, vt)


def _window_attn_kernel(sink_ref, q_ref, k_ref, v_ref, o_ref, *, R, G, C, L):
    rows = R * BLOCK
    d = q_ref.shape[-1]
    g = pl.program_id(0) % G
    n = pl.program_id(1)
    q = q_ref[0].reshape(rows, d)
    start = pl.multiple_of(C + n * BLOCK, BLOCK)
    kc = k_ref[0, 0:C, :]
    vc = v_ref[0, 0:C, :]
    kl = k_ref[0, pl.ds(start, 3 * BLOCK), :]
    vl = v_ref[0, pl.ds(start, 3 * BLOCK), :]
    nt = (((1,), (1,)), ((), ()))
    s_ctx = lax.dot_general(q, kc, nt, preferred_element_type=F32)
    s_loc = lax.dot_general(q, kl, nt, preferred_element_type=F32)
    qi = lax.broadcasted_iota(jnp.int32, (rows, 3 * BLOCK), 0) & (BLOCK - 1)
    kj = lax.broadcasted_iota(jnp.int32, (rows, 3 * BLOCK), 1)
    kpos = (n - 1) * BLOCK + kj
    valid = (jnp.abs(kj - BLOCK - qi) <= WINDOW) & (kpos >= 0) & (kpos < L)
    s_loc = jnp.where(valid, s_loc, NEG_INF)
    sink = jnp.concatenate([jnp.full((BLOCK, 1), sink_ref[g * R + r], F32) for r in range(R)], axis=0)
    m = jnp.maximum(jnp.maximum(jnp.max(s_ctx, axis=-1, keepdims=True),
                                jnp.max(s_loc, axis=-1, keepdims=True)), sink)
    e_ctx = jnp.exp(s_ctx - m)
    e_loc = jnp.exp(s_loc - m)
    den = (jnp.sum(e_ctx, axis=-1, keepdims=True) + jnp.sum(e_loc, axis=-1, keepdims=True)
           + jnp.exp(sink - m))
    inv = 1.0 / den
    o = (jnp.dot((e_ctx * inv).astype(BF16), vc, preferred_element_type=F32)
         + jnp.dot((e_loc * inv).astype(BF16), vl, preferred_element_type=F32))
    o_ref[0] = o.reshape(R, BLOCK, d)


def window_attention(q, kpad, vpad, sink, G, C, L):
    BG, R, _, d = q.shape
    Lp = kpad.shape[1]
    kern = functools.partial(_window_attn_kernel, R=R, G=G, C=C, L=L)
    return pl.pallas_call(
        kern,
        grid=(BG, L // BLOCK),
        in_specs=[pl.BlockSpec(memory_space=pltpu.SMEM),
                  pl.BlockSpec((1, R, BLOCK, d), lambda b, i: (b, 0, i, 0)),
                  pl.BlockSpec((1, Lp, d), lambda b, i: (b, 0, 0)),
                  pl.BlockSpec((1, Lp, d), lambda b, i: (b, 0, 0))],
        out_specs=pl.BlockSpec((1, R, BLOCK, d), lambda b, i: (b, 0, i, 0)),
        out_shape=jax.ShapeDtypeStruct((BG, R, L, d), F32),
        compiler_params=_params("parallel", "arbitrary"),
        name="window_attn",
    )(sink.astype(F32), q, kpad, vpad)


_CAND_PAIRS = tuple((r, s) for r in range(PEER_TOPK) for s in range(PEER_TOPK)
                    if (r + 1) * (s + 1) <= PEER_TOPK)
_CAND_ROWS = -(-len(_CAND_PAIRS) // 8) * 8


def _top_rows(v, n):
    iota = lax.broadcasted_iota(jnp.int32, v.shape, 0)
    rows = []
    for r in range(n):
        m = jnp.max(v, axis=0, keepdims=True)
        rows.append(m)
        if r + 1 < n:
            first = jnp.min(jnp.where(v == m, iota, v.shape[0]), axis=0, keepdims=True)
            v = jnp.where(iota == first, -jnp.inf, v)
    return rows


def _peer_select_kernel(s_ref, ea_ref, eb_ref, th_ref, c_ref, *, tm):
    n_lane = tm // 128

    def body(it, carry):
        h = it // n_lane
        lanes = pl.ds(pl.multiple_of((it % n_lane) * 128, 128), 128)
        a = s_ref[h, 0, :, lanes]
        b = s_ref[h, 1, :, lanes]
        ta = _top_rows(a, PEER_TOPK)
        tb = _top_rows(b, PEER_TOPK)
        c_ref[...] = jnp.full(c_ref.shape, -jnp.inf, F32)
        for k, (r, s) in enumerate(_CAND_PAIRS):
            c_ref[k:k + 1, :] = ta[r] + tb[s]
        best = _top_rows(c_ref[...], PEER_TOPK)
        tau = best[PEER_TOPK - 1]
        zsum = jnp.zeros_like(best[0])
        for bk in best:
            zsum = zsum + jnp.exp(bk - best[0])
        theta = jnp.full(a.shape, jnp.inf, F32)
        for r in reversed(range(PEER_TOPK)):
            th_r = jnp.full(tau.shape, jnp.inf, F32)
            for s in range(PEER_TOPK // (r + 1)):
                th_r = jnp.minimum(th_r, jnp.where(ta[r] + tb[s] >= tau, tb[s], jnp.inf))
            theta = jnp.where(a == ta[r], th_r, theta)
        th_ref[h, :, lanes] = theta
        ea_ref[h, :, lanes] = jnp.exp(a - ta[0]) / zsum
        eb_ref[h, :, lanes] = jnp.exp(b - tb[0])
        return carry

    lax.fori_loop(0, PEER_HEADS * n_lane, body, 0)


def peer_select(st, tm):
    H, _, n, T = st.shape
    kern = functools.partial(_peer_select_kernel, tm=tm)
    return pl.pallas_call(
        kern,
        grid=(T // tm,),
        in_specs=[pl.BlockSpec((H, 2, n, tm), lambda t: (0, 0, 0, t))],
        out_specs=[pl.BlockSpec((H, n, tm), lambda t: (0, 0, t))] * 3,
        out_shape=[jax.ShapeDtypeStruct((H, n, T), F32)] * 3,
        scratch_shapes=[pltpu.VMEM((_CAND_ROWS, 128), F32)],
        compiler_params=_params("parallel"),
        name="peer_select",
    )(st)


def _peer_kernel(x_ref, u_ref, vt_ref, s2_ref, th_ref, ea_ref, eb_ref, res_ref, gate_ref, o_ref,
                 acc_ref, act_ref, g_ref, *, ni, tm):
    c = pl.program_id(1)

    @pl.when(c == 0)
    def _():
        acc_ref[...] = jnp.zeros_like(acc_ref)

    act_ref[...] = jax.nn.gelu(lax.dot_general(u_ref[...], x_ref[...], (((1,), (1,)), ((), ())),
                                               preferred_element_type=F32))

    for ts in range(tm // 128):
        lanes = slice(ts * 128, (ts + 1) * 128)
        for ii in range(ni):
            rows = slice(ii * N_KEYS, (ii + 1) * N_KEYS)
            w = jnp.zeros((N_KEYS, 128), F32)
            for h in range(PEER_HEADS):
                sel = s2_ref[h, :, lanes] >= th_ref[h, ii:ii + 1, lanes]
                w = w + jnp.where(sel, ea_ref[h, ii:ii + 1, lanes] * eb_ref[h, :, lanes], 0.0)
            g_ref[rows, lanes] = (w * act_ref[rows, lanes]).astype(BF16)

    acc_ref[...] += jnp.dot(vt_ref[...], g_ref[...], preferred_element_type=F32)

    @pl.when(c == pl.num_programs(1) - 1)
    def _():
        o_ref[...] = res_ref[...] + gate_ref[0] * acc_ref[...].T


def peer_dense(x, u, vt, st, tht, eat, ebt, res, gate, rows_per_seg, tm, ni):
    T, D = x.shape
    E = u.shape[0]
    ec = ni * N_KEYS
    kern = functools.partial(_peer_kernel, ni=ni, tm=tm)
    nseg = gate.shape[0]
    blocks_per_seg = rows_per_seg // tm
    rows_of_chunk = pl.BlockSpec((PEER_HEADS, ni, tm), lambda t, c: (0, c, t))
    return pl.pallas_call(
        kern,
        grid=(T // tm, E // ec),
        in_specs=[pl.BlockSpec((tm, D), lambda t, c: (t, 0)),
                  pl.BlockSpec((ec, D), lambda t, c: (c, 0)),
                  pl.BlockSpec((D, ec), lambda t, c: (0, c)),
                  pl.BlockSpec((PEER_HEADS, None, N_KEYS, tm), lambda t, c: (0, 1, 0, t)),
                  rows_of_chunk,
                  rows_of_chunk,
                  pl.BlockSpec((PEER_HEADS, N_KEYS, tm), lambda t, c: (0, 0, t)),
                  pl.BlockSpec((tm, D), lambda t, c: (t, 0)),
                  pl.BlockSpec((1, 1, D), lambda t, c: (jnp.minimum(t // blocks_per_seg, nseg - 1), 0, 0))],
        out_specs=pl.BlockSpec((tm, D), lambda t, c: (t, 0)),
        out_shape=jax.ShapeDtypeStruct((T, D), F32),
        scratch_shapes=[pltpu.VMEM((D, tm), F32),
                        pltpu.VMEM((ec, tm), F32),
                        pltpu.VMEM((ec, tm), BF16)],
        compiler_params=_params("parallel", "arbitrary"),
        name="peer_dense",
    )(x, u, vt, st, tht, eat, ebt, res, gate.reshape(nseg, 1, D))


def peer(qp, h_bf, keys, u_bf, vt_bf, res, gate, rows_per_seg, tm):
    T = qp.shape[0]
    dk = keys.shape[-1]
    kflat = keys.reshape(PEER_HEADS * 2, N_KEYS, dk).astype(BF16)
    st = keys_times_qT(kflat, qp, tm).reshape(PEER_HEADS, 2, N_KEYS, T)
    eat, ebt, tht = peer_select(st, 256)
    return peer_dense(h_bf, u_bf, vt_bf, st, tht, eat, ebt, res, gate, rows_per_seg, tm, 8)


def rmsnorm(x, g):
    return x * lax.rsqrt(jnp.mean(x * x, axis=-1, keepdims=True) + EPS) * g


def axial_rope(n_tok, dim):
    rows = n_tok // GRID_W
    row = jnp.repeat(jnp.arange(rows, dtype=F32), GRID_W)
    col = jnp.tile(jnp.arange(GRID_W, dtype=F32), rows)
    axis_dim = dim // 2
    inv_freq = ROPE_THETA ** (-jnp.arange(0, axis_dim, 2, dtype=F32) / axis_dim)
    ang = jnp.concatenate([row[:, None] * inv_freq, col[:, None] * inv_freq], axis=-1)
    return jnp.cos(ang), jnp.sin(ang)


def apply_rope(x, cos, sin):
    shape = (1, x.shape[1]) + (1,) * (x.ndim - 3) + (cos.shape[-1],)
    c, s = cos.reshape(shape), sin.reshape(shape)
    x1, x2 = jnp.split(x, 2, axis=-1)
    return jnp.concatenate([x1 * c - x2 * s, x1 * s + x2 * c], axis=-1)


def short_conv(u, w, b):
    up = jnp.pad(u, ((0, 0), (1, 1), (0, 0)))
    return up[:, :-2] * w[0] + up[:, 1:-1] * w[1] + up[:, 2:] * w[2] + b


def hyena_filters(n_tok, w1, b1, w2, b2, w3, b3, freq):
    hp = lax.Precision.HIGHEST
    t = jnp.linspace(0.0, 1.0, n_tok, dtype=F32)[:, None]
    w = (2.0 * math.pi / n_tok) * jnp.arange(n_tok, dtype=F32)[:, None]
    f = jnp.linspace(1e-4, HY_BANDS - 1, HY_BANDS, dtype=F32)[None, :]
    feat = jnp.concatenate([t, jnp.cos(f * w), -jnp.sin(f * w)], axis=-1)
    hdn = jnp.sin(freq * (jnp.dot(feat, w1, precision=hp) + b1))
    hdn = jnp.sin(freq * (jnp.dot(hdn, w2, precision=hp) + b2))
    filt = (jnp.dot(hdn, w3, precision=hp) + b3).reshape(n_tok, HY_ORDER, 2, HY_CH)
    deltas = jnp.abs(jnp.linspace(math.log(HY_TARGET) / HY_SLOW, math.log(HY_TARGET) / HY_FAST, HY_CH,
                                  dtype=F32))
    filt = filt * jnp.exp(-t[:, :, None, None] * deltas)
    fwd, bwd = filt[:, :, 0], filt[:, :, 1]
    kfull = jnp.concatenate([fwd, jnp.zeros_like(fwd[:1]), bwd[1:][::-1]], axis=0)
    return kfull / jnp.sum(jnp.abs(kfull), axis=0, keepdims=True)


def long_conv(z, kf):
    n = z.shape[1]
    zf = jnp.fft.rfft(z, n=2 * n, axis=1)
    hf = jnp.fft.rfft(kf, n=2 * n, axis=0)
    return jnp.fft.irfft(zf * hf[None], n=2 * n, axis=1)[:, :n]


def hyena_small(u, conv_w, conv_b, filter_params, bias):
    n = u.shape[1]
    u = short_conv(u, conv_w, conv_b)
    v, x1, x2 = jnp.split(u, 3, axis=-1)
    kfull = hyena_filters(n, *filter_params)
    z = x1 * (long_conv(v, kfull[:, 0]) + bias[0] * v)
    return x2 * (long_conv(z, kfull[:, 1]) + bias[1] * z)


def _hi_lo(x):
    hi = x.astype(BF16)
    return hi, (x - hi.astype(F32)).astype(BF16)


def _dot3(a, b):
    d = lambda x, y: jnp.dot(x, y, preferred_element_type=F32)
    return d(a[0], b[0]) + d(a[1], b[0]) + d(a[0], b[1])


def _dft_constants(n_tok):
    n = 2 * n_tok
    n1 = n // 128
    a1 = 2.0 * np.pi * np.outer(np.arange(n1), np.arange(n1)) / n1
    a2 = 2.0 * np.pi * np.outer(np.arange(128), np.arange(128)) / 128
    at = 2.0 * np.pi * np.outer(np.arange(n1), np.arange(128)) / n
    c1, s1, c2, s2 = np.cos(a1), np.sin(a1), np.cos(a2), np.sin(a2)
    pair = lambda m: _hi_lo(jnp.asarray(m, F32))
    return dict(
        f1_half=pair(np.concatenate([c1[:, :n1 // 2], -s1[:, :n1 // 2]], axis=0)),
        f1_full=pair(np.concatenate([c1, -s1], axis=0)),
        m_fwd=pair(np.block([[c2, -s2], [s2, c2]])),
        m_inv=pair(np.block([[c2, s2], [-s2, c2]])),
        g_half=pair(np.concatenate([c1[:n1 // 2], -s1[:n1 // 2]], axis=1)),
        tr=jnp.asarray(np.cos(at), F32), ti=jnp.asarray(-np.sin(at), F32))


def _dft_fwd(seqs, f1, tr, ti, m_fwd):
    n1 = tr.shape[0]
    y = _dot3(f1, _hi_lo(jnp.concatenate(seqs, axis=1)))
    rows = []
    for k in range(len(seqs)):
        yr, yi = y[:n1, k * 128:(k + 1) * 128], y[n1:, k * 128:(k + 1) * 128]
        rows.append(jnp.concatenate([yr * tr - yi * ti, yr * ti + yi * tr], axis=1))
    return _dot3(_hi_lo(jnp.concatenate(rows, axis=0)), m_fwd)


def _dft_inv_half(p, g_half, tr, ti, m_inv):
    n1 = tr.shape[0]
    u = _dot3(_hi_lo(p), m_inv)
    cols = []
    for k in range(p.shape[0] // n1):
        ur, ui = u[k * n1:(k + 1) * n1, :128], u[k * n1:(k + 1) * n1, 128:]
        cols.append(jnp.concatenate([ur * tr + ui * ti, ui * tr - ur * ti], axis=0))
    return _dot3(g_half, _hi_lo(jnp.concatenate(cols, axis=1))) * (1.0 / (n1 * 128))


def _pairs(refs):
    return (refs[0][...], refs[1][...])


def _spectrum_kernel(a_ref, f1h, f1l, tr_ref, ti_ref, mh, ml, o_ref):
    cb, n1 = a_ref.shape[0], tr_ref.shape[0]
    x = _dft_fwd([a_ref[k] for k in range(cb)], _pairs((f1h, f1l)), tr_ref[...], ti_ref[...],
                 _pairs((mh, ml)))
    o_ref[...] = x.reshape(cb, n1, 256)


def filter_spectrum(kf, consts, cb):
    items, n1, _ = kf.shape
    full = lambda shape: pl.BlockSpec(shape, lambda i: (0,) * len(shape))
    f1, m = consts["f1_full"], consts["m_fwd"]
    return pl.pallas_call(
        _spectrum_kernel,
        grid=(items // cb,),
        in_specs=[pl.BlockSpec((cb, n1, 128), lambda i: (i, 0, 0)),
                  full(f1[0].shape), full(f1[1].shape), full((n1, 128)), full((n1, 128)),
                  full(m[0].shape), full(m[1].shape)],
        out_specs=pl.BlockSpec((cb, n1, 256), lambda i: (i, 0, 0)),
        out_shape=jax.ShapeDtypeStruct((items, n1, 256), F32),
        compiler_params=_params("parallel"),
        name="filter_spectrum",
    )(kf, f1[0], f1[1], consts["tr"], consts["ti"], m[0], m[1])


def _conv_gate_kernel(bias_ref, u_ref, g_ref, h_ref, f1h, f1l, tr_ref, ti_ref, mfh, mfl, mih, mil, gh, gl,
                      o_ref, *, cb):
    tr, ti = tr_ref[...], ti_ref[...]
    n1 = tr.shape[0]
    c0 = pl.program_id(1) * cb
    u = [u_ref[0, k] for k in range(cb)]
    x = _dft_fwd(u, _pairs((f1h, f1l)), tr, ti, _pairs((mfh, mfl)))
    prod = []
    for k in range(cb):
        xr, xi = x[k * n1:(k + 1) * n1, :128], x[k * n1:(k + 1) * n1, 128:]
        hr, hi = h_ref[k, :, :128], h_ref[k, :, 128:]
        prod.append(jnp.concatenate([xr * hr - xi * hi, xr * hi + xi * hr], axis=1))
    y = _dft_inv_half(jnp.concatenate(prod, axis=0), _pairs((gh, gl)), tr, ti, _pairs((mih, mil)))
    for k in range(cb):
        o_ref[0, k] = g_ref[0, k] * (y[:, k * 128:(k + 1) * 128] + bias_ref[c0 + k] * u[k])


def conv_gate(u, gate, spec, bias, consts, cb):
    B, C, half, _ = u.shape
    n1 = 2 * half
    full = lambda shape: pl.BlockSpec(shape, lambda b, c: (0,) * len(shape))
    seq = pl.BlockSpec((1, cb, half, 128), lambda b, c: (b, c, 0, 0))
    mats = [*consts["f1_half"], consts["tr"], consts["ti"], *consts["m_fwd"], *consts["m_inv"],
            *consts["g_half"]]
    return pl.pallas_call(
        functools.partial(_conv_gate_kernel, cb=cb),
        grid=(B, C // cb),
        in_specs=[pl.BlockSpec(memory_space=pltpu.SMEM), seq, seq,
                  pl.BlockSpec((cb, n1, 256), lambda b, c: (c, 0, 0))] + [full(m.shape) for m in mats],
        out_specs=seq,
        out_shape=jax.ShapeDtypeStruct(u.shape, F32),
        compiler_params=_params("parallel", "arbitrary"),
        name="conv_gate",
    )(bias.astype(F32), u, gate, spec, *mats)


def hyena(u, conv_w, conv_b, filter_params, bias):
    B, n, _ = u.shape
    half = n // 128
    consts = _dft_constants(n)
    u = short_conv(u, conv_w, conv_b)
    seqs = jnp.transpose(u, (0, 2, 1)).reshape(B, 3, HY_CH, half, 128)
    v, x1, x2 = seqs[:, 0], seqs[:, 1], seqs[:, 2]
    kfull = hyena_filters(n, *filter_params)
    kf = jnp.transpose(kfull, (1, 2, 0)).reshape(HY_ORDER * HY_CH, 2 * half, 128)
    spec = filter_spectrum(kf, consts, HY_CH_BLOCK).reshape(HY_ORDER, HY_CH, 2 * half, 256)
    z = conv_gate(v, x1, spec[0], bias[0], consts, HY_CH_BLOCK)
    o = conv_gate(z, x2, spec[1], bias[1], consts, HY_CH_BLOCK)
    return jnp.transpose(o.reshape(B, HY_CH, n), (0, 2, 1))


def _head_major(t):
    B, L, H, d = t.shape
    return jnp.transpose(t, (0, 2, 1, 3)).reshape(B * H, L, d)


def _head_major_t(t):
    B, L, H, d = t.shape
    return jnp.transpose(t, (0, 2, 3, 1)).reshape(B * H, d, L)


def _q_layout(t, G, R):
    B, L, _, d = t.shape
    return jnp.transpose(t.reshape(B, L, G, R, d), (0, 2, 3, 1, 4)).reshape(B * G, R, L, d)


def _from_q_layout(o, B, G, R):
    _, _, L, d = o.shape
    return jnp.transpose(o.reshape(B, G, R, L, d), (0, 3, 1, 2, 4)).reshape(B, L, G * R * d)


def _key_chunk(n_keys):
    for c in range(KEY_CHUNK_CAP, 0, -128):
        if n_keys % c == 0:
            return c
    raise ValueError(f"no key chunk for {n_keys}")


def kernel(x, c, ctx, c_ctx, w_mod, b_mod, g_norm_mix, w_in, g_qk_diff, lambda_diff, g_qk_win, sink_win, g_qk_glob, hy_conv_w, hy_conv_b, hy_w1, hy_b1, hy_w2, hy_b2, hy_w3, hy_b3, hy_freq, hy_bias, g_mix_out, w_out, g_norm_ffn, peer_wq, peer_keys, peer_u, peer_v):
    B, L, D = x.shape
    C = ctx.shape[1]
    depth = w_mod.shape[0]
    TM = 512
    n_lat = B * L
    n_ctx = B * C
    assert L % TM == 0 and n_ctx % TM == 0
    nseg = B + 1

    rope_half = axial_rope(L, DIFF_QK_DIM)
    rope_full = axial_rope(L, HEAD_DIM)
    sc = jnp.concatenate([jax.nn.silu(c), jax.nn.silu(c_ctx)[None]], axis=0)
    sc = jnp.pad(sc, ((0, 8 - nseg), (0, 0))).astype(BF16)

    xall = jnp.concatenate([x.reshape(n_lat, D), ctx.reshape(n_ctx, D)], axis=0)

    for i in range(depth):
        want_ctx = i < depth - 1
        lambda_init = 0.8 - 0.6 * math.exp(-0.3 * i)
        mod = mm(sc, w_mod[i].astype(BF16), 8, 1024)[:nseg] + b_mod[i]
        sh1, s1, g1, sh2, s2, g2 = jnp.split(mod, N_MOD, axis=-1)

        p, = normmod_mm(xall, g_norm_mix[i], sh1, s1, w_in[i].astype(BF16), L, TM, F32, False)
        pl_lat = p[:n_lat].reshape(B, L, -1)
        pc_ctx = p[n_lat:].reshape(B, C, -1)
        pa, pw, pg, ph = jnp.split(pl_lat, SPLITS, axis=-1)
        pac, pwc, pgc, phc = jnp.split(pc_ctx, SPLITS, axis=-1)

        def diff_heads(t):
            B_, L_, _ = t.shape
            q, k, v = jnp.split(t, 3, axis=-1)
            q = rmsnorm(q.reshape(B_, L_, DIFF_HEADS, 2, DIFF_QK_DIM), g_qk_diff[i, 0])
            k = rmsnorm(k.reshape(B_, L_, DIFF_HEADS, 2, DIFF_QK_DIM), g_qk_diff[i, 1])
            return q, k, v.reshape(B_, L_, DIFF_HEADS, HEAD_DIM)

        qa, ka, va = diff_heads(pa)
        qac, kac, vac = diff_heads(pac)
        qa, ka = apply_rope(qa, *rope_half), apply_rope(ka, *rope_half)
        lam_vec = lambda_diff[i]
        lam = (jnp.exp(jnp.sum(lam_vec[0] * lam_vec[1])) - jnp.exp(jnp.sum(lam_vec[2] * lam_vec[3]))
               + lambda_init)
        coefs = jnp.stack([jnp.ones((), F32), -lam])
        dscale = DIFF_QK_DIM ** -0.5 * LOG2E

        def diff_q(q):
            return jnp.stack([_head_major(q[..., b_, :] * dscale) for b_ in range(2)])[:, :, None].astype(BF16)

        def diff_k(k):
            return jnp.stack([_head_major(k[..., b_, :]) for b_ in range(2)]).astype(BF16)

        kka = jnp.concatenate([kac, ka], axis=1)
        vva = jnp.concatenate([vac, va], axis=1)
        oa = dense_attention(diff_q(qa), diff_k(kka), _head_major_t(vva).astype(BF16), coefs, None,
                             DIFF_HEADS, ATTN_TQ_DIFF, _key_chunk(C + L))
        oa = _from_q_layout(oa, B, DIFF_HEADS, 1)

        def gqa_heads(t, n_q, n_kv, g_qk):
            B_, L_, _ = t.shape
            q, k, v = jnp.split(t, [n_q * HEAD_DIM, (n_q + n_kv) * HEAD_DIM], axis=-1)
            q = rmsnorm(q.reshape(B_, L_, n_q, HEAD_DIM), g_qk[0])
            k = rmsnorm(k.reshape(B_, L_, n_kv, HEAD_DIM), g_qk[1])
            return q, k, v.reshape(B_, L_, n_kv, HEAD_DIM)

        hscale = HEAD_DIM ** -0.5
        hscale2 = hscale * LOG2E
        qw, kw, vw = gqa_heads(pw, WIN_HEADS, WIN_KV_HEADS, g_qk_win[i])
        qwc, kwc, vwc = gqa_heads(pwc, WIN_HEADS, WIN_KV_HEADS, g_qk_win[i])
        qw, kw = apply_rope(qw, *rope_full), apply_rope(kw, *rope_full)
        zblk = jnp.zeros((B, BLOCK, WIN_KV_HEADS, HEAD_DIM), F32)
        kpad = _head_major(jnp.concatenate([kwc, zblk, kw, zblk], axis=1)).astype(BF16)
        vpad = _head_major(jnp.concatenate([vwc, zblk, vw, zblk], axis=1)).astype(BF16)
        Rw = WIN_HEADS // WIN_KV_HEADS
        ob = window_attention(_q_layout(qw * hscale, WIN_KV_HEADS, Rw).astype(BF16), kpad, vpad,
                              sink_win[i], WIN_KV_HEADS, C, L)
        ob = _from_q_layout(ob, B, WIN_KV_HEADS, Rw)

        qg, kg, vg = gqa_heads(pg, GLOB_HEADS, GLOB_KV_HEADS, g_qk_glob[i])
        qgc, kgc, vgc = gqa_heads(pgc, GLOB_HEADS, GLOB_KV_HEADS, g_qk_glob[i])
        qg, kg = apply_rope(qg, *rope_full), apply_rope(kg, *rope_full)
        Rg = GLOB_HEADS // GLOB_KV_HEADS
        one = jnp.ones((1,), F32)
        kkg = _head_major(jnp.concatenate([kgc, kg], axis=1)).astype(BF16)[None]
        vvg = _head_major_t(jnp.concatenate([vgc, vg], axis=1)).astype(BF16)
        og = dense_attention(_q_layout(qg * hscale2, GLOB_KV_HEADS, Rg).astype(BF16)[None], kkg, vvg,
                             one, None, GLOB_KV_HEADS, ATTN_TQ_GLOB, _key_chunk(C + L))
        og = _from_q_layout(og, B, GLOB_KV_HEADS, Rg)

        filt = (hy_w1[i], hy_b1[i], hy_w2[i], hy_b2[i], hy_w3[i], hy_b3[i], hy_freq[i])
        oh = hyena(ph, hy_conv_w[i], hy_conv_b[i], filt, hy_bias[i])

        mixed = [jnp.concatenate([oa, ob, og, oh], axis=-1).reshape(n_lat, D)]
        if want_ctx:
            oac = dense_attention(diff_q(qac), diff_k(kac), _head_major_t(vac).astype(BF16), coefs, None,
                                  DIFF_HEADS, C, C)
            oac = _from_q_layout(oac, B, DIFF_HEADS, 1)
            obc = dense_attention(_q_layout(qwc * hscale2, WIN_KV_HEADS, Rw).astype(BF16)[None],
                                  _head_major(kwc).astype(BF16)[None], _head_major_t(vwc).astype(BF16),
                                  one, sink_win[i], WIN_KV_HEADS, C, C)
            obc = _from_q_layout(obc, B, WIN_KV_HEADS, Rw)
            ogc = dense_attention(_q_layout(qgc * hscale2, GLOB_KV_HEADS, Rg).astype(BF16)[None],
                                  _head_major(kgc).astype(BF16)[None], _head_major_t(vgc).astype(BF16),
                                  one, None, GLOB_KV_HEADS, C, C)
            ogc = _from_q_layout(ogc, B, GLOB_KV_HEADS, Rg)
            ohc = hyena_small(phc, hy_conv_w[i], hy_conv_b[i], filt, hy_bias[i])
            mixed.append(jnp.concatenate([oac, obc, ogc, ohc], axis=-1).reshape(n_ctx, D))
        o = jnp.concatenate(mixed, axis=0)
        n_rows = o.shape[0]

        oh_ = rmsnorm(o.reshape(n_rows, N_OUT_HEADS, HEAD_DIM), g_mix_out[i].reshape(N_OUT_HEADS, HEAD_DIM))
        head_scale = jnp.where(jnp.arange(N_OUT_HEADS) < DIFF_HEADS, 1.0 - lambda_init, 1.0)[:, None]
        om = (oh_ * head_scale.astype(F32)).reshape(n_rows, D)
        xcur = mm_residual(om.astype(BF16), w_out[i].astype(BF16), xall, g1, L, TM)

        qp, h2 = normmod_mm(xcur, g_norm_ffn[i], sh2, s2, peer_wq[i].astype(BF16), L, TM, BF16, True)
        xall = peer(qp, h2, peer_keys[i], peer_u[i].astype(BF16), jnp.transpose(peer_v[i].astype(BF16)),
                    xcur, g2, L, TM)

    return xall[:n_lat].reshape(B, L, D)
```

```python
import functools
import math

import jax
import jax.numpy as jnp
import numpy as np
from jax import lax
from jax.experimental import pallas as pl
from jax.experimental.pallas import tpu as pltpu

F32 = jnp.float32
BF16 = jnp.bfloat16

GRID_W = 64
HEAD_DIM = 64
BLOCK = 128
WINDOW = 128
ROPE_THETA = 10000.0
EPS = 1e-6
NEG_INF = -1e30
N_MOD = 6
DIFF_HEADS = 4
DIFF_QK_DIM = 32
WIN_HEADS = 4
WIN_KV_HEADS = 2
GLOB_HEADS = 4
GLOB_KV_HEADS = 2
N_OUT_HEADS = 16
HY_CH = 256
HY_ORDER = 2
HY_BANDS = 16
HY_TARGET = 1e-2
HY_FAST = 0.3
HY_SLOW = 1.5
W_DIFF = 768
W_WIN = 512
W_GLOB = 512
SPLITS = (W_DIFF, W_DIFF + W_WIN, W_DIFF + W_WIN + W_GLOB)
PEER_HEADS = 8
N_KEYS = 128
PEER_TOPK = 16

VMEM_LIMIT = 56 * 1024 * 1024
ATTN_TQ_DIFF = 512
ATTN_TQ_GLOB = 256
KEY_CHUNK_CAP = 1408
HY_CH_BLOCK = 16


def _params(*sem):
    return pltpu.CompilerParams(dimension_semantics=sem, vmem_limit_bytes=VMEM_LIMIT)


def _mm_kernel(a_ref, b_ref, o_ref):
    o_ref[...] = jnp.dot(a_ref[...], b_ref[...], preferred_element_type=F32)


def mm(a, b, tm, tn):
    M, K = a.shape
    N = b.shape[1]
    return pl.pallas_call(
        _mm_kernel,
        grid=(M // tm, N // tn),
        in_specs=[pl.BlockSpec((tm, K), lambda i, j: (i, 0)),
                  pl.BlockSpec((K, tn), lambda i, j: (0, j))],
        out_specs=pl.BlockSpec((tm, tn), lambda i, j: (i, j)),
        out_shape=jax.ShapeDtypeStruct((M, N), F32),
        compiler_params=_params("parallel", "arbitrary"),
        name="mm",
    )(a, b)


def _seg_spec(rows_per_seg, tm, nseg, D):
    blocks_per_seg = rows_per_seg // tm
    return pl.BlockSpec((1, 1, D), lambda i: (jnp.minimum(i // blocks_per_seg, nseg - 1), 0, 0))


def _normmod_mm_kernel(x_ref, g_ref, sh_ref, sc_ref, w_ref, o_ref, *h_out):
    x = x_ref[...]
    y = x * lax.rsqrt(jnp.mean(x * x, axis=-1, keepdims=True) + EPS) * g_ref[...]
    h = (y * (1.0 + sc_ref[0]) + sh_ref[0]).astype(BF16)
    if h_out:
        h_out[0][...] = h
    o_ref[...] = jnp.dot(h, w_ref[...], preferred_element_type=F32).astype(o_ref.dtype)


def normmod_mm(x, g, shift, scale, w, rows_per_seg, tm, out_dtype, with_h):
    M, D = x.shape
    N = w.shape[1]
    nseg = shift.shape[0]
    seg = _seg_spec(rows_per_seg, tm, nseg, D)
    row_block = lambda n: pl.BlockSpec((tm, n), lambda i: (i, 0))
    out_specs, out_shape = [row_block(N)], [jax.ShapeDtypeStruct((M, N), out_dtype)]
    if with_h:
        out_specs.append(row_block(D))
        out_shape.append(jax.ShapeDtypeStruct((M, D), BF16))
    return pl.pallas_call(
        _normmod_mm_kernel,
        grid=(M // tm,),
        in_specs=[row_block(D), pl.BlockSpec((1, D), lambda i: (0, 0)), seg, seg,
                  pl.BlockSpec((D, N), lambda i: (0, 0))],
        out_specs=out_specs,
        out_shape=out_shape,
        compiler_params=_params("parallel"),
        name="normmod_mm",
    )(x, g.reshape(1, D), shift.reshape(nseg, 1, D), scale.reshape(nseg, 1, D), w)


def _mm_residual_kernel(a_ref, w_ref, x_ref, gate_ref, o_ref):
    o_ref[...] = x_ref[...] + gate_ref[0] * jnp.dot(a_ref[...], w_ref[...], preferred_element_type=F32)


def mm_residual(a, w, x, gate, rows_per_seg, tm):
    M, K = a.shape
    N = w.shape[1]
    nseg = gate.shape[0]
    row_block = lambda n: pl.BlockSpec((tm, n), lambda i: (i, 0))
    return pl.pallas_call(
        _mm_residual_kernel,
        grid=(M // tm,),
        in_specs=[row_block(K), pl.BlockSpec((K, N), lambda i: (0, 0)), row_block(N),
                  _seg_spec(rows_per_seg, tm, nseg, N)],
        out_specs=row_block(N),
        out_shape=jax.ShapeDtypeStruct((M, N), F32),
        compiler_params=_params("parallel"),
        name="mm_residual",
    )(a, w, x, gate.reshape(nseg, 1, N))


def _scores_kernel(k_ref, q_ref, o_ref):
    d = k_ref.shape[-1]
    for p in range(k_ref.shape[0]):
        o_ref[p] = lax.dot_general(k_ref[p], q_ref[:, p * d:(p + 1) * d], (((1,), (1,)), ((), ())),
                                   preferred_element_type=F32)


def keys_times_qT(keys, q, tn):
    P, n, d = keys.shape
    T = q.shape[0]
    return pl.pallas_call(
        _scores_kernel,
        grid=(T // tn,),
        in_specs=[pl.BlockSpec((P, n, d), lambda j: (0, 0, 0)),
                  pl.BlockSpec((tn, P * d), lambda j: (j, 0))],
        out_specs=pl.BlockSpec((P, n, tn), lambda j: (0, 0, j)),
        out_shape=jax.ShapeDtypeStruct((P, n, T), F32),
        compiler_params=_params("parallel"),
        name="peer_scores_t",
    )(keys, q)


LOG2E = math.log2(math.e)


def _dense_attn_kernel(coef_ref, sink_ref, q_ref, k_ref, vt_ref, o_ref, sa_ref, sb_ref, *, nbr, R, G,
                       tq, tk, nk, has_sink):
    cols = R * tq
    d = q_ref.shape[-1]
    dv = vt_ref.shape[1]
    g = pl.program_id(0) % G
    nt = (((1,), (1,)), ((), ()))
    bufs = (sa_ref, sb_ref)
    out = jnp.zeros((dv, cols), F32)
    for br in range(nbr):
        q = q_ref[br, 0].reshape(cols, d)
        if has_sink:
            m = jnp.concatenate([jnp.full((1, tq), sink_ref[g * R + r] * LOG2E, F32)
                                 for r in range(R)], axis=1)
            l = jnp.ones((1, cols), F32)
        else:
            m = jnp.full((1, cols), NEG_INF, F32)
            l = jnp.zeros((1, cols), F32)
        acc = jnp.zeros((dv, cols), F32)
        bufs[0][...] = lax.dot_general(k_ref[br, 0, 0:tk, :], q, nt, preferred_element_type=F32)
        for j in range(nk):
            if j + 1 < nk:
                bufs[(j + 1) % 2][...] = lax.dot_general(k_ref[br, 0, (j + 1) * tk:(j + 2) * tk, :], q, nt,
                                                         preferred_element_type=F32)
            s = bufs[j % 2][...]
            m_new = jnp.maximum(m, jnp.max(s, axis=0, keepdims=True))
            alpha = jnp.exp2(m - m_new)
            p = jnp.exp2(s - m_new)
            l = alpha * l + jnp.sum(p, axis=0, keepdims=True)
            acc = alpha * acc + jnp.dot(vt_ref[0, :, j * tk:(j + 1) * tk], p.astype(BF16),
                                        preferred_element_type=F32)
            m = m_new
        out = out + coef_ref[br] * (acc / l)
    o_ref[0] = out.T.reshape(R, tq, dv)


def dense_attention(q, k, vt, coefs, sink, G, tq, tk):
    nbr, BG, R, Lq, d = q.shape
    Lk = k.shape[2]
    dv = vt.shape[1]
    has_sink = sink is not None
    if sink is None:
        sink = jnp.zeros((1,), F32)
    kern = functools.partial(_dense_attn_kernel, nbr=nbr, R=R, G=G, tq=tq, tk=tk, nk=Lk // tk,
                             has_sink=has_sink)
    return pl.pallas_call(
        kern,
        grid=(BG, Lq // tq),
        in_specs=[pl.BlockSpec(memory_space=pltpu.SMEM),
                  pl.BlockSpec(memory_space=pltpu.SMEM),
                  pl.BlockSpec((nbr, 1, R, tq, d), lambda b, i: (0, b, 0, i, 0)),
                  pl.BlockSpec((nbr, 1, Lk, d), lambda b, i: (0, b, 0, 0)),
                  pl.BlockSpec((1, dv, Lk), lambda b, i: (b, 0, 0))],
        out_specs=pl.BlockSpec((1, R, tq, dv), lambda b, i: (b, 0, i, 0)),
        out_shape=jax.ShapeDtypeStruct((BG, R, Lq, dv), F32),
        scratch_shapes=[pltpu.VMEM((tk, R * tq), F32), pltpu.VMEM((tk, R * tq), F32)],
        compiler_params=_params("parallel", "arbitrary"),
        name="dense_attn",
    )(coefs.astype(F32), sink.astype(F32), q, k, vt)


def _window_attn_kernel(sink_ref, q_ref, k_ref, v_ref, o_ref, *, R, G, C, L):
    rows = R * BLOCK
    d = q_ref.shape[-1]
    g = pl.program_id(0) % G
    n = pl.program_id(1)
    q = q_ref[0].reshape(rows, d)
    start = pl.multiple_of(C + n * BLOCK, BLOCK)
    kc = k_ref[0, 0:C, :]
    vc = v_ref[0, 0:C, :]
    kl = k_ref[0, pl.ds(start, 3 * BLOCK), :]
    vl = v_ref[0, pl.ds(start, 3 * BLOCK), :]
    nt = (((1,), (1,)), ((), ()))
    s_ctx = lax.dot_general(q, kc, nt, preferred_element_type=F32)
    s_loc = lax.dot_general(q, kl, nt, preferred_element_type=F32)
    qi = lax.broadcasted_iota(jnp.int32, (rows, 3 * BLOCK), 0) & (BLOCK - 1)
    kj = lax.broadcasted_iota(jnp.int32, (rows, 3 * BLOCK), 1)
    kpos = (n - 1) * BLOCK + kj
    valid = (jnp.abs(kj - BLOCK - qi) <= WINDOW) & (kpos >= 0) & (kpos < L)
    s_loc = jnp.where(valid, s_loc, NEG_INF)
    sink = jnp.concatenate([jnp.full((BLOCK, 1), sink_ref[g * R + r], F32) for r in range(R)], axis=0)
    m = jnp.maximum(jnp.maximum(jnp.max(s_ctx, axis=-1, keepdims=True),
                                jnp.max(s_loc, axis=-1, keepdims=True)), sink)
    e_ctx = jnp.exp(s_ctx - m)
    e_loc = jnp.exp(s_loc - m)
    den = (jnp.sum(e_ctx, axis=-1, keepdims=True) + jnp.sum(e_loc, axis=-1, keepdims=True)
           + jnp.exp(sink - m))
    inv = 1.0 / den
    o = (jnp.dot((e_ctx * inv).astype(BF16), vc, preferred_element_type=F32)
         + jnp.dot((e_loc * inv).astype(BF16), vl, preferred_element_type=F32))
    o_ref[0] = o.reshape(R, BLOCK, d)


def window_attention(q, kpad, vpad, sink, G, C, L):
    BG, R, _, d = q.shape
    Lp = kpad.shape[1]
    kern = functools.partial(_window_attn_kernel, R=R, G=G, C=C, L=L)
    return pl.pallas_call(
        kern,
        grid=(BG, L // BLOCK),
        in_specs=[pl.BlockSpec(memory_space=pltpu.SMEM),
                  pl.BlockSpec((1, R, BLOCK, d), lambda b, i: (b, 0, i, 0)),
                  pl.BlockSpec((1, Lp, d), lambda b, i: (b, 0, 0)),
                  pl.BlockSpec((1, Lp, d), lambda b, i: (b, 0, 0))],
        out_specs=pl.BlockSpec((1, R, BLOCK, d), lambda b, i: (b, 0, i, 0)),
        out_shape=jax.ShapeDtypeStruct((BG, R, L, d), F32),
        compiler_params=_params("parallel", "arbitrary"),
        name="window_attn",
    )(sink.astype(F32), q, kpad, vpad)


_CAND_PAIRS = tuple((r, s) for r in range(PEER_TOPK) for s in range(PEER_TOPK)
                    if (r + 1) * (s + 1) <= PEER_TOPK)
_CAND_ROWS = -(-len(_CAND_PAIRS) // 8) * 8


def _top_rows(vs, n):
    vs = list(vs)
    iota = lax.broadcasted_iota(jnp.int32, vs[0].shape, 0).astype(F32)
    rows = [[] for _ in vs]
    for r in range(n):
        for i, v in enumerate(vs):
            m = jnp.max(v, axis=0, keepdims=True)
            rows[i].append(m)
            if r + 1 < n:
                first = jnp.min(jnp.where(v == m, iota, float(v.shape[0])), axis=0, keepdims=True)
                vs[i] = jnp.where(iota == first, -jnp.inf, v)
    return rows


def _peer_select_kernel(s_ref, ea_ref, q_ref, eb_ref, code_ref, c_ref, *, tm):
    n_lane = tm // 128

    def body(it, carry):
        h = it // n_lane
        lanes = pl.ds(pl.multiple_of((it % n_lane) * 128, 128), 128)
        a = s_ref[h, 0, :, lanes]
        b = s_ref[h, 1, :, lanes]
        ta, tb = _top_rows((a, b), PEER_TOPK)
        c_ref[...] = jnp.full(c_ref.shape, -jnp.inf, F32)
        for k, (r, s) in enumerate(_CAND_PAIRS):
            c_ref[k:k + 1, :] = ta[r] + tb[s]
        best, = _top_rows((c_ref[...],), PEER_TOPK)
        tau = best[PEER_TOPK - 1]
        zsum = jnp.zeros_like(best[0])
        for bk in best:
            zsum = zsum + jnp.exp(bk - best[0])
        code_b = jnp.zeros(b.shape, F32)
        for s in range(PEER_TOPK):
            code_b = code_b + jnp.where(tb[s] > b, 1.0, 0.0)
        code_t = [jnp.zeros(tau.shape, F32)]
        for s in range(1, PEER_TOPK):
            code_t.append(jnp.where(tb[s] == tb[s - 1], code_t[s - 1], float(s)))
        q = jnp.full(a.shape, -1.0, F32)
        for r in reversed(range(PEER_TOPK)):
            q_r = jnp.full(tau.shape, -1.0, F32)
            for s in range(PEER_TOPK // (r + 1)):
                q_r = jnp.maximum(q_r, jnp.where(ta[r] + tb[s] >= tau, code_t[s], -1.0))
            q = jnp.where(a == ta[r], q_r, q)
        q_ref[h, :, lanes] = q
        code_ref[h, :, lanes] = code_b.astype(BF16)
        ea_ref[h, :, lanes] = jnp.exp(a - ta[0]) / zsum
        eb_ref[h, :, lanes] = jnp.exp(b - tb[0]).astype(BF16)
        return carry

    lax.fori_loop(0, PEER_HEADS * n_lane, body, 0)


def peer_select(st, tm):
    H, _, n, T = st.shape
    kern = functools.partial(_peer_select_kernel, tm=tm)
    return pl.pallas_call(
        kern,
        grid=(T // tm,),
        in_specs=[pl.BlockSpec((H, 2, n, tm), lambda t: (0, 0, 0, t))],
        out_specs=[pl.BlockSpec((H, n, tm), lambda t: (0, 0, t))] * 4,
        out_shape=[jax.ShapeDtypeStruct((H, n, T), dt) for dt in (F32, F32, BF16, BF16)],
        scratch_shapes=[pltpu.VMEM((_CAND_ROWS, 128), F32)],
        compiler_params=_params("parallel"),
        name="peer_select",
    )(st)


def _peer_kernel(x_ref, u_ref, vt_ref, ea_ref, q_ref, eb_ref, code_ref, res_ref, gate_ref, o_ref,
                 acc_ref, act_ref, g_ref, *, ni, tm):
    c = pl.program_id(1)

    @pl.when(c == 0)
    def _():
        acc_ref[...] = jnp.zeros_like(acc_ref)

    act_ref[...] = jax.nn.gelu(lax.dot_general(u_ref[...], x_ref[...], (((1,), (1,)), ((), ())),
                                               preferred_element_type=F32))

    for ts in range(tm // 128):
        lanes = slice(ts * 128, (ts + 1) * 128)
        for ii in range(ni):
            rows = slice(ii * N_KEYS, (ii + 1) * N_KEYS)
            w = jnp.zeros((N_KEYS, 128), BF16)
            for h in range(PEER_HEADS):
                sel = code_ref[h, :, lanes] <= q_ref[h, ii:ii + 1, lanes].astype(BF16)
                gate = ea_ref[h, ii:ii + 1, lanes].astype(BF16) * eb_ref[h, :, lanes]
                w = w + jnp.where(sel, gate, jnp.zeros_like(gate))
            g_ref[rows, lanes] = w * act_ref[rows, lanes].astype(BF16)

    acc_ref[...] += jnp.dot(vt_ref[...], g_ref[...], preferred_element_type=F32)

    @pl.when(c == pl.num_programs(1) - 1)
    def _():
        o_ref[...] = res_ref[...] + gate_ref[0] * acc_ref[...].T


def peer_dense(x, u, vt, eat, qt, ebt, codet, res, gate, rows_per_seg, tm, ni):
    T, D = x.shape
    E = u.shape[0]
    ec = ni * N_KEYS
    kern = functools.partial(_peer_kernel, ni=ni, tm=tm)
    nseg = gate.shape[0]
    blocks_per_seg = rows_per_seg // tm
    rows_of_chunk = pl.BlockSpec((PEER_HEADS, ni, tm), lambda t, c: (0, c, t))
    return pl.pallas_call(
        kern,
        grid=(T // tm, E // ec),
        in_specs=[pl.BlockSpec((tm, D), lambda t, c: (t, 0)),
                  pl.BlockSpec((ec, D), lambda t, c: (c, 0)),
                  pl.BlockSpec((D, ec), lambda t, c: (0, c)),
                  rows_of_chunk,
                  rows_of_chunk,
                  pl.BlockSpec((PEER_HEADS, N_KEYS, tm), lambda t, c: (0, 0, t)),
                  pl.BlockSpec((PEER_HEADS, N_KEYS, tm), lambda t, c: (0, 0, t)),
                  pl.BlockSpec((tm, D), lambda t, c: (t, 0)),
                  pl.BlockSpec((1, 1, D), lambda t, c: (jnp.minimum(t // blocks_per_seg, nseg - 1), 0, 0))],
        out_specs=pl.BlockSpec((tm, D), lambda t, c: (t, 0)),
        out_shape=jax.ShapeDtypeStruct((T, D), F32),
        scratch_shapes=[pltpu.VMEM((D, tm), F32),
                        pltpu.VMEM((ec, tm), F32),
                        pltpu.VMEM((ec, tm), BF16)],
        compiler_params=_params("parallel", "arbitrary"),
        name="peer_dense",
    )(x, u, vt, eat, qt, ebt, codet, res, gate.reshape(nseg, 1, D))


def peer(qp, h_bf, keys, u_bf, vt_bf, res, gate, rows_per_seg, tm):
    T = qp.shape[0]
    dk = keys.shape[-1]
    kflat = keys.reshape(PEER_HEADS * 2, N_KEYS, dk).astype(BF16)
    st = keys_times_qT(kflat, qp, tm).reshape(PEER_HEADS, 2, N_KEYS, T)
    eat, qt, ebt, codet = peer_select(st, 256)
    return peer_dense(h_bf, u_bf, vt_bf, eat, qt, ebt, codet, res, gate, rows_per_seg, tm, 8)


def rmsnorm(x, g):
    return x * lax.rsqrt(jnp.mean(x * x, axis=-1, keepdims=True) + EPS) * g


def axial_rope(n_tok, dim):
    rows = n_tok // GRID_W
    row = jnp.repeat(jnp.arange(rows, dtype=F32), GRID_W)
    col = jnp.tile(jnp.arange(GRID_W, dtype=F32), rows)
    axis_dim = dim // 2
    inv_freq = ROPE_THETA ** (-jnp.arange(0, axis_dim, 2, dtype=F32) / axis_dim)
    ang = jnp.concatenate([row[:, None] * inv_freq, col[:, None] * inv_freq], axis=-1)
    return jnp.cos(ang), jnp.sin(ang)


def apply_rope(x, cos, sin):
    shape = (1, x.shape[1]) + (1,) * (x.ndim - 3) + (cos.shape[-1],)
    c, s = cos.reshape(shape), sin.reshape(shape)
    x1, x2 = jnp.split(x, 2, axis=-1)
    return jnp.concatenate([x1 * c - x2 * s, x1 * s + x2 * c], axis=-1)


def short_conv(u, w, b):
    up = jnp.pad(u, ((0, 0), (1, 1), (0, 0)))
    return up[:, :-2] * w[0] + up[:, 1:-1] * w[1] + up[:, 2:] * w[2] + b


def hyena_filters(n_tok, w1, b1, w2, b2, w3, b3, freq):
    hp = lax.Precision.HIGHEST
    t = jnp.linspace(0.0, 1.0, n_tok, dtype=F32)[:, None]
    w = (2.0 * math.pi / n_tok) * jnp.arange(n_tok, dtype=F32)[:, None]
    f = jnp.linspace(1e-4, HY_BANDS - 1, HY_BANDS, dtype=F32)[None, :]
    feat = jnp.concatenate([t, jnp.cos(f * w), -jnp.sin(f * w)], axis=-1)
    hdn = jnp.sin(freq * (jnp.dot(feat, w1, precision=hp) + b1))
    hdn = jnp.sin(freq * (jnp.dot(hdn, w2, precision=hp) + b2))
    filt = (jnp.dot(hdn, w3, precision=hp) + b3).reshape(n_tok, HY_ORDER, 2, HY_CH)
    deltas = jnp.abs(jnp.linspace(math.log(HY_TARGET) / HY_SLOW, math.log(HY_TARGET) / HY_FAST, HY_CH,
                                  dtype=F32))
    filt = filt * jnp.exp(-t[:, :, None, None] * deltas)
    fwd, bwd = filt[:, :, 0], filt[:, :, 1]
    kfull = jnp.concatenate([fwd, jnp.zeros_like(fwd[:1]), bwd[1:][::-1]], axis=0)
    return kfull / jnp.sum(jnp.abs(kfull), axis=0, keepdims=True)


def hyena_filters_t(n_tok, w1, b1, w2, b2, w3, b3, freq):
    hp = lax.Precision.HIGHEST
    t = jnp.linspace(0.0, 1.0, n_tok, dtype=F32)[:, None]
    w = (2.0 * math.pi / n_tok) * jnp.arange(n_tok, dtype=F32)[:, None]
    f = jnp.linspace(1e-4, HY_BANDS - 1, HY_BANDS, dtype=F32)[None, :]
    feat = jnp.concatenate([t, jnp.cos(f * w), -jnp.sin(f * w)], axis=-1)
    hdn = jnp.sin(freq * (jnp.dot(feat, w1, precision=hp) + b1))
    hdn = jnp.sin(freq * (jnp.dot(hdn, w2, precision=hp) + b2))
    deltas = jnp.abs(jnp.linspace(math.log(HY_TARGET) / HY_SLOW, math.log(HY_TARGET) / HY_FAST, HY_CH,
                                  dtype=F32))
    w3t = jnp.transpose(w3.reshape(-1, HY_ORDER, 2, HY_CH), (2, 1, 3, 0))
    b3t = jnp.transpose(b3.reshape(HY_ORDER, 2, HY_CH), (1, 0, 2))[..., None]

    def half(d, hidden, times):
        decay = jnp.exp(-deltas[:, None] * times[None, :])
        return (jnp.einsum('ock,nk->ocn', w3t[d], hidden, precision=hp) + b3t[d]) * decay

    fwd = half(0, hdn, t[:, 0])
    bwd_rev = half(1, hdn[::-1], t[::-1, 0])
    kfull = jnp.concatenate([fwd, jnp.zeros_like(fwd[..., :1]), bwd_rev[..., :n_tok - 1]], axis=-1)
    kfull = kfull / jnp.sum(jnp.abs(kfull), axis=-1, keepdims=True)
    return kfull.reshape(HY_ORDER * HY_CH, 2 * n_tok)


def long_conv(z, kf):
    n = z.shape[1]
    zf = jnp.fft.rfft(z, n=2 * n, axis=1)
    hf = jnp.fft.rfft(kf, n=2 * n, axis=0)
    return jnp.fft.irfft(zf * hf[None], n=2 * n, axis=1)[:, :n]


def hyena_small(u, conv_w, conv_b, filter_params, bias):
    n = u.shape[1]
    u = short_conv(u, conv_w, conv_b)
    v, x1, x2 = jnp.split(u, 3, axis=-1)
    kfull = hyena_filters(n, *filter_params)
    z = x1 * (long_conv(v, kfull[:, 0]) + bias[0] * v)
    return x2 * (long_conv(z, kfull[:, 1]) + bias[1] * z)


def _hi_lo(x):
    hi = x.astype(BF16)
    return hi, (x - hi.astype(F32)).astype(BF16)


def _dot3(a, b):
    d = lambda x, y: jnp.dot(x, y, preferred_element_type=F32)
    return d(a[0], b[0]) + d(a[1], b[0]) + d(a[0], b[1])


def _dft_constants(n_tok):
    n = 2 * n_tok
    n1 = n // 128
    a1 = 2.0 * np.pi * np.outer(np.arange(n1), np.arange(n1)) / n1
    a2 = 2.0 * np.pi * np.outer(np.arange(128), np.arange(128)) / 128
    at = 2.0 * np.pi * np.outer(np.arange(n1), np.arange(128)) / n
    c1, s1, c2, s2 = np.cos(a1), np.sin(a1), np.cos(a2), np.sin(a2)
    pair = lambda m: _hi_lo(jnp.asarray(m, F32))
    return dict(
        f1_half=pair(np.concatenate([c1[:, :n1 // 2], -s1[:, :n1 // 2]], axis=0)),
        f1_full=pair(np.concatenate([c1, -s1], axis=0)),
        m_fwd=pair(np.block([[c2, -s2], [s2, c2]])),
        m_inv=pair(np.block([[c2, s2], [-s2, c2]])),
        g_half=pair(np.concatenate([c1[:n1 // 2], -s1[:n1 // 2]], axis=1)),
        tr=jnp.asarray(np.cos(at), F32), ti=jnp.asarray(-np.sin(at), F32))


def _dft_fwd(seqs, f1, tr, ti, m_fwd):
    n1 = tr.shape[0]
    y = _dot3(f1, _hi_lo(jnp.concatenate(seqs, axis=1)))
    rows = []
    for k in range(len(seqs)):
        yr, yi = y[:n1, k * 128:(k + 1) * 128], y[n1:, k * 128:(k + 1) * 128]
        rows.append(jnp.concatenate([yr * tr - yi * ti, yr * ti + yi * tr], axis=1))
    return _dot3(_hi_lo(jnp.concatenate(rows, axis=0)), m_fwd)


def _dft_inv_half(p, g_half, tr, ti, m_inv):
    n1 = tr.shape[0]
    u = _dot3(_hi_lo(p), m_inv)
    cols = []
    for k in range(p.shape[0] // n1):
        ur, ui = u[k * n1:(k + 1) * n1, :128], u[k * n1:(k + 1) * n1, 128:]
        cols.append(jnp.concatenate([ur * tr + ui * ti, ui * tr - ur * ti], axis=0))
    return _dot3(g_half, _hi_lo(jnp.concatenate(cols, axis=1))) * (1.0 / (n1 * 128))


def _pairs(refs):
    return (refs[0][...], refs[1][...])


def _spectrum_kernel(a_ref, f1h, f1l, tr_ref, ti_ref, mh, ml, o_ref):
    cb, n1 = a_ref.shape[0], tr_ref.shape[0]
    x = _dft_fwd([a_ref[k] for k in range(cb)], _pairs((f1h, f1l)), tr_ref[...], ti_ref[...],
                 _pairs((mh, ml)))
    o_ref[...] = x.reshape(cb, n1, 256)


def filter_spectrum(kf, consts, cb):
    items, n1, _ = kf.shape
    full = lambda shape: pl.BlockSpec(shape, lambda i: (0,) * len(shape))
    f1, m = consts["f1_full"], consts["m_fwd"]
    return pl.pallas_call(
        _spectrum_kernel,
        grid=(items // cb,),
        in_specs=[pl.BlockSpec((cb, n1, 128), lambda i: (i, 0, 0)),
                  full(f1[0].shape), full(f1[1].shape), full((n1, 128)), full((n1, 128)),
                  full(m[0].shape), full(m[1].shape)],
        out_specs=pl.BlockSpec((cb, n1, 256), lambda i: (i, 0, 0)),
        out_shape=jax.ShapeDtypeStruct((items, n1, 256), F32),
        compiler_params=_params("parallel"),
        name="filter_spectrum",
    )(kf, f1[0], f1[1], consts["tr"], consts["ti"], m[0], m[1])


def _conv_gate_kernel(bias_ref, u_ref, g_ref, h_ref, f1h, f1l, tr_ref, ti_ref, mfh, mfl, mih, mil, gh, gl,
                      o_ref, *, cb):
    tr, ti = tr_ref[...], ti_ref[...]
    n1 = tr.shape[0]
    c0 = pl.program_id(1) * cb
    u = [u_ref[0, k] for k in range(cb)]
    x = _dft_fwd(u, _pairs((f1h, f1l)), tr, ti, _pairs((mfh, mfl)))
    prod = []
    for k in range(cb):
        xr, xi = x[k * n1:(k + 1) * n1, :128], x[k * n1:(k + 1) * n1, 128:]
        hr, hi = h_ref[k, :, :128], h_ref[k, :, 128:]
        prod.append(jnp.concatenate([xr * hr - xi * hi, xr * hi + xi * hr], axis=1))
    y = _dft_inv_half(jnp.concatenate(prod, axis=0), _pairs((gh, gl)), tr, ti, _pairs((mih, mil)))
    for k in range(cb):
        o_ref[0, k] = g_ref[0, k] * (y[:, k * 128:(k + 1) * 128] + bias_ref[c0 + k] * u[k])


def conv_gate(u, gate, spec, bias, consts, cb):
    B, C, half, _ = u.shape
    n1 = 2 * half
    full = lambda shape: pl.BlockSpec(shape, lambda b, c: (0,) * len(shape))
    seq = pl.BlockSpec((1, cb, half, 128), lambda b, c: (b, c, 0, 0))
    mats = [*consts["f1_half"], consts["tr"], consts["ti"], *consts["m_fwd"], *consts["m_inv"],
            *consts["g_half"]]
    return pl.pallas_call(
        functools.partial(_conv_gate_kernel, cb=cb),
        grid=(B, C // cb),
        in_specs=[pl.BlockSpec(memory_space=pltpu.SMEM), seq, seq,
                  pl.BlockSpec((cb, n1, 256), lambda b, c: (c, 0, 0))] + [full(m.shape) for m in mats],
        out_specs=seq,
        out_shape=jax.ShapeDtypeStruct(u.shape, F32),
        compiler_params=_params("parallel", "arbitrary"),
        name="conv_gate",
    )(bias.astype(F32), u, gate, spec, *mats)


def hyena(u, conv_w, conv_b, filter_params, bias):
    B, n, _ = u.shape
    half = n // 128
    consts = _dft_constants(n)
    u = short_conv(u, conv_w, conv_b)
    seqs = jnp.transpose(u, (0, 2, 1)).reshape(B, 3, HY_CH, half, 128)
    v, x1, x2 = seqs[:, 0], seqs[:, 1], seqs[:, 2]
    kf = hyena_filters_t(n, *filter_params).reshape(HY_ORDER * HY_CH, 2 * half, 128)
    spec = filter_spectrum(kf, consts, HY_CH_BLOCK).reshape(HY_ORDER, HY_CH, 2 * half, 256)
    z = conv_gate(v, x1, spec[0], bias[0], consts, HY_CH_BLOCK)
    o = conv_gate(z, x2, spec[1], bias[1], consts, HY_CH_BLOCK)
    return jnp.transpose(o.reshape(B, HY_CH, n), (0, 2, 1))


def _head_major(t):
    B, L, H, d = t.shape
    return jnp.transpose(t, (0, 2, 1, 3)).reshape(B * H, L, d)


def _head_major_t(t):
    B, L, H, d = t.shape
    return jnp.transpose(t, (0, 2, 3, 1)).reshape(B * H, d, L)


def _q_layout(t, G, R):
    B, L, _, d = t.shape
    return jnp.transpose(t.reshape(B, L, G, R, d), (0, 2, 3, 1, 4)).reshape(B * G, R, L, d)


def _from_q_layout(o, B, G, R):
    _, _, L, d = o.shape
    return jnp.transpose(o.reshape(B, G, R, L, d), (0, 3, 1, 2, 4)).reshape(B, L, G * R * d)


def _key_chunk(n_keys):
    for c in range(KEY_CHUNK_CAP, 0, -128):
        if n_keys % c == 0:
            return c
    raise ValueError(f"no key chunk for {n_keys}")


def kernel(x, c, ctx, c_ctx, w_mod, b_mod, g_norm_mix, w_in, g_qk_diff, lambda_diff, g_qk_win, sink_win, g_qk_glob, hy_conv_w, hy_conv_b, hy_w1, hy_b1, hy_w2, hy_b2, hy_w3, hy_b3, hy_freq, hy_bias, g_mix_out, w_out, g_norm_ffn, peer_wq, peer_keys, peer_u, peer_v):
    B, L, D = x.shape
    C = ctx.shape[1]
    depth = w_mod.shape[0]
    TM = 512
    n_lat = B * L
    n_ctx = B * C
    assert L % TM == 0 and n_ctx % TM == 0
    nseg = B + 1

    rope_half = axial_rope(L, DIFF_QK_DIM)
    rope_full = axial_rope(L, HEAD_DIM)
    sc = jnp.concatenate([jax.nn.silu(c), jax.nn.silu(c_ctx)[None]], axis=0)
    sc = jnp.pad(sc, ((0, 8 - nseg), (0, 0))).astype(BF16)

    xall = jnp.concatenate([x.reshape(n_lat, D), ctx.reshape(n_ctx, D)], axis=0)

    for i in range(depth):
        want_ctx = i < depth - 1
        lambda_init = 0.8 - 0.6 * math.exp(-0.3 * i)
        mod = mm(sc, w_mod[i].astype(BF16), 8, 1024)[:nseg] + b_mod[i]
        sh1, s1, g1, sh2, s2, g2 = jnp.split(mod, N_MOD, axis=-1)

        p, = normmod_mm(xall, g_norm_mix[i], sh1, s1, w_in[i].astype(BF16), L, TM, F32, False)
        pl_lat = p[:n_lat].reshape(B, L, -1)
        pc_ctx = p[n_lat:].reshape(B, C, -1)
        pa, pw, pg, ph = jnp.split(pl_lat, SPLITS, axis=-1)
        pac, pwc, pgc, phc = jnp.split(pc_ctx, SPLITS, axis=-1)

        def diff_heads(t):
            B_, L_, _ = t.shape
            q, k, v = jnp.split(t, 3, axis=-1)
            q = rmsnorm(q.reshape(B_, L_, DIFF_HEADS, 2, DIFF_QK_DIM), g_qk_diff[i, 0])
            k = rmsnorm(k.reshape(B_, L_, DIFF_HEADS, 2, DIFF_QK_DIM), g_qk_diff[i, 1])
            return q, k, v.reshape(B_, L_, DIFF_HEADS, HEAD_DIM)

        qa, ka, va = diff_heads(pa)
        qac, kac, vac = diff_heads(pac)
        qa, ka = apply_rope(qa, *rope_half), apply_rope(ka, *rope_half)
        lam_vec = lambda_diff[i]
        lam = (jnp.exp(jnp.sum(lam_vec[0] * lam_vec[1])) - jnp.exp(jnp.sum(lam_vec[2] * lam_vec[3]))
               + lambda_init)
        coefs = jnp.stack([jnp.ones((), F32), -lam])
        dscale = DIFF_QK_DIM ** -0.5 * LOG2E

        def diff_q(q):
            return jnp.stack([_head_major(q[..., b_, :] * dscale) for b_ in range(2)])[:, :, None].astype(BF16)

        def diff_k(k):
            return jnp.stack([_head_major(k[..., b_, :]) for b_ in range(2)]).astype(BF16)

        kka = jnp.concatenate([kac, ka], axis=1)
        vva = jnp.concatenate([vac, va], axis=1)
        oa = dense_attention(diff_q(qa), diff_k(kka), _head_major_t(vva).astype(BF16), coefs, None,
                             DIFF_HEADS, ATTN_TQ_DIFF, _key_chunk(C + L))
        oa = _from_q_layout(oa, B, DIFF_HEADS, 1)

        def gqa_heads(t, n_q, n_kv, g_qk):
            B_, L_, _ = t.shape
            q, k, v = jnp.split(t, [n_q * HEAD_DIM, (n_q + n_kv) * HEAD_DIM], axis=-1)
            q = rmsnorm(q.reshape(B_, L_, n_q, HEAD_DIM), g_qk[0])
            k = rmsnorm(k.reshape(B_, L_, n_kv, HEAD_DIM), g_qk[1])
            return q, k, v.reshape(B_, L_, n_kv, HEAD_DIM)

        hscale = HEAD_DIM ** -0.5
        hscale2 = hscale * LOG2E
        qw, kw, vw = gqa_heads(pw, WIN_HEADS, WIN_KV_HEADS, g_qk_win[i])
        qwc, kwc, vwc = gqa_heads(pwc, WIN_HEADS, WIN_KV_HEADS, g_qk_win[i])
        qw, kw = apply_rope(qw, *rope_full), apply_rope(kw, *rope_full)
        zblk = jnp.zeros((B, BLOCK, WIN_KV_HEADS, HEAD_DIM), F32)
        kpad = _head_major(jnp.concatenate([kwc, zblk, kw, zblk], axis=1)).astype(BF16)
        vpad = _head_major(jnp.concatenate([vwc, zblk, vw, zblk], axis=1)).astype(BF16)
        Rw = WIN_HEADS // WIN_KV_HEADS
        ob = window_attention(_q_layout(qw * hscale, WIN_KV_HEADS, Rw).astype(BF16), kpad, vpad,
                              sink_win[i], WIN_KV_HEADS, C, L)
        ob = _from_q_layout(ob, B, WIN_KV_HEADS, Rw)

        qg, kg, vg = gqa_heads(pg, GLOB_HEADS, GLOB_KV_HEADS, g_qk_glob[i])
        qgc, kgc, vgc = gqa_heads(pgc, GLOB_HEADS, GLOB_KV_HEADS, g_qk_glob[i])
        qg, kg = apply_rope(qg, *rope_full), apply_rope(kg, *rope_full)
        Rg = GLOB_HEADS // GLOB_KV_HEADS
        one = jnp.ones((1,), F32)
        kkg = _head_major(jnp.concatenate([kgc, kg], axis=1)).astype(BF16)[None]
        vvg = _head_major_t(jnp.concatenate([vgc, vg], axis=1)).astype(BF16)
        og = dense_attention(_q_layout(qg * hscale2, GLOB_KV_HEADS, Rg).astype(BF16)[None], kkg, vvg,
                             one, None, GLOB_KV_HEADS, ATTN_TQ_GLOB, _key_chunk(C + L))
        og = _from_q_layout(og, B, GLOB_KV_HEADS, Rg)

        filt = (hy_w1[i], hy_b1[i], hy_w2[i], hy_b2[i], hy_w3[i], hy_b3[i], hy_freq[i])
        oh = hyena(ph, hy_conv_w[i], hy_conv_b[i], filt, hy_bias[i])

        mixed = [jnp.concatenate([oa, ob, og, oh], axis=-1).reshape(n_lat, D)]
        if want_ctx:
            oac = dense_attention(diff_q(qac), diff_k(kac), _head_major_t(vac).astype(BF16), coefs, None,
                                  DIFF_HEADS, C, C)
            oac = _from_q_layout(oac, B, DIFF_HEADS, 1)
            obc = dense_attention(_q_layout(qwc * hscale2, WIN_KV_HEADS, Rw).astype(BF16)[None],
                                  _head_major(kwc).astype(BF16)[None], _head_major_t(vwc).astype(BF16),
                                  one, sink_win[i], WIN_KV_HEADS, C, C)
            obc = _from_q_layout(obc, B, WIN_KV_HEADS, Rw)
            ogc = dense_attention(_q_layout(qgc * hscale2, GLOB_KV_HEADS, Rg).astype(BF16)[None],
                                  _head_major(kgc).astype(BF16)[None], _head_major_t(vgc).astype(BF16),
                                  one, None, GLOB_KV_HEADS, C, C)
            ogc = _from_q_layout(ogc, B, GLOB_KV_HEADS, Rg)
            ohc = hyena_small(phc, hy_conv_w[i], hy_conv_b[i], filt, hy_bias[i])
            mixed.append(jnp.concatenate([oac, obc, ogc, ohc], axis=-1).reshape(n_ctx, D))
        o = jnp.concatenate(mixed, axis=0)
        n_rows = o.shape[0]

        oh_ = rmsnorm(o.reshape(n_rows, N_OUT_HEADS, HEAD_DIM), g_mix_out[i].reshape(N_OUT_HEADS, HEAD_DIM))
        head_scale = jnp.where(jnp.arange(N_OUT_HEADS) < DIFF_HEADS, 1.0 - lambda_init, 1.0)[:, None]
        om = (oh_ * head_scale.astype(F32)).reshape(n_rows, D)
        xcur = mm_residual(om.astype(BF16), w_out[i].astype(BF16), xall, g1, L, TM)

        qp, h2 = normmod_mm(xcur, g_norm_ffn[i], sh2, s2, peer_wq[i].astype(BF16), L, TM, BF16, True)
        xall = peer(qp, h2, peer_keys[i], peer_u[i].astype(BF16), jnp.transpose(peer_v[i].astype(BF16)),
                    xcur, g2, L, TM)

    return xall[:n_lat].reshape(B, L, D)
```

```python
import functools
import math

import jax
import jax.numpy as jnp
import numpy as np
from jax import lax
from jax.experimental import pallas as pl
from jax.experimental.pallas import tpu as pltpu

F32 = jnp.float32
BF16 = jnp.bfloat16

GRID_W = 64
HEAD_DIM = 64
BLOCK = 128
WINDOW = 128
ROPE_THETA = 10000.0
EPS = 1e-6
NEG_INF = -1e30
N_MOD = 6
DIFF_HEADS = 4
DIFF_QK_DIM = 32
WIN_HEADS = 4
WIN_KV_HEADS = 2
GLOB_HEADS = 4
GLOB_KV_HEADS = 2
N_OUT_HEADS = 16
HY_CH = 256
HY_ORDER = 2
HY_BANDS = 16
HY_TARGET = 1e-2
HY_FAST = 0.3
HY_SLOW = 1.5
W_DIFF = 768
W_WIN = 512
W_GLOB = 512
W_HY = 768
PEER_HEADS = 8
N_KEYS = 128
PEER_TOPK = 16

VMEM_LIMIT = 56 * 1024 * 1024
ATTN_TQ_DIFF = 512
ATTN_TQ_GLOB = 256
KEY_CHUNK_CAP = 1408
HY_CH_BLOCK = 16


def _params(*sem):
    return pltpu.CompilerParams(dimension_semantics=sem, vmem_limit_bytes=VMEM_LIMIT)


def _mm_kernel(a_ref, b_ref, o_ref):
    o_ref[...] = jnp.dot(a_ref[...], b_ref[...], preferred_element_type=F32)


def mm(a, b, tm, tn):
    M, K = a.shape
    N = b.shape[1]
    return pl.pallas_call(
        _mm_kernel,
        grid=(M // tm, N // tn),
        in_specs=[pl.BlockSpec((tm, K), lambda i, j: (i, 0)),
                  pl.BlockSpec((K, tn), lambda i, j: (0, j))],
        out_specs=pl.BlockSpec((tm, tn), lambda i, j: (i, j)),
        out_shape=jax.ShapeDtypeStruct((M, N), F32),
        compiler_params=_params("parallel", "arbitrary"),
        name="mm",
    )(a, b)


def _seg_spec(rows_per_seg, tm, nseg, D):
    blocks_per_seg = rows_per_seg // tm
    return pl.BlockSpec((1, 1, D), lambda i: (jnp.minimum(i // blocks_per_seg, nseg - 1), 0, 0))


def _normmod_mm_kernel(x_ref, g_ref, sh_ref, sc_ref, w_ref, *outs, widths, with_h):
    x = x_ref[...]
    y = x * lax.rsqrt(jnp.mean(x * x, axis=-1, keepdims=True) + EPS) * g_ref[...]
    h = (y * (1.0 + sc_ref[0]) + sh_ref[0]).astype(BF16)
    if with_h:
        outs[-1][...] = h
    r = jnp.dot(h, w_ref[...], preferred_element_type=F32)
    lo = 0
    for o_ref, n in zip(outs, widths):
        o_ref[...] = r[:, lo:lo + n].astype(o_ref.dtype)
        lo += n


def normmod_mm(x, g, shift, scale, w, rows_per_seg, tm, out_dtype, with_h, widths=None):
    M, D = x.shape
    N = w.shape[1]
    widths = (N,) if widths is None else tuple(widths)
    nseg = shift.shape[0]
    seg = _seg_spec(rows_per_seg, tm, nseg, D)
    row_block = lambda n: pl.BlockSpec((tm, n), lambda i: (i, 0))
    out_specs = [row_block(n) for n in widths]
    out_shape = [jax.ShapeDtypeStruct((M, n), out_dtype) for n in widths]
    if with_h:
        out_specs.append(row_block(D))
        out_shape.append(jax.ShapeDtypeStruct((M, D), BF16))
    return pl.pallas_call(
        functools.partial(_normmod_mm_kernel, widths=widths, with_h=with_h),
        grid=(M // tm,),
        in_specs=[row_block(D), pl.BlockSpec((1, D), lambda i: (0, 0)), seg, seg,
                  pl.BlockSpec((D, N), lambda i: (0, 0))],
        out_specs=out_specs,
        out_shape=out_shape,
        compiler_params=_params("parallel"),
        name="normmod_mm",
    )(x, g.reshape(1, D), shift.reshape(nseg, 1, D), scale.reshape(nseg, 1, D), w)


def _mm_residual_kernel(a_ref, w_ref, x_ref, gate_ref, o_ref):
    o_ref[...] = x_ref[...] + gate_ref[0] * jnp.dot(a_ref[...], w_ref[...], preferred_element_type=F32)


def mm_residual(a, w, x, gate, rows_per_seg, tm):
    M, K = a.shape
    N = w.shape[1]
    nseg = gate.shape[0]
    row_block = lambda n: pl.BlockSpec((tm, n), lambda i: (i, 0))
    return pl.pallas_call(
        _mm_residual_kernel,
        grid=(M // tm,),
        in_specs=[row_block(K), pl.BlockSpec((K, N), lambda i: (0, 0)), row_block(N),
                  _seg_spec(rows_per_seg, tm, nseg, N)],
        out_specs=row_block(N),
        out_shape=jax.ShapeDtypeStruct((M, N), F32),
        compiler_params=_params("parallel"),
        name="mm_residual",
    )(a, w, x, gate.reshape(nseg, 1, N))


def _scores_kernel(k_ref, q_ref, o_ref):
    d = k_ref.shape[-1]
    for p in range(k_ref.shape[0]):
        o_ref[p] = lax.dot_general(k_ref[p], q_ref[:, p * d:(p + 1) * d], (((1,), (1,)), ((), ())),
                                   preferred_element_type=F32)


def keys_times_qT(keys, q, tn):
    P, n, d = keys.shape
    T = q.shape[0]
    return pl.pallas_call(
        _scores_kernel,
        grid=(T // tn,),
        in_specs=[pl.BlockSpec((P, n, d), lambda j: (0, 0, 0)),
                  pl.BlockSpec((tn, P * d), lambda j: (j, 0))],
        out_specs=pl.BlockSpec((P, n, tn), lambda j: (0, 0, j)),
        out_shape=jax.ShapeDtypeStruct((P, n, T), F32),
        compiler_params=_params("parallel"),
        name="peer_scores_t",
    )(keys, q)


LOG2E = math.log2(math.e)


def _dense_attn_kernel(coef_ref, sink_ref, q_ref, k_ref, vt_ref, o_ref, sa_ref, sb_ref, *, nbr, R, G,
                       HP, tq, tk, nk, has_sink):
    cols = R * tq
    d = q_ref.shape[-1]
    dv = vt_ref.shape[1]
    g0 = (pl.program_id(0) % (G // HP)) * HP
    nt = (((1,), (1,)), ((), ()))
    bufs = (sa_ref, sb_ref)
    heads = []
    for hp in range(HP):
        out = jnp.zeros((dv, cols), F32)
        for br in range(nbr):
            q = q_ref[br, hp].reshape(cols, d)
            if has_sink:
                m = jnp.concatenate([jnp.full((1, tq), sink_ref[(g0 + hp) * R + r] * LOG2E, F32)
                                     for r in range(R)], axis=1)
                l = jnp.ones((1, cols), F32)
            else:
                m = jnp.full((1, cols), NEG_INF, F32)
                l = jnp.zeros((1, cols), F32)
            acc = jnp.zeros((dv, cols), F32)
            bufs[0][...] = lax.dot_general(k_ref[br, hp, 0:tk, :], q, nt, preferred_element_type=F32)
            for j in range(nk):
                if j + 1 < nk:
                    bufs[(j + 1) % 2][...] = lax.dot_general(k_ref[br, hp, (j + 1) * tk:(j + 2) * tk, :], q,
                                                             nt, preferred_element_type=F32)
                s = bufs[j % 2][...]
                m_new = jnp.maximum(m, jnp.max(s, axis=0, keepdims=True))
                alpha = jnp.exp2(m - m_new)
                p = jnp.exp2(s - m_new)
                l = alpha * l + jnp.sum(p, axis=0, keepdims=True)
                acc = alpha * acc + jnp.dot(vt_ref[hp, :, j * tk:(j + 1) * tk], p.astype(BF16),
                                            preferred_element_type=F32)
                m = m_new
            out = out + coef_ref[br] * (acc / l)
        out_t = out.T
        heads += [out_t[r * tq:(r + 1) * tq] for r in range(R)]
    o_ref[0] = jnp.concatenate(heads, axis=1)


def dense_attention(q, k, vt, coefs, sink, B, G, HP, tq, tk):
    nbr, BG, R, Lq, d = q.shape
    Lk = k.shape[2]
    dv = vt.shape[1]
    assert (HP * R * dv) % 128 == 0 and G % HP == 0
    has_sink = sink is not None
    if sink is None:
        sink = jnp.zeros((1,), F32)
    kern = functools.partial(_dense_attn_kernel, nbr=nbr, R=R, G=G, HP=HP, tq=tq, tk=tk, nk=Lk // tk,
                             has_sink=has_sink)
    gp = G // HP
    return pl.pallas_call(
        kern,
        grid=(BG // HP, Lq // tq),
        in_specs=[pl.BlockSpec(memory_space=pltpu.SMEM),
                  pl.BlockSpec(memory_space=pltpu.SMEM),
                  pl.BlockSpec((nbr, HP, R, tq, d), lambda b, i: (0, b, 0, i, 0)),
                  pl.BlockSpec((nbr, HP, Lk, d), lambda b, i: (0, b, 0, 0)),
                  pl.BlockSpec((HP, dv, Lk), lambda b, i: (b, 0, 0))],
        out_specs=pl.BlockSpec((1, tq, HP * R * dv), lambda b, i: (b // gp, i, b % gp)),
        out_shape=jax.ShapeDtypeStruct((B, Lq, G * R * dv), F32),
        scratch_shapes=[pltpu.VMEM((tk, R * tq), F32), pltpu.VMEM((tk, R * tq), F32)],
        compiler_params=_params("parallel", "arbitrary"),
        name="dense_attn",
    )(coefs.astype(F32), sink.astype(F32), q, k, vt)


def _window_attn_kernel(sink_ref, q_ref, k_ref, v_ref, o_ref, *, R, G, C, L):
    rows = R * BLOCK
    d = q_ref.shape[-1]
    g = pl.program_id(0) % G
    n = pl.program_id(1)
    q = q_ref[0].reshape(rows, d)
    start = pl.multiple_of(C + n * BLOCK, BLOCK)
    kc = k_ref[0, 0:C, :]
    vc = v_ref[0, 0:C, :]
    kl = k_ref[0, pl.ds(start, 3 * BLOCK), :]
    vl = v_ref[0, pl.ds(start, 3 * BLOCK), :]
    nt = (((1,), (1,)), ((), ()))
    s_ctx = lax.dot_general(q, kc, nt, preferred_element_type=F32)
    s_loc = lax.dot_general(q, kl, nt, preferred_element_type=F32)
    qi = lax.broadcasted_iota(jnp.int32, (rows, 3 * BLOCK), 0) & (BLOCK - 1)
    kj = lax.broadcasted_iota(jnp.int32, (rows, 3 * BLOCK), 1)
    kpos = (n - 1) * BLOCK + kj
    valid = (jnp.abs(kj - BLOCK - qi) <= WINDOW) & (kpos >= 0) & (kpos < L)
    s_loc = jnp.where(valid, s_loc, NEG_INF)
    sink = jnp.concatenate([jnp.full((BLOCK, 1), sink_ref[g * R + r], F32) for r in range(R)], axis=0)
    m = jnp.maximum(jnp.maximum(jnp.max(s_ctx, axis=-1, keepdims=True),
                                jnp.max(s_loc, axis=-1, keepdims=True)), sink)
    e_ctx = jnp.exp(s_ctx - m)
    e_loc = jnp.exp(s_loc - m)
    den = (jnp.sum(e_ctx, axis=-1, keepdims=True) + jnp.sum(e_loc, axis=-1, keepdims=True)
           + jnp.exp(sink - m))
    inv = 1.0 / den
    o = (jnp.dot((e_ctx * inv).astype(BF16), vc, preferred_element_type=F32)
         + jnp.dot((e_loc * inv).astype(BF16), vl, preferred_element_type=F32))
    o_ref[0] = jnp.concatenate([o[r * BLOCK:(r + 1) * BLOCK] for r in range(R)], axis=1)


def window_attention(q, kpad, vpad, sink, B, G, C, L):
    BG, R, _, d = q.shape
    Lp = kpad.shape[1]
    kern = functools.partial(_window_attn_kernel, R=R, G=G, C=C, L=L)
    return pl.pallas_call(
        kern,
        grid=(BG, L // BLOCK),
        in_specs=[pl.BlockSpec(memory_space=pltpu.SMEM),
                  pl.BlockSpec((1, R, BLOCK, d), lambda b, i: (b, 0, i, 0)),
                  pl.BlockSpec((1, Lp, d), lambda b, i: (b, 0, 0)),
                  pl.BlockSpec((1, Lp, d), lambda b, i: (b, 0, 0))],
        out_specs=pl.BlockSpec((1, BLOCK, R * d), lambda b, i: (b // G, i, b % G)),
        out_shape=jax.ShapeDtypeStruct((B, L, G * R * d), F32),
        compiler_params=_params("parallel", "arbitrary"),
        name="window_attn",
    )(sink.astype(F32), q, kpad, vpad)


_CAND_PAIRS = tuple((r, s) for r in range(PEER_TOPK) for s in range(PEER_TOPK)
                    if (r + 1) * (s + 1) <= PEER_TOPK)
_CAND_ROWS = -(-len(_CAND_PAIRS) // 8) * 8


def _top_rows(vs, n):
    vs = list(vs)
    iota = lax.broadcasted_iota(jnp.int32, vs[0].shape, 0).astype(F32)
    rows = [[] for _ in vs]
    for r in range(n):
        for i, v in enumerate(vs):
            m = jnp.max(v, axis=0, keepdims=True)
            rows[i].append(m)
            if r + 1 < n:
                first = jnp.min(jnp.where(v == m, iota, float(v.shape[0])), axis=0, keepdims=True)
                vs[i] = jnp.where(iota == first, -jnp.inf, v)
    return rows


def _peer_select_kernel(s_ref, ea_ref, q_ref, eb_ref, code_ref, c_ref, *, tm):
    n_lane = tm // 128

    def body(it, carry):
        h = it // n_lane
        lanes = pl.ds(pl.multiple_of((it % n_lane) * 128, 128), 128)
        a = s_ref[h, 0, :, lanes]
        b = s_ref[h, 1, :, lanes]
        ta, tb = _top_rows((a, b), PEER_TOPK)
        c_ref[...] = jnp.full(c_ref.shape, -jnp.inf, F32)
        for k, (r, s) in enumerate(_CAND_PAIRS):
            c_ref[k:k + 1, :] = ta[r] + tb[s]
        best, = _top_rows((c_ref[...],), PEER_TOPK)
        tau = best[PEER_TOPK - 1]
        zsum = jnp.zeros_like(best[0])
        for bk in best:
            zsum = zsum + jnp.exp(bk - best[0])
        code_b = jnp.zeros(b.shape, F32)
        for s in range(PEER_TOPK):
            code_b = code_b + jnp.where(tb[s] > b, 1.0, 0.0)
        code_t = [jnp.zeros(tau.shape, F32)]
        for s in range(1, PEER_TOPK):
            code_t.append(jnp.where(tb[s] == tb[s - 1], code_t[s - 1], float(s)))
        q = jnp.full(a.shape, -1.0, F32)
        for r in reversed(range(PEER_TOPK)):
            q_r = jnp.full(tau.shape, -1.0, F32)
            for s in range(PEER_TOPK // (r + 1)):
                q_r = jnp.maximum(q_r, jnp.where(ta[r] + tb[s] >= tau, code_t[s], -1.0))
            q = jnp.where(a == ta[r], q_r, q)
        q_ref[h, :, lanes] = q
        code_ref[h, :, lanes] = code_b.astype(BF16)
        ea_ref[h, :, lanes] = jnp.exp(a - ta[0]) / zsum
        eb_ref[h, :, lanes] = jnp.exp(b - tb[0]).astype(BF16)
        return carry

    lax.fori_loop(0, PEER_HEADS * n_lane, body, 0)


def peer_select(st, tm):
    H, _, n, T = st.shape
    kern = functools.partial(_peer_select_kernel, tm=tm)
    return pl.pallas_call(
        kern,
        grid=(T // tm,),
        in_specs=[pl.BlockSpec((H, 2, n, tm), lambda t: (0, 0, 0, t))],
        out_specs=[pl.BlockSpec((H, n, tm), lambda t: (0, 0, t))] * 4,
        out_shape=[jax.ShapeDtypeStruct((H, n, T), dt) for dt in (F32, F32, BF16, BF16)],
        scratch_shapes=[pltpu.VMEM((_CAND_ROWS, 128), F32)],
        compiler_params=_params("parallel"),
        name="peer_select",
    )(st)


def _gelu_tanh(x):
    k = -2.0 * math.sqrt(2.0 / math.pi) * LOG2E
    return x / (1.0 + jnp.exp2(x * (k + (k * 0.044715) * (x * x))))


def _peer_kernel(x_ref, u_ref, vt_ref, ea_ref, q_ref, eb_ref, code_ref, res_ref, gate_ref, o_ref,
                 acc_ref, act_ref, g_ref, *, ni, tm):
    c = pl.program_id(1)

    @pl.when(c == 0)
    def _():
        acc_ref[...] = jnp.zeros_like(acc_ref)

    act_ref[...] = _gelu_tanh(lax.dot_general(u_ref[...], x_ref[...], (((1,), (1,)), ((), ())),
                                              preferred_element_type=F32))

    for ts in range(tm // 128):
        lanes = slice(ts * 128, (ts + 1) * 128)
        for ii in range(ni):
            rows = slice(ii * N_KEYS, (ii + 1) * N_KEYS)
            w = jnp.zeros((N_KEYS, 128), BF16)
            for h in range(PEER_HEADS):
                sel = code_ref[h, :, lanes] <= q_ref[h, ii:ii + 1, lanes].astype(BF16)
                gate = ea_ref[h, ii:ii + 1, lanes].astype(BF16) * eb_ref[h, :, lanes]
                w = w + jnp.where(sel, gate, jnp.zeros_like(gate))
            g_ref[rows, lanes] = w * act_ref[rows, lanes].astype(BF16)

    acc_ref[...] += jnp.dot(vt_ref[...], g_ref[...], preferred_element_type=F32)

    @pl.when(c == pl.num_programs(1) - 1)
    def _():
        o_ref[...] = res_ref[...] + gate_ref[0] * acc_ref[...].T


def peer_dense(x, u, vt, eat, qt, ebt, codet, res, gate, rows_per_seg, tm, ni):
    T, D = x.shape
    E = u.shape[0]
    ec = ni * N_KEYS
    kern = functools.partial(_peer_kernel, ni=ni, tm=tm)
    nseg = gate.shape[0]
    blocks_per_seg = rows_per_seg // tm
    rows_of_chunk = pl.BlockSpec((PEER_HEADS, ni, tm), lambda t, c: (0, c, t))
    return pl.pallas_call(
        kern,
        grid=(T // tm, E // ec),
        in_specs=[pl.BlockSpec((tm, D), lambda t, c: (t, 0)),
                  pl.BlockSpec((ec, D), lambda t, c: (c, 0)),
                  pl.BlockSpec((D, ec), lambda t, c: (0, c)),
                  rows_of_chunk,
                  rows_of_chunk,
                  pl.BlockSpec((PEER_HEADS, N_KEYS, tm), lambda t, c: (0, 0, t)),
                  pl.BlockSpec((PEER_HEADS, N_KEYS, tm), lambda t, c: (0, 0, t)),
                  pl.BlockSpec((tm, D), lambda t, c: (t, 0)),
                  pl.BlockSpec((1, 1, D), lambda t, c: (jnp.minimum(t // blocks_per_seg, nseg - 1), 0, 0))],
        out_specs=pl.BlockSpec((tm, D), lambda t, c: (t, 0)),
        out_shape=jax.ShapeDtypeStruct((T, D), F32),
        scratch_shapes=[pltpu.VMEM((D, tm), F32),
                        pltpu.VMEM((ec, tm), F32),
                        pltpu.VMEM((ec, tm), BF16)],
        compiler_params=_params("parallel", "arbitrary"),
        name="peer_dense",
    )(x, u, vt, eat, qt, ebt, codet, res, gate.reshape(nseg, 1, D))


def peer(qp, h_bf, keys, u_bf, vt_bf, res, gate, rows_per_seg, tm):
    T = qp.shape[0]
    dk = keys.shape[-1]
    kflat = keys.reshape(PEER_HEADS * 2, N_KEYS, dk).astype(BF16)
    st = keys_times_qT(kflat, qp, tm).reshape(PEER_HEADS, 2, N_KEYS, T)
    eat, qt, ebt, codet = peer_select(st, 256)
    return peer_dense(h_bf, u_bf, vt_bf, eat, qt, ebt, codet, res, gate, rows_per_seg, tm, 8)


def rmsnorm(x, g):
    return x * lax.rsqrt(jnp.mean(x * x, axis=-1, keepdims=True) + EPS) * g


def axial_rope(n_tok, dim):
    rows = n_tok // GRID_W
    row = jnp.repeat(jnp.arange(rows, dtype=F32), GRID_W)
    col = jnp.tile(jnp.arange(GRID_W, dtype=F32), rows)
    axis_dim = dim // 2
    inv_freq = ROPE_THETA ** (-jnp.arange(0, axis_dim, 2, dtype=F32) / axis_dim)
    ang = jnp.concatenate([row[:, None] * inv_freq, col[:, None] * inv_freq], axis=-1)
    return jnp.cos(ang), jnp.sin(ang)


def apply_rope(x, cos, sin):
    shape = (1, x.shape[1]) + (1,) * (x.ndim - 3) + (cos.shape[-1],)
    c, s = cos.reshape(shape), sin.reshape(shape)
    x1, x2 = jnp.split(x, 2, axis=-1)
    return jnp.concatenate([x1 * c - x2 * s, x1 * s + x2 * c], axis=-1)


def short_conv(u, w, b):
    up = jnp.pad(u, ((0, 0), (1, 1), (0, 0)))
    return up[:, :-2] * w[0] + up[:, 1:-1] * w[1] + up[:, 2:] * w[2] + b


def hyena_filters(n_tok, w1, b1, w2, b2, w3, b3, freq):
    hp = lax.Precision.HIGHEST
    t = jnp.linspace(0.0, 1.0, n_tok, dtype=F32)[:, None]
    w = (2.0 * math.pi / n_tok) * jnp.arange(n_tok, dtype=F32)[:, None]
    f = jnp.linspace(1e-4, HY_BANDS - 1, HY_BANDS, dtype=F32)[None, :]
    feat = jnp.concatenate([t, jnp.cos(f * w), -jnp.sin(f * w)], axis=-1)
    hdn = jnp.sin(freq * (jnp.dot(feat, w1, precision=hp) + b1))
    hdn = jnp.sin(freq * (jnp.dot(hdn, w2, precision=hp) + b2))
    filt = (jnp.dot(hdn, w3, precision=hp) + b3).reshape(n_tok, HY_ORDER, 2, HY_CH)
    deltas = jnp.abs(jnp.linspace(math.log(HY_TARGET) / HY_SLOW, math.log(HY_TARGET) / HY_FAST, HY_CH,
                                  dtype=F32))
    filt = filt * jnp.exp(-t[:, :, None, None] * deltas)
    fwd, bwd = filt[:, :, 0], filt[:, :, 1]
    kfull = jnp.concatenate([fwd, jnp.zeros_like(fwd[:1]), bwd[1:][::-1]], axis=0)
    return kfull / jnp.sum(jnp.abs(kfull), axis=0, keepdims=True)


def hyena_filters_t(n_tok, w1, b1, w2, b2, w3, b3, freq):
    hp = lax.Precision.HIGHEST
    t = jnp.linspace(0.0, 1.0, n_tok, dtype=F32)[:, None]
    w = (2.0 * math.pi / n_tok) * jnp.arange(n_tok, dtype=F32)[:, None]
    f = jnp.linspace(1e-4, HY_BANDS - 1, HY_BANDS, dtype=F32)[None, :]
    feat = jnp.concatenate([t, jnp.cos(f * w), -jnp.sin(f * w)], axis=-1)
    hdn = jnp.sin(freq * (jnp.dot(feat, w1, precision=hp) + b1))
    hdn = jnp.sin(freq * (jnp.dot(hdn, w2, precision=hp) + b2))
    deltas = jnp.abs(jnp.linspace(math.log(HY_TARGET) / HY_SLOW, math.log(HY_TARGET) / HY_FAST, HY_CH,
                                  dtype=F32))
    w3t = jnp.transpose(w3.reshape(-1, HY_ORDER, 2, HY_CH), (2, 1, 3, 0))
    b3t = jnp.transpose(b3.reshape(HY_ORDER, 2, HY_CH), (1, 0, 2))[..., None]

    def half(d, hidden, times):
        decay = jnp.exp(-deltas[:, None] * times[None, :])
        return (jnp.einsum('ock,nk->ocn', w3t[d], hidden, precision=hp) + b3t[d]) * decay

    fwd = half(0, hdn, t[:, 0])
    bwd_rev = half(1, hdn[::-1], t[::-1, 0])
    kfull = jnp.concatenate([fwd, jnp.zeros_like(fwd[..., :1]), bwd_rev[..., :n_tok - 1]], axis=-1)
    kfull = kfull / jnp.sum(jnp.abs(kfull), axis=-1, keepdims=True)
    return kfull.reshape(HY_ORDER * HY_CH, 2 * n_tok)


def long_conv(z, kf):
    n = z.shape[1]
    zf = jnp.fft.rfft(z, n=2 * n, axis=1)
    hf = jnp.fft.rfft(kf, n=2 * n, axis=0)
    return jnp.fft.irfft(zf * hf[None], n=2 * n, axis=1)[:, :n]


def hyena_small(u, conv_w, conv_b, filter_params, bias):
    n = u.shape[1]
    u = short_conv(u, conv_w, conv_b)
    v, x1, x2 = jnp.split(u, 3, axis=-1)
    kfull = hyena_filters(n, *filter_params)
    z = x1 * (long_conv(v, kfull[:, 0]) + bias[0] * v)
    return x2 * (long_conv(z, kfull[:, 1]) + bias[1] * z)


def _hi_lo(x):
    hi = x.astype(BF16)
    return hi, (x - hi.astype(F32)).astype(BF16)


def _dot3(a, b):
    d = lambda x, y: jnp.dot(x, y, preferred_element_type=F32)
    return d(a[0], b[0]) + d(a[1], b[0]) + d(a[0], b[1])


def _dft_constants(n_tok):
    n = 2 * n_tok
    n1 = n // 128
    a1 = 2.0 * np.pi * np.outer(np.arange(n1), np.arange(n1)) / n1
    a2 = 2.0 * np.pi * np.outer(np.arange(128), np.arange(128)) / 128
    at = 2.0 * np.pi * np.outer(np.arange(n1), np.arange(128)) / n
    c1, s1, c2, s2 = np.cos(a1), np.sin(a1), np.cos(a2), np.sin(a2)
    pair = lambda m: _hi_lo(jnp.asarray(m, F32))
    return dict(
        f1_half=pair(np.concatenate([c1[:, :n1 // 2], -s1[:, :n1 // 2]], axis=0)),
        f1_full=pair(np.concatenate([c1, -s1], axis=0)),
        m_fwd=pair(np.block([[c2, -s2], [s2, c2]])),
        m_inv=pair(np.block([[c2, s2], [-s2, c2]])),
        g_half=pair(np.concatenate([c1[:n1 // 2], -s1[:n1 // 2]], axis=1)),
        tr=jnp.asarray(np.cos(at), F32), ti=jnp.asarray(-np.sin(at), F32))


def _dft_fwd(seqs, f1, tr, ti, m_fwd):
    n1 = tr.shape[0]
    y = _dot3(f1, _hi_lo(jnp.concatenate(seqs, axis=1)))
    rows = []
    for k in range(len(seqs)):
        yr, yi = y[:n1, k * 128:(k + 1) * 128], y[n1:, k * 128:(k + 1) * 128]
        rows.append(jnp.concatenate([yr * tr - yi * ti, yr * ti + yi * tr], axis=1))
    return _dot3(_hi_lo(jnp.concatenate(rows, axis=0)), m_fwd)


def _dft_inv_half(p, g_half, tr, ti, m_inv):
    n1 = tr.shape[0]
    u = _dot3(_hi_lo(p), m_inv)
    cols = []
    for k in range(p.shape[0] // n1):
        ur, ui = u[k * n1:(k + 1) * n1, :128], u[k * n1:(k + 1) * n1, 128:]
        cols.append(jnp.concatenate([ur * tr + ui * ti, ui * tr - ur * ti], axis=0))
    return _dot3(g_half, _hi_lo(jnp.concatenate(cols, axis=1))) * (1.0 / (n1 * 128))


def _pairs(refs):
    return (refs[0][...], refs[1][...])


def _spectrum_kernel(a_ref, f1h, f1l, tr_ref, ti_ref, mh, ml, o_ref):
    cb, n1 = a_ref.shape[0], tr_ref.shape[0]
    x = _dft_fwd([a_ref[k] for k in range(cb)], _pairs((f1h, f1l)), tr_ref[...], ti_ref[...],
                 _pairs((mh, ml)))
    o_ref[...] = x.reshape(cb, n1, 256)


def filter_spectrum(kf, consts, cb):
    items, n1, _ = kf.shape
    full = lambda shape: pl.BlockSpec(shape, lambda i: (0,) * len(shape))
    f1, m = consts["f1_full"], consts["m_fwd"]
    return pl.pallas_call(
        _spectrum_kernel,
        grid=(items // cb,),
        in_specs=[pl.BlockSpec((cb, n1, 128), lambda i: (i, 0, 0)),
                  full(f1[0].shape), full(f1[1].shape), full((n1, 128)), full((n1, 128)),
                  full(m[0].shape), full(m[1].shape)],
        out_specs=pl.BlockSpec((cb, n1, 256), lambda i: (i, 0, 0)),
        out_shape=jax.ShapeDtypeStruct((items, n1, 256), F32),
        compiler_params=_params("parallel"),
        name="filter_spectrum",
    )(kf, f1[0], f1[1], consts["tr"], consts["ti"], m[0], m[1])


def _conv_gate_kernel(bias_ref, u_ref, g_ref, h_ref, f1h, f1l, tr_ref, ti_ref, mfh, mfl, mih, mil, gh, gl,
                      o_ref, *, cb):
    tr, ti = tr_ref[...], ti_ref[...]
    n1 = tr.shape[0]
    c0 = pl.program_id(1) * cb
    u = [u_ref[0, k] for k in range(cb)]
    x = _dft_fwd(u, _pairs((f1h, f1l)), tr, ti, _pairs((mfh, mfl)))
    prod = []
    for k in range(cb):
        xr, xi = x[k * n1:(k + 1) * n1, :128], x[k * n1:(k + 1) * n1, 128:]
        hr, hi = h_ref[k, :, :128], h_ref[k, :, 128:]
        prod.append(jnp.concatenate([xr * hr - xi * hi, xr * hi + xi * hr], axis=1))
    y = _dft_inv_half(jnp.concatenate(prod, axis=0), _pairs((gh, gl)), tr, ti, _pairs((mih, mil)))
    for k in range(cb):
        o_ref[0, k] = g_ref[0, k] * (y[:, k * 128:(k + 1) * 128] + bias_ref[c0 + k] * u[k])


def conv_gate(u, gate, spec, bias, consts, cb):
    B, C, half, _ = u.shape
    n1 = 2 * half
    full = lambda shape: pl.BlockSpec(shape, lambda b, c: (0,) * len(shape))
    seq = pl.BlockSpec((1, cb, half, 128), lambda b, c: (b, c, 0, 0))
    mats = [*consts["f1_half"], consts["tr"], consts["ti"], *consts["m_fwd"], *consts["m_inv"],
            *consts["g_half"]]
    return pl.pallas_call(
        functools.partial(_conv_gate_kernel, cb=cb),
        grid=(B, C // cb),
        in_specs=[pl.BlockSpec(memory_space=pltpu.SMEM), seq, seq,
                  pl.BlockSpec((cb, n1, 256), lambda b, c: (c, 0, 0))] + [full(m.shape) for m in mats],
        out_specs=seq,
        out_shape=jax.ShapeDtypeStruct(u.shape, F32),
        compiler_params=_params("parallel", "arbitrary"),
        name="conv_gate",
    )(bias.astype(F32), u, gate, spec, *mats)


def hyena(u, conv_w, conv_b, filter_params, bias):
    B, n, _ = u.shape
    half = n // 128
    consts = _dft_constants(n)
    u = short_conv(u, conv_w, conv_b)
    seqs = jnp.transpose(u, (0, 2, 1)).reshape(B, 3, HY_CH, half, 128)
    v, x1, x2 = seqs[:, 0], seqs[:, 1], seqs[:, 2]
    kf = hyena_filters_t(n, *filter_params).reshape(HY_ORDER * HY_CH, 2 * half, 128)
    spec = filter_spectrum(kf, consts, HY_CH_BLOCK).reshape(HY_ORDER, HY_CH, 2 * half, 256)
    z = conv_gate(v, x1, spec[0], bias[0], consts, HY_CH_BLOCK)
    o = conv_gate(z, x2, spec[1], bias[1], consts, HY_CH_BLOCK)
    return jnp.transpose(o.reshape(B, HY_CH, n), (0, 2, 1))


def _head_major(t):
    B, L, H, d = t.shape
    return jnp.transpose(t, (0, 2, 1, 3)).reshape(B * H, L, d)


def _head_major_t(t):
    B, L, H, d = t.shape
    return jnp.transpose(t, (0, 2, 3, 1)).reshape(B * H, d, L)


def _q_layout(t, G, R):
    B, L, _, d = t.shape
    return jnp.transpose(t.reshape(B, L, G, R, d), (0, 2, 3, 1, 4)).reshape(B * G, R, L, d)


def _key_chunk(n_keys):
    for c in range(KEY_CHUNK_CAP, 0, -128):
        if n_keys % c == 0:
            return c
    raise ValueError(f"no key chunk for {n_keys}")


def kernel(x, c, ctx, c_ctx, w_mod, b_mod, g_norm_mix, w_in, g_qk_diff, lambda_diff, g_qk_win, sink_win, g_qk_glob, hy_conv_w, hy_conv_b, hy_w1, hy_b1, hy_w2, hy_b2, hy_w3, hy_b3, hy_freq, hy_bias, g_mix_out, w_out, g_norm_ffn, peer_wq, peer_keys, peer_u, peer_v):
    B, L, D = x.shape
    C = ctx.shape[1]
    depth = w_mod.shape[0]
    TM = 512
    n_lat = B * L
    n_ctx = B * C
    assert L % TM == 0 and n_ctx % TM == 0
    nseg = B + 1

    rope_half = axial_rope(L, DIFF_QK_DIM)
    rope_full = axial_rope(L, HEAD_DIM)
    sc = jnp.concatenate([jax.nn.silu(c), jax.nn.silu(c_ctx)[None]], axis=0)
    sc = jnp.pad(sc, ((0, 8 - nseg), (0, 0))).astype(BF16)

    xall = jnp.concatenate([x.reshape(n_lat, D), ctx.reshape(n_ctx, D)], axis=0)

    for i in range(depth):
        want_ctx = i < depth - 1
        lambda_init = 0.8 - 0.6 * math.exp(-0.3 * i)
        mod = mm(sc, w_mod[i].astype(BF16), 8, 1024)[:nseg] + b_mod[i]
        sh1, s1, g1, sh2, s2, g2 = jnp.split(mod, N_MOD, axis=-1)

        groups = normmod_mm(xall, g_norm_mix[i], sh1, s1, w_in[i].astype(BF16), L, TM, F32, False,
                            widths=(W_DIFF, W_WIN, W_GLOB, W_HY))
        pa, pw, pg, ph = (t[:n_lat].reshape(B, L, -1) for t in groups)
        pac, pwc, pgc, phc = (t[n_lat:].reshape(B, C, -1) for t in groups)

        def diff_heads(t):
            B_, L_, _ = t.shape
            q, k, v = jnp.split(t, 3, axis=-1)
            q = rmsnorm(q.reshape(B_, L_, DIFF_HEADS, 2, DIFF_QK_DIM), g_qk_diff[i, 0])
            k = rmsnorm(k.reshape(B_, L_, DIFF_HEADS, 2, DIFF_QK_DIM), g_qk_diff[i, 1])
            return q, k, v.reshape(B_, L_, DIFF_HEADS, HEAD_DIM)

        qa, ka, va = diff_heads(pa)
        qac, kac, vac = diff_heads(pac)
        qa, ka = apply_rope(qa, *rope_half), apply_rope(ka, *rope_half)
        lam_vec = lambda_diff[i]
        lam = (jnp.exp(jnp.sum(lam_vec[0] * lam_vec[1])) - jnp.exp(jnp.sum(lam_vec[2] * lam_vec[3]))
               + lambda_init)
        coefs = jnp.stack([jnp.ones((), F32), -lam])
        dscale = DIFF_QK_DIM ** -0.5 * LOG2E

        def diff_q(q):
            return jnp.stack([_head_major(q[..., b_, :] * dscale) for b_ in range(2)])[:, :, None].astype(BF16)

        def diff_k(k):
            return jnp.stack([_head_major(k[..., b_, :]) for b_ in range(2)]).astype(BF16)

        kka = jnp.concatenate([kac, ka], axis=1)
        vva = jnp.concatenate([vac, va], axis=1)
        oa = dense_attention(diff_q(qa), diff_k(kka), _head_major_t(vva).astype(BF16), coefs, None,
                             B, DIFF_HEADS, 2, ATTN_TQ_DIFF, _key_chunk(C + L))

        def gqa_heads(t, n_q, n_kv, g_qk):
            B_, L_, _ = t.shape
            q, k, v = jnp.split(t, [n_q * HEAD_DIM, (n_q + n_kv) * HEAD_DIM], axis=-1)
            q = rmsnorm(q.reshape(B_, L_, n_q, HEAD_DIM), g_qk[0])
            k = rmsnorm(k.reshape(B_, L_, n_kv, HEAD_DIM), g_qk[1])
            return q, k, v.reshape(B_, L_, n_kv, HEAD_DIM)

        hscale = HEAD_DIM ** -0.5
        hscale2 = hscale * LOG2E
        qw, kw, vw = gqa_heads(pw, WIN_HEADS, WIN_KV_HEADS, g_qk_win[i])
        qwc, kwc, vwc = gqa_heads(pwc, WIN_HEADS, WIN_KV_HEADS, g_qk_win[i])
        qw, kw = apply_rope(qw, *rope_full), apply_rope(kw, *rope_full)
        zblk = jnp.zeros((B, BLOCK, WIN_KV_HEADS, HEAD_DIM), F32)
        kpad = _head_major(jnp.concatenate([kwc, zblk, kw, zblk], axis=1)).astype(BF16)
        vpad = _head_major(jnp.concatenate([vwc, zblk, vw, zblk], axis=1)).astype(BF16)
        Rw = WIN_HEADS // WIN_KV_HEADS
        ob = window_attention(_q_layout(qw * hscale, WIN_KV_HEADS, Rw).astype(BF16), kpad, vpad,
                              sink_win[i], B, WIN_KV_HEADS, C, L)

        qg, kg, vg = gqa_heads(pg, GLOB_HEADS, GLOB_KV_HEADS, g_qk_glob[i])
        qgc, kgc, vgc = gqa_heads(pgc, GLOB_HEADS, GLOB_KV_HEADS, g_qk_glob[i])
        qg, kg = apply_rope(qg, *rope_full), apply_rope(kg, *rope_full)
        Rg = GLOB_HEADS // GLOB_KV_HEADS
        one = jnp.ones((1,), F32)
        kkg = _head_major(jnp.concatenate([kgc, kg], axis=1)).astype(BF16)[None]
        vvg = _head_major_t(jnp.concatenate([vgc, vg], axis=1)).astype(BF16)
        og = dense_attention(_q_layout(qg * hscale2, GLOB_KV_HEADS, Rg).astype(BF16)[None], kkg, vvg,
                             one, None, B, GLOB_KV_HEADS, 1, ATTN_TQ_GLOB, _key_chunk(C + L))

        filt = (hy_w1[i], hy_b1[i], hy_w2[i], hy_b2[i], hy_w3[i], hy_b3[i], hy_freq[i])
        oh = hyena(ph, hy_conv_w[i], hy_conv_b[i], filt, hy_bias[i])

        mixed = [jnp.concatenate([oa, ob, og, oh], axis=-1).reshape(n_lat, D)]
        if want_ctx:
            oac = dense_attention(diff_q(qac), diff_k(kac), _head_major_t(vac).astype(BF16), coefs, None,
                                  B, DIFF_HEADS, 2, C, C)
            obc = dense_attention(_q_layout(qwc * hscale2, WIN_KV_HEADS, Rw).astype(BF16)[None],
                                  _head_major(kwc).astype(BF16)[None], _head_major_t(vwc).astype(BF16),
                                  one, sink_win[i], B, WIN_KV_HEADS, 1, C, C)
            ogc = dense_attention(_q_layout(qgc * hscale2, GLOB_KV_HEADS, Rg).astype(BF16)[None],
                                  _head_major(kgc).astype(BF16)[None], _head_major_t(vgc).astype(BF16),
                                  one, None, B, GLOB_KV_HEADS, 1, C, C)
            ohc = hyena_small(phc, hy_conv_w[i], hy_conv_b[i], filt, hy_bias[i])
            mixed.append(jnp.concatenate([oac, obc, ogc, ohc], axis=-1).reshape(n_ctx, D))
        o = jnp.concatenate(mixed, axis=0)
        n_rows = o.shape[0]

        oh_ = rmsnorm(o.reshape(n_rows, N_OUT_HEADS, HEAD_DIM), g_mix_out[i].reshape(N_OUT_HEADS, HEAD_DIM))
        head_scale = jnp.where(jnp.arange(N_OUT_HEADS) < DIFF_HEADS, 1.0 - lambda_init, 1.0)[:, None]
        om = (oh_ * head_scale.astype(F32)).reshape(n_rows, D)
        xcur = mm_residual(om.astype(BF16), w_out[i].astype(BF16), xall, g1, L, TM)

        qp, h2 = normmod_mm(xcur, g_norm_ffn[i], sh2, s2, peer_wq[i].astype(BF16), L, TM, BF16, True)
        xall = peer(qp, h2, peer_keys[i], peer_u[i].astype(BF16), jnp.transpose(peer_v[i].astype(BF16)),
                    xcur, g2, L, TM)

    return xall[:n_lat].reshape(B, L, D)
```

```python
import functools
import math

import jax
import jax.numpy as jnp
import numpy as np
from jax import lax
from jax.experimental import pallas as pl
from jax.experimental.pallas import tpu as pltpu

F32 = jnp.float32
BF16 = jnp.bfloat16

GRID_W = 64
HEAD_DIM = 64
BLOCK = 128
WINDOW = 128
ROPE_THETA = 10000.0
EPS = 1e-6
NEG_INF = -1e30
N_MOD = 6
DIFF_HEADS = 4
DIFF_QK_DIM = 32
WIN_HEADS = 4
WIN_KV_HEADS = 2
GLOB_HEADS = 4
GLOB_KV_HEADS = 2
N_OUT_HEADS = 16
HY_CH = 256
HY_ORDER = 2
HY_BANDS = 16
HY_TARGET = 1e-2
HY_FAST = 0.3
HY_SLOW = 1.5
W_DIFF = 768
W_WIN = 512
W_GLOB = 512
W_HY = 768
PEER_HEADS = 8
N_KEYS = 128
PEER_TOPK = 16

VMEM_LIMIT = 56 * 1024 * 1024
ATTN_TQ_DIFF = 512
ATTN_TQ_GLOB = 256
KEY_CHUNK_CAP = 1408
HY_CH_BLOCK = 16
WIN_QUERY_BLOCKS = 4


def _params(*sem):
    return pltpu.CompilerParams(dimension_semantics=sem, vmem_limit_bytes=VMEM_LIMIT)


def _mm_kernel(a_ref, b_ref, o_ref):
    o_ref[...] = jnp.dot(a_ref[...], b_ref[...], preferred_element_type=F32)


def mm(a, b, tm, tn):
    M, K = a.shape
    N = b.shape[1]
    return pl.pallas_call(
        _mm_kernel,
        grid=(M // tm, N // tn),
        in_specs=[pl.BlockSpec((tm, K), lambda i, j: (i, 0)),
                  pl.BlockSpec((K, tn), lambda i, j: (0, j))],
        out_specs=pl.BlockSpec((tm, tn), lambda i, j: (i, j)),
        out_shape=jax.ShapeDtypeStruct((M, N), F32),
        compiler_params=_params("parallel", "arbitrary"),
        name="mm",
    )(a, b)


def _seg_spec(rows_per_seg, tm, nseg, D):
    blocks_per_seg = rows_per_seg // tm
    return pl.BlockSpec((1, 1, D), lambda i: (jnp.minimum(i // blocks_per_seg, nseg - 1), 0, 0))


def _normmod_mm_kernel(x_ref, g_ref, sh_ref, sc_ref, w_ref, *outs, widths, with_h):
    x = x_ref[...]
    y = x * lax.rsqrt(jnp.mean(x * x, axis=-1, keepdims=True) + EPS) * g_ref[...]
    h = (y * (1.0 + sc_ref[0]) + sh_ref[0]).astype(BF16)
    if with_h:
        outs[-1][...] = h
    r = jnp.dot(h, w_ref[...], preferred_element_type=F32)
    lo = 0
    for o_ref, n in zip(outs, widths):
        o_ref[...] = r[:, lo:lo + n].astype(o_ref.dtype)
        lo += n


def normmod_mm(x, g, shift, scale, w, rows_per_seg, tm, out_dtype, with_h, widths=None):
    M, D = x.shape
    N = w.shape[1]
    widths = (N,) if widths is None else tuple(widths)
    nseg = shift.shape[0]
    seg = _seg_spec(rows_per_seg, tm, nseg, D)
    row_block = lambda n: pl.BlockSpec((tm, n), lambda i: (i, 0))
    out_specs = [row_block(n) for n in widths]
    out_shape = [jax.ShapeDtypeStruct((M, n), out_dtype) for n in widths]
    if with_h:
        out_specs.append(row_block(D))
        out_shape.append(jax.ShapeDtypeStruct((M, D), BF16))
    return pl.pallas_call(
        functools.partial(_normmod_mm_kernel, widths=widths, with_h=with_h),
        grid=(M // tm,),
        in_specs=[row_block(D), pl.BlockSpec((1, D), lambda i: (0, 0)), seg, seg,
                  pl.BlockSpec((D, N), lambda i: (0, 0))],
        out_specs=out_specs,
        out_shape=out_shape,
        compiler_params=_params("parallel"),
        name="normmod_mm",
    )(x, g.reshape(1, D), shift.reshape(nseg, 1, D), scale.reshape(nseg, 1, D), w)


def _mm_residual_kernel(a_ref, w_ref, x_ref, gate_ref, o_ref):
    o_ref[...] = x_ref[...] + gate_ref[0] * jnp.dot(a_ref[...], w_ref[...], preferred_element_type=F32)


def mm_residual(a, w, x, gate, rows_per_seg, tm):
    M, K = a.shape
    N = w.shape[1]
    nseg = gate.shape[0]
    row_block = lambda n: pl.BlockSpec((tm, n), lambda i: (i, 0))
    return pl.pallas_call(
        _mm_residual_kernel,
        grid=(M // tm,),
        in_specs=[row_block(K), pl.BlockSpec((K, N), lambda i: (0, 0)), row_block(N),
                  _seg_spec(rows_per_seg, tm, nseg, N)],
        out_specs=row_block(N),
        out_shape=jax.ShapeDtypeStruct((M, N), F32),
        compiler_params=_params("parallel"),
        name="mm_residual",
    )(a, w, x, gate.reshape(nseg, 1, N))


def _scores_kernel(k_ref, q_ref, o_ref):
    d = k_ref.shape[-1]
    for p in range(k_ref.shape[0]):
        o_ref[p] = lax.dot_general(k_ref[p], q_ref[:, p * d:(p + 1) * d], (((1,), (1,)), ((), ())),
                                   preferred_element_type=F32)


def keys_times_qT(keys, q, tn):
    P, n, d = keys.shape
    T = q.shape[0]
    return pl.pallas_call(
        _scores_kernel,
        grid=(T // tn,),
        in_specs=[pl.BlockSpec((P, n, d), lambda j: (0, 0, 0)),
                  pl.BlockSpec((tn, P * d), lambda j: (j, 0))],
        out_specs=pl.BlockSpec((P, n, tn), lambda j: (0, 0, j)),
        out_shape=jax.ShapeDtypeStruct((P, n, T), F32),
        compiler_params=_params("parallel"),
        name="peer_scores_t",
    )(keys, q)


LOG2E = math.log2(math.e)


def _dense_attn_kernel(coef_ref, sink_ref, q_ref, k_ref, vt_ref, o_ref, sa_ref, sb_ref, *, nbr, R, G,
                       HP, tq, tk, nk, has_sink):
    cols = R * tq
    d = q_ref.shape[-1]
    dv = vt_ref.shape[1]
    g0 = (pl.program_id(0) % (G // HP)) * HP
    nt = (((1,), (1,)), ((), ()))
    bufs = (sa_ref, sb_ref)
    heads = []
    for hp in range(HP):
        out = jnp.zeros((dv, cols), F32)
        for br in range(nbr):
            q = q_ref[br, hp].reshape(cols, d)
            if has_sink:
                m = jnp.concatenate([jnp.full((1, tq), sink_ref[(g0 + hp) * R + r] * LOG2E, F32)
                                     for r in range(R)], axis=1)
                l = jnp.ones((1, cols), F32)
            else:
                m = jnp.full((1, cols), NEG_INF, F32)
                l = jnp.zeros((1, cols), F32)
            acc = jnp.zeros((dv, cols), F32)
            bufs[0][...] = lax.dot_general(k_ref[br, hp, 0:tk, :], q, nt, preferred_element_type=F32)
            for j in range(nk):
                if j + 1 < nk:
                    bufs[(j + 1) % 2][...] = lax.dot_general(k_ref[br, hp, (j + 1) * tk:(j + 2) * tk, :], q,
                                                             nt, preferred_element_type=F32)
                s = bufs[j % 2][...]
                m_new = jnp.maximum(m, jnp.max(s, axis=0, keepdims=True))
                alpha = jnp.exp2(m - m_new)
                p = jnp.exp2(s - m_new)
                l = alpha * l + jnp.sum(p, axis=0, keepdims=True)
                acc = alpha * acc + jnp.dot(vt_ref[hp, :, j * tk:(j + 1) * tk], p.astype(BF16),
                                            preferred_element_type=F32)
                m = m_new
            out = out + coef_ref[br] * (acc / l)
        out_t = out.T
        heads += [out_t[r * tq:(r + 1) * tq] for r in range(R)]
    o_ref[0] = jnp.concatenate(heads, axis=1)


def dense_attention(q, k, vt, coefs, sink, B, G, HP, tq, tk):
    nbr, BG, R, Lq, d = q.shape
    Lk = k.shape[2]
    dv = vt.shape[1]
    assert (HP * R * dv) % 128 == 0 and G % HP == 0
    has_sink = sink is not None
    if sink is None:
        sink = jnp.zeros((1,), F32)
    kern = functools.partial(_dense_attn_kernel, nbr=nbr, R=R, G=G, HP=HP, tq=tq, tk=tk, nk=Lk // tk,
                             has_sink=has_sink)
    gp = G // HP
    return pl.pallas_call(
        kern,
        grid=(BG // HP, Lq // tq),
        in_specs=[pl.BlockSpec(memory_space=pltpu.SMEM),
                  pl.BlockSpec(memory_space=pltpu.SMEM),
                  pl.BlockSpec((nbr, HP, R, tq, d), lambda b, i: (0, b, 0, i, 0)),
                  pl.BlockSpec((nbr, HP, Lk, d), lambda b, i: (0, b, 0, 0)),
                  pl.BlockSpec((HP, dv, Lk), lambda b, i: (b, 0, 0))],
        out_specs=pl.BlockSpec((1, tq, HP * R * dv), lambda b, i: (b // gp, i, b % gp)),
        out_shape=jax.ShapeDtypeStruct((B, Lq, G * R * dv), F32),
        scratch_shapes=[pltpu.VMEM((tk, R * tq), F32), pltpu.VMEM((tk, R * tq), F32)],
        compiler_params=_params("parallel", "arbitrary"),
        name="dense_attn",
    )(coefs.astype(F32), sink.astype(F32), q, k, vt)


def _window_attn_kernel(sink_ref, q_ref, k_ref, v_ref, o_ref, *, R, G, C, L, QB):
    rows = R * BLOCK
    d = q_ref.shape[-1]
    g = pl.program_id(0) % G
    kc = k_ref[0, 0:C, :]
    vc = v_ref[0, 0:C, :]
    nt = (((1,), (1,)), ((), ()))
    qi = lax.broadcasted_iota(jnp.int32, (rows, 3 * BLOCK), 0) & (BLOCK - 1)
    kj = lax.broadcasted_iota(jnp.int32, (rows, 3 * BLOCK), 1)
    in_window = jnp.abs(kj - BLOCK - qi) <= WINDOW
    sink = jnp.concatenate([jnp.full((BLOCK, 1), sink_ref[g * R + r], F32) for r in range(R)], axis=0)
    for qb in range(QB):
        n = pl.program_id(1) * QB + qb
        q = q_ref[0, :, qb * BLOCK:(qb + 1) * BLOCK, :].reshape(rows, d)
        start = pl.multiple_of(C + n * BLOCK, BLOCK)
        kl = k_ref[0, pl.ds(start, 3 * BLOCK), :]
        vl = v_ref[0, pl.ds(start, 3 * BLOCK), :]
        s_ctx = lax.dot_general(q, kc, nt, preferred_element_type=F32)
        s_loc = lax.dot_general(q, kl, nt, preferred_element_type=F32)
        kpos = (n - 1) * BLOCK + kj
        s_loc = jnp.where(in_window & (kpos >= 0) & (kpos < L), s_loc, NEG_INF)
        m = jnp.maximum(jnp.maximum(jnp.max(s_ctx, axis=-1, keepdims=True),
                                    jnp.max(s_loc, axis=-1, keepdims=True)), sink)
        e_ctx = jnp.exp(s_ctx - m)
        e_loc = jnp.exp(s_loc - m)
        den = (jnp.sum(e_ctx, axis=-1, keepdims=True) + jnp.sum(e_loc, axis=-1, keepdims=True)
               + jnp.exp(sink - m))
        inv = 1.0 / den
        o = (jnp.dot((e_ctx * inv).astype(BF16), vc, preferred_element_type=F32)
             + jnp.dot((e_loc * inv).astype(BF16), vl, preferred_element_type=F32))
        o_ref[0, qb * BLOCK:(qb + 1) * BLOCK, :] = jnp.concatenate(
            [o[r * BLOCK:(r + 1) * BLOCK] for r in range(R)], axis=1)


def window_attention(q, kpad, vpad, sink, B, G, C, L):
    BG, R, _, d = q.shape
    Lp = kpad.shape[1]
    QB = WIN_QUERY_BLOCKS
    kern = functools.partial(_window_attn_kernel, R=R, G=G, C=C, L=L, QB=QB)
    return pl.pallas_call(
        kern,
        grid=(BG, L // (BLOCK * QB)),
        in_specs=[pl.BlockSpec(memory_space=pltpu.SMEM),
                  pl.BlockSpec((1, R, QB * BLOCK, d), lambda b, i: (b, 0, i, 0)),
                  pl.BlockSpec((1, Lp, d), lambda b, i: (b, 0, 0)),
                  pl.BlockSpec((1, Lp, d), lambda b, i: (b, 0, 0))],
        out_specs=pl.BlockSpec((1, QB * BLOCK, R * d), lambda b, i: (b // G, i, b % G)),
        out_shape=jax.ShapeDtypeStruct((B, L, G * R * d), F32),
        compiler_params=_params("parallel", "arbitrary"),
        name="window_attn",
    )(sink.astype(F32), q, kpad, vpad)


_CAND_PAIRS = tuple((r, s) for r in range(PEER_TOPK) for s in range(PEER_TOPK)
                    if (r + 1) * (s + 1) <= PEER_TOPK)
_CAND_ROWS = -(-len(_CAND_PAIRS) // 8) * 8


def _top_rows(vs, n):
    vs = list(vs)
    iota = lax.broadcasted_iota(jnp.int32, vs[0].shape, 0).astype(F32)
    rows = [[] for _ in vs]
    for r in range(n):
        for i, v in enumerate(vs):
            m = jnp.max(v, axis=0, keepdims=True)
            rows[i].append(m)
            if r + 1 < n:
                first = jnp.min(jnp.where(v == m, iota, float(v.shape[0])), axis=0, keepdims=True)
                vs[i] = jnp.where(iota == first, -jnp.inf, v)
    return rows


def _peer_select_kernel(s_ref, ea_ref, q_ref, eb_ref, code_ref, c_ref, *, tm):
    n_lane = tm // 128

    def body(h, carry):
        tiles = [slice(t * 128, (t + 1) * 128) for t in range(n_lane)]
        ab = [s_ref[h, p, :, lanes] for lanes in tiles for p in (0, 1)]
        tops = _top_rows(ab, PEER_TOPK)
        for t, lanes in enumerate(tiles):
            a, b, ta, tb = ab[2 * t], ab[2 * t + 1], tops[2 * t], tops[2 * t + 1]
            c_ref[...] = jnp.full(c_ref.shape, -jnp.inf, F32)
            for k, (r, s) in enumerate(_CAND_PAIRS):
                c_ref[k:k + 1, :] = ta[r] + tb[s]
            best, = _top_rows((c_ref[...],), PEER_TOPK)
            tau = best[PEER_TOPK - 1]
            zsum = jnp.zeros_like(best[0])
            for bk in best:
                zsum = zsum + jnp.exp(bk - best[0])
            code_b = jnp.zeros(b.shape, F32)
            for s in range(PEER_TOPK):
                code_b = code_b + jnp.where(tb[s] > b, 1.0, 0.0)
            code_t = [jnp.zeros(tau.shape, F32)]
            for s in range(1, PEER_TOPK):
                code_t.append(jnp.where(tb[s] == tb[s - 1], code_t[s - 1], float(s)))
            q = jnp.full(a.shape, -1.0, F32)
            for r in reversed(range(PEER_TOPK)):
                q_r = jnp.full(tau.shape, -1.0, F32)
                for s in range(PEER_TOPK // (r + 1)):
                    q_r = jnp.maximum(q_r, jnp.where(ta[r] + tb[s] >= tau, code_t[s], -1.0))
                q = jnp.where(a == ta[r], q_r, q)
            q_ref[h, :, lanes] = q
            code_ref[h, :, lanes] = code_b.astype(BF16)
            ea_ref[h, :, lanes] = jnp.exp(a - ta[0]) / zsum
            eb_ref[h, :, lanes] = jnp.exp(b - tb[0]).astype(BF16)
        return carry

    lax.fori_loop(0, PEER_HEADS, body, 0)


def peer_select(st, tm):
    H, _, n, T = st.shape
    kern = functools.partial(_peer_select_kernel, tm=tm)
    return pl.pallas_call(
        kern,
        grid=(T // tm,),
        in_specs=[pl.BlockSpec((H, 2, n, tm), lambda t: (0, 0, 0, t))],
        out_specs=[pl.BlockSpec((H, n, tm), lambda t: (0, 0, t))] * 4,
        out_shape=[jax.ShapeDtypeStruct((H, n, T), dt) for dt in (F32, F32, BF16, BF16)],
        scratch_shapes=[pltpu.VMEM((_CAND_ROWS, 128), F32)],
        compiler_params=_params("parallel"),
        name="peer_select",
    )(st)


def _gelu_tanh(x):
    k = -2.0 * math.sqrt(2.0 / math.pi) * LOG2E
    return x / (1.0 + jnp.exp2(x * (k + (k * 0.044715) * (x * x))))


def _peer_kernel(x_ref, u_ref, vt_ref, ea_ref, q_ref, eb_ref, code_ref, res_ref, gate_ref, o_ref,
                 acc_ref, act_ref, g_ref, *, ni, tm):
    c = pl.program_id(1)

    @pl.when(c == 0)
    def _():
        acc_ref[...] = jnp.zeros_like(acc_ref)

    act_ref[...] = _gelu_tanh(lax.dot_general(u_ref[...], x_ref[...], (((1,), (1,)), ((), ())),
                                              preferred_element_type=F32))

    for ts in range(tm // 128):
        lanes = slice(ts * 128, (ts + 1) * 128)
        for ii in range(ni):
            rows = slice(ii * N_KEYS, (ii + 1) * N_KEYS)
            w = jnp.zeros((N_KEYS, 128), BF16)
            for h in range(PEER_HEADS):
                sel = code_ref[h, :, lanes] <= q_ref[h, ii:ii + 1, lanes].astype(BF16)
                gate = ea_ref[h, ii:ii + 1, lanes].astype(BF16) * eb_ref[h, :, lanes]
                w = w + jnp.where(sel, gate, jnp.zeros_like(gate))
            g_ref[rows, lanes] = w * act_ref[rows, lanes].astype(BF16)

    acc_ref[...] += jnp.dot(vt_ref[...], g_ref[...], preferred_element_type=F32)

    @pl.when(c == pl.num_programs(1) - 1)
    def _():
        o_ref[...] = res_ref[...] + gate_ref[0] * acc_ref[...].T


def peer_dense(x, u, vt, eat, qt, ebt, codet, res, gate, rows_per_seg, tm, ni):
    T, D = x.shape
    E = u.shape[0]
    ec = ni * N_KEYS
    kern = functools.partial(_peer_kernel, ni=ni, tm=tm)
    nseg = gate.shape[0]
    blocks_per_seg = rows_per_seg // tm
    rows_of_chunk = pl.BlockSpec((PEER_HEADS, ni, tm), lambda t, c: (0, c, t))
    return pl.pallas_call(
        kern,
        grid=(T // tm, E // ec),
        in_specs=[pl.BlockSpec((tm, D), lambda t, c: (t, 0)),
                  pl.BlockSpec((ec, D), lambda t, c: (c, 0)),
                  pl.BlockSpec((D, ec), lambda t, c: (0, c)),
                  rows_of_chunk,
                  rows_of_chunk,
                  pl.BlockSpec((PEER_HEADS, N_KEYS, tm), lambda t, c: (0, 0, t)),
                  pl.BlockSpec((PEER_HEADS, N_KEYS, tm), lambda t, c: (0, 0, t)),
                  pl.BlockSpec((tm, D), lambda t, c: (t, 0)),
                  pl.BlockSpec((1, 1, D), lambda t, c: (jnp.minimum(t // blocks_per_seg, nseg - 1), 0, 0))],
        out_specs=pl.BlockSpec((tm, D), lambda t, c: (t, 0)),
        out_shape=jax.ShapeDtypeStruct((T, D), F32),
        scratch_shapes=[pltpu.VMEM((D, tm), F32),
                        pltpu.VMEM((ec, tm), F32),
                        pltpu.VMEM((ec, tm), BF16)],
        compiler_params=_params("parallel", "arbitrary"),
        name="peer_dense",
    )(x, u, vt, eat, qt, ebt, codet, res, gate.reshape(nseg, 1, D))


def peer(qp, h_bf, keys, u_bf, vt_bf, res, gate, rows_per_seg, tm):
    T = qp.shape[0]
    dk = keys.shape[-1]
    kflat = keys.reshape(PEER_HEADS * 2, N_KEYS, dk).astype(BF16)
    st = keys_times_qT(kflat, qp, tm).reshape(PEER_HEADS, 2, N_KEYS, T)
    eat, qt, ebt, codet = peer_select(st, tm)
    return peer_dense(h_bf, u_bf, vt_bf, eat, qt, ebt, codet, res, gate, rows_per_seg, tm, 8)


def rmsnorm(x, g):
    return x * lax.rsqrt(jnp.mean(x * x, axis=-1, keepdims=True) + EPS) * g


def axial_rope(n_tok, dim):
    rows = n_tok // GRID_W
    row = jnp.repeat(jnp.arange(rows, dtype=F32), GRID_W)
    col = jnp.tile(jnp.arange(GRID_W, dtype=F32), rows)
    axis_dim = dim // 2
    inv_freq = ROPE_THETA ** (-jnp.arange(0, axis_dim, 2, dtype=F32) / axis_dim)
    ang = jnp.concatenate([row[:, None] * inv_freq, col[:, None] * inv_freq], axis=-1)
    return jnp.cos(ang), jnp.sin(ang)


def apply_rope(x, cos, sin):
    shape = (1, x.shape[1]) + (1,) * (x.ndim - 3) + (cos.shape[-1],)
    c, s = cos.reshape(shape), sin.reshape(shape)
    x1, x2 = jnp.split(x, 2, axis=-1)
    return jnp.concatenate([x1 * c - x2 * s, x1 * s + x2 * c], axis=-1)


def short_conv(u, w, b):
    up = jnp.pad(u, ((0, 0), (1, 1), (0, 0)))
    return up[:, :-2] * w[0] + up[:, 1:-1] * w[1] + up[:, 2:] * w[2] + b


def hyena_filters(n_tok, w1, b1, w2, b2, w3, b3, freq):
    hp = lax.Precision.HIGHEST
    t = jnp.linspace(0.0, 1.0, n_tok, dtype=F32)[:, None]
    w = (2.0 * math.pi / n_tok) * jnp.arange(n_tok, dtype=F32)[:, None]
    f = jnp.linspace(1e-4, HY_BANDS - 1, HY_BANDS, dtype=F32)[None, :]
    feat = jnp.concatenate([t, jnp.cos(f * w), -jnp.sin(f * w)], axis=-1)
    hdn = jnp.sin(freq * (jnp.dot(feat, w1, precision=hp) + b1))
    hdn = jnp.sin(freq * (jnp.dot(hdn, w2, precision=hp) + b2))
    filt = (jnp.dot(hdn, w3, precision=hp) + b3).reshape(n_tok, HY_ORDER, 2, HY_CH)
    deltas = jnp.abs(jnp.linspace(math.log(HY_TARGET) / HY_SLOW, math.log(HY_TARGET) / HY_FAST, HY_CH,
                                  dtype=F32))
    filt = filt * jnp.exp(-t[:, :, None, None] * deltas)
    fwd, bwd = filt[:, :, 0], filt[:, :, 1]
    kfull = jnp.concatenate([fwd, jnp.zeros_like(fwd[:1]), bwd[1:][::-1]], axis=0)
    return kfull / jnp.sum(jnp.abs(kfull), axis=0, keepdims=True)


def hyena_filters_t(n_tok, w1, b1, w2, b2, w3, b3, freq):
    hp = lax.Precision.HIGHEST
    t = jnp.linspace(0.0, 1.0, n_tok, dtype=F32)[:, None]
    w = (2.0 * math.pi / n_tok) * jnp.arange(n_tok, dtype=F32)[:, None]
    f = jnp.linspace(1e-4, HY_BANDS - 1, HY_BANDS, dtype=F32)[None, :]
    feat = jnp.concatenate([t, jnp.cos(f * w), -jnp.sin(f * w)], axis=-1)
    hdn = jnp.sin(freq * (jnp.dot(feat, w1, precision=hp) + b1))
    hdn = jnp.sin(freq * (jnp.dot(hdn, w2, precision=hp) + b2))
    deltas = jnp.abs(jnp.linspace(math.log(HY_TARGET) / HY_SLOW, math.log(HY_TARGET) / HY_FAST, HY_CH,
                                  dtype=F32))
    w3t = jnp.transpose(w3.reshape(-1, HY_ORDER, 2, HY_CH), (2, 1, 3, 0))
    b3t = jnp.transpose(b3.reshape(HY_ORDER, 2, HY_CH), (1, 0, 2))[..., None]

    def half(d, hidden, times):
        decay = jnp.exp(-deltas[:, None] * times[None, :])
        return (jnp.einsum('ock,nk->ocn', w3t[d], hidden, precision=hp) + b3t[d]) * decay

    fwd = half(0, hdn, t[:, 0])
    bwd_rev = half(1, hdn[::-1], t[::-1, 0])
    kfull = jnp.concatenate([fwd, jnp.zeros_like(fwd[..., :1]), bwd_rev[..., :n_tok - 1]], axis=-1)
    kfull = kfull / jnp.sum(jnp.abs(kfull), axis=-1, keepdims=True)
    return kfull.reshape(HY_ORDER * HY_CH, 2 * n_tok)


def long_conv(z, kf):
    n = z.shape[1]
    zf = jnp.fft.rfft(z, n=2 * n, axis=1)
    hf = jnp.fft.rfft(kf, n=2 * n, axis=0)
    return jnp.fft.irfft(zf * hf[None], n=2 * n, axis=1)[:, :n]


def hyena_small(u, conv_w, conv_b, filter_params, bias):
    n = u.shape[1]
    u = short_conv(u, conv_w, conv_b)
    v, x1, x2 = jnp.split(u, 3, axis=-1)
    kfull = hyena_filters(n, *filter_params)
    z = x1 * (long_conv(v, kfull[:, 0]) + bias[0] * v)
    return x2 * (long_conv(z, kfull[:, 1]) + bias[1] * z)


def _hi_lo(x):
    hi = x.astype(BF16)
    return hi, (x - hi.astype(F32)).astype(BF16)


def _dot3(a, b):
    d = lambda x, y: jnp.dot(x, y, preferred_element_type=F32)
    return d(a[0], b[0]) + d(a[1], b[0]) + d(a[0], b[1])


def _dft_constants(n_tok):
    n = 2 * n_tok
    n1 = n // 128
    a1 = 2.0 * np.pi * np.outer(np.arange(n1), np.arange(n1)) / n1
    a2 = 2.0 * np.pi * np.outer(np.arange(128), np.arange(128)) / 128
    at = 2.0 * np.pi * np.outer(np.arange(n1), np.arange(128)) / n
    c1, s1, c2, s2 = np.cos(a1), np.sin(a1), np.cos(a2), np.sin(a2)
    pair = lambda m: _hi_lo(jnp.asarray(m, F32))
    return dict(
        f1_half=pair(np.concatenate([c1[:, :n1 // 2], -s1[:, :n1 // 2]], axis=0)),
        f1_full=pair(np.concatenate([c1, -s1], axis=0)),
        m_fwd=pair(np.block([[c2, -s2], [s2, c2]])),
        m_inv=pair(np.block([[c2, s2], [-s2, c2]])),
        g_half=pair(np.concatenate([c1[:n1 // 2], -s1[:n1 // 2]], axis=1)),
        tr=jnp.asarray(np.cos(at), F32), ti=jnp.asarray(-np.sin(at), F32))


def _dft_fwd(seqs, f1, tr, ti, m_fwd):
    n1 = tr.shape[0]
    y = _dot3(f1, _hi_lo(jnp.concatenate(seqs, axis=1)))
    rows = []
    for k in range(len(seqs)):
        yr, yi = y[:n1, k * 128:(k + 1) * 128], y[n1:, k * 128:(k + 1) * 128]
        rows.append(jnp.concatenate([yr * tr - yi * ti, yr * ti + yi * tr], axis=1))
    return _dot3(_hi_lo(jnp.concatenate(rows, axis=0)), m_fwd)


def _dft_inv_half(p, g_half, tr, ti, m_inv):
    n1 = tr.shape[0]
    u = _dot3(_hi_lo(p), m_inv)
    cols = []
    for k in range(p.shape[0] // n1):
        ur, ui = u[k * n1:(k + 1) * n1, :128], u[k * n1:(k + 1) * n1, 128:]
        cols.append(jnp.concatenate([ur * tr + ui * ti, ui * tr - ur * ti], axis=0))
    return _dot3(g_half, _hi_lo(jnp.concatenate(cols, axis=1))) * (1.0 / (n1 * 128))


def _pairs(refs):
    return (refs[0][...], refs[1][...])


def _spectrum_kernel(a_ref, f1h, f1l, tr_ref, ti_ref, mh, ml, o_ref):
    cb, n1 = a_ref.shape[0], tr_ref.shape[0]
    x = _dft_fwd([a_ref[k] for k in range(cb)], _pairs((f1h, f1l)), tr_ref[...], ti_ref[...],
                 _pairs((mh, ml)))
    o_ref[...] = x.reshape(cb, n1, 256)


def filter_spectrum(kf, consts, cb):
    items, n1, _ = kf.shape
    full = lambda shape: pl.BlockSpec(shape, lambda i: (0,) * len(shape))
    f1, m = consts["f1_full"], consts["m_fwd"]
    return pl.pallas_call(
        _spectrum_kernel,
        grid=(items // cb,),
        in_specs=[pl.BlockSpec((cb, n1, 128), lambda i: (i, 0, 0)),
                  full(f1[0].shape), full(f1[1].shape), full((n1, 128)), full((n1, 128)),
                  full(m[0].shape), full(m[1].shape)],
        out_specs=pl.BlockSpec((cb, n1, 256), lambda i: (i, 0, 0)),
        out_shape=jax.ShapeDtypeStruct((items, n1, 256), F32),
        compiler_params=_params("parallel"),
        name="filter_spectrum",
    )(kf, f1[0], f1[1], consts["tr"], consts["ti"], m[0], m[1])


def _conv_gate_kernel(bias_ref, u_ref, g_ref, h_ref, f1h, f1l, tr_ref, ti_ref, mfh, mfl, mih, mil, gh, gl,
                      o_ref, *, cb):
    tr, ti = tr_ref[...], ti_ref[...]
    n1 = tr.shape[0]
    c0 = pl.program_id(1) * cb
    u = [u_ref[0, k] for k in range(cb)]
    x = _dft_fwd(u, _pairs((f1h, f1l)), tr, ti, _pairs((mfh, mfl)))
    prod = []
    for k in range(cb):
        xr, xi = x[k * n1:(k + 1) * n1, :128], x[k * n1:(k + 1) * n1, 128:]
        hr, hi = h_ref[k, :, :128], h_ref[k, :, 128:]
        prod.append(jnp.concatenate([xr * hr - xi * hi, xr * hi + xi * hr], axis=1))
    y = _dft_inv_half(jnp.concatenate(prod, axis=0), _pairs((gh, gl)), tr, ti, _pairs((mih, mil)))
    for k in range(cb):
        o_ref[0, k] = g_ref[0, k] * (y[:, k * 128:(k + 1) * 128] + bias_ref[c0 + k] * u[k])


def conv_gate(u, gate, spec, bias, consts, cb):
    B, C, half, _ = u.shape
    n1 = 2 * half
    full = lambda shape: pl.BlockSpec(shape, lambda b, c: (0,) * len(shape))
    seq = pl.BlockSpec((1, cb, half, 128), lambda b, c: (b, c, 0, 0))
    mats = [*consts["f1_half"], consts["tr"], consts["ti"], *consts["m_fwd"], *consts["m_inv"],
            *consts["g_half"]]
    return pl.pallas_call(
        functools.partial(_conv_gate_kernel, cb=cb),
        grid=(B, C // cb),
        in_specs=[pl.BlockSpec(memory_space=pltpu.SMEM), seq, seq,
                  pl.BlockSpec((cb, n1, 256), lambda b, c: (c, 0, 0))] + [full(m.shape) for m in mats],
        out_specs=seq,
        out_shape=jax.ShapeDtypeStruct(u.shape, F32),
        compiler_params=_params("parallel", "arbitrary"),
        name="conv_gate",
    )(bias.astype(F32), u, gate, spec, *mats)


def hyena(u, conv_w, conv_b, filter_params, bias):
    B, n, _ = u.shape
    half = n // 128
    consts = _dft_constants(n)
    u = short_conv(u, conv_w, conv_b)
    seqs = jnp.transpose(u, (0, 2, 1)).reshape(B, 3, HY_CH, half, 128)
    v, x1, x2 = seqs[:, 0], seqs[:, 1], seqs[:, 2]
    kf = hyena_filters_t(n, *filter_params).reshape(HY_ORDER * HY_CH, 2 * half, 128)
    spec = filter_spectrum(kf, consts, HY_CH_BLOCK).reshape(HY_ORDER, HY_CH, 2 * half, 256)
    z = conv_gate(v, x1, spec[0], bias[0], consts, HY_CH_BLOCK)
    o = conv_gate(z, x2, spec[1], bias[1], consts, HY_CH_BLOCK)
    return jnp.transpose(o.reshape(B, HY_CH, n), (0, 2, 1))


def _head_major(t):
    B, L, H, d = t.shape
    return jnp.transpose(t, (0, 2, 1, 3)).reshape(B * H, L, d)


def _head_major_t(t):
    B, L, H, d = t.shape
    return jnp.transpose(t, (0, 2, 3, 1)).reshape(B * H, d, L)


def _q_layout(t, G, R):
    B, L, _, d = t.shape
    return jnp.transpose(t.reshape(B, L, G, R, d), (0, 2, 3, 1, 4)).reshape(B * G, R, L, d)


def _key_chunk(n_keys):
    for c in range(KEY_CHUNK_CAP, 0, -128):
        if n_keys % c == 0:
            return c
    raise ValueError(f"no key chunk for {n_keys}")


def kernel(x, c, ctx, c_ctx, w_mod, b_mod, g_norm_mix, w_in, g_qk_diff, lambda_diff, g_qk_win, sink_win, g_qk_glob, hy_conv_w, hy_conv_b, hy_w1, hy_b1, hy_w2, hy_b2, hy_w3, hy_b3, hy_freq, hy_bias, g_mix_out, w_out, g_norm_ffn, peer_wq, peer_keys, peer_u, peer_v):
    B, L, D = x.shape
    C = ctx.shape[1]
    depth = w_mod.shape[0]
    TM = 512
    n_lat = B * L
    n_ctx = B * C
    assert L % TM == 0 and n_ctx % TM == 0
    nseg = B + 1

    rope_half = axial_rope(L, DIFF_QK_DIM)
    rope_full = axial_rope(L, HEAD_DIM)
    sc = jnp.concatenate([jax.nn.silu(c), jax.nn.silu(c_ctx)[None]], axis=0)
    sc = jnp.pad(sc, ((0, 8 - nseg), (0, 0))).astype(BF16)

    xall = jnp.concatenate([x.reshape(n_lat, D), ctx.reshape(n_ctx, D)], axis=0)

    for i in range(depth):
        want_ctx = i < depth - 1
        lambda_init = 0.8 - 0.6 * math.exp(-0.3 * i)
        mod = mm(sc, w_mod[i].astype(BF16), 8, 1024)[:nseg] + b_mod[i]
        sh1, s1, g1, sh2, s2, g2 = jnp.split(mod, N_MOD, axis=-1)

        groups = normmod_mm(xall, g_norm_mix[i], sh1, s1, w_in[i].astype(BF16), L, TM, F32, False,
                            widths=(W_DIFF, W_WIN, W_GLOB, W_HY))
        pa, pw, pg, ph = (t[:n_lat].reshape(B, L, -1) for t in groups)
        pac, pwc, pgc, phc = (t[n_lat:].reshape(B, C, -1) for t in groups)

        def diff_heads(t):
            B_, L_, _ = t.shape
            q, k, v = jnp.split(t, 3, axis=-1)
            q = rmsnorm(q.reshape(B_, L_, DIFF_HEADS, 2, DIFF_QK_DIM), g_qk_diff[i, 0])
            k = rmsnorm(k.reshape(B_, L_, DIFF_HEADS, 2, DIFF_QK_DIM), g_qk_diff[i, 1])
            return q, k, v.reshape(B_, L_, DIFF_HEADS, HEAD_DIM)

        qa, ka, va = diff_heads(pa)
        qac, kac, vac = diff_heads(pac)
        qa, ka = apply_rope(qa, *rope_half), apply_rope(ka, *rope_half)
        lam_vec = lambda_diff[i]
        lam = (jnp.exp(jnp.sum(lam_vec[0] * lam_vec[1])) - jnp.exp(jnp.sum(lam_vec[2] * lam_vec[3]))
               + lambda_init)
        coefs = jnp.stack([jnp.ones((), F32), -lam])
        dscale = DIFF_QK_DIM ** -0.5 * LOG2E

        def diff_q(q):
            return jnp.stack([_head_major(q[..., b_, :] * dscale) for b_ in range(2)])[:, :, None].astype(BF16)

        def diff_k(k):
            return jnp.stack([_head_major(k[..., b_, :]) for b_ in range(2)]).astype(BF16)

        kka = jnp.concatenate([kac, ka], axis=1)
        vva = jnp.concatenate([vac, va], axis=1)
        oa = dense_attention(diff_q(qa), diff_k(kka), _head_major_t(vva).astype(BF16), coefs, None,
                             B, DIFF_HEADS, 2, ATTN_TQ_DIFF, _key_chunk(C + L))

        def gqa_heads(t, n_q, n_kv, g_qk):
            B_, L_, _ = t.shape
            q, k, v = jnp.split(t, [n_q * HEAD_DIM, (n_q + n_kv) * HEAD_DIM], axis=-1)
            q = rmsnorm(q.reshape(B_, L_, n_q, HEAD_DIM), g_qk[0])
            k = rmsnorm(k.reshape(B_, L_, n_kv, HEAD_DIM), g_qk[1])
            return q, k, v.reshape(B_, L_, n_kv, HEAD_DIM)

        hscale = HEAD_DIM ** -0.5
        hscale2 = hscale * LOG2E
        qw, kw, vw = gqa_heads(pw, WIN_HEADS, WIN_KV_HEADS, g_qk_win[i])
        qwc, kwc, vwc = gqa_heads(pwc, WIN_HEADS, WIN_KV_HEADS, g_qk_win[i])
        qw, kw = apply_rope(qw, *rope_full), apply_rope(kw, *rope_full)
        zblk = jnp.zeros((B, BLOCK, WIN_KV_HEADS, HEAD_DIM), F32)
        kpad = _head_major(jnp.concatenate([kwc, zblk, kw, zblk], axis=1)).astype(BF16)
        vpad = _head_major(jnp.concatenate([vwc, zblk, vw, zblk], axis=1)).astype(BF16)
        Rw = WIN_HEADS // WIN_KV_HEADS
        ob = window_attention(_q_layout(qw * hscale, WIN_KV_HEADS, Rw).astype(BF16), kpad, vpad,
                              sink_win[i], B, WIN_KV_HEADS, C, L)

        qg, kg, vg = gqa_heads(pg, GLOB_HEADS, GLOB_KV_HEADS, g_qk_glob[i])
        qgc, kgc, vgc = gqa_heads(pgc, GLOB_HEADS, GLOB_KV_HEADS, g_qk_glob[i])
        qg, kg = apply_rope(qg, *rope_full), apply_rope(kg, *rope_full)
        Rg = GLOB_HEADS // GLOB_KV_HEADS
        one = jnp.ones((1,), F32)
        kkg = _head_major(jnp.concatenate([kgc, kg], axis=1)).astype(BF16)[None]
        vvg = _head_major_t(jnp.concatenate([vgc, vg], axis=1)).astype(BF16)
        og = dense_attention(_q_layout(qg * hscale2, GLOB_KV_HEADS, Rg).astype(BF16)[None], kkg, vvg,
                             one, None, B, GLOB_KV_HEADS, 1, ATTN_TQ_GLOB, _key_chunk(C + L))

        filt = (hy_w1[i], hy_b1[i], hy_w2[i], hy_b2[i], hy_w3[i], hy_b3[i], hy_freq[i])
        oh = hyena(ph, hy_conv_w[i], hy_conv_b[i], filt, hy_bias[i])

        mixed = [jnp.concatenate([oa, ob, og, oh], axis=-1).reshape(n_lat, D)]
        if want_ctx:
            oac = dense_attention(diff_q(qac), diff_k(kac), _head_major_t(vac).astype(BF16), coefs, None,
                                  B, DIFF_HEADS, 2, C, C)
            obc = dense_attention(_q_layout(qwc * hscale2, WIN_KV_HEADS, Rw).astype(BF16)[None],
                                  _head_major(kwc).astype(BF16)[None], _head_major_t(vwc).astype(BF16),
                                  one, sink_win[i], B, WIN_KV_HEADS, 1, C, C)
            ogc = dense_attention(_q_layout(qgc * hscale2, GLOB_KV_HEADS, Rg).astype(BF16)[None],
                                  _head_major(kgc).astype(BF16)[None], _head_major_t(vgc).astype(BF16),
                                  one, None, B, GLOB_KV_HEADS, 1, C, C)
            ohc = hyena_small(phc, hy_conv_w[i], hy_conv_b[i], filt, hy_bias[i])
            mixed.append(jnp.concatenate([oac, obc, ogc, ohc], axis=-1).reshape(n_ctx, D))
        o = jnp.concatenate(mixed, axis=0)
        n_rows = o.shape[0]

        oh_ = rmsnorm(o.reshape(n_rows, N_OUT_HEADS, HEAD_DIM), g_mix_out[i].reshape(N_OUT_HEADS, HEAD_DIM))
        head_scale = jnp.where(jnp.arange(N_OUT_HEADS) < DIFF_HEADS, 1.0 - lambda_init, 1.0)[:, None]
        om = (oh_ * head_scale.astype(F32)).reshape(n_rows, D)
        xcur = mm_residual(om.astype(BF16), w_out[i].astype(BF16), xall, g1, L, TM)

        qp, h2 = normmod_mm(xcur, g_norm_ffn[i], sh2, s2, peer_wq[i].astype(BF16), L, TM, BF16, True)
        xall = peer(qp, h2, peer_keys[i], peer_u[i].astype(BF16), jnp.transpose(peer_v[i].astype(BF16)),
                    xcur, g2, L, TM)

    return xall[:n_lat].reshape(B, L, D)
```

```python
import functools
import math

import jax
import jax.numpy as jnp
import numpy as np
from jax import lax
from jax.experimental import pallas as pl
from jax.experimental.pallas import tpu as pltpu

F32 = jnp.float32
BF16 = jnp.bfloat16

GRID_W = 64
HEAD_DIM = 64
BLOCK = 128
WINDOW = 128
ROPE_THETA = 10000.0
EPS = 1e-6
NEG_INF = -1e30
N_MOD = 6
DIFF_HEADS = 4
DIFF_QK_DIM = 32
WIN_HEADS = 4
WIN_KV_HEADS = 2
GLOB_HEADS = 4
GLOB_KV_HEADS = 2
N_OUT_HEADS = 16
HY_CH = 256
HY_ORDER = 2
HY_BANDS = 16
HY_TARGET = 1e-2
HY_FAST = 0.3
HY_SLOW = 1.5
W_DIFF = 768
W_WIN = 512
W_GLOB = 512
W_HY = 768
GROUP_W = 256
PEER_HEADS = 8
N_KEYS = 128
PEER_TOPK = 16

VMEM_LIMIT = 56 * 1024 * 1024
ATTN_TQ_DIFF = 512
ATTN_TQ_GLOB = 256
KEY_CHUNK_CAP = 1408
HY_CH_BLOCK = 16
WIN_QUERY_BLOCKS = 4


def _params(*sem):
    return pltpu.CompilerParams(dimension_semantics=sem, vmem_limit_bytes=VMEM_LIMIT)


def _mm_kernel(a_ref, b_ref, o_ref):
    o_ref[...] = jnp.dot(a_ref[...], b_ref[...], preferred_element_type=F32)


def mm(a, b, tm, tn):
    M, K = a.shape
    N = b.shape[1]
    return pl.pallas_call(
        _mm_kernel,
        grid=(M // tm, N // tn),
        in_specs=[pl.BlockSpec((tm, K), lambda i, j: (i, 0)),
                  pl.BlockSpec((K, tn), lambda i, j: (0, j))],
        out_specs=pl.BlockSpec((tm, tn), lambda i, j: (i, j)),
        out_shape=jax.ShapeDtypeStruct((M, N), F32),
        compiler_params=_params("parallel", "arbitrary"),
        name="mm",
    )(a, b)


def _seg_spec(rows_per_seg, tm, nseg, D):
    blocks_per_seg = rows_per_seg // tm
    return pl.BlockSpec((1, 1, D), lambda i: (jnp.minimum(i // blocks_per_seg, nseg - 1), 0, 0))


def _normmod_mm_kernel(x_ref, g_ref, sh_ref, sc_ref, w_ref, *outs, widths, with_h):
    x = x_ref[...]
    y = x * lax.rsqrt(jnp.mean(x * x, axis=-1, keepdims=True) + EPS) * g_ref[...]
    h = (y * (1.0 + sc_ref[0]) + sh_ref[0]).astype(BF16)
    if with_h:
        outs[-1][...] = h
    r = jnp.dot(h, w_ref[...], preferred_element_type=F32)
    lo = 0
    for o_ref, n in zip(outs, widths):
        o_ref[...] = r[:, lo:lo + n].astype(o_ref.dtype)
        lo += n


def normmod_mm(x, g, shift, scale, w, rows_per_seg, tm, out_dtype, with_h, widths=None):
    M, D = x.shape
    N = w.shape[1]
    widths = (N,) if widths is None else tuple(widths)
    nseg = shift.shape[0]
    seg = _seg_spec(rows_per_seg, tm, nseg, D)
    row_block = lambda n: pl.BlockSpec((tm, n), lambda i: (i, 0))
    out_specs = [row_block(n) for n in widths]
    out_shape = [jax.ShapeDtypeStruct((M, n), out_dtype) for n in widths]
    if with_h:
        out_specs.append(row_block(D))
        out_shape.append(jax.ShapeDtypeStruct((M, D), BF16))
    return pl.pallas_call(
        functools.partial(_normmod_mm_kernel, widths=widths, with_h=with_h),
        grid=(M // tm,),
        in_specs=[row_block(D), pl.BlockSpec((1, D), lambda i: (0, 0)), seg, seg,
                  pl.BlockSpec((D, N), lambda i: (0, 0))],
        out_specs=out_specs,
        out_shape=out_shape,
        compiler_params=_params("parallel"),
        name="normmod_mm",
    )(x, g.reshape(1, D), shift.reshape(nseg, 1, D), scale.reshape(nseg, 1, D), w)


def _mm_residual_kernel(a_ref, w_ref, x_ref, gate_ref, o_ref):
    o_ref[...] = x_ref[...] + gate_ref[0] * jnp.dot(a_ref[...], w_ref[...], preferred_element_type=F32)


def mm_residual(a, w, x, gate, rows_per_seg, tm):
    M, K = a.shape
    N = w.shape[1]
    nseg = gate.shape[0]
    row_block = lambda n: pl.BlockSpec((tm, n), lambda i: (i, 0))
    return pl.pallas_call(
        _mm_residual_kernel,
        grid=(M // tm,),
        in_specs=[row_block(K), pl.BlockSpec((K, N), lambda i: (0, 0)), row_block(N),
                  _seg_spec(rows_per_seg, tm, nseg, N)],
        out_specs=row_block(N),
        out_shape=jax.ShapeDtypeStruct((M, N), F32),
        compiler_params=_params("parallel"),
        name="mm_residual",
    )(a, w, x, gate.reshape(nseg, 1, N))


def _scores_kernel(k_ref, q_ref, o_ref):
    d = k_ref.shape[-1]
    for p in range(k_ref.shape[0]):
        o_ref[p] = lax.dot_general(k_ref[p], q_ref[:, p * d:(p + 1) * d], (((1,), (1,)), ((), ())),
                                   preferred_element_type=F32)


def keys_times_qT(keys, q, tn):
    P, n, d = keys.shape
    T = q.shape[0]
    return pl.pallas_call(
        _scores_kernel,
        grid=(T // tn,),
        in_specs=[pl.BlockSpec((P, n, d), lambda j: (0, 0, 0)),
                  pl.BlockSpec((tn, P * d), lambda j: (j, 0))],
        out_specs=pl.BlockSpec((P, n, tn), lambda j: (0, 0, j)),
        out_shape=jax.ShapeDtypeStruct((P, n, T), F32),
        compiler_params=_params("parallel"),
        name="peer_scores_t",
    )(keys, q)


def _head_prep_kernel(t_ref, cos_ref, sin_ref, gain_ref, bd_ref, pm_ref, ex_ref, o_ref):
    d3 = lambda parts, m: sum(jnp.dot(p, m, preferred_element_type=F32) for p in parts)

    def split3(v):
        hi, lo = _hi_lo(v)
        rest = v - hi.astype(F32) - lo.astype(F32)
        return hi, lo, rest.astype(BF16)

    x = t_ref[...]
    ms = d3(_hi_lo(x * x), bd_ref[...])
    y = x * lax.rsqrt(ms + EPS) * gain_ref[...]
    ex = ex_ref[...]
    cos = d3(split3(cos_ref[...]), ex)
    sin = d3(split3(sin_ref[...]), ex)
    o_ref[...] = (y * cos + d3(_hi_lo(y), pm_ref[...]) * sin).astype(BF16)


def head_prep(t, width, d, gain, cos_rows, sin_rows, tm):
    rows = t.shape[0]
    hd = d // 2
    col = np.arange(width)
    same_head = (col[:, None] // d) == (col[None, :] // d)
    bd = jnp.asarray(same_head / d, BF16)
    src = np.where((col % d) < hd, col + hd, col - hd)
    pm = np.zeros((width, width), np.float32)
    pm[src, col] = np.where((col % d) < hd, -1.0, 1.0)
    ex = np.zeros((hd, width), np.float32)
    ex[col % hd, col] = 1.0
    full = lambda a: pl.BlockSpec(a.shape, lambda i: (0,) * a.ndim)
    consts = (gain.reshape(1, width).astype(F32), bd, jnp.asarray(pm, BF16), jnp.asarray(ex, BF16))
    return pl.pallas_call(
        _head_prep_kernel,
        grid=(rows // tm,),
        in_specs=[pl.BlockSpec((tm, width), lambda i: (i, 0)),
                  pl.BlockSpec((tm, hd), lambda i: (i, 0)),
                  pl.BlockSpec((tm, hd), lambda i: (i, 0))] + [full(c) for c in consts],
        out_specs=pl.BlockSpec((tm, width), lambda i: (i, 0)),
        out_shape=jax.ShapeDtypeStruct((rows, width), BF16),
        compiler_params=_params("parallel"),
        name="head_prep",
    )(t, cos_rows, sin_rows, *consts)


LOG2E = math.log2(math.e)


def _dense_attn_kernel(coef_ref, sink_ref, q_ref, k_ref, vt_ref, o_ref, sa_ref, sb_ref, *, nbr, R, G,
                       HP, tq, tk, nk, has_sink):
    cols = R * tq
    d = q_ref.shape[-1]
    dv = vt_ref.shape[1]
    g0 = (pl.program_id(0) % (G // HP)) * HP
    nt = (((1,), (1,)), ((), ()))
    bufs = (sa_ref, sb_ref)
    heads = []
    for hp in range(HP):
        out = jnp.zeros((dv, cols), F32)
        for br in range(nbr):
            q = q_ref[br, hp].reshape(cols, d)
            if has_sink:
                m = jnp.concatenate([jnp.full((1, tq), sink_ref[(g0 + hp) * R + r] * LOG2E, F32)
                                     for r in range(R)], axis=1)
                l = jnp.ones((1, cols), F32)
            else:
                m = jnp.full((1, cols), NEG_INF, F32)
                l = jnp.zeros((1, cols), F32)
            acc = jnp.zeros((dv, cols), F32)
            bufs[0][...] = lax.dot_general(k_ref[br, hp, 0:tk, :], q, nt, preferred_element_type=F32)
            for j in range(nk):
                if j + 1 < nk:
                    bufs[(j + 1) % 2][...] = lax.dot_general(k_ref[br, hp, (j + 1) * tk:(j + 2) * tk, :], q,
                                                             nt, preferred_element_type=F32)
                s = bufs[j % 2][...]
                m_new = jnp.maximum(m, jnp.max(s, axis=0, keepdims=True))
                alpha = jnp.exp2(m - m_new)
                p = jnp.exp2(s - m_new)
                l = alpha * l + jnp.sum(p, axis=0, keepdims=True)
                acc = alpha * acc + jnp.dot(vt_ref[hp, :, j * tk:(j + 1) * tk], p.astype(BF16),
                                            preferred_element_type=F32)
                m = m_new
            out = out + coef_ref[br] * (acc / l)
        out_t = out.T
        heads += [out_t[r * tq:(r + 1) * tq] for r in range(R)]
    o_ref[0] = jnp.concatenate(heads, axis=1)


def dense_attention(q, k, vt, coefs, sink, B, G, HP, tq, tk):
    nbr, BG, R, Lq, d = q.shape
    Lk = k.shape[2]
    dv = vt.shape[1]
    assert (HP * R * dv) % 128 == 0 and G % HP == 0
    has_sink = sink is not None
    if sink is None:
        sink = jnp.zeros((1,), F32)
    kern = functools.partial(_dense_attn_kernel, nbr=nbr, R=R, G=G, HP=HP, tq=tq, tk=tk, nk=Lk // tk,
                             has_sink=has_sink)
    gp = G // HP
    return pl.pallas_call(
        kern,
        grid=(BG // HP, Lq // tq),
        in_specs=[pl.BlockSpec(memory_space=pltpu.SMEM),
                  pl.BlockSpec(memory_space=pltpu.SMEM),
                  pl.BlockSpec((nbr, HP, R, tq, d), lambda b, i: (0, b, 0, i, 0)),
                  pl.BlockSpec((nbr, HP, Lk, d), lambda b, i: (0, b, 0, 0)),
                  pl.BlockSpec((HP, dv, Lk), lambda b, i: (b, 0, 0))],
        out_specs=pl.BlockSpec((1, tq, HP * R * dv), lambda b, i: (b // gp, i, b % gp)),
        out_shape=jax.ShapeDtypeStruct((B, Lq, G * R * dv), F32),
        scratch_shapes=[pltpu.VMEM((tk, R * tq), F32), pltpu.VMEM((tk, R * tq), F32)],
        compiler_params=_params("parallel", "arbitrary"),
        name="dense_attn",
    )(coefs.astype(F32), sink.astype(F32), q, k, vt)


def _window_attn_kernel(sink_ref, q_ref, k_ref, v_ref, o_ref, *, R, G, C, L, QB):
    rows = R * BLOCK
    d = q_ref.shape[-1]
    g = pl.program_id(0) % G
    kc = k_ref[0, 0:C, :]
    vc = v_ref[0, 0:C, :]
    nt = (((1,), (1,)), ((), ()))
    qi = lax.broadcasted_iota(jnp.int32, (rows, 3 * BLOCK), 0) & (BLOCK - 1)
    kj = lax.broadcasted_iota(jnp.int32, (rows, 3 * BLOCK), 1)
    in_window = jnp.abs(kj - BLOCK - qi) <= WINDOW
    sink = jnp.concatenate([jnp.full((BLOCK, 1), sink_ref[g * R + r] * LOG2E, F32) for r in range(R)],
                           axis=0)
    for qb in range(QB):
        n = pl.program_id(1) * QB + qb
        q = q_ref[0, :, qb * BLOCK:(qb + 1) * BLOCK, :].reshape(rows, d)
        start = pl.multiple_of(C + n * BLOCK, BLOCK)
        kl = k_ref[0, pl.ds(start, 3 * BLOCK), :]
        vl = v_ref[0, pl.ds(start, 3 * BLOCK), :]
        s_ctx = lax.dot_general(q, kc, nt, preferred_element_type=F32)
        s_loc = lax.dot_general(q, kl, nt, preferred_element_type=F32)
        kpos = (n - 1) * BLOCK + kj
        s_loc = jnp.where(in_window & (kpos >= 0) & (kpos < L), s_loc, NEG_INF)
        m = jnp.maximum(jnp.maximum(jnp.max(s_ctx, axis=-1, keepdims=True),
                                    jnp.max(s_loc, axis=-1, keepdims=True)), sink)
        e_ctx = jnp.exp2(s_ctx - m)
        e_loc = jnp.exp2(s_loc - m)
        den = (jnp.sum(e_ctx, axis=-1, keepdims=True) + jnp.sum(e_loc, axis=-1, keepdims=True)
               + jnp.exp2(sink - m))
        inv = 1.0 / den
        o = (jnp.dot((e_ctx * inv).astype(BF16), vc, preferred_element_type=F32)
             + jnp.dot((e_loc * inv).astype(BF16), vl, preferred_element_type=F32))
        o_ref[0, qb * BLOCK:(qb + 1) * BLOCK, :] = jnp.concatenate(
            [o[r * BLOCK:(r + 1) * BLOCK] for r in range(R)], axis=1)


def window_attention(q, kpad, vpad, sink, B, G, C, L):
    BG, R, _, d = q.shape
    Lp = kpad.shape[1]
    QB = WIN_QUERY_BLOCKS
    kern = functools.partial(_window_attn_kernel, R=R, G=G, C=C, L=L, QB=QB)
    return pl.pallas_call(
        kern,
        grid=(BG, L // (BLOCK * QB)),
        in_specs=[pl.BlockSpec(memory_space=pltpu.SMEM),
                  pl.BlockSpec((1, R, QB * BLOCK, d), lambda b, i: (b, 0, i, 0)),
                  pl.BlockSpec((1, Lp, d), lambda b, i: (b, 0, 0)),
                  pl.BlockSpec((1, Lp, d), lambda b, i: (b, 0, 0))],
        out_specs=pl.BlockSpec((1, QB * BLOCK, R * d), lambda b, i: (b // G, i, b % G)),
        out_shape=jax.ShapeDtypeStruct((B, L, G * R * d), F32),
        compiler_params=_params("parallel", "arbitrary"),
        name="window_attn",
    )(sink.astype(F32), q, kpad, vpad)


_CAND_PAIRS = tuple((r, s) for r in range(PEER_TOPK) for s in range(PEER_TOPK)
                    if (r + 1) * (s + 1) <= PEER_TOPK)
_CAND_ROWS = -(-len(_CAND_PAIRS) // 8) * 8


def _top_rows(vs, n):
    vs = list(vs)
    iota = lax.broadcasted_iota(jnp.int32, vs[0].shape, 0).astype(F32)
    rows = [[] for _ in vs]
    for r in range(n):
        for i, v in enumerate(vs):
            m = jnp.max(v, axis=0, keepdims=True)
            rows[i].append(m)
            if r + 1 < n:
                first = jnp.min(jnp.where(v == m, iota, float(v.shape[0])), axis=0, keepdims=True)
                vs[i] = jnp.where(iota == first, -jnp.inf, v)
    return rows


def _peer_select_kernel(s_ref, ea_ref, q_ref, eb_ref, code_ref, c_ref, *, tm):
    n_lane = tm // 128

    def body(h, carry):
        tiles = [slice(t * 128, (t + 1) * 128) for t in range(n_lane)]
        ab = [s_ref[h, p, :, lanes] for lanes in tiles for p in (0, 1)]
        tops = _top_rows(ab, PEER_TOPK)
        for t, lanes in enumerate(tiles):
            a, b, ta, tb = ab[2 * t], ab[2 * t + 1], tops[2 * t], tops[2 * t + 1]
            c_ref[...] = jnp.full(c_ref.shape, -jnp.inf, F32)
            for k, (r, s) in enumerate(_CAND_PAIRS):
                c_ref[k:k + 1, :] = ta[r] + tb[s]
            best, = _top_rows((c_ref[...],), PEER_TOPK)
            tau = best[PEER_TOPK - 1]
            zsum = jnp.zeros_like(best[0])
            for bk in best:
                zsum = zsum + jnp.exp(bk - best[0])
            code_b = jnp.zeros(b.shape, F32)
            for s in range(PEER_TOPK):
                code_b = code_b + jnp.where(tb[s] > b, 1.0, 0.0)
            code_t = [jnp.zeros(tau.shape, F32)]
            for s in range(1, PEER_TOPK):
                code_t.append(jnp.where(tb[s] == tb[s - 1], code_t[s - 1], float(s)))
            q = jnp.full(a.shape, -1.0, F32)
            for r in reversed(range(PEER_TOPK)):
                q_r = jnp.full(tau.shape, -1.0, F32)
                for s in range(PEER_TOPK // (r + 1)):
                    q_r = jnp.maximum(q_r, jnp.where(ta[r] + tb[s] >= tau, code_t[s], -1.0))
                q = jnp.where(a == ta[r], q_r, q)
            q_ref[h, :, lanes] = q
            code_ref[h, :, lanes] = code_b.astype(BF16)
            ea_ref[h, :, lanes] = jnp.exp(a - ta[0]) / zsum
            eb_ref[h, :, lanes] = jnp.exp(b - tb[0]).astype(BF16)
        return carry

    lax.fori_loop(0, PEER_HEADS, body, 0)


def peer_select(st, tm):
    H, _, n, T = st.shape
    kern = functools.partial(_peer_select_kernel, tm=tm)
    return pl.pallas_call(
        kern,
        grid=(T // tm,),
        in_specs=[pl.BlockSpec((H, 2, n, tm), lambda t: (0, 0, 0, t))],
        out_specs=[pl.BlockSpec((H, n, tm), lambda t: (0, 0, t))] * 4,
        out_shape=[jax.ShapeDtypeStruct((H, n, T), dt) for dt in (F32, F32, BF16, BF16)],
        scratch_shapes=[pltpu.VMEM((_CAND_ROWS, 128), F32)],
        compiler_params=_params("parallel"),
        name="peer_select",
    )(st)


def _gelu_tanh(x):
    k = -2.0 * math.sqrt(2.0 / math.pi) * LOG2E
    return x / (1.0 + jnp.exp2(x * (k + (k * 0.044715) * (x * x))))


def _peer_kernel(x_ref, u_ref, vt_ref, ea_ref, q_ref, eb_ref, code_ref, res_ref, gate_ref, o_ref,
                 acc_ref, act_ref, g_ref, *, ni, tm):
    c = pl.program_id(1)

    @pl.when(c == 0)
    def _():
        acc_ref[...] = jnp.zeros_like(acc_ref)

    act_ref[...] = _gelu_tanh(lax.dot_general(u_ref[...], x_ref[...], (((1,), (1,)), ((), ())),
                                              preferred_element_type=F32))

    for ts in range(tm // 128):
        lanes = slice(ts * 128, (ts + 1) * 128)
        for ii in range(ni):
            rows = slice(ii * N_KEYS, (ii + 1) * N_KEYS)
            w = jnp.zeros((N_KEYS, 128), BF16)
            for h in range(PEER_HEADS):
                sel = code_ref[h, :, lanes] <= q_ref[h, ii:ii + 1, lanes].astype(BF16)
                gate = ea_ref[h, ii:ii + 1, lanes].astype(BF16) * eb_ref[h, :, lanes]
                w = w + jnp.where(sel, gate, jnp.zeros_like(gate))
            g_ref[rows, lanes] = w * act_ref[rows, lanes].astype(BF16)

    acc_ref[...] += jnp.dot(vt_ref[...], g_ref[...], preferred_element_type=F32)

    @pl.when(c == pl.num_programs(1) - 1)
    def _():
        o_ref[...] = res_ref[...] + gate_ref[0] * acc_ref[...].T


def peer_dense(x, u, vt, eat, qt, ebt, codet, res, gate, rows_per_seg, tm, ni):
    T, D = x.shape
    E = u.shape[0]
    ec = ni * N_KEYS
    kern = functools.partial(_peer_kernel, ni=ni, tm=tm)
    nseg = gate.shape[0]
    blocks_per_seg = rows_per_seg // tm
    rows_of_chunk = pl.BlockSpec((PEER_HEADS, ni, tm), lambda t, c: (0, c, t))
    return pl.pallas_call(
        kern,
        grid=(T // tm, E // ec),
        in_specs=[pl.BlockSpec((tm, D), lambda t, c: (t, 0)),
                  pl.BlockSpec((ec, D), lambda t, c: (c, 0)),
                  pl.BlockSpec((D, ec), lambda t, c: (0, c)),
                  rows_of_chunk,
                  rows_of_chunk,
                  pl.BlockSpec((PEER_HEADS, N_KEYS, tm), lambda t, c: (0, 0, t)),
                  pl.BlockSpec((PEER_HEADS, N_KEYS, tm), lambda t, c: (0, 0, t)),
                  pl.BlockSpec((tm, D), lambda t, c: (t, 0)),
                  pl.BlockSpec((1, 1, D), lambda t, c: (jnp.minimum(t // blocks_per_seg, nseg - 1), 0, 0))],
        out_specs=pl.BlockSpec((tm, D), lambda t, c: (t, 0)),
        out_shape=jax.ShapeDtypeStruct((T, D), F32),
        scratch_shapes=[pltpu.VMEM((D, tm), F32),
                        pltpu.VMEM((ec, tm), F32),
                        pltpu.VMEM((ec, tm), BF16)],
        compiler_params=_params("parallel", "arbitrary"),
        name="peer_dense",
    )(x, u, vt, eat, qt, ebt, codet, res, gate.reshape(nseg, 1, D))


def peer(qp, h_bf, keys, u_bf, vt_bf, res, gate, rows_per_seg, tm):
    T = qp.shape[0]
    dk = keys.shape[-1]
    kflat = keys.reshape(PEER_HEADS * 2, N_KEYS, dk).astype(BF16)
    st = keys_times_qT(kflat, qp, tm).reshape(PEER_HEADS, 2, N_KEYS, T)
    eat, qt, ebt, codet = peer_select(st, tm)
    return peer_dense(h_bf, u_bf, vt_bf, eat, qt, ebt, codet, res, gate, rows_per_seg, tm, 8)


def rmsnorm(x, g):
    return x * lax.rsqrt(jnp.mean(x * x, axis=-1, keepdims=True) + EPS) * g


def axial_rope(n_tok, dim):
    rows = n_tok // GRID_W
    row = jnp.repeat(jnp.arange(rows, dtype=F32), GRID_W)
    col = jnp.tile(jnp.arange(GRID_W, dtype=F32), rows)
    axis_dim = dim // 2
    inv_freq = ROPE_THETA ** (-jnp.arange(0, axis_dim, 2, dtype=F32) / axis_dim)
    ang = jnp.concatenate([row[:, None] * inv_freq, col[:, None] * inv_freq], axis=-1)
    return jnp.cos(ang), jnp.sin(ang)


def apply_rope(x, cos, sin):
    shape = (1, x.shape[1]) + (1,) * (x.ndim - 3) + (cos.shape[-1],)
    c, s = cos.reshape(shape), sin.reshape(shape)
    x1, x2 = jnp.split(x, 2, axis=-1)
    return jnp.concatenate([x1 * c - x2 * s, x1 * s + x2 * c], axis=-1)


def short_conv(u, w, b):
    up = jnp.pad(u, ((0, 0), (1, 1), (0, 0)))
    return up[:, :-2] * w[0] + up[:, 1:-1] * w[1] + up[:, 2:] * w[2] + b


def hyena_filters(n_tok, w1, b1, w2, b2, w3, b3, freq):
    hp = lax.Precision.HIGHEST
    t = jnp.linspace(0.0, 1.0, n_tok, dtype=F32)[:, None]
    w = (2.0 * math.pi / n_tok) * jnp.arange(n_tok, dtype=F32)[:, None]
    f = jnp.linspace(1e-4, HY_BANDS - 1, HY_BANDS, dtype=F32)[None, :]
    feat = jnp.concatenate([t, jnp.cos(f * w), -jnp.sin(f * w)], axis=-1)
    hdn = jnp.sin(freq * (jnp.dot(feat, w1, precision=hp) + b1))
    hdn = jnp.sin(freq * (jnp.dot(hdn, w2, precision=hp) + b2))
    filt = (jnp.dot(hdn, w3, precision=hp) + b3).reshape(n_tok, HY_ORDER, 2, HY_CH)
    deltas = jnp.abs(jnp.linspace(math.log(HY_TARGET) / HY_SLOW, math.log(HY_TARGET) / HY_FAST, HY_CH,
                                  dtype=F32))
    filt = filt * jnp.exp(-t[:, :, None, None] * deltas)
    fwd, bwd = filt[:, :, 0], filt[:, :, 1]
    kfull = jnp.concatenate([fwd, jnp.zeros_like(fwd[:1]), bwd[1:][::-1]], axis=0)
    return kfull / jnp.sum(jnp.abs(kfull), axis=0, keepdims=True)


def hyena_filters_t(n_tok, w1, b1, w2, b2, w3, b3, freq):
    hp = lax.Precision.HIGHEST
    t = jnp.linspace(0.0, 1.0, n_tok, dtype=F32)[:, None]
    w = (2.0 * math.pi / n_tok) * jnp.arange(n_tok, dtype=F32)[:, None]
    f = jnp.linspace(1e-4, HY_BANDS - 1, HY_BANDS, dtype=F32)[None, :]
    feat = jnp.concatenate([t, jnp.cos(f * w), -jnp.sin(f * w)], axis=-1)
    hdn = jnp.sin(freq * (jnp.dot(feat, w1, precision=hp) + b1))
    hdn = jnp.sin(freq * (jnp.dot(hdn, w2, precision=hp) + b2))
    deltas = jnp.abs(jnp.linspace(math.log(HY_TARGET) / HY_SLOW, math.log(HY_TARGET) / HY_FAST, HY_CH,
                                  dtype=F32))
    w3t = jnp.transpose(w3.reshape(-1, HY_ORDER, 2, HY_CH), (2, 1, 3, 0))
    b3t = jnp.transpose(b3.reshape(HY_ORDER, 2, HY_CH), (1, 0, 2))[..., None]

    def half(d, hidden, times):
        decay = jnp.exp(-deltas[:, None] * times[None, :])
        return (jnp.einsum('ock,nk->ocn', w3t[d], hidden, precision=hp) + b3t[d]) * decay

    fwd = half(0, hdn, t[:, 0])
    bwd_rev = half(1, hdn[::-1], t[::-1, 0])
    kfull = jnp.concatenate([fwd, jnp.zeros_like(fwd[..., :1]), bwd_rev[..., :n_tok - 1]], axis=-1)
    kfull = kfull / jnp.sum(jnp.abs(kfull), axis=-1, keepdims=True)
    return kfull.reshape(HY_ORDER * HY_CH, 2 * n_tok)


def long_conv(z, kf):
    n = z.shape[1]
    zf = jnp.fft.rfft(z, n=2 * n, axis=1)
    hf = jnp.fft.rfft(kf, n=2 * n, axis=0)
    return jnp.fft.irfft(zf * hf[None], n=2 * n, axis=1)[:, :n]


def hyena_small(u, conv_w, conv_b, filter_params, bias):
    n = u.shape[1]
    u = short_conv(u, conv_w, conv_b)
    v, x1, x2 = jnp.split(u, 3, axis=-1)
    kfull = hyena_filters(n, *filter_params)
    z = x1 * (long_conv(v, kfull[:, 0]) + bias[0] * v)
    return x2 * (long_conv(z, kfull[:, 1]) + bias[1] * z)


def _hi_lo(x):
    hi = x.astype(BF16)
    return hi, (x - hi.astype(F32)).astype(BF16)


def _dot3(a, b):
    d = lambda x, y: jnp.dot(x, y, preferred_element_type=F32)
    return d(a[0], b[0]) + d(a[1], b[0]) + d(a[0], b[1])


def _dft_constants(n_tok):
    n = 2 * n_tok
    n1 = n // 128
    a1 = 2.0 * np.pi * np.outer(np.arange(n1), np.arange(n1)) / n1
    a2 = 2.0 * np.pi * np.outer(np.arange(128), np.arange(128)) / 128
    at = 2.0 * np.pi * np.outer(np.arange(n1), np.arange(128)) / n
    c1, s1, c2, s2 = np.cos(a1), np.sin(a1), np.cos(a2), np.sin(a2)
    pair = lambda m: _hi_lo(jnp.asarray(m, F32))
    return dict(
        f1_half=pair(np.concatenate([c1[:, :n1 // 2], -s1[:, :n1 // 2]], axis=0)),
        f1_full=pair(np.concatenate([c1, -s1], axis=0)),
        m_fwd=pair(np.block([[c2, -s2], [s2, c2]])),
        m_inv=pair(np.block([[c2, s2], [-s2, c2]])),
        g_half=pair(np.concatenate([c1[:n1 // 2], -s1[:n1 // 2]], axis=1)),
        tr=jnp.asarray(np.cos(at), F32), ti=jnp.asarray(-np.sin(at), F32))


def _dft_fwd(seqs, f1, tr, ti, m_fwd):
    n1 = tr.shape[0]
    y = _dot3(f1, _hi_lo(jnp.concatenate(seqs, axis=1)))
    rows = []
    for k in range(len(seqs)):
        yr, yi = y[:n1, k * 128:(k + 1) * 128], y[n1:, k * 128:(k + 1) * 128]
        rows.append(jnp.concatenate([yr * tr - yi * ti, yr * ti + yi * tr], axis=1))
    return _dot3(_hi_lo(jnp.concatenate(rows, axis=0)), m_fwd)


def _dft_inv_half(p, g_half, tr, ti, m_inv):
    n1 = tr.shape[0]
    u = _dot3(_hi_lo(p), m_inv)
    cols = []
    for k in range(p.shape[0] // n1):
        ur, ui = u[k * n1:(k + 1) * n1, :128], u[k * n1:(k + 1) * n1, 128:]
        cols.append(jnp.concatenate([ur * tr + ui * ti, ui * tr - ur * ti], axis=0))
    return _dot3(g_half, _hi_lo(jnp.concatenate(cols, axis=1))) * (1.0 / (n1 * 128))


def _pairs(refs):
    return (refs[0][...], refs[1][...])


def _spectrum_kernel(a_ref, f1h, f1l, tr_ref, ti_ref, mh, ml, o_ref):
    cb, n1 = a_ref.shape[0], tr_ref.shape[0]
    x = _dft_fwd([a_ref[k] for k in range(cb)], _pairs((f1h, f1l)), tr_ref[...], ti_ref[...],
                 _pairs((mh, ml)))
    o_ref[...] = x.reshape(cb, n1, 256)


def filter_spectrum(kf, consts, cb):
    items, n1, _ = kf.shape
    full = lambda shape: pl.BlockSpec(shape, lambda i: (0,) * len(shape))
    f1, m = consts["f1_full"], consts["m_fwd"]
    return pl.pallas_call(
        _spectrum_kernel,
        grid=(items // cb,),
        in_specs=[pl.BlockSpec((cb, n1, 128), lambda i: (i, 0, 0)),
                  full(f1[0].shape), full(f1[1].shape), full((n1, 128)), full((n1, 128)),
                  full(m[0].shape), full(m[1].shape)],
        out_specs=pl.BlockSpec((cb, n1, 256), lambda i: (i, 0, 0)),
        out_shape=jax.ShapeDtypeStruct((items, n1, 256), F32),
        compiler_params=_params("parallel"),
        name="filter_spectrum",
    )(kf, f1[0], f1[1], consts["tr"], consts["ti"], m[0], m[1])


def _conv_gate_kernel(bias_ref, u_ref, g_ref, h_ref, f1h, f1l, tr_ref, ti_ref, mfh, mfl, mih, mil, gh, gl,
                      o_ref, *, cb):
    tr, ti = tr_ref[...], ti_ref[...]
    n1 = tr.shape[0]
    c0 = pl.program_id(1) * cb
    u = [u_ref[0, k] for k in range(cb)]
    x = _dft_fwd(u, _pairs((f1h, f1l)), tr, ti, _pairs((mfh, mfl)))
    prod = []
    for k in range(cb):
        xr, xi = x[k * n1:(k + 1) * n1, :128], x[k * n1:(k + 1) * n1, 128:]
        hr, hi = h_ref[k, :, :128], h_ref[k, :, 128:]
        prod.append(jnp.concatenate([xr * hr - xi * hi, xr * hi + xi * hr], axis=1))
    y = _dft_inv_half(jnp.concatenate(prod, axis=0), _pairs((gh, gl)), tr, ti, _pairs((mih, mil)))
    for k in range(cb):
        o_ref[0, k] = g_ref[0, k] * (y[:, k * 128:(k + 1) * 128] + bias_ref[c0 + k] * u[k])


def conv_gate(u, gate, spec, bias, consts, cb):
    B, C, half, _ = u.shape
    n1 = 2 * half
    full = lambda shape: pl.BlockSpec(shape, lambda b, c: (0,) * len(shape))
    seq = pl.BlockSpec((1, cb, half, 128), lambda b, c: (b, c, 0, 0))
    mats = [*consts["f1_half"], consts["tr"], consts["ti"], *consts["m_fwd"], *consts["m_inv"],
            *consts["g_half"]]
    return pl.pallas_call(
        functools.partial(_conv_gate_kernel, cb=cb),
        grid=(B, C // cb),
        in_specs=[pl.BlockSpec(memory_space=pltpu.SMEM), seq, seq,
                  pl.BlockSpec((cb, n1, 256), lambda b, c: (c, 0, 0))] + [full(m.shape) for m in mats],
        out_specs=seq,
        out_shape=jax.ShapeDtypeStruct(u.shape, F32),
        compiler_params=_params("parallel", "arbitrary"),
        name="conv_gate",
    )(bias.astype(F32), u, gate, spec, *mats)


def hyena(u, conv_w, conv_b, filter_params, bias):
    B, n, _ = u.shape
    half = n // 128
    consts = _dft_constants(n)
    u = short_conv(u, conv_w, conv_b)
    seqs = jnp.transpose(u, (0, 2, 1)).reshape(B, 3, HY_CH, half, 128)
    v, x1, x2 = seqs[:, 0], seqs[:, 1], seqs[:, 2]
    kf = hyena_filters_t(n, *filter_params).reshape(HY_ORDER * HY_CH, 2 * half, 128)
    spec = filter_spectrum(kf, consts, HY_CH_BLOCK).reshape(HY_ORDER, HY_CH, 2 * half, 256)
    z = conv_gate(v, x1, spec[0], bias[0], consts, HY_CH_BLOCK)
    o = conv_gate(z, x2, spec[1], bias[1], consts, HY_CH_BLOCK)
    return jnp.transpose(o.reshape(B, HY_CH, n), (0, 2, 1))


def _head_major(t):
    B, L, H, d = t.shape
    return jnp.transpose(t, (0, 2, 1, 3)).reshape(B * H, L, d)


def _head_major_t(t):
    B, L, H, d = t.shape
    return jnp.transpose(t, (0, 2, 3, 1)).reshape(B * H, d, L)


def _q_layout(t, G, R):
    B, L, _, d = t.shape
    return jnp.transpose(t.reshape(B, L, G, R, d), (0, 2, 3, 1, 4)).reshape(B * G, R, L, d)


def _key_chunk(n_keys):
    for c in range(KEY_CHUNK_CAP, 0, -128):
        if n_keys % c == 0:
            return c
    raise ValueError(f"no key chunk for {n_keys}")


def kernel(x, c, ctx, c_ctx, w_mod, b_mod, g_norm_mix, w_in, g_qk_diff, lambda_diff, g_qk_win, sink_win, g_qk_glob, hy_conv_w, hy_conv_b, hy_w1, hy_b1, hy_w2, hy_b2, hy_w3, hy_b3, hy_freq, hy_bias, g_mix_out, w_out, g_norm_ffn, peer_wq, peer_keys, peer_u, peer_v):
    B, L, D = x.shape
    C = ctx.shape[1]
    depth = w_mod.shape[0]
    TM = 512
    n_lat = B * L
    n_ctx = B * C
    assert L % TM == 0 and n_ctx % TM == 0
    nseg = B + 1

    def rope_rows(dim):
        cos, sin = axial_rope(L, dim)
        return (jnp.concatenate([jnp.tile(cos, (B, 1)), jnp.ones((n_ctx, dim // 2), F32)], axis=0),
                jnp.concatenate([jnp.tile(sin, (B, 1)), jnp.zeros((n_ctx, dim // 2), F32)], axis=0))

    rope_rows_half = rope_rows(DIFF_QK_DIM)
    rope_rows_full = rope_rows(HEAD_DIM)
    sc = jnp.concatenate([jax.nn.silu(c), jax.nn.silu(c_ctx)[None]], axis=0)
    sc = jnp.pad(sc, ((0, 8 - nseg), (0, 0))).astype(BF16)

    xall = jnp.concatenate([x.reshape(n_lat, D), ctx.reshape(n_ctx, D)], axis=0)

    for i in range(depth):
        want_ctx = i < depth - 1
        lambda_init = 0.8 - 0.6 * math.exp(-0.3 * i)
        mod = mm(sc, w_mod[i].astype(BF16), 8, 1024)[:nseg] + b_mod[i]
        sh1, s1, g1, sh2, s2, g2 = jnp.split(mod, N_MOD, axis=-1)

        groups = normmod_mm(xall, g_norm_mix[i], sh1, s1, w_in[i].astype(BF16), L, TM, F32, False,
                            widths=(W_DIFF, W_WIN, W_GLOB, W_HY))
        ga_, gw_, gg_, gh_ = groups
        ph, phc = gh_[:n_lat].reshape(B, L, -1), gh_[n_lat:].reshape(B, C, -1)

        def gains(g_qk, n_q, n_k, d):
            return jnp.concatenate([jnp.tile(g_qk[0], n_q) * (d ** -0.5 * LOG2E), jnp.tile(g_qk[1], n_k)])

        def split_rows(t, *shape):
            return t[:n_lat].reshape(B, L, *shape), t[n_lat:].reshape(B, C, *shape)

        qka, qkac = split_rows(head_prep(ga_, 2 * GROUP_W, DIFF_QK_DIM,
                                         gains(g_qk_diff[i], 2 * DIFF_HEADS, 2 * DIFF_HEADS, DIFF_QK_DIM),
                                         *rope_rows_half, TM), 2, DIFF_HEADS, 2, DIFF_QK_DIM)
        va, vac = split_rows(ga_[:, 2 * GROUP_W:].astype(BF16), DIFF_HEADS, HEAD_DIM)
        lam_vec = lambda_diff[i]
        lam = (jnp.exp(jnp.sum(lam_vec[0] * lam_vec[1])) - jnp.exp(jnp.sum(lam_vec[2] * lam_vec[3]))
               + lambda_init)
        coefs = jnp.stack([jnp.ones((), F32), -lam])

        def by_branch(t):
            return jnp.stack([_head_major(t[..., b_, :]) for b_ in range(2)])

        kka = jnp.concatenate([qkac[:, :, 1], qka[:, :, 1]], axis=1)
        vva = jnp.concatenate([vac, va], axis=1)
        oa = dense_attention(by_branch(qka[:, :, 0])[:, :, None], by_branch(kka), _head_major_t(vva), coefs, None,
                             B, DIFF_HEADS, 2, ATTN_TQ_DIFF, _key_chunk(C + L))

        def gqa_parts(g_, g_qk, n_q, n_kv):
            wq, wk = n_q * HEAD_DIM, n_kv * HEAD_DIM
            qk, qkc = split_rows(head_prep(g_, wq + wk, HEAD_DIM, gains(g_qk, n_q, n_kv, HEAD_DIM),
                                           *rope_rows_full, TM), wq + wk)
            v, vc = split_rows(g_[:, wq + wk:].astype(BF16), n_kv, HEAD_DIM)
            heads = lambda t, n: t.reshape(t.shape[0], t.shape[1], n, HEAD_DIM)
            return (heads(qk[..., :wq], n_q), heads(qk[..., wq:], n_kv), v,
                    heads(qkc[..., :wq], n_q), heads(qkc[..., wq:], n_kv), vc)

        qw, kw, vw, qwc, kwc, vwc = gqa_parts(gw_, g_qk_win[i], WIN_HEADS, WIN_KV_HEADS)
        zblk = jnp.zeros((B, BLOCK, WIN_KV_HEADS, HEAD_DIM), BF16)
        kpad = _head_major(jnp.concatenate([kwc, zblk, kw, zblk], axis=1))
        vpad = _head_major(jnp.concatenate([vwc, zblk, vw, zblk], axis=1))
        Rw = WIN_HEADS // WIN_KV_HEADS
        ob = window_attention(_q_layout(qw, WIN_KV_HEADS, Rw), kpad, vpad, sink_win[i], B, WIN_KV_HEADS, C, L)

        qg, kg, vg, qgc, kgc, vgc = gqa_parts(gg_, g_qk_glob[i], GLOB_HEADS, GLOB_KV_HEADS)
        Rg = GLOB_HEADS // GLOB_KV_HEADS
        one = jnp.ones((1,), F32)
        kkg = _head_major(jnp.concatenate([kgc, kg], axis=1))[None]
        vvg = _head_major_t(jnp.concatenate([vgc, vg], axis=1))
        og = dense_attention(_q_layout(qg, GLOB_KV_HEADS, Rg)[None], kkg, vvg,
                             one, None, B, GLOB_KV_HEADS, 1, ATTN_TQ_GLOB, _key_chunk(C + L))

        filt = (hy_w1[i], hy_b1[i], hy_w2[i], hy_b2[i], hy_w3[i], hy_b3[i], hy_freq[i])
        oh = hyena(ph, hy_conv_w[i], hy_conv_b[i], filt, hy_bias[i])

        mixed = [jnp.concatenate([oa, ob, og, oh], axis=-1).reshape(n_lat, D)]
        if want_ctx:
            oac = dense_attention(by_branch(qkac[:, :, 0])[:, :, None], by_branch(qkac[:, :, 1]),
                                  _head_major_t(vac), coefs, None, B, DIFF_HEADS, 2, C, C)
            obc = dense_attention(_q_layout(qwc, WIN_KV_HEADS, Rw)[None], _head_major(kwc)[None],
                                  _head_major_t(vwc), one, sink_win[i], B, WIN_KV_HEADS, 1, C, C)
            ogc = dense_attention(_q_layout(qgc, GLOB_KV_HEADS, Rg)[None], _head_major(kgc)[None],
                                  _head_major_t(vgc), one, None, B, GLOB_KV_HEADS, 1, C, C)
            ohc = hyena_small(phc, hy_conv_w[i], hy_conv_b[i], filt, hy_bias[i])
            mixed.append(jnp.concatenate([oac, obc, ogc, ohc], axis=-1).reshape(n_ctx, D))
        o = jnp.concatenate(mixed, axis=0)
        n_rows = o.shape[0]

        oh_ = rmsnorm(o.reshape(n_rows, N_OUT_HEADS, HEAD_DIM), g_mix_out[i].reshape(N_OUT_HEADS, HEAD_DIM))
        head_scale = jnp.where(jnp.arange(N_OUT_HEADS) < DIFF_HEADS, 1.0 - lambda_init, 1.0)[:, None]
        om = (oh_ * head_scale.astype(F32)).reshape(n_rows, D)
        xcur = mm_residual(om.astype(BF16), w_out[i].astype(BF16), xall, g1, L, TM)

        qp, h2 = normmod_mm(xcur, g_norm_ffn[i], sh2, s2, peer_wq[i].astype(BF16), L, TM, BF16, True)
        xall = peer(qp, h2, peer_keys[i], peer_u[i].astype(BF16), jnp.transpose(peer_v[i].astype(BF16)),
                    xcur, g2, L, TM)

    return xall[:n_lat].reshape(B, L, D)
```

```python
import functools
import math

import jax
import jax.numpy as jnp
import numpy as np
from jax import lax
from jax.experimental import pallas as pl
from jax.experimental.pallas import tpu as pltpu

F32 = jnp.float32
BF16 = jnp.bfloat16

GRID_W = 64
HEAD_DIM = 64
BLOCK = 128
WINDOW = 128
ROPE_THETA = 10000.0
EPS = 1e-6
NEG_INF = -1e30
N_MOD = 6
DIFF_HEADS = 4
DIFF_QK_DIM = 32
WIN_HEADS = 4
WIN_KV_HEADS = 2
GLOB_HEADS = 4
GLOB_KV_HEADS = 2
N_OUT_HEADS = 16
HY_CH = 256
HY_ORDER = 2
HY_BANDS = 16
HY_TARGET = 1e-2
HY_FAST = 0.3
HY_SLOW = 1.5
W_DIFF = 768
W_WIN = 512
W_GLOB = 512
W_HY = 768
GROUP_W = 256
PEER_HEADS = 8
N_KEYS = 128
PEER_TOPK = 16

VMEM_LIMIT = 56 * 1024 * 1024
ATTN_TQ_DIFF = 512
ATTN_TQ_GLOB = 256
KEY_CHUNK_CAP = 1408
HY_CH_BLOCK = 16
WIN_QUERY_BLOCKS = 4


def _params(*sem):
    return pltpu.CompilerParams(dimension_semantics=sem, vmem_limit_bytes=VMEM_LIMIT)


def _mm_kernel(a_ref, b_ref, o_ref):
    o_ref[...] = jnp.dot(a_ref[...], b_ref[...], preferred_element_type=F32)


def mm(a, b, tm, tn):
    M, K = a.shape
    N = b.shape[1]
    return pl.pallas_call(
        _mm_kernel,
        grid=(M // tm, N // tn),
        in_specs=[pl.BlockSpec((tm, K), lambda i, j: (i, 0)),
                  pl.BlockSpec((K, tn), lambda i, j: (0, j))],
        out_specs=pl.BlockSpec((tm, tn), lambda i, j: (i, j)),
        out_shape=jax.ShapeDtypeStruct((M, N), F32),
        compiler_params=_params("parallel", "arbitrary"),
        name="mm",
    )(a, b)


def _seg_spec(rows_per_seg, tm, nseg, D):
    blocks_per_seg = rows_per_seg // tm
    return pl.BlockSpec((1, 1, D), lambda i: (jnp.minimum(i // blocks_per_seg, nseg - 1), 0, 0))


def _normmod_mm_kernel(x_ref, g_ref, sh_ref, sc_ref, w_ref, *outs, widths, with_h):
    x = x_ref[...]
    y = x * lax.rsqrt(jnp.mean(x * x, axis=-1, keepdims=True) + EPS) * g_ref[...]
    h = (y * (1.0 + sc_ref[0]) + sh_ref[0]).astype(BF16)
    if with_h:
        outs[-1][...] = h
    r = jnp.dot(h, w_ref[...], preferred_element_type=F32)
    lo = 0
    for o_ref, n in zip(outs, widths):
        o_ref[...] = r[:, lo:lo + n].astype(o_ref.dtype)
        lo += n


def normmod_mm(x, g, shift, scale, w, rows_per_seg, tm, out_dtype, with_h, widths=None):
    M, D = x.shape
    N = w.shape[1]
    widths = (N,) if widths is None else tuple(widths)
    nseg = shift.shape[0]
    seg = _seg_spec(rows_per_seg, tm, nseg, D)
    row_block = lambda n: pl.BlockSpec((tm, n), lambda i: (i, 0))
    out_specs = [row_block(n) for n in widths]
    out_shape = [jax.ShapeDtypeStruct((M, n), out_dtype) for n in widths]
    if with_h:
        out_specs.append(row_block(D))
        out_shape.append(jax.ShapeDtypeStruct((M, D), BF16))
    return pl.pallas_call(
        functools.partial(_normmod_mm_kernel, widths=widths, with_h=with_h),
        grid=(M // tm,),
        in_specs=[row_block(D), pl.BlockSpec((1, D), lambda i: (0, 0)), seg, seg,
                  pl.BlockSpec((D, N), lambda i: (0, 0))],
        out_specs=out_specs,
        out_shape=out_shape,
        compiler_params=_params("parallel"),
        name="normmod_mm",
    )(x, g.reshape(1, D), shift.reshape(nseg, 1, D), scale.reshape(nseg, 1, D), w)


def _mm_residual_kernel(a_ref, w_ref, x_ref, gate_ref, o_ref):
    o_ref[...] = x_ref[...] + gate_ref[0] * jnp.dot(a_ref[...], w_ref[...], preferred_element_type=F32)


def mm_residual(a, w, x, gate, rows_per_seg, tm):
    M, K = a.shape
    N = w.shape[1]
    nseg = gate.shape[0]
    row_block = lambda n: pl.BlockSpec((tm, n), lambda i: (i, 0))
    return pl.pallas_call(
        _mm_residual_kernel,
        grid=(M // tm,),
        in_specs=[row_block(K), pl.BlockSpec((K, N), lambda i: (0, 0)), row_block(N),
                  _seg_spec(rows_per_seg, tm, nseg, N)],
        out_specs=row_block(N),
        out_shape=jax.ShapeDtypeStruct((M, N), F32),
        compiler_params=_params("parallel"),
        name="mm_residual",
    )(a, w, x, gate.reshape(nseg, 1, N))


def _scores_kernel(k_ref, q_ref, o_ref):
    d = k_ref.shape[-1]
    for p in range(k_ref.shape[0]):
        o_ref[p] = lax.dot_general(k_ref[p], q_ref[:, p * d:(p + 1) * d], (((1,), (1,)), ((), ())),
                                   preferred_element_type=F32)


def keys_times_qT(keys, q, tn):
    P, n, d = keys.shape
    T = q.shape[0]
    return pl.pallas_call(
        _scores_kernel,
        grid=(T // tn,),
        in_specs=[pl.BlockSpec((P, n, d), lambda j: (0, 0, 0)),
                  pl.BlockSpec((tn, P * d), lambda j: (j, 0))],
        out_specs=pl.BlockSpec((P, n, tn), lambda j: (0, 0, j)),
        out_shape=jax.ShapeDtypeStruct((P, n, T), F32),
        compiler_params=_params("parallel"),
        name="peer_scores_t",
    )(keys, q)


def _head_prep_kernel(t_ref, cos_ref, sin_ref, gain_ref, bd_ref, pm_ref, ex_ref, o_ref):
    d3 = lambda parts, m: sum(jnp.dot(p, m, preferred_element_type=F32) for p in parts)

    def split3(v):
        hi, lo = _hi_lo(v)
        rest = v - hi.astype(F32) - lo.astype(F32)
        return hi, lo, rest.astype(BF16)

    x = t_ref[...]
    ms = d3(_hi_lo(x * x), bd_ref[...])
    y = x * lax.rsqrt(ms + EPS) * gain_ref[...]
    ex = ex_ref[...]
    cos = d3(split3(cos_ref[...]), ex)
    sin = d3(split3(sin_ref[...]), ex)
    o_ref[...] = (y * cos + d3(_hi_lo(y), pm_ref[...]) * sin).astype(BF16)


def head_prep(t, width, d, gain, cos_rows, sin_rows, tm):
    rows = t.shape[0]
    hd = d // 2
    col = np.arange(width)
    same_head = (col[:, None] // d) == (col[None, :] // d)
    bd = jnp.asarray(same_head / d, BF16)
    src = np.where((col % d) < hd, col + hd, col - hd)
    pm = np.zeros((width, width), np.float32)
    pm[src, col] = np.where((col % d) < hd, -1.0, 1.0)
    ex = np.zeros((hd, width), np.float32)
    ex[col % hd, col] = 1.0
    full = lambda a: pl.BlockSpec(a.shape, lambda i: (0,) * a.ndim)
    consts = (gain.reshape(1, width).astype(F32), bd, jnp.asarray(pm, BF16), jnp.asarray(ex, BF16))
    return pl.pallas_call(
        _head_prep_kernel,
        grid=(rows // tm,),
        in_specs=[pl.BlockSpec((tm, width), lambda i: (i, 0)),
                  pl.BlockSpec((tm, hd), lambda i: (i, 0)),
                  pl.BlockSpec((tm, hd), lambda i: (i, 0))] + [full(c) for c in consts],
        out_specs=pl.BlockSpec((tm, width), lambda i: (i, 0)),
        out_shape=jax.ShapeDtypeStruct((rows, width), BF16),
        compiler_params=_params("parallel"),
        name="head_prep",
    )(t, cos_rows, sin_rows, *consts)


LOG2E = math.log2(math.e)


def _dense_attn_kernel(coef_ref, sink_ref, q_ref, k_ref, vt_ref, o_ref, sa_ref, sb_ref, *, nbr, R, G,
                       HP, d, tq, tk, nk, has_sink):
    cols = R * tq
    dv = vt_ref.shape[1]
    kw = k_ref.shape[-1]
    g0 = (pl.program_id(0) % (G // HP)) * HP
    nt = (((1,), (1,)), ((), ()))
    bufs = (sa_ref, sb_ref)
    lane = lax.broadcasted_iota(jnp.int32, (1, kw), 1)
    heads = []
    for hp in range(HP):
        out = jnp.zeros((dv, cols), F32)
        for br in range(nbr):
            k_lo = (hp * nbr + br) * d
            parts = []
            for r in range(R):
                q_lo = ((hp * R + r) * nbr + br) * d
                blk = q_ref[0, :, (q_lo // kw) * kw:(q_lo // kw + 1) * kw]
                if (q_lo - k_lo) % kw:
                    sh = (q_lo - k_lo) % kw
                    blk = jnp.concatenate([blk[:, sh:], blk[:, :sh]], axis=1)
                parts.append(jnp.where((lane >= k_lo) & (lane < k_lo + d), blk, jnp.zeros_like(blk)))
            q = parts[0] if R == 1 else jnp.concatenate(parts, axis=0)
            if has_sink:
                m = jnp.concatenate([jnp.full((1, tq), sink_ref[(g0 + hp) * R + r] * LOG2E, F32)
                                     for r in range(R)], axis=1)
                l = jnp.ones((1, cols), F32)
            else:
                m = jnp.full((1, cols), NEG_INF, F32)
                l = jnp.zeros((1, cols), F32)
            acc = jnp.zeros((dv, cols), F32)
            bufs[0][...] = lax.dot_general(k_ref[0, 0:tk, :], q, nt, preferred_element_type=F32)
            for j in range(nk):
                if j + 1 < nk:
                    bufs[(j + 1) % 2][...] = lax.dot_general(k_ref[0, (j + 1) * tk:(j + 2) * tk, :], q,
                                                             nt, preferred_element_type=F32)
                s = bufs[j % 2][...]
                m_new = jnp.maximum(m, jnp.max(s, axis=0, keepdims=True))
                alpha = jnp.exp2(m - m_new)
                p = jnp.exp2(s - m_new)
                l = alpha * l + jnp.sum(p, axis=0, keepdims=True)
                acc = alpha * acc + jnp.dot(vt_ref[hp, :, j * tk:(j + 1) * tk], p.astype(BF16),
                                            preferred_element_type=F32)
                m = m_new
            out = out + coef_ref[br] * (acc / l)
        out_t = out.T
        heads += [out_t[r * tq:(r + 1) * tq] for r in range(R)]
    o_ref[0] = jnp.concatenate(heads, axis=1)


def dense_attention(qsrc, ksrc, k_lane0, vt, coefs, sink, nbr, G, R, d, tq, tk):
    B, Lq, _ = qsrc.shape
    Lk = ksrc.shape[1]
    dv = vt.shape[1]
    HP = 128 // (nbr * d)
    assert G % HP == 0 and (HP * R * dv) % 128 == 0 and k_lane0 % 128 == 0
    has_sink = sink is not None
    if sink is None:
        sink = jnp.zeros((1,), F32)
    kern = functools.partial(_dense_attn_kernel, nbr=nbr, R=R, G=G, HP=HP, d=d, tq=tq, tk=tk, nk=Lk // tk,
                             has_sink=has_sink)
    gp = G // HP
    qw = HP * R * nbr * d
    return pl.pallas_call(
        kern,
        grid=(B * gp, Lq // tq),
        in_specs=[pl.BlockSpec(memory_space=pltpu.SMEM),
                  pl.BlockSpec(memory_space=pltpu.SMEM),
                  pl.BlockSpec((1, tq, qw), lambda b, i: (b // gp, i, b % gp)),
                  pl.BlockSpec((1, Lk, 128), lambda b, i: (b // gp, 0, k_lane0 // 128 + b % gp)),
                  pl.BlockSpec((HP, dv, Lk), lambda b, i: (b, 0, 0))],
        out_specs=pl.BlockSpec((1, tq, HP * R * dv), lambda b, i: (b // gp, i, b % gp)),
        out_shape=jax.ShapeDtypeStruct((B, Lq, G * R * dv), F32),
        scratch_shapes=[pltpu.VMEM((tk, R * tq), F32), pltpu.VMEM((tk, R * tq), F32)],
        compiler_params=_params("parallel", "arbitrary"),
        name="dense_attn",
    )(coefs.astype(F32), sink.astype(F32), qsrc, ksrc, vt)


def _window_attn_kernel(sink_ref, q_ref, k_ref, v_ref, o_ref, *, R, G, C, L, QB):
    rows = R * BLOCK
    d = q_ref.shape[-1]
    g = pl.program_id(0) % G
    kc = k_ref[0, 0:C, :]
    vc = v_ref[0, 0:C, :]
    nt = (((1,), (1,)), ((), ()))
    qi = lax.broadcasted_iota(jnp.int32, (rows, 3 * BLOCK), 0) & (BLOCK - 1)
    kj = lax.broadcasted_iota(jnp.int32, (rows, 3 * BLOCK), 1)
    in_window = jnp.abs(kj - BLOCK - qi) <= WINDOW
    sink = jnp.concatenate([jnp.full((BLOCK, 1), sink_ref[g * R + r] * LOG2E, F32) for r in range(R)],
                           axis=0)
    for qb in range(QB):
        n = pl.program_id(1) * QB + qb
        q = q_ref[0, :, qb * BLOCK:(qb + 1) * BLOCK, :].reshape(rows, d)
        start = pl.multiple_of(C + n * BLOCK, BLOCK)
        kl = k_ref[0, pl.ds(start, 3 * BLOCK), :]
        vl = v_ref[0, pl.ds(start, 3 * BLOCK), :]
        s_ctx = lax.dot_general(q, kc, nt, preferred_element_type=F32)
        s_loc = lax.dot_general(q, kl, nt, preferred_element_type=F32)
        kpos = (n - 1) * BLOCK + kj
        s_loc = jnp.where(in_window & (kpos >= 0) & (kpos < L), s_loc, NEG_INF)
        m = jnp.maximum(jnp.maximum(jnp.max(s_ctx, axis=-1, keepdims=True),
                                    jnp.max(s_loc, axis=-1, keepdims=True)), sink)
        e_ctx = jnp.exp2(s_ctx - m)
        e_loc = jnp.exp2(s_loc - m)
        den = (jnp.sum(e_ctx, axis=-1, keepdims=True) + jnp.sum(e_loc, axis=-1, keepdims=True)
               + jnp.exp2(sink - m))
        inv = 1.0 / den
        o = (jnp.dot((e_ctx * inv).astype(BF16), vc, preferred_element_type=F32)
             + jnp.dot((e_loc * inv).astype(BF16), vl, preferred_element_type=F32))
        o_ref[0, qb * BLOCK:(qb + 1) * BLOCK, :] = jnp.concatenate(
            [o[r * BLOCK:(r + 1) * BLOCK] for r in range(R)], axis=1)


def window_attention(q, kpad, vpad, sink, B, G, C, L):
    BG, R, _, d = q.shape
    Lp = kpad.shape[1]
    QB = WIN_QUERY_BLOCKS
    kern = functools.partial(_window_attn_kernel, R=R, G=G, C=C, L=L, QB=QB)
    return pl.pallas_call(
        kern,
        grid=(BG, L // (BLOCK * QB)),
        in_specs=[pl.BlockSpec(memory_space=pltpu.SMEM),
                  pl.BlockSpec((1, R, QB * BLOCK, d), lambda b, i: (b, 0, i, 0)),
                  pl.BlockSpec((1, Lp, d), lambda b, i: (b, 0, 0)),
                  pl.BlockSpec((1, Lp, d), lambda b, i: (b, 0, 0))],
        out_specs=pl.BlockSpec((1, QB * BLOCK, R * d), lambda b, i: (b // G, i, b % G)),
        out_shape=jax.ShapeDtypeStruct((B, L, G * R * d), F32),
        compiler_params=_params("parallel", "arbitrary"),
        name="window_attn",
    )(sink.astype(F32), q, kpad, vpad)


_CAND_PAIRS = tuple((r, s) for r in range(PEER_TOPK) for s in range(PEER_TOPK)
                    if (r + 1) * (s + 1) <= PEER_TOPK)
_CAND_ROWS = -(-len(_CAND_PAIRS) // 8) * 8


def _top_rows(vs, n):
    vs = list(vs)
    iota = lax.broadcasted_iota(jnp.int32, vs[0].shape, 0).astype(F32)
    rows = [[] for _ in vs]
    for r in range(n):
        for i, v in enumerate(vs):
            m = jnp.max(v, axis=0, keepdims=True)
            rows[i].append(m)
            if r + 1 < n:
                first = jnp.min(jnp.where(v == m, iota, float(v.shape[0])), axis=0, keepdims=True)
                vs[i] = jnp.where(iota == first, -jnp.inf, v)
    return rows


def _peer_select_kernel(s_ref, ea_ref, q_ref, eb_ref, code_ref, c_ref, *, tm):
    n_lane = tm // 128

    def body(h, carry):
        tiles = [slice(t * 128, (t + 1) * 128) for t in range(n_lane)]
        ab = [s_ref[h, p, :, lanes] for lanes in tiles for p in (0, 1)]
        tops = _top_rows(ab, PEER_TOPK)
        for t, lanes in enumerate(tiles):
            a, b, ta, tb = ab[2 * t], ab[2 * t + 1], tops[2 * t], tops[2 * t + 1]
            c_ref[...] = jnp.full(c_ref.shape, -jnp.inf, F32)
            for k, (r, s) in enumerate(_CAND_PAIRS):
                c_ref[k:k + 1, :] = ta[r] + tb[s]
            best, = _top_rows((c_ref[...],), PEER_TOPK)
            tau = best[PEER_TOPK - 1]
            zsum = jnp.zeros_like(best[0])
            for bk in best:
                zsum = zsum + jnp.exp(bk - best[0])
            code_b = jnp.zeros(b.shape, F32)
            for s in range(PEER_TOPK):
                code_b = code_b + jnp.where(tb[s] > b, 1.0, 0.0)
            code_t = [jnp.zeros(tau.shape, F32)]
            for s in range(1, PEER_TOPK):
                code_t.append(jnp.where(tb[s] == tb[s - 1], code_t[s - 1], float(s)))
            q = jnp.full(a.shape, -1.0, F32)
            for r in reversed(range(PEER_TOPK)):
                q_r = jnp.full(tau.shape, -1.0, F32)
                for s in range(PEER_TOPK // (r + 1)):
                    q_r = jnp.maximum(q_r, jnp.where(ta[r] + tb[s] >= tau, code_t[s], -1.0))
                q = jnp.where(a == ta[r], q_r, q)
            q_ref[h, :, lanes] = q
            code_ref[h, :, lanes] = code_b.astype(BF16)
            ea_ref[h, :, lanes] = jnp.exp(a - ta[0]) / zsum
            eb_ref[h, :, lanes] = jnp.exp(b - tb[0]).astype(BF16)
        return carry

    lax.fori_loop(0, PEER_HEADS, body, 0)


def peer_select(st, tm):
    H, _, n, T = st.shape
    kern = functools.partial(_peer_select_kernel, tm=tm)
    return pl.pallas_call(
        kern,
        grid=(T // tm,),
        in_specs=[pl.BlockSpec((H, 2, n, tm), lambda t: (0, 0, 0, t))],
        out_specs=[pl.BlockSpec((H, n, tm), lambda t: (0, 0, t))] * 4,
        out_shape=[jax.ShapeDtypeStruct((H, n, T), dt) for dt in (F32, F32, BF16, BF16)],
        scratch_shapes=[pltpu.VMEM((_CAND_ROWS, 128), F32)],
        compiler_params=_params("parallel"),
        name="peer_select",
    )(st)


def _gelu_tanh(x):
    k = -2.0 * math.sqrt(2.0 / math.pi) * LOG2E
    return x / (1.0 + jnp.exp2(x * (k + (k * 0.044715) * (x * x))))


def _peer_kernel(x_ref, u_ref, vt_ref, ea_ref, q_ref, eb_ref, code_ref, res_ref, gate_ref, o_ref,
                 acc_ref, act_ref, g_ref, *, ni, tm):
    c = pl.program_id(1)

    @pl.when(c == 0)
    def _():
        acc_ref[...] = jnp.zeros_like(acc_ref)

    act_ref[...] = _gelu_tanh(lax.dot_general(u_ref[...], x_ref[...], (((1,), (1,)), ((), ())),
                                              preferred_element_type=F32))

    for ts in range(tm // 128):
        lanes = slice(ts * 128, (ts + 1) * 128)
        for ii in range(ni):
            rows = slice(ii * N_KEYS, (ii + 1) * N_KEYS)
            w = jnp.zeros((N_KEYS, 128), BF16)
            for h in range(PEER_HEADS):
                sel = code_ref[h, :, lanes] <= q_ref[h, ii:ii + 1, lanes].astype(BF16)
                gate = ea_ref[h, ii:ii + 1, lanes].astype(BF16) * eb_ref[h, :, lanes]
                w = w + jnp.where(sel, gate, jnp.zeros_like(gate))
            g_ref[rows, lanes] = w * act_ref[rows, lanes].astype(BF16)

    acc_ref[...] += jnp.dot(vt_ref[...], g_ref[...], preferred_element_type=F32)

    @pl.when(c == pl.num_programs(1) - 1)
    def _():
        o_ref[...] = res_ref[...] + gate_ref[0] * acc_ref[...].T


def peer_dense(x, u, vt, eat, qt, ebt, codet, res, gate, rows_per_seg, tm, ni):
    T, D = x.shape
    E = u.shape[0]
    ec = ni * N_KEYS
    kern = functools.partial(_peer_kernel, ni=ni, tm=tm)
    nseg = gate.shape[0]
    blocks_per_seg = rows_per_seg // tm
    rows_of_chunk = pl.BlockSpec((PEER_HEADS, ni, tm), lambda t, c: (0, c, t))
    return pl.pallas_call(
        kern,
        grid=(T // tm, E // ec),
        in_specs=[pl.BlockSpec((tm, D), lambda t, c: (t, 0)),
                  pl.BlockSpec((ec, D), lambda t, c: (c, 0)),
                  pl.BlockSpec((D, ec), lambda t, c: (0, c)),
                  rows_of_chunk,
                  rows_of_chunk,
                  pl.BlockSpec((PEER_HEADS, N_KEYS, tm), lambda t, c: (0, 0, t)),
                  pl.BlockSpec((PEER_HEADS, N_KEYS, tm), lambda t, c: (0, 0, t)),
                  pl.BlockSpec((tm, D), lambda t, c: (t, 0)),
                  pl.BlockSpec((1, 1, D), lambda t, c: (jnp.minimum(t // blocks_per_seg, nseg - 1), 0, 0))],
        out_specs=pl.BlockSpec((tm, D), lambda t, c: (t, 0)),
        out_shape=jax.ShapeDtypeStruct((T, D), F32),
        scratch_shapes=[pltpu.VMEM((D, tm), F32),
                        pltpu.VMEM((ec, tm), F32),
                        pltpu.VMEM((ec, tm), BF16)],
        compiler_params=_params("parallel", "arbitrary"),
        name="peer_dense",
    )(x, u, vt, eat, qt, ebt, codet, res, gate.reshape(nseg, 1, D))


def peer(qp, h_bf, keys, u_bf, vt_bf, res, gate, rows_per_seg, tm):
    T = qp.shape[0]
    dk = keys.shape[-1]
    kflat = keys.reshape(PEER_HEADS * 2, N_KEYS, dk).astype(BF16)
    st = keys_times_qT(kflat, qp, tm).reshape(PEER_HEADS, 2, N_KEYS, T)
    eat, qt, ebt, codet = peer_select(st, tm)
    return peer_dense(h_bf, u_bf, vt_bf, eat, qt, ebt, codet, res, gate, rows_per_seg, tm, 8)


def rmsnorm(x, g):
    return x * lax.rsqrt(jnp.mean(x * x, axis=-1, keepdims=True) + EPS) * g


def axial_rope(n_tok, dim):
    rows = n_tok // GRID_W
    row = jnp.repeat(jnp.arange(rows, dtype=F32), GRID_W)
    col = jnp.tile(jnp.arange(GRID_W, dtype=F32), rows)
    axis_dim = dim // 2
    inv_freq = ROPE_THETA ** (-jnp.arange(0, axis_dim, 2, dtype=F32) / axis_dim)
    ang = jnp.concatenate([row[:, None] * inv_freq, col[:, None] * inv_freq], axis=-1)
    return jnp.cos(ang), jnp.sin(ang)


def apply_rope(x, cos, sin):
    shape = (1, x.shape[1]) + (1,) * (x.ndim - 3) + (cos.shape[-1],)
    c, s = cos.reshape(shape), sin.reshape(shape)
    x1, x2 = jnp.split(x, 2, axis=-1)
    return jnp.concatenate([x1 * c - x2 * s, x1 * s + x2 * c], axis=-1)


def short_conv(u, w, b):
    up = jnp.pad(u, ((0, 0), (1, 1), (0, 0)))
    return up[:, :-2] * w[0] + up[:, 1:-1] * w[1] + up[:, 2:] * w[2] + b


def hyena_filters(n_tok, w1, b1, w2, b2, w3, b3, freq):
    hp = lax.Precision.HIGHEST
    t = jnp.linspace(0.0, 1.0, n_tok, dtype=F32)[:, None]
    w = (2.0 * math.pi / n_tok) * jnp.arange(n_tok, dtype=F32)[:, None]
    f = jnp.linspace(1e-4, HY_BANDS - 1, HY_BANDS, dtype=F32)[None, :]
    feat = jnp.concatenate([t, jnp.cos(f * w), -jnp.sin(f * w)], axis=-1)
    hdn = jnp.sin(freq * (jnp.dot(feat, w1, precision=hp) + b1))
    hdn = jnp.sin(freq * (jnp.dot(hdn, w2, precision=hp) + b2))
    filt = (jnp.dot(hdn, w3, precision=hp) + b3).reshape(n_tok, HY_ORDER, 2, HY_CH)
    deltas = jnp.abs(jnp.linspace(math.log(HY_TARGET) / HY_SLOW, math.log(HY_TARGET) / HY_FAST, HY_CH,
                                  dtype=F32))
    filt = filt * jnp.exp(-t[:, :, None, None] * deltas)
    fwd, bwd = filt[:, :, 0], filt[:, :, 1]
    kfull = jnp.concatenate([fwd, jnp.zeros_like(fwd[:1]), bwd[1:][::-1]], axis=0)
    return kfull / jnp.sum(jnp.abs(kfull), axis=0, keepdims=True)


def hyena_filters_t(n_tok, w1, b1, w2, b2, w3, b3, freq):
    hp = lax.Precision.HIGHEST
    t = jnp.linspace(0.0, 1.0, n_tok, dtype=F32)[:, None]
    w = (2.0 * math.pi / n_tok) * jnp.arange(n_tok, dtype=F32)[:, None]
    f = jnp.linspace(1e-4, HY_BANDS - 1, HY_BANDS, dtype=F32)[None, :]
    feat = jnp.concatenate([t, jnp.cos(f * w), -jnp.sin(f * w)], axis=-1)
    hdn = jnp.sin(freq * (jnp.dot(feat, w1, precision=hp) + b1))
    hdn = jnp.sin(freq * (jnp.dot(hdn, w2, precision=hp) + b2))
    deltas = jnp.abs(jnp.linspace(math.log(HY_TARGET) / HY_SLOW, math.log(HY_TARGET) / HY_FAST, HY_CH,
                                  dtype=F32))
    w3t = jnp.transpose(w3.reshape(-1, HY_ORDER, 2, HY_CH), (2, 1, 3, 0))
    b3t = jnp.transpose(b3.reshape(HY_ORDER, 2, HY_CH), (1, 0, 2))[..., None]

    def half(d, hidden, times):
        decay = jnp.exp(-deltas[:, None] * times[None, :])
        return (jnp.einsum('ock,nk->ocn', w3t[d], hidden, precision=hp) + b3t[d]) * decay

    fwd = half(0, hdn, t[:, 0])
    bwd_rev = half(1, hdn[::-1], t[::-1, 0])
    kfull = jnp.concatenate([fwd, jnp.zeros_like(fwd[..., :1]), bwd_rev[..., :n_tok - 1]], axis=-1)
    kfull = kfull / jnp.sum(jnp.abs(kfull), axis=-1, keepdims=True)
    return kfull.reshape(HY_ORDER * HY_CH, 2 * n_tok)


def long_conv(z, kf):
    n = z.shape[1]
    zf = jnp.fft.rfft(z, n=2 * n, axis=1)
    hf = jnp.fft.rfft(kf, n=2 * n, axis=0)
    return jnp.fft.irfft(zf * hf[None], n=2 * n, axis=1)[:, :n]


def hyena_small(u, conv_w, conv_b, filter_params, bias):
    n = u.shape[1]
    u = short_conv(u, conv_w, conv_b)
    v, x1, x2 = jnp.split(u, 3, axis=-1)
    kfull = hyena_filters(n, *filter_params)
    z = x1 * (long_conv(v, kfull[:, 0]) + bias[0] * v)
    return x2 * (long_conv(z, kfull[:, 1]) + bias[1] * z)


def _hi_lo(x):
    hi = x.astype(BF16)
    return hi, (x - hi.astype(F32)).astype(BF16)


def _dot3(a, b):
    d = lambda x, y: jnp.dot(x, y, preferred_element_type=F32)
    return d(a[0], b[0]) + d(a[1], b[0]) + d(a[0], b[1])


def _dft_constants(n_tok):
    n = 2 * n_tok
    n1 = n // 128
    a1 = 2.0 * np.pi * np.outer(np.arange(n1), np.arange(n1)) / n1
    a2 = 2.0 * np.pi * np.outer(np.arange(128), np.arange(128)) / 128
    at = 2.0 * np.pi * np.outer(np.arange(n1), np.arange(128)) / n
    c1, s1, c2, s2 = np.cos(a1), np.sin(a1), np.cos(a2), np.sin(a2)
    pair = lambda m: _hi_lo(jnp.asarray(m, F32))
    return dict(
        f1_half=pair(np.concatenate([c1[:, :n1 // 2], -s1[:, :n1 // 2]], axis=0)),
        f1_full=pair(np.concatenate([c1, -s1], axis=0)),
        m_fwd=pair(np.block([[c2, -s2], [s2, c2]])),
        m_inv=pair(np.block([[c2, s2], [-s2, c2]])),
        g_half=pair(np.concatenate([c1[:n1 // 2], -s1[:n1 // 2]], axis=1)),
        tr=jnp.asarray(np.cos(at), F32), ti=jnp.asarray(-np.sin(at), F32))


def _dft_fwd(seqs, f1, tr, ti, m_fwd):
    n1 = tr.shape[0]
    y = _dot3(f1, _hi_lo(jnp.concatenate(seqs, axis=1)))
    rows = []
    for k in range(len(seqs)):
        yr, yi = y[:n1, k * 128:(k + 1) * 128], y[n1:, k * 128:(k + 1) * 128]
        rows.append(jnp.concatenate([yr * tr - yi * ti, yr * ti + yi * tr], axis=1))
    return _dot3(_hi_lo(jnp.concatenate(rows, axis=0)), m_fwd)


def _dft_inv_half(p, g_half, tr, ti, m_inv):
    n1 = tr.shape[0]
    u = _dot3(_hi_lo(p), m_inv)
    cols = []
    for k in range(p.shape[0] // n1):
        ur, ui = u[k * n1:(k + 1) * n1, :128], u[k * n1:(k + 1) * n1, 128:]
        cols.append(jnp.concatenate([ur * tr + ui * ti, ui * tr - ur * ti], axis=0))
    return _dot3(g_half, _hi_lo(jnp.concatenate(cols, axis=1))) * (1.0 / (n1 * 128))


def _pairs(refs):
    return (refs[0][...], refs[1][...])


def _spectrum_kernel(a_ref, f1h, f1l, tr_ref, ti_ref, mh, ml, o_ref):
    cb, n1 = a_ref.shape[0], tr_ref.shape[0]
    x = _dft_fwd([a_ref[k] for k in range(cb)], _pairs((f1h, f1l)), tr_ref[...], ti_ref[...],
                 _pairs((mh, ml)))
    o_ref[...] = x.reshape(cb, n1, 256)


def filter_spectrum(kf, consts, cb):
    items, n1, _ = kf.shape
    full = lambda shape: pl.BlockSpec(shape, lambda i: (0,) * len(shape))
    f1, m = consts["f1_full"], consts["m_fwd"]
    return pl.pallas_call(
        _spectrum_kernel,
        grid=(items // cb,),
        in_specs=[pl.BlockSpec((cb, n1, 128), lambda i: (i, 0, 0)),
                  full(f1[0].shape), full(f1[1].shape), full((n1, 128)), full((n1, 128)),
                  full(m[0].shape), full(m[1].shape)],
        out_specs=pl.BlockSpec((cb, n1, 256), lambda i: (i, 0, 0)),
        out_shape=jax.ShapeDtypeStruct((items, n1, 256), F32),
        compiler_params=_params("parallel"),
        name="filter_spectrum",
    )(kf, f1[0], f1[1], consts["tr"], consts["ti"], m[0], m[1])


def _conv_gate_kernel(bias_ref, u_ref, g_ref, h_ref, f1h, f1l, tr_ref, ti_ref, mfh, mfl, mih, mil, gh, gl,
                      o_ref, *, cb):
    tr, ti = tr_ref[...], ti_ref[...]
    n1 = tr.shape[0]
    c0 = pl.program_id(1) * cb
    u = [u_ref[0, k] for k in range(cb)]
    x = _dft_fwd(u, _pairs((f1h, f1l)), tr, ti, _pairs((mfh, mfl)))
    prod = []
    for k in range(cb):
        xr, xi = x[k * n1:(k + 1) * n1, :128], x[k * n1:(k + 1) * n1, 128:]
        hr, hi = h_ref[k, :, :128], h_ref[k, :, 128:]
        prod.append(jnp.concatenate([xr * hr - xi * hi, xr * hi + xi * hr], axis=1))
    y = _dft_inv_half(jnp.concatenate(prod, axis=0), _pairs((gh, gl)), tr, ti, _pairs((mih, mil)))
    for k in range(cb):
        o_ref[0, k] = g_ref[0, k] * (y[:, k * 128:(k + 1) * 128] + bias_ref[c0 + k] * u[k])


def conv_gate(u, gate, spec, bias, consts, cb):
    B, C, half, _ = u.shape
    n1 = 2 * half
    full = lambda shape: pl.BlockSpec(shape, lambda b, c: (0,) * len(shape))
    seq = pl.BlockSpec((1, cb, half, 128), lambda b, c: (b, c, 0, 0))
    mats = [*consts["f1_half"], consts["tr"], consts["ti"], *consts["m_fwd"], *consts["m_inv"],
            *consts["g_half"]]
    return pl.pallas_call(
        functools.partial(_conv_gate_kernel, cb=cb),
        grid=(B, C // cb),
        in_specs=[pl.BlockSpec(memory_space=pltpu.SMEM), seq, seq,
                  pl.BlockSpec((cb, n1, 256), lambda b, c: (c, 0, 0))] + [full(m.shape) for m in mats],
        out_specs=seq,
        out_shape=jax.ShapeDtypeStruct(u.shape, F32),
        compiler_params=_params("parallel", "arbitrary"),
        name="conv_gate",
    )(bias.astype(F32), u, gate, spec, *mats)


def hyena(u, conv_w, conv_b, filter_params, bias):
    B, n, _ = u.shape
    half = n // 128
    consts = _dft_constants(n)
    u = short_conv(u, conv_w, conv_b)
    seqs = jnp.transpose(u, (0, 2, 1)).reshape(B, 3, HY_CH, half, 128)
    v, x1, x2 = seqs[:, 0], seqs[:, 1], seqs[:, 2]
    kf = hyena_filters_t(n, *filter_params).reshape(HY_ORDER * HY_CH, 2 * half, 128)
    spec = filter_spectrum(kf, consts, HY_CH_BLOCK).reshape(HY_ORDER, HY_CH, 2 * half, 256)
    z = conv_gate(v, x1, spec[0], bias[0], consts, HY_CH_BLOCK)
    o = conv_gate(z, x2, spec[1], bias[1], consts, HY_CH_BLOCK)
    return jnp.transpose(o.reshape(B, HY_CH, n), (0, 2, 1))


def _head_major(t):
    B, L, H, d = t.shape
    return jnp.transpose(t, (0, 2, 1, 3)).reshape(B * H, L, d)


def _head_major_t(t):
    B, L, H, d = t.shape
    return jnp.transpose(t, (0, 2, 3, 1)).reshape(B * H, d, L)


def _q_layout(t, G, R):
    B, L, _, d = t.shape
    return jnp.transpose(t.reshape(B, L, G, R, d), (0, 2, 3, 1, 4)).reshape(B * G, R, L, d)


def _key_chunk(n_keys):
    for c in range(KEY_CHUNK_CAP, 0, -128):
        if n_keys % c == 0:
            return c
    raise ValueError(f"no key chunk for {n_keys}")


def kernel(x, c, ctx, c_ctx, w_mod, b_mod, g_norm_mix, w_in, g_qk_diff, lambda_diff, g_qk_win, sink_win, g_qk_glob, hy_conv_w, hy_conv_b, hy_w1, hy_b1, hy_w2, hy_b2, hy_w3, hy_b3, hy_freq, hy_bias, g_mix_out, w_out, g_norm_ffn, peer_wq, peer_keys, peer_u, peer_v):
    B, L, D = x.shape
    C = ctx.shape[1]
    depth = w_mod.shape[0]
    TM = 512
    n_lat = B * L
    n_ctx = B * C
    assert L % TM == 0 and n_ctx % TM == 0
    nseg = B + 1

    def rope_rows(dim):
        cos, sin = axial_rope(L, dim)
        return (jnp.concatenate([jnp.tile(cos, (B, 1)), jnp.ones((n_ctx, dim // 2), F32)], axis=0),
                jnp.concatenate([jnp.tile(sin, (B, 1)), jnp.zeros((n_ctx, dim // 2), F32)], axis=0))

    rope_rows_half = rope_rows(DIFF_QK_DIM)
    rope_rows_full = rope_rows(HEAD_DIM)
    sc = jnp.concatenate([jax.nn.silu(c), jax.nn.silu(c_ctx)[None]], axis=0)
    sc = jnp.pad(sc, ((0, 8 - nseg), (0, 0))).astype(BF16)

    xall = jnp.concatenate([x.reshape(n_lat, D), ctx.reshape(n_ctx, D)], axis=0)

    for i in range(depth):
        want_ctx = i < depth - 1
        lambda_init = 0.8 - 0.6 * math.exp(-0.3 * i)
        mod = mm(sc, w_mod[i].astype(BF16), 8, 1024)[:nseg] + b_mod[i]
        sh1, s1, g1, sh2, s2, g2 = jnp.split(mod, N_MOD, axis=-1)

        groups = normmod_mm(xall, g_norm_mix[i], sh1, s1, w_in[i].astype(BF16), L, TM, F32, False,
                            widths=(W_DIFF, W_WIN, W_GLOB, W_HY))
        ga_, gw_, gg_, gh_ = groups
        ph, phc = gh_[:n_lat].reshape(B, L, -1), gh_[n_lat:].reshape(B, C, -1)

        def gains(g_qk, n_q, n_k, d):
            return jnp.concatenate([jnp.tile(g_qk[0], n_q) * (d ** -0.5 * LOG2E), jnp.tile(g_qk[1], n_k)])

        def split_rows(t, *shape):
            return t[:n_lat].reshape(B, L, *shape), t[n_lat:].reshape(B, C, *shape)

        def rows_of(t):
            lat, cx = t[:n_lat].reshape(B, L, -1), t[n_lat:].reshape(B, C, -1)
            return lat, cx, jnp.concatenate([cx, lat], axis=1)

        qka, qkac, qka_keys = rows_of(head_prep(ga_, 2 * GROUP_W, DIFF_QK_DIM,
                                                gains(g_qk_diff[i], 2 * DIFF_HEADS, 2 * DIFF_HEADS, DIFF_QK_DIM),
                                                *rope_rows_half, TM))
        va, vac = split_rows(ga_[:, 2 * GROUP_W:].astype(BF16), DIFF_HEADS, HEAD_DIM)
        lam_vec = lambda_diff[i]
        lam = (jnp.exp(jnp.sum(lam_vec[0] * lam_vec[1])) - jnp.exp(jnp.sum(lam_vec[2] * lam_vec[3]))
               + lambda_init)
        coefs = jnp.stack([jnp.ones((), F32), -lam])
        vva = jnp.concatenate([vac, va], axis=1)
        oa = dense_attention(qka, qka_keys, GROUP_W, _head_major_t(vva), coefs, None,
                             2, DIFF_HEADS, 1, DIFF_QK_DIM, ATTN_TQ_DIFF, _key_chunk(C + L))

        def gqa_parts(g_, g_qk, n_q, n_kv):
            wq, wk = n_q * HEAD_DIM, n_kv * HEAD_DIM
            qk = head_prep(g_, wq + wk, HEAD_DIM, gains(g_qk, n_q, n_kv, HEAD_DIM), *rope_rows_full, TM)
            v, vc = split_rows(g_[:, wq + wk:].astype(BF16), n_kv, HEAD_DIM)
            return rows_of(qk) + (v, vc)

        heads_of = lambda t, lo, n: t[..., lo:lo + n * HEAD_DIM].reshape(t.shape[0], t.shape[1], n, HEAD_DIM)
        WQ = WIN_HEADS * HEAD_DIM
        qkw, qkwc, _, vw, vwc = gqa_parts(gw_, g_qk_win[i], WIN_HEADS, WIN_KV_HEADS)
        qw, kw, kwc = heads_of(qkw, 0, WIN_HEADS), heads_of(qkw, WQ, WIN_KV_HEADS), heads_of(qkwc, WQ, WIN_KV_HEADS)
        zblk = jnp.zeros((B, BLOCK, WIN_KV_HEADS, HEAD_DIM), BF16)
        kpad = _head_major(jnp.concatenate([kwc, zblk, kw, zblk], axis=1))
        vpad = _head_major(jnp.concatenate([vwc, zblk, vw, zblk], axis=1))
        Rw = WIN_HEADS // WIN_KV_HEADS
        ob = window_attention(_q_layout(qw, WIN_KV_HEADS, Rw), kpad, vpad, sink_win[i], B, WIN_KV_HEADS, C, L)

        qkg, qkgc, qkg_keys, vg, vgc = gqa_parts(gg_, g_qk_glob[i], GLOB_HEADS, GLOB_KV_HEADS)
        Rg = GLOB_HEADS // GLOB_KV_HEADS
        one = jnp.ones((1,), F32)
        vvg = _head_major_t(jnp.concatenate([vgc, vg], axis=1))
        og = dense_attention(qkg, qkg_keys, GLOB_HEADS * HEAD_DIM, vvg, one, None,
                             1, GLOB_KV_HEADS, Rg, HEAD_DIM, ATTN_TQ_GLOB, _key_chunk(C + L))

        filt = (hy_w1[i], hy_b1[i], hy_w2[i], hy_b2[i], hy_w3[i], hy_b3[i], hy_freq[i])
        oh = hyena(ph, hy_conv_w[i], hy_conv_b[i], filt, hy_bias[i])

        mixed = [jnp.concatenate([oa, ob, og, oh], axis=-1).reshape(n_lat, D)]
        if want_ctx:
            oac = dense_attention(qkac, qkac, GROUP_W, _head_major_t(vac), coefs, None,
                                  2, DIFF_HEADS, 1, DIFF_QK_DIM, C, C)
            obc = dense_attention(qkwc, qkwc, WQ, _head_major_t(vwc), one, sink_win[i],
                                  1, WIN_KV_HEADS, Rw, HEAD_DIM, C, C)
            ogc = dense_attention(qkgc, qkgc, GLOB_HEADS * HEAD_DIM, _head_major_t(vgc), one, None,
                                  1, GLOB_KV_HEADS, Rg, HEAD_DIM, C, C)
            ohc = hyena_small(phc, hy_conv_w[i], hy_conv_b[i], filt, hy_bias[i])
            mixed.append(jnp.concatenate([oac, obc, ogc, ohc], axis=-1).reshape(n_ctx, D))
        o = jnp.concatenate(mixed, axis=0)
        n_rows = o.shape[0]

        oh_ = rmsnorm(o.reshape(n_rows, N_OUT_HEADS, HEAD_DIM), g_mix_out[i].reshape(N_OUT_HEADS, HEAD_DIM))
        head_scale = jnp.where(jnp.arange(N_OUT_HEADS) < DIFF_HEADS, 1.0 - lambda_init, 1.0)[:, None]
        om = (oh_ * head_scale.astype(F32)).reshape(n_rows, D)
        xcur = mm_residual(om.astype(BF16), w_out[i].astype(BF16), xall, g1, L, TM)

        qp, h2 = normmod_mm(xcur, g_norm_ffn[i], sh2, s2, peer_wq[i].astype(BF16), L, TM, BF16, True)
        xall = peer(qp, h2, peer_keys[i], peer_u[i].astype(BF16), jnp.transpose(peer_v[i].astype(BF16)),
                    xcur, g2, L, TM)

    return xall[:n_lat].reshape(B, L, D)
```

```python
import functools
import math

import jax
import jax.numpy as jnp
import numpy as np
from jax import lax
from jax.experimental import pallas as pl
from jax.experimental.pallas import tpu as pltpu

F32 = jnp.float32
BF16 = jnp.bfloat16

GRID_W = 64
HEAD_DIM = 64
BLOCK = 128
WINDOW = 128
ROPE_THETA = 10000.0
EPS = 1e-6
NEG_INF = -1e30
N_MOD = 6
DIFF_HEADS = 4
DIFF_QK_DIM = 32
WIN_HEADS = 4
WIN_KV_HEADS = 2
GLOB_HEADS = 4
GLOB_KV_HEADS = 2
N_OUT_HEADS = 16
HY_CH = 256
HY_ORDER = 2
HY_BANDS = 16
HY_TARGET = 1e-2
HY_FAST = 0.3
HY_SLOW = 1.5
W_DIFF = 768
W_WIN = 512
W_GLOB = 512
W_HY = 768
GROUP_W = 256
PEER_HEADS = 8
N_KEYS = 128
PEER_TOPK = 16

VMEM_LIMIT = 56 * 1024 * 1024
ATTN_TQ_DIFF = 512
ATTN_TQ_GLOB = 256
KEY_CHUNK_CAP = 1408
HY_CH_BLOCK = 16
WIN_QUERY_BLOCKS = 4


def _params(*sem):
    return pltpu.CompilerParams(dimension_semantics=sem, vmem_limit_bytes=VMEM_LIMIT)


def _mm_kernel(a_ref, b_ref, o_ref):
    o_ref[...] = jnp.dot(a_ref[...], b_ref[...], preferred_element_type=F32)


def mm(a, b, tm, tn):
    M, K = a.shape
    N = b.shape[1]
    return pl.pallas_call(
        _mm_kernel,
        grid=(M // tm, N // tn),
        in_specs=[pl.BlockSpec((tm, K), lambda i, j: (i, 0)),
                  pl.BlockSpec((K, tn), lambda i, j: (0, j))],
        out_specs=pl.BlockSpec((tm, tn), lambda i, j: (i, j)),
        out_shape=jax.ShapeDtypeStruct((M, N), F32),
        compiler_params=_params("parallel", "arbitrary"),
        name="mm",
    )(a, b)


def _seg_spec(rows_per_seg, tm, nseg, D):
    blocks_per_seg = rows_per_seg // tm
    return pl.BlockSpec((1, 1, D), lambda i: (jnp.minimum(i // blocks_per_seg, nseg - 1), 0, 0))


def _normmod_mm_kernel(x_ref, g_ref, sh_ref, sc_ref, w_ref, *outs, widths, with_h):
    x = x_ref[...]
    y = x * lax.rsqrt(jnp.mean(x * x, axis=-1, keepdims=True) + EPS) * g_ref[...]
    h = (y * (1.0 + sc_ref[0]) + sh_ref[0]).astype(BF16)
    if with_h:
        outs[-1][...] = h
    r = jnp.dot(h, w_ref[...], preferred_element_type=F32)
    lo = 0
    for o_ref, n in zip(outs, widths):
        o_ref[...] = r[:, lo:lo + n].astype(o_ref.dtype)
        lo += n


def normmod_mm(x, g, shift, scale, w, rows_per_seg, tm, out_dtype, with_h, widths=None):
    M, D = x.shape
    N = w.shape[1]
    widths = (N,) if widths is None else tuple(widths)
    nseg = shift.shape[0]
    seg = _seg_spec(rows_per_seg, tm, nseg, D)
    row_block = lambda n: pl.BlockSpec((tm, n), lambda i: (i, 0))
    out_specs = [row_block(n) for n in widths]
    out_shape = [jax.ShapeDtypeStruct((M, n), out_dtype) for n in widths]
    if with_h:
        out_specs.append(row_block(D))
        out_shape.append(jax.ShapeDtypeStruct((M, D), BF16))
    return pl.pallas_call(
        functools.partial(_normmod_mm_kernel, widths=widths, with_h=with_h),
        grid=(M // tm,),
        in_specs=[row_block(D), pl.BlockSpec((1, D), lambda i: (0, 0)), seg, seg,
                  pl.BlockSpec((D, N), lambda i: (0, 0))],
        out_specs=out_specs,
        out_shape=out_shape,
        compiler_params=_params("parallel"),
        name="normmod_mm",
    )(x, g.reshape(1, D), shift.reshape(nseg, 1, D), scale.reshape(nseg, 1, D), w)


def _mm_residual_kernel(a_ref, w_ref, x_ref, gate_ref, o_ref):
    o_ref[...] = x_ref[...] + gate_ref[0] * jnp.dot(a_ref[...], w_ref[...], preferred_element_type=F32)


def mm_residual(a, w, x, gate, rows_per_seg, tm):
    M, K = a.shape
    N = w.shape[1]
    nseg = gate.shape[0]
    row_block = lambda n: pl.BlockSpec((tm, n), lambda i: (i, 0))
    return pl.pallas_call(
        _mm_residual_kernel,
        grid=(M // tm,),
        in_specs=[row_block(K), pl.BlockSpec((K, N), lambda i: (0, 0)), row_block(N),
                  _seg_spec(rows_per_seg, tm, nseg, N)],
        out_specs=row_block(N),
        out_shape=jax.ShapeDtypeStruct((M, N), F32),
        compiler_params=_params("parallel"),
        name="mm_residual",
    )(a, w, x, gate.reshape(nseg, 1, N))


def _scores_kernel(k_ref, q_ref, o_ref):
    d = k_ref.shape[-1]
    for p in range(k_ref.shape[0]):
        o_ref[p] = lax.dot_general(k_ref[p], q_ref[:, p * d:(p + 1) * d], (((1,), (1,)), ((), ())),
                                   preferred_element_type=F32)


def keys_times_qT(keys, q, tn):
    P, n, d = keys.shape
    T = q.shape[0]
    return pl.pallas_call(
        _scores_kernel,
        grid=(T // tn,),
        in_specs=[pl.BlockSpec((P, n, d), lambda j: (0, 0, 0)),
                  pl.BlockSpec((tn, P * d), lambda j: (j, 0))],
        out_specs=pl.BlockSpec((P, n, tn), lambda j: (0, 0, j)),
        out_shape=jax.ShapeDtypeStruct((P, n, T), F32),
        compiler_params=_params("parallel"),
        name="peer_scores_t",
    )(keys, q)


def _head_prep_kernel(t_ref, cos_ref, sin_ref, gain_ref, bd_ref, pm_ref, ex_ref, o_ref):
    d3 = lambda parts, m: sum(jnp.dot(p, m, preferred_element_type=F32) for p in parts)

    def split3(v):
        hi, lo = _hi_lo(v)
        rest = v - hi.astype(F32) - lo.astype(F32)
        return hi, lo, rest.astype(BF16)

    x = t_ref[...]
    ms = d3(_hi_lo(x * x), bd_ref[...])
    y = x * lax.rsqrt(ms + EPS) * gain_ref[...]
    ex = ex_ref[...]
    cos = d3(split3(cos_ref[...]), ex)
    sin = d3(split3(sin_ref[...]), ex)
    o_ref[...] = (y * cos + d3(_hi_lo(y), pm_ref[...]) * sin).astype(BF16)


def head_prep(t, width, d, gain, cos_rows, sin_rows, tm):
    rows = t.shape[0]
    hd = d // 2
    col = np.arange(width)
    same_head = (col[:, None] // d) == (col[None, :] // d)
    bd = jnp.asarray(same_head / d, BF16)
    src = np.where((col % d) < hd, col + hd, col - hd)
    pm = np.zeros((width, width), np.float32)
    pm[src, col] = np.where((col % d) < hd, -1.0, 1.0)
    ex = np.zeros((hd, width), np.float32)
    ex[col % hd, col] = 1.0
    full = lambda a: pl.BlockSpec(a.shape, lambda i: (0,) * a.ndim)
    consts = (gain.reshape(1, width).astype(F32), bd, jnp.asarray(pm, BF16), jnp.asarray(ex, BF16))
    return pl.pallas_call(
        _head_prep_kernel,
        grid=(rows // tm,),
        in_specs=[pl.BlockSpec((tm, width), lambda i: (i, 0)),
                  pl.BlockSpec((tm, hd), lambda i: (i, 0)),
                  pl.BlockSpec((tm, hd), lambda i: (i, 0))] + [full(c) for c in consts],
        out_specs=pl.BlockSpec((tm, width), lambda i: (i, 0)),
        out_shape=jax.ShapeDtypeStruct((rows, width), BF16),
        compiler_params=_params("parallel"),
        name="head_prep",
    )(t, cos_rows, sin_rows, *consts)


LOG2E = math.log2(math.e)


def _dense_attn_kernel(coef_ref, sink_ref, q_ref, k_ref, vt_ref, o_ref, sa_ref, sb_ref, *, nbr, R, G,
                       HP, d, tq, tk, nk, has_sink):
    cols = R * tq
    dv = vt_ref.shape[1]
    kw = k_ref.shape[-1]
    g0 = (pl.program_id(0) % (G // HP)) * HP
    nt = (((1,), (1,)), ((), ()))
    bufs = (sa_ref, sb_ref)
    lane = lax.broadcasted_iota(jnp.int32, (1, kw), 1)
    heads = []
    for hp in range(HP):
        out = jnp.zeros((dv, cols), F32)
        for br in range(nbr):
            k_lo = (hp * nbr + br) * d
            parts = []
            for r in range(R):
                q_lo = ((hp * R + r) * nbr + br) * d
                blk = q_ref[0, :, (q_lo // kw) * kw:(q_lo // kw + 1) * kw]
                if (q_lo - k_lo) % kw:
                    sh = (q_lo - k_lo) % kw
                    blk = jnp.concatenate([blk[:, sh:], blk[:, :sh]], axis=1)
                parts.append(jnp.where((lane >= k_lo) & (lane < k_lo + d), blk, jnp.zeros_like(blk)))
            q = parts[0] if R == 1 else jnp.concatenate(parts, axis=0)
            if has_sink:
                m = jnp.concatenate([jnp.full((1, tq), sink_ref[(g0 + hp) * R + r] * LOG2E, F32)
                                     for r in range(R)], axis=1)
                l = jnp.ones((1, cols), F32)
            else:
                m = jnp.full((1, cols), NEG_INF, F32)
                l = jnp.zeros((1, cols), F32)
            acc = jnp.zeros((dv, cols), F32)
            bufs[0][...] = lax.dot_general(k_ref[0, 0:tk, :], q, nt, preferred_element_type=F32)
            for j in range(nk):
                if j + 1 < nk:
                    bufs[(j + 1) % 2][...] = lax.dot_general(k_ref[0, (j + 1) * tk:(j + 2) * tk, :], q,
                                                             nt, preferred_element_type=F32)
                s = bufs[j % 2][...]
                m_new = jnp.maximum(m, jnp.max(s, axis=0, keepdims=True))
                alpha = jnp.exp2(m - m_new)
                p = jnp.exp2(s - m_new)
                l = alpha * l + jnp.sum(p, axis=0, keepdims=True)
                acc = alpha * acc + jnp.dot(vt_ref[hp, :, j * tk:(j + 1) * tk], p.astype(BF16),
                                            preferred_element_type=F32)
                m = m_new
            out = out + coef_ref[br] * (acc / l)
        out_t = out.T
        heads += [out_t[r * tq:(r + 1) * tq] for r in range(R)]
    o_ref[0] = jnp.concatenate(heads, axis=1)


def dense_attention(qsrc, ksrc, k_lane0, vt, coefs, sink, nbr, G, R, d, tq, tk):
    B, Lq, _ = qsrc.shape
    Lk = ksrc.shape[1]
    dv = vt.shape[1]
    HP = 128 // (nbr * d)
    assert G % HP == 0 and (HP * R * dv) % 128 == 0 and k_lane0 % 128 == 0
    has_sink = sink is not None
    if sink is None:
        sink = jnp.zeros((1,), F32)
    kern = functools.partial(_dense_attn_kernel, nbr=nbr, R=R, G=G, HP=HP, d=d, tq=tq, tk=tk, nk=Lk // tk,
                             has_sink=has_sink)
    gp = G // HP
    qw = HP * R * nbr * d
    return pl.pallas_call(
        kern,
        grid=(B * gp, Lq // tq),
        in_specs=[pl.BlockSpec(memory_space=pltpu.SMEM),
                  pl.BlockSpec(memory_space=pltpu.SMEM),
                  pl.BlockSpec((1, tq, qw), lambda b, i: (b // gp, i, b % gp)),
                  pl.BlockSpec((1, Lk, 128), lambda b, i: (b // gp, 0, k_lane0 // 128 + b % gp)),
                  pl.BlockSpec((HP, dv, Lk), lambda b, i: (b, 0, 0))],
        out_specs=pl.BlockSpec((1, tq, HP * R * dv), lambda b, i: (b // gp, i, b % gp)),
        out_shape=jax.ShapeDtypeStruct((B, Lq, G * R * dv), F32),
        scratch_shapes=[pltpu.VMEM((tk, R * tq), F32), pltpu.VMEM((tk, R * tq), F32)],
        compiler_params=_params("parallel", "arbitrary"),
        name="dense_attn",
    )(coefs.astype(F32), sink.astype(F32), qsrc, ksrc, vt)


def _window_attn_kernel(sink_ref, q_ref, k_ref, v_ref, o_ref, *, R, G, C, L, QB):
    rows = R * BLOCK
    kw = k_ref.shape[-1]
    d = kw // G
    kc = k_ref[0, 0:C, :]
    vc = v_ref[0, 0:C, :]
    nt = (((1,), (1,)), ((), ()))
    qi = lax.broadcasted_iota(jnp.int32, (rows, 3 * BLOCK), 0) & (BLOCK - 1)
    kj = lax.broadcasted_iota(jnp.int32, (rows, 3 * BLOCK), 1)
    in_window = jnp.abs(kj - BLOCK - qi) <= WINDOW
    lane = lax.broadcasted_iota(jnp.int32, (1, kw), 1)
    for qb in range(QB):
        n = pl.program_id(1) * QB + qb
        start = pl.multiple_of(C + n * BLOCK, BLOCK)
        kl = k_ref[0, pl.ds(start, 3 * BLOCK), :]
        vl = v_ref[0, pl.ds(start, 3 * BLOCK), :]
        kpos = (n - 1) * BLOCK + kj
        valid = in_window & (kpos >= 0) & (kpos < L)
        heads = []
        for g in range(G):
            parts = []
            for r in range(R):
                q_lo, k_lo = (g * R + r) * d, g * d
                blk = q_ref[0, qb * BLOCK:(qb + 1) * BLOCK, (q_lo // kw) * kw:(q_lo // kw + 1) * kw]
                if (q_lo - k_lo) % kw:
                    sh = (q_lo - k_lo) % kw
                    blk = jnp.concatenate([blk[:, sh:], blk[:, :sh]], axis=1)
                parts.append(jnp.where((lane >= k_lo) & (lane < k_lo + d), blk, jnp.zeros_like(blk)))
            q = jnp.concatenate(parts, axis=0)
            sink = jnp.concatenate([jnp.full((BLOCK, 1), sink_ref[g * R + r] * LOG2E, F32)
                                    for r in range(R)], axis=0)
            s_ctx = lax.dot_general(q, kc, nt, preferred_element_type=F32)
            s_loc = jnp.where(valid, lax.dot_general(q, kl, nt, preferred_element_type=F32), NEG_INF)
            m = jnp.maximum(jnp.maximum(jnp.max(s_ctx, axis=-1, keepdims=True),
                                        jnp.max(s_loc, axis=-1, keepdims=True)), sink)
            e_ctx = jnp.exp2(s_ctx - m)
            e_loc = jnp.exp2(s_loc - m)
            den = (jnp.sum(e_ctx, axis=-1, keepdims=True) + jnp.sum(e_loc, axis=-1, keepdims=True)
                   + jnp.exp2(sink - m))
            inv = 1.0 / den
            o = (jnp.dot((e_ctx * inv).astype(BF16), vc, preferred_element_type=F32)
                 + jnp.dot((e_loc * inv).astype(BF16), vl, preferred_element_type=F32))
            heads += [o[r * BLOCK:(r + 1) * BLOCK, g * d:(g + 1) * d] for r in range(R)]
        o_ref[0, qb * BLOCK:(qb + 1) * BLOCK, :] = jnp.concatenate(heads, axis=1)


def window_attention(qsrc, kpad, vpad, sink, G, R, C, L):
    B, _, _ = qsrc.shape
    Lp, kw = kpad.shape[1], kpad.shape[2]
    d = kw // G
    QB = WIN_QUERY_BLOCKS
    kern = functools.partial(_window_attn_kernel, R=R, G=G, C=C, L=L, QB=QB)
    return pl.pallas_call(
        kern,
        grid=(B, L // (BLOCK * QB)),
        in_specs=[pl.BlockSpec(memory_space=pltpu.SMEM),
                  pl.BlockSpec((1, QB * BLOCK, G * R * d), lambda b, i: (b, i, 0)),
                  pl.BlockSpec((1, Lp, kw), lambda b, i: (b, 0, 0)),
                  pl.BlockSpec((1, Lp, kw), lambda b, i: (b, 0, 0))],
        out_specs=pl.BlockSpec((1, QB * BLOCK, G * R * d), lambda b, i: (b, i, 0)),
        out_shape=jax.ShapeDtypeStruct((B, L, G * R * d), F32),
        compiler_params=_params("parallel", "arbitrary"),
        name="window_attn",
    )(sink.astype(F32), qsrc, kpad, vpad)


_CAND_PAIRS = tuple((r, s) for r in range(PEER_TOPK) for s in range(PEER_TOPK)
                    if (r + 1) * (s + 1) <= PEER_TOPK)
_CAND_ROWS = -(-len(_CAND_PAIRS) // 8) * 8


def _top_rows(vs, n):
    vs = list(vs)
    iota = lax.broadcasted_iota(jnp.int32, vs[0].shape, 0).astype(F32)
    rows = [[] for _ in vs]
    for r in range(n):
        for i, v in enumerate(vs):
            m = jnp.max(v, axis=0, keepdims=True)
            rows[i].append(m)
            if r + 1 < n:
                first = jnp.min(jnp.where(v == m, iota, float(v.shape[0])), axis=0, keepdims=True)
                vs[i] = jnp.where(iota == first, -jnp.inf, v)
    return rows


def _peer_select_kernel(s_ref, ea_ref, q_ref, eb_ref, code_ref, c_ref, *, tm):
    n_lane = tm // 128

    def body(h, carry):
        tiles = [slice(t * 128, (t + 1) * 128) for t in range(n_lane)]
        ab = [s_ref[h, p, :, lanes] for lanes in tiles for p in (0, 1)]
        tops = _top_rows(ab, PEER_TOPK)
        for t, lanes in enumerate(tiles):
            a, b, ta, tb = ab[2 * t], ab[2 * t + 1], tops[2 * t], tops[2 * t + 1]
            c_ref[...] = jnp.full(c_ref.shape, -jnp.inf, F32)
            for k, (r, s) in enumerate(_CAND_PAIRS):
                c_ref[k:k + 1, :] = ta[r] + tb[s]
            best, = _top_rows((c_ref[...],), PEER_TOPK)
            tau = best[PEER_TOPK - 1]
            zsum = jnp.zeros_like(best[0])
            for bk in best:
                zsum = zsum + jnp.exp(bk - best[0])
            code_b = jnp.zeros(b.shape, F32)
            for s in range(PEER_TOPK):
                code_b = code_b + jnp.where(tb[s] > b, 1.0, 0.0)
            code_t = [jnp.zeros(tau.shape, F32)]
            for s in range(1, PEER_TOPK):
                code_t.append(jnp.where(tb[s] == tb[s - 1], code_t[s - 1], float(s)))
            q = jnp.full(a.shape, -1.0, F32)
            for r in reversed(range(PEER_TOPK)):
                q_r = jnp.full(tau.shape, -1.0, F32)
                for s in range(PEER_TOPK // (r + 1)):
                    q_r = jnp.maximum(q_r, jnp.where(ta[r] + tb[s] >= tau, code_t[s], -1.0))
                q = jnp.where(a == ta[r], q_r, q)
            q_ref[h, :, lanes] = q
            code_ref[h, :, lanes] = code_b.astype(BF16)
            ea_ref[h, :, lanes] = jnp.exp(a - ta[0]) / zsum
            eb_ref[h, :, lanes] = jnp.exp(b - tb[0]).astype(BF16)
        return carry

    lax.fori_loop(0, PEER_HEADS, body, 0)


def peer_select(st, tm):
    H, _, n, T = st.shape
    kern = functools.partial(_peer_select_kernel, tm=tm)
    return pl.pallas_call(
        kern,
        grid=(T // tm,),
        in_specs=[pl.BlockSpec((H, 2, n, tm), lambda t: (0, 0, 0, t))],
        out_specs=[pl.BlockSpec((H, n, tm), lambda t: (0, 0, t))] * 4,
        out_shape=[jax.ShapeDtypeStruct((H, n, T), dt) for dt in (F32, F32, BF16, BF16)],
        scratch_shapes=[pltpu.VMEM((_CAND_ROWS, 128), F32)],
        compiler_params=_params("parallel"),
        name="peer_select",
    )(st)


def _gelu_tanh(x):
    k = -2.0 * math.sqrt(2.0 / math.pi) * LOG2E
    return x / (1.0 + jnp.exp2(x * (k + (k * 0.044715) * (x * x))))


def _peer_kernel(x_ref, u_ref, vt_ref, ea_ref, q_ref, eb_ref, code_ref, res_ref, gate_ref, o_ref,
                 acc_ref, act_ref, g_ref, *, ni, tm):
    c = pl.program_id(1)

    @pl.when(c == 0)
    def _():
        acc_ref[...] = jnp.zeros_like(acc_ref)

    act_ref[...] = _gelu_tanh(lax.dot_general(u_ref[...], x_ref[...], (((1,), (1,)), ((), ())),
                                              preferred_element_type=F32))

    for ts in range(tm // 128):
        lanes = slice(ts * 128, (ts + 1) * 128)
        for ii in range(ni):
            rows = slice(ii * N_KEYS, (ii + 1) * N_KEYS)
            w = jnp.zeros((N_KEYS, 128), BF16)
            for h in range(PEER_HEADS):
                sel = code_ref[h, :, lanes] <= q_ref[h, ii:ii + 1, lanes].astype(BF16)
                gate = ea_ref[h, ii:ii + 1, lanes].astype(BF16) * eb_ref[h, :, lanes]
                w = w + jnp.where(sel, gate, jnp.zeros_like(gate))
            g_ref[rows, lanes] = w * act_ref[rows, lanes].astype(BF16)

    acc_ref[...] += jnp.dot(vt_ref[...], g_ref[...], preferred_element_type=F32)

    @pl.when(c == pl.num_programs(1) - 1)
    def _():
        o_ref[...] = res_ref[...] + gate_ref[0] * acc_ref[...].T


def peer_dense(x, u, vt, eat, qt, ebt, codet, res, gate, rows_per_seg, tm, ni):
    T, D = x.shape
    E = u.shape[0]
    ec = ni * N_KEYS
    kern = functools.partial(_peer_kernel, ni=ni, tm=tm)
    nseg = gate.shape[0]
    blocks_per_seg = rows_per_seg // tm
    rows_of_chunk = pl.BlockSpec((PEER_HEADS, ni, tm), lambda t, c: (0, c, t))
    return pl.pallas_call(
        kern,
        grid=(T // tm, E // ec),
        in_specs=[pl.BlockSpec((tm, D), lambda t, c: (t, 0)),
                  pl.BlockSpec((ec, D), lambda t, c: (c, 0)),
                  pl.BlockSpec((D, ec), lambda t, c: (0, c)),
                  rows_of_chunk,
                  rows_of_chunk,
                  pl.BlockSpec((PEER_HEADS, N_KEYS, tm), lambda t, c: (0, 0, t)),
                  pl.BlockSpec((PEER_HEADS, N_KEYS, tm), lambda t, c: (0, 0, t)),
                  pl.BlockSpec((tm, D), lambda t, c: (t, 0)),
                  pl.BlockSpec((1, 1, D), lambda t, c: (jnp.minimum(t // blocks_per_seg, nseg - 1), 0, 0))],
        out_specs=pl.BlockSpec((tm, D), lambda t, c: (t, 0)),
        out_shape=jax.ShapeDtypeStruct((T, D), F32),
        scratch_shapes=[pltpu.VMEM((D, tm), F32),
                        pltpu.VMEM((ec, tm), F32),
                        pltpu.VMEM((ec, tm), BF16)],
        compiler_params=_params("parallel", "arbitrary"),
        name="peer_dense",
    )(x, u, vt, eat, qt, ebt, codet, res, gate.reshape(nseg, 1, D))


def peer(qp, h_bf, keys, u_bf, vt_bf, res, gate, rows_per_seg, tm):
    T = qp.shape[0]
    dk = keys.shape[-1]
    kflat = keys.reshape(PEER_HEADS * 2, N_KEYS, dk).astype(BF16)
    st = keys_times_qT(kflat, qp, tm).reshape(PEER_HEADS, 2, N_KEYS, T)
    eat, qt, ebt, codet = peer_select(st, tm)
    return peer_dense(h_bf, u_bf, vt_bf, eat, qt, ebt, codet, res, gate, rows_per_seg, tm, 8)


def rmsnorm(x, g):
    return x * lax.rsqrt(jnp.mean(x * x, axis=-1, keepdims=True) + EPS) * g


def axial_rope(n_tok, dim):
    rows = n_tok // GRID_W
    row = jnp.repeat(jnp.arange(rows, dtype=F32), GRID_W)
    col = jnp.tile(jnp.arange(GRID_W, dtype=F32), rows)
    axis_dim = dim // 2
    inv_freq = ROPE_THETA ** (-jnp.arange(0, axis_dim, 2, dtype=F32) / axis_dim)
    ang = jnp.concatenate([row[:, None] * inv_freq, col[:, None] * inv_freq], axis=-1)
    return jnp.cos(ang), jnp.sin(ang)


def apply_rope(x, cos, sin):
    shape = (1, x.shape[1]) + (1,) * (x.ndim - 3) + (cos.shape[-1],)
    c, s = cos.reshape(shape), sin.reshape(shape)
    x1, x2 = jnp.split(x, 2, axis=-1)
    return jnp.concatenate([x1 * c - x2 * s, x1 * s + x2 * c], axis=-1)


def short_conv(u, w, b):
    up = jnp.pad(u, ((0, 0), (1, 1), (0, 0)))
    return up[:, :-2] * w[0] + up[:, 1:-1] * w[1] + up[:, 2:] * w[2] + b


def hyena_filters(n_tok, w1, b1, w2, b2, w3, b3, freq):
    hp = lax.Precision.HIGHEST
    t = jnp.linspace(0.0, 1.0, n_tok, dtype=F32)[:, None]
    w = (2.0 * math.pi / n_tok) * jnp.arange(n_tok, dtype=F32)[:, None]
    f = jnp.linspace(1e-4, HY_BANDS - 1, HY_BANDS, dtype=F32)[None, :]
    feat = jnp.concatenate([t, jnp.cos(f * w), -jnp.sin(f * w)], axis=-1)
    hdn = jnp.sin(freq * (jnp.dot(feat, w1, precision=hp) + b1))
    hdn = jnp.sin(freq * (jnp.dot(hdn, w2, precision=hp) + b2))
    filt = (jnp.dot(hdn, w3, precision=hp) + b3).reshape(n_tok, HY_ORDER, 2, HY_CH)
    deltas = jnp.abs(jnp.linspace(math.log(HY_TARGET) / HY_SLOW, math.log(HY_TARGET) / HY_FAST, HY_CH,
                                  dtype=F32))
    filt = filt * jnp.exp(-t[:, :, None, None] * deltas)
    fwd, bwd = filt[:, :, 0], filt[:, :, 1]
    kfull = jnp.concatenate([fwd, jnp.zeros_like(fwd[:1]), bwd[1:][::-1]], axis=0)
    return kfull / jnp.sum(jnp.abs(kfull), axis=0, keepdims=True)


def hyena_filters_t(n_tok, w1, b1, w2, b2, w3, b3, freq):
    hp = lax.Precision.HIGHEST
    t = jnp.linspace(0.0, 1.0, n_tok, dtype=F32)[:, None]
    w = (2.0 * math.pi / n_tok) * jnp.arange(n_tok, dtype=F32)[:, None]
    f = jnp.linspace(1e-4, HY_BANDS - 1, HY_BANDS, dtype=F32)[None, :]
    feat = jnp.concatenate([t, jnp.cos(f * w), -jnp.sin(f * w)], axis=-1)
    hdn = jnp.sin(freq * (jnp.dot(feat, w1, precision=hp) + b1))
    hdn = jnp.sin(freq * (jnp.dot(hdn, w2, precision=hp) + b2))
    deltas = jnp.abs(jnp.linspace(math.log(HY_TARGET) / HY_SLOW, math.log(HY_TARGET) / HY_FAST, HY_CH,
                                  dtype=F32))
    w3t = jnp.transpose(w3.reshape(-1, HY_ORDER, 2, HY_CH), (2, 1, 3, 0))
    b3t = jnp.transpose(b3.reshape(HY_ORDER, 2, HY_CH), (1, 0, 2))[..., None]

    def half(d, hidden, times):
        decay = jnp.exp(-deltas[:, None] * times[None, :])
        return (jnp.einsum('ock,nk->ocn', w3t[d], hidden, precision=hp) + b3t[d]) * decay

    fwd = half(0, hdn, t[:, 0])
    bwd_rev = half(1, hdn[::-1], t[::-1, 0])
    kfull = jnp.concatenate([fwd, jnp.zeros_like(fwd[..., :1]), bwd_rev[..., :n_tok - 1]], axis=-1)
    kfull = kfull / jnp.sum(jnp.abs(kfull), axis=-1, keepdims=True)
    return kfull.reshape(HY_ORDER * HY_CH, 2 * n_tok)


def long_conv(z, kf):
    n = z.shape[1]
    zf = jnp.fft.rfft(z, n=2 * n, axis=1)
    hf = jnp.fft.rfft(kf, n=2 * n, axis=0)
    return jnp.fft.irfft(zf * hf[None], n=2 * n, axis=1)[:, :n]


def hyena_small(u, conv_w, conv_b, filter_params, bias):
    n = u.shape[1]
    u = short_conv(u, conv_w, conv_b)
    v, x1, x2 = jnp.split(u, 3, axis=-1)
    kfull = hyena_filters(n, *filter_params)
    z = x1 * (long_conv(v, kfull[:, 0]) + bias[0] * v)
    return x2 * (long_conv(z, kfull[:, 1]) + bias[1] * z)


def _hi_lo(x):
    hi = x.astype(BF16)
    return hi, (x - hi.astype(F32)).astype(BF16)


def _dot3(a, b):
    d = lambda x, y: jnp.dot(x, y, preferred_element_type=F32)
    return d(a[0], b[0]) + d(a[1], b[0]) + d(a[0], b[1])


def _dft_constants(n_tok):
    n = 2 * n_tok
    n1 = n // 128
    a1 = 2.0 * np.pi * np.outer(np.arange(n1), np.arange(n1)) / n1
    a2 = 2.0 * np.pi * np.outer(np.arange(128), np.arange(128)) / 128
    at = 2.0 * np.pi * np.outer(np.arange(n1), np.arange(128)) / n
    c1, s1, c2, s2 = np.cos(a1), np.sin(a1), np.cos(a2), np.sin(a2)
    pair = lambda m: _hi_lo(jnp.asarray(m, F32))
    return dict(
        f1_half=pair(np.concatenate([c1[:, :n1 // 2], -s1[:, :n1 // 2]], axis=0)),
        f1_full=pair(np.concatenate([c1, -s1], axis=0)),
        m_fwd=pair(np.block([[c2, -s2], [s2, c2]])),
        m_inv=pair(np.block([[c2, s2], [-s2, c2]])),
        g_half=pair(np.concatenate([c1[:n1 // 2], -s1[:n1 // 2]], axis=1)),
        tr=jnp.asarray(np.cos(at), F32), ti=jnp.asarray(-np.sin(at), F32))


def _dft_fwd(seqs, f1, tr, ti, m_fwd):
    n1 = tr.shape[0]
    y = _dot3(f1, _hi_lo(jnp.concatenate(seqs, axis=1)))
    rows = []
    for k in range(len(seqs)):
        yr, yi = y[:n1, k * 128:(k + 1) * 128], y[n1:, k * 128:(k + 1) * 128]
        rows.append(jnp.concatenate([yr * tr - yi * ti, yr * ti + yi * tr], axis=1))
    return _dot3(_hi_lo(jnp.concatenate(rows, axis=0)), m_fwd)


def _dft_inv_half(p, g_half, tr, ti, m_inv):
    n1 = tr.shape[0]
    u = _dot3(_hi_lo(p), m_inv)
    cols = []
    for k in range(p.shape[0] // n1):
        ur, ui = u[k * n1:(k + 1) * n1, :128], u[k * n1:(k + 1) * n1, 128:]
        cols.append(jnp.concatenate([ur * tr + ui * ti, ui * tr - ur * ti], axis=0))
    return _dot3(g_half, _hi_lo(jnp.concatenate(cols, axis=1))) * (1.0 / (n1 * 128))


def _pairs(refs):
    return (refs[0][...], refs[1][...])


def _spectrum_kernel(a_ref, f1h, f1l, tr_ref, ti_ref, mh, ml, o_ref):
    cb, n1 = a_ref.shape[0], tr_ref.shape[0]
    x = _dft_fwd([a_ref[k] for k in range(cb)], _pairs((f1h, f1l)), tr_ref[...], ti_ref[...],
                 _pairs((mh, ml)))
    o_ref[...] = x.reshape(cb, n1, 256)


def filter_spectrum(kf, consts, cb):
    items, n1, _ = kf.shape
    full = lambda shape: pl.BlockSpec(shape, lambda i: (0,) * len(shape))
    f1, m = consts["f1_full"], consts["m_fwd"]
    return pl.pallas_call(
        _spectrum_kernel,
        grid=(items // cb,),
        in_specs=[pl.BlockSpec((cb, n1, 128), lambda i: (i, 0, 0)),
                  full(f1[0].shape), full(f1[1].shape), full((n1, 128)), full((n1, 128)),
                  full(m[0].shape), full(m[1].shape)],
        out_specs=pl.BlockSpec((cb, n1, 256), lambda i: (i, 0, 0)),
        out_shape=jax.ShapeDtypeStruct((items, n1, 256), F32),
        compiler_params=_params("parallel"),
        name="filter_spectrum",
    )(kf, f1[0], f1[1], consts["tr"], consts["ti"], m[0], m[1])


def _conv_gate_kernel(bias_ref, u_ref, g_ref, h_ref, f1h, f1l, tr_ref, ti_ref, mfh, mfl, mih, mil, gh, gl,
                      o_ref, *, cb):
    tr, ti = tr_ref[...], ti_ref[...]
    n1 = tr.shape[0]
    c0 = pl.program_id(1) * cb
    u = [u_ref[0, k] for k in range(cb)]
    x = _dft_fwd(u, _pairs((f1h, f1l)), tr, ti, _pairs((mfh, mfl)))
    prod = []
    for k in range(cb):
        xr, xi = x[k * n1:(k + 1) * n1, :128], x[k * n1:(k + 1) * n1, 128:]
        hr, hi = h_ref[k, :, :128], h_ref[k, :, 128:]
        prod.append(jnp.concatenate([xr * hr - xi * hi, xr * hi + xi * hr], axis=1))
    y = _dft_inv_half(jnp.concatenate(prod, axis=0), _pairs((gh, gl)), tr, ti, _pairs((mih, mil)))
    for k in range(cb):
        o_ref[0, k] = g_ref[0, k] * (y[:, k * 128:(k + 1) * 128] + bias_ref[c0 + k] * u[k])


def conv_gate(u, gate, spec, bias, consts, cb):
    B, C, half, _ = u.shape
    n1 = 2 * half
    full = lambda shape: pl.BlockSpec(shape, lambda b, c: (0,) * len(shape))
    seq = pl.BlockSpec((1, cb, half, 128), lambda b, c: (b, c, 0, 0))
    mats = [*consts["f1_half"], consts["tr"], consts["ti"], *consts["m_fwd"], *consts["m_inv"],
            *consts["g_half"]]
    return pl.pallas_call(
        functools.partial(_conv_gate_kernel, cb=cb),
        grid=(B, C // cb),
        in_specs=[pl.BlockSpec(memory_space=pltpu.SMEM), seq, seq,
                  pl.BlockSpec((cb, n1, 256), lambda b, c: (c, 0, 0))] + [full(m.shape) for m in mats],
        out_specs=seq,
        out_shape=jax.ShapeDtypeStruct(u.shape, F32),
        compiler_params=_params("parallel", "arbitrary"),
        name="conv_gate",
    )(bias.astype(F32), u, gate, spec, *mats)


def hyena(u, conv_w, conv_b, filter_params, bias):
    B, n, _ = u.shape
    half = n // 128
    consts = _dft_constants(n)
    u = short_conv(u, conv_w, conv_b)
    seqs = jnp.transpose(u, (0, 2, 1)).reshape(B, 3, HY_CH, half, 128)
    v, x1, x2 = seqs[:, 0], seqs[:, 1], seqs[:, 2]
    kf = hyena_filters_t(n, *filter_params).reshape(HY_ORDER * HY_CH, 2 * half, 128)
    spec = filter_spectrum(kf, consts, HY_CH_BLOCK).reshape(HY_ORDER, HY_CH, 2 * half, 256)
    z = conv_gate(v, x1, spec[0], bias[0], consts, HY_CH_BLOCK)
    o = conv_gate(z, x2, spec[1], bias[1], consts, HY_CH_BLOCK)
    return jnp.transpose(o.reshape(B, HY_CH, n), (0, 2, 1))


def _head_major_t(t):
    B, L, H, d = t.shape
    return jnp.transpose(t, (0, 2, 3, 1)).reshape(B * H, d, L)


def _key_chunk(n_keys):
    for c in range(KEY_CHUNK_CAP, 0, -128):
        if n_keys % c == 0:
            return c
    raise ValueError(f"no key chunk for {n_keys}")


def kernel(x, c, ctx, c_ctx, w_mod, b_mod, g_norm_mix, w_in, g_qk_diff, lambda_diff, g_qk_win, sink_win, g_qk_glob, hy_conv_w, hy_conv_b, hy_w1, hy_b1, hy_w2, hy_b2, hy_w3, hy_b3, hy_freq, hy_bias, g_mix_out, w_out, g_norm_ffn, peer_wq, peer_keys, peer_u, peer_v):
    B, L, D = x.shape
    C = ctx.shape[1]
    depth = w_mod.shape[0]
    TM = 512
    n_lat = B * L
    n_ctx = B * C
    assert L % TM == 0 and n_ctx % TM == 0
    nseg = B + 1

    def rope_rows(dim):
        cos, sin = axial_rope(L, dim)
        return (jnp.concatenate([jnp.tile(cos, (B, 1)), jnp.ones((n_ctx, dim // 2), F32)], axis=0),
                jnp.concatenate([jnp.tile(sin, (B, 1)), jnp.zeros((n_ctx, dim // 2), F32)], axis=0))

    rope_rows_half = rope_rows(DIFF_QK_DIM)
    rope_rows_full = rope_rows(HEAD_DIM)
    sc = jnp.concatenate([jax.nn.silu(c), jax.nn.silu(c_ctx)[None]], axis=0)
    sc = jnp.pad(sc, ((0, 8 - nseg), (0, 0))).astype(BF16)

    xall = jnp.concatenate([x.reshape(n_lat, D), ctx.reshape(n_ctx, D)], axis=0)

    for i in range(depth):
        want_ctx = i < depth - 1
        lambda_init = 0.8 - 0.6 * math.exp(-0.3 * i)
        mod = mm(sc, w_mod[i].astype(BF16), 8, 1024)[:nseg] + b_mod[i]
        sh1, s1, g1, sh2, s2, g2 = jnp.split(mod, N_MOD, axis=-1)

        groups = normmod_mm(xall, g_norm_mix[i], sh1, s1, w_in[i].astype(BF16), L, TM, F32, False,
                            widths=(W_DIFF, W_WIN, W_GLOB, W_HY))
        ga_, gw_, gg_, gh_ = groups
        ph, phc = gh_[:n_lat].reshape(B, L, -1), gh_[n_lat:].reshape(B, C, -1)

        def gains(g_qk, n_q, n_k, d):
            return jnp.concatenate([jnp.tile(g_qk[0], n_q) * (d ** -0.5 * LOG2E), jnp.tile(g_qk[1], n_k)])

        def split_rows(t, *shape):
            return t[:n_lat].reshape(B, L, *shape), t[n_lat:].reshape(B, C, *shape)

        def rows_of(t):
            lat, cx = t[:n_lat].reshape(B, L, -1), t[n_lat:].reshape(B, C, -1)
            return lat, cx, jnp.concatenate([cx, lat], axis=1)

        qka, qkac, qka_keys = rows_of(head_prep(ga_, 2 * GROUP_W, DIFF_QK_DIM,
                                                gains(g_qk_diff[i], 2 * DIFF_HEADS, 2 * DIFF_HEADS, DIFF_QK_DIM),
                                                *rope_rows_half, TM))
        va, vac = split_rows(ga_[:, 2 * GROUP_W:].astype(BF16), DIFF_HEADS, HEAD_DIM)
        lam_vec = lambda_diff[i]
        lam = (jnp.exp(jnp.sum(lam_vec[0] * lam_vec[1])) - jnp.exp(jnp.sum(lam_vec[2] * lam_vec[3]))
               + lambda_init)
        coefs = jnp.stack([jnp.ones((), F32), -lam])
        vva = jnp.concatenate([vac, va], axis=1)
        oa = dense_attention(qka, qka_keys, GROUP_W, _head_major_t(vva), coefs, None,
                             2, DIFF_HEADS, 1, DIFF_QK_DIM, ATTN_TQ_DIFF, _key_chunk(C + L))

        def gqa_parts(g_, g_qk, n_q, n_kv):
            wq, wk = n_q * HEAD_DIM, n_kv * HEAD_DIM
            qk = head_prep(g_, wq + wk, HEAD_DIM, gains(g_qk, n_q, n_kv, HEAD_DIM), *rope_rows_full, TM)
            v, vc = split_rows(g_[:, wq + wk:].astype(BF16), n_kv, HEAD_DIM)
            return rows_of(qk) + (v, vc)

        WQ = WIN_HEADS * HEAD_DIM
        WK = WIN_KV_HEADS * HEAD_DIM
        qkw, qkwc, _, vw, vwc = gqa_parts(gw_, g_qk_win[i], WIN_HEADS, WIN_KV_HEADS)
        zblk = jnp.zeros((B, BLOCK, WK), BF16)
        kpad = jnp.concatenate([qkwc[..., WQ:], zblk, qkw[..., WQ:], zblk], axis=1)
        vpad = jnp.concatenate([vwc.reshape(B, C, WK), zblk, vw.reshape(B, L, WK), zblk], axis=1)
        Rw = WIN_HEADS // WIN_KV_HEADS
        ob = window_attention(qkw, kpad, vpad, sink_win[i], WIN_KV_HEADS, Rw, C, L)

        qkg, qkgc, qkg_keys, vg, vgc = gqa_parts(gg_, g_qk_glob[i], GLOB_HEADS, GLOB_KV_HEADS)
        Rg = GLOB_HEADS // GLOB_KV_HEADS
        one = jnp.ones((1,), F32)
        vvg = _head_major_t(jnp.concatenate([vgc, vg], axis=1))
        og = dense_attention(qkg, qkg_keys, GLOB_HEADS * HEAD_DIM, vvg, one, None,
                             1, GLOB_KV_HEADS, Rg, HEAD_DIM, ATTN_TQ_GLOB, _key_chunk(C + L))

        filt = (hy_w1[i], hy_b1[i], hy_w2[i], hy_b2[i], hy_w3[i], hy_b3[i], hy_freq[i])
        oh = hyena(ph, hy_conv_w[i], hy_conv_b[i], filt, hy_bias[i])

        mixed = [jnp.concatenate([oa, ob, og, oh], axis=-1).reshape(n_lat, D)]
        if want_ctx:
            oac = dense_attention(qkac, qkac, GROUP_W, _head_major_t(vac), coefs, None,
                                  2, DIFF_HEADS, 1, DIFF_QK_DIM, C, C)
            obc = dense_attention(qkwc, qkwc, WQ, _head_major_t(vwc), one, sink_win[i],
                                  1, WIN_KV_HEADS, Rw, HEAD_DIM, C, C)
            ogc = dense_attention(qkgc, qkgc, GLOB_HEADS * HEAD_DIM, _head_major_t(vgc), one, None,
                                  1, GLOB_KV_HEADS, Rg, HEAD_DIM, C, C)
            ohc = hyena_small(phc, hy_conv_w[i], hy_conv_b[i], filt, hy_bias[i])
            mixed.append(jnp.concatenate([oac, obc, ogc, ohc], axis=-1).reshape(n_ctx, D))
        o = jnp.concatenate(mixed, axis=0)
        n_rows = o.shape[0]

        oh_ = rmsnorm(o.reshape(n_rows, N_OUT_HEADS, HEAD_DIM), g_mix_out[i].reshape(N_OUT_HEADS, HEAD_DIM))
        head_scale = jnp.where(jnp.arange(N_OUT_HEADS) < DIFF_HEADS, 1.0 - lambda_init, 1.0)[:, None]
        om = (oh_ * head_scale.astype(F32)).reshape(n_rows, D)
        xcur = mm_residual(om.astype(BF16), w_out[i].astype(BF16), xall, g1, L, TM)

        qp, h2 = normmod_mm(xcur, g_norm_ffn[i], sh2, s2, peer_wq[i].astype(BF16), L, TM, BF16, True)
        xall = peer(qp, h2, peer_keys[i], peer_u[i].astype(BF16), jnp.transpose(peer_v[i].astype(BF16)),
                    xcur, g2, L, TM)

    return xall[:n_lat].reshape(B, L, D)
```

```python
import functools
import math

import jax
import jax.numpy as jnp
import numpy as np
from jax import lax
from jax.experimental import pallas as pl
from jax.experimental.pallas import tpu as pltpu

F32 = jnp.float32
BF16 = jnp.bfloat16

GRID_W = 64
HEAD_DIM = 64
BLOCK = 128
WINDOW = 128
ROPE_THETA = 10000.0
EPS = 1e-6
NEG_INF = -1e30
N_MOD = 6
DIFF_HEADS = 4
DIFF_QK_DIM = 32
WIN_HEADS = 4
WIN_KV_HEADS = 2
GLOB_HEADS = 4
GLOB_KV_HEADS = 2
N_OUT_HEADS = 16
HY_CH = 256
HY_ORDER = 2
HY_BANDS = 16
HY_TARGET = 1e-2
HY_FAST = 0.3
HY_SLOW = 1.5
W_DIFF = 768
W_WIN = 512
W_GLOB = 512
W_HY = 768
GROUP_W = 256
PEER_HEADS = 8
N_KEYS = 128
PEER_TOPK = 16

VMEM_LIMIT = 56 * 1024 * 1024
ATTN_TQ_DIFF = 512
ATTN_TQ_GLOB = 256
KEY_CHUNK_CAP = 1408
HY_CH_BLOCK = 16
WIN_QUERY_BLOCKS = 4


def _params(*sem):
    return pltpu.CompilerParams(dimension_semantics=sem, vmem_limit_bytes=VMEM_LIMIT)


def _mm_kernel(a_ref, b_ref, o_ref):
    o_ref[...] = jnp.dot(a_ref[...], b_ref[...], preferred_element_type=F32)


def mm(a, b, tm, tn):
    M, K = a.shape
    N = b.shape[1]
    return pl.pallas_call(
        _mm_kernel,
        grid=(M // tm, N // tn),
        in_specs=[pl.BlockSpec((tm, K), lambda i, j: (i, 0)),
                  pl.BlockSpec((K, tn), lambda i, j: (0, j))],
        out_specs=pl.BlockSpec((tm, tn), lambda i, j: (i, j)),
        out_shape=jax.ShapeDtypeStruct((M, N), F32),
        compiler_params=_params("parallel", "arbitrary"),
        name="mm",
    )(a, b)


def _seg_spec(rows_per_seg, tm, nseg, D):
    blocks_per_seg = rows_per_seg // tm
    return pl.BlockSpec((1, 1, D), lambda i: (jnp.minimum(i // blocks_per_seg, nseg - 1), 0, 0))


def _normmod_mm_kernel(x_ref, g_ref, sh_ref, sc_ref, w_ref, *outs, widths, with_h):
    x = x_ref[...]
    y = x * lax.rsqrt(jnp.mean(x * x, axis=-1, keepdims=True) + EPS) * g_ref[...]
    h = (y * (1.0 + sc_ref[0]) + sh_ref[0]).astype(BF16)
    if with_h:
        outs[-1][...] = h
    r = jnp.dot(h, w_ref[...], preferred_element_type=F32)
    lo = 0
    for o_ref, n in zip(outs, widths):
        o_ref[...] = r[:, lo:lo + n].astype(o_ref.dtype)
        lo += n


def normmod_mm(x, g, shift, scale, w, rows_per_seg, tm, out_dtype, with_h, widths=None):
    M, D = x.shape
    N = w.shape[1]
    widths = (N,) if widths is None else tuple(widths)
    nseg = shift.shape[0]
    seg = _seg_spec(rows_per_seg, tm, nseg, D)
    row_block = lambda n: pl.BlockSpec((tm, n), lambda i: (i, 0))
    out_specs = [row_block(n) for n in widths]
    out_shape = [jax.ShapeDtypeStruct((M, n), out_dtype) for n in widths]
    if with_h:
        out_specs.append(row_block(D))
        out_shape.append(jax.ShapeDtypeStruct((M, D), BF16))
    return pl.pallas_call(
        functools.partial(_normmod_mm_kernel, widths=widths, with_h=with_h),
        grid=(M // tm,),
        in_specs=[row_block(D), pl.BlockSpec((1, D), lambda i: (0, 0)), seg, seg,
                  pl.BlockSpec((D, N), lambda i: (0, 0))],
        out_specs=out_specs,
        out_shape=out_shape,
        compiler_params=_params("parallel"),
        name="normmod_mm",
    )(x, g.reshape(1, D), shift.reshape(nseg, 1, D), scale.reshape(nseg, 1, D), w)


def _merge_residual_kernel(*refs, n_in, d):
    in_refs, (gain_ref, bd_ref, w_ref, x_ref, gate_ref, o_ref) = refs[:n_in], refs[n_in:]
    bd = bd_ref[...]
    parts, lo = [], 0
    for r in in_refs:
        o = r[...]
        ms = sum(jnp.dot(p, bd, preferred_element_type=F32) for p in _hi_lo(o * o))
        parts.append((o * lax.rsqrt(ms + EPS) * gain_ref[:, lo:lo + o.shape[1]]).astype(BF16))
        lo += o.shape[1]
    a = jnp.concatenate(parts, axis=1)
    o_ref[...] = x_ref[...] + gate_ref[0] * jnp.dot(a, w_ref[...], preferred_element_type=F32)


def merge_residual(outs, d, gain, w, x, gate, rows_per_seg, tm):
    M, width = outs[0].shape
    K, N = w.shape
    nseg = gate.shape[0]
    col = np.arange(width)
    bd = jnp.asarray((col[:, None] // d == col[None, :] // d) / d, BF16)
    row_block = lambda n: pl.BlockSpec((tm, n), lambda i: (i, 0))
    return pl.pallas_call(
        functools.partial(_merge_residual_kernel, n_in=len(outs), d=d),
        grid=(M // tm,),
        in_specs=[row_block(width)] * len(outs) + [
            pl.BlockSpec((1, K), lambda i: (0, 0)), pl.BlockSpec((width, width), lambda i: (0, 0)),
            pl.BlockSpec((K, N), lambda i: (0, 0)), row_block(N), _seg_spec(rows_per_seg, tm, nseg, N)],
        out_specs=row_block(N),
        out_shape=jax.ShapeDtypeStruct((M, N), F32),
        compiler_params=_params("parallel"),
        name="merge_residual",
    )(*outs, gain.reshape(1, K).astype(F32), bd, w, x, gate.reshape(nseg, 1, N))


def _scores_kernel(k_ref, q_ref, o_ref):
    d = k_ref.shape[-1]
    for p in range(k_ref.shape[0]):
        o_ref[p] = lax.dot_general(k_ref[p], q_ref[:, p * d:(p + 1) * d], (((1,), (1,)), ((), ())),
                                   preferred_element_type=F32)


def keys_times_qT(keys, q, tn):
    P, n, d = keys.shape
    T = q.shape[0]
    return pl.pallas_call(
        _scores_kernel,
        grid=(T // tn,),
        in_specs=[pl.BlockSpec((P, n, d), lambda j: (0, 0, 0)),
                  pl.BlockSpec((tn, P * d), lambda j: (j, 0))],
        out_specs=pl.BlockSpec((P, n, tn), lambda j: (0, 0, j)),
        out_shape=jax.ShapeDtypeStruct((P, n, T), F32),
        compiler_params=_params("parallel"),
        name="peer_scores_t",
    )(keys, q)


def _head_prep_kernel(t_ref, cos_ref, sin_ref, gain_ref, bd_ref, pm_ref, ex_ref, o_ref):
    d3 = lambda parts, m: sum(jnp.dot(p, m, preferred_element_type=F32) for p in parts)

    def split3(v):
        hi, lo = _hi_lo(v)
        rest = v - hi.astype(F32) - lo.astype(F32)
        return hi, lo, rest.astype(BF16)

    x = t_ref[...]
    ms = d3(_hi_lo(x * x), bd_ref[...])
    y = x * lax.rsqrt(ms + EPS) * gain_ref[...]
    ex = ex_ref[...]
    cos = d3(split3(cos_ref[...]), ex)
    sin = d3(split3(sin_ref[...]), ex)
    o_ref[...] = (y * cos + d3(_hi_lo(y), pm_ref[...]) * sin).astype(BF16)


def head_prep(t, width, d, gain, cos_rows, sin_rows, tm):
    rows = t.shape[0]
    hd = d // 2
    col = np.arange(width)
    same_head = (col[:, None] // d) == (col[None, :] // d)
    bd = jnp.asarray(same_head / d, BF16)
    src = np.where((col % d) < hd, col + hd, col - hd)
    pm = np.zeros((width, width), np.float32)
    pm[src, col] = np.where((col % d) < hd, -1.0, 1.0)
    ex = np.zeros((hd, width), np.float32)
    ex[col % hd, col] = 1.0
    full = lambda a: pl.BlockSpec(a.shape, lambda i: (0,) * a.ndim)
    consts = (gain.reshape(1, width).astype(F32), bd, jnp.asarray(pm, BF16), jnp.asarray(ex, BF16))
    return pl.pallas_call(
        _head_prep_kernel,
        grid=(rows // tm,),
        in_specs=[pl.BlockSpec((tm, width), lambda i: (i, 0)),
                  pl.BlockSpec((tm, hd), lambda i: (i, 0)),
                  pl.BlockSpec((tm, hd), lambda i: (i, 0))] + [full(c) for c in consts],
        out_specs=pl.BlockSpec((tm, width), lambda i: (i, 0)),
        out_shape=jax.ShapeDtypeStruct((rows, width), BF16),
        compiler_params=_params("parallel"),
        name="head_prep",
    )(t, cos_rows, sin_rows, *consts)


LOG2E = math.log2(math.e)


def _dense_attn_kernel(coef_ref, sink_ref, q_ref, k_ref, vt_ref, o_ref, sa_ref, sb_ref, *, nbr, R, G,
                       HP, d, tq, tk, nk, has_sink):
    cols = R * tq
    dv = vt_ref.shape[1]
    kw = k_ref.shape[-1]
    g0 = (pl.program_id(0) % (G // HP)) * HP
    nt = (((1,), (1,)), ((), ()))
    bufs = (sa_ref, sb_ref)
    lane = lax.broadcasted_iota(jnp.int32, (1, kw), 1)
    heads = []
    for hp in range(HP):
        out = jnp.zeros((dv, cols), F32)
        for br in range(nbr):
            k_lo = (hp * nbr + br) * d
            parts = []
            for r in range(R):
                q_lo = ((hp * R + r) * nbr + br) * d
                blk = q_ref[0, :, (q_lo // kw) * kw:(q_lo // kw + 1) * kw]
                if (q_lo - k_lo) % kw:
                    sh = (q_lo - k_lo) % kw
                    blk = jnp.concatenate([blk[:, sh:], blk[:, :sh]], axis=1)
                parts.append(jnp.where((lane >= k_lo) & (lane < k_lo + d), blk, jnp.zeros_like(blk)))
            q = parts[0] if R == 1 else jnp.concatenate(parts, axis=0)
            if has_sink:
                m = jnp.concatenate([jnp.full((1, tq), sink_ref[(g0 + hp) * R + r] * LOG2E, F32)
                                     for r in range(R)], axis=1)
                l = jnp.ones((1, cols), F32)
            else:
                m = jnp.full((1, cols), NEG_INF, F32)
                l = jnp.zeros((1, cols), F32)
            acc = jnp.zeros((dv, cols), F32)
            bufs[0][...] = lax.dot_general(k_ref[0, 0:tk, :], q, nt, preferred_element_type=F32)
            for j in range(nk):
                if j + 1 < nk:
                    bufs[(j + 1) % 2][...] = lax.dot_general(k_ref[0, (j + 1) * tk:(j + 2) * tk, :], q,
                                                             nt, preferred_element_type=F32)
                s = bufs[j % 2][...]
                m_new = jnp.maximum(m, jnp.max(s, axis=0, keepdims=True))
                alpha = jnp.exp2(m - m_new)
                p = jnp.exp2(s - m_new)
                l = alpha * l + jnp.sum(p, axis=0, keepdims=True)
                acc = alpha * acc + jnp.dot(vt_ref[hp, :, j * tk:(j + 1) * tk], p.astype(BF16),
                                            preferred_element_type=F32)
                m = m_new
            out = out + coef_ref[br] * (acc / l)
        out_t = out.T
        heads += [out_t[r * tq:(r + 1) * tq] for r in range(R)]
    o_ref[0] = jnp.concatenate(heads, axis=1)


def dense_attention(qsrc, ksrc, k_lane0, vt, coefs, sink, nbr, G, R, d, tq, tk):
    B, Lq, _ = qsrc.shape
    Lk = ksrc.shape[1]
    dv = vt.shape[1]
    HP = 128 // (nbr * d)
    assert G % HP == 0 and (HP * R * dv) % 128 == 0 and k_lane0 % 128 == 0
    has_sink = sink is not None
    if sink is None:
        sink = jnp.zeros((1,), F32)
    kern = functools.partial(_dense_attn_kernel, nbr=nbr, R=R, G=G, HP=HP, d=d, tq=tq, tk=tk, nk=Lk // tk,
                             has_sink=has_sink)
    gp = G // HP
    qw = HP * R * nbr * d
    return pl.pallas_call(
        kern,
        grid=(B * gp, Lq // tq),
        in_specs=[pl.BlockSpec(memory_space=pltpu.SMEM),
                  pl.BlockSpec(memory_space=pltpu.SMEM),
                  pl.BlockSpec((1, tq, qw), lambda b, i: (b // gp, i, b % gp)),
                  pl.BlockSpec((1, Lk, 128), lambda b, i: (b // gp, 0, k_lane0 // 128 + b % gp)),
                  pl.BlockSpec((HP, dv, Lk), lambda b, i: (b, 0, 0))],
        out_specs=pl.BlockSpec((1, tq, HP * R * dv), lambda b, i: (b // gp, i, b % gp)),
        out_shape=jax.ShapeDtypeStruct((B, Lq, G * R * dv), F32),
        scratch_shapes=[pltpu.VMEM((tk, R * tq), F32), pltpu.VMEM((tk, R * tq), F32)],
        compiler_params=_params("parallel", "arbitrary"),
        name="dense_attn",
    )(coefs.astype(F32), sink.astype(F32), qsrc, ksrc, vt)


def _window_attn_kernel(sink_ref, q_ref, k_ref, v_ref, o_ref, *, R, G, C, L, QB):
    rows = R * BLOCK
    kw = k_ref.shape[-1]
    d = kw // G
    kc = k_ref[0, 0:C, :]
    vc = v_ref[0, 0:C, :]
    nt = (((1,), (1,)), ((), ()))
    qi = lax.broadcasted_iota(jnp.int32, (rows, 3 * BLOCK), 0) & (BLOCK - 1)
    kj = lax.broadcasted_iota(jnp.int32, (rows, 3 * BLOCK), 1)
    in_window = jnp.abs(kj - BLOCK - qi) <= WINDOW
    lane = lax.broadcasted_iota(jnp.int32, (1, kw), 1)
    for qb in range(QB):
        n = pl.program_id(1) * QB + qb
        start = pl.multiple_of(C + n * BLOCK, BLOCK)
        kl = k_ref[0, pl.ds(start, 3 * BLOCK), :]
        vl = v_ref[0, pl.ds(start, 3 * BLOCK), :]
        kpos = (n - 1) * BLOCK + kj
        valid = in_window & (kpos >= 0) & (kpos < L)
        heads = []
        for g in range(G):
            parts = []
            for r in range(R):
                q_lo, k_lo = (g * R + r) * d, g * d
                blk = q_ref[0, qb * BLOCK:(qb + 1) * BLOCK, (q_lo // kw) * kw:(q_lo // kw + 1) * kw]
                if (q_lo - k_lo) % kw:
                    sh = (q_lo - k_lo) % kw
                    blk = jnp.concatenate([blk[:, sh:], blk[:, :sh]], axis=1)
                parts.append(jnp.where((lane >= k_lo) & (lane < k_lo + d), blk, jnp.zeros_like(blk)))
            q = jnp.concatenate(parts, axis=0)
            sink = jnp.concatenate([jnp.full((BLOCK, 1), sink_ref[g * R + r] * LOG2E, F32)
                                    for r in range(R)], axis=0)
            s_ctx = lax.dot_general(q, kc, nt, preferred_element_type=F32)
            s_loc = jnp.where(valid, lax.dot_general(q, kl, nt, preferred_element_type=F32), NEG_INF)
            m = jnp.maximum(jnp.maximum(jnp.max(s_ctx, axis=-1, keepdims=True),
                                        jnp.max(s_loc, axis=-1, keepdims=True)), sink)
            e_ctx = jnp.exp2(s_ctx - m)
            e_loc = jnp.exp2(s_loc - m)
            den = (jnp.sum(e_ctx, axis=-1, keepdims=True) + jnp.sum(e_loc, axis=-1, keepdims=True)
                   + jnp.exp2(sink - m))
            inv = 1.0 / den
            o = (jnp.dot((e_ctx * inv).astype(BF16), vc, preferred_element_type=F32)
                 + jnp.dot((e_loc * inv).astype(BF16), vl, preferred_element_type=F32))
            heads += [o[r * BLOCK:(r + 1) * BLOCK, g * d:(g + 1) * d] for r in range(R)]
        o_ref[0, qb * BLOCK:(qb + 1) * BLOCK, :] = jnp.concatenate(heads, axis=1)


def window_attention(qsrc, kpad, vpad, sink, G, R, C, L):
    B, _, _ = qsrc.shape
    Lp, kw = kpad.shape[1], kpad.shape[2]
    d = kw // G
    QB = WIN_QUERY_BLOCKS
    kern = functools.partial(_window_attn_kernel, R=R, G=G, C=C, L=L, QB=QB)
    return pl.pallas_call(
        kern,
        grid=(B, L // (BLOCK * QB)),
        in_specs=[pl.BlockSpec(memory_space=pltpu.SMEM),
                  pl.BlockSpec((1, QB * BLOCK, G * R * d), lambda b, i: (b, i, 0)),
                  pl.BlockSpec((1, Lp, kw), lambda b, i: (b, 0, 0)),
                  pl.BlockSpec((1, Lp, kw), lambda b, i: (b, 0, 0))],
        out_specs=pl.BlockSpec((1, QB * BLOCK, G * R * d), lambda b, i: (b, i, 0)),
        out_shape=jax.ShapeDtypeStruct((B, L, G * R * d), F32),
        compiler_params=_params("parallel", "arbitrary"),
        name="window_attn",
    )(sink.astype(F32), qsrc, kpad, vpad)


_CAND_PAIRS = tuple((r, s) for r in range(PEER_TOPK) for s in range(PEER_TOPK)
                    if (r + 1) * (s + 1) <= PEER_TOPK)
_CAND_ROWS = -(-len(_CAND_PAIRS) // 8) * 8


def _top_rows(vs, n):
    vs = list(vs)
    iota = lax.broadcasted_iota(jnp.int32, vs[0].shape, 0).astype(F32)
    rows = [[] for _ in vs]
    for r in range(n):
        for i, v in enumerate(vs):
            m = jnp.max(v, axis=0, keepdims=True)
            rows[i].append(m)
            if r + 1 < n:
                first = jnp.min(jnp.where(v == m, iota, float(v.shape[0])), axis=0, keepdims=True)
                vs[i] = jnp.where(iota == first, -jnp.inf, v)
    return rows


def _peer_select_kernel(s_ref, ea_ref, q_ref, eb_ref, code_ref, c_ref, *, tm):
    n_lane = tm // 128

    def body(h, carry):
        tiles = [slice(t * 128, (t + 1) * 128) for t in range(n_lane)]
        ab = [s_ref[h, p, :, lanes] for lanes in tiles for p in (0, 1)]
        tops = _top_rows(ab, PEER_TOPK)
        for t, lanes in enumerate(tiles):
            a, b, ta, tb = ab[2 * t], ab[2 * t + 1], tops[2 * t], tops[2 * t + 1]
            c_ref[...] = jnp.full(c_ref.shape, -jnp.inf, F32)
            for k, (r, s) in enumerate(_CAND_PAIRS):
                c_ref[k:k + 1, :] = ta[r] + tb[s]
            best, = _top_rows((c_ref[...],), PEER_TOPK)
            tau = best[PEER_TOPK - 1]
            zsum = jnp.zeros_like(best[0])
            for bk in best:
                zsum = zsum + jnp.exp(bk - best[0])
            code_b = jnp.zeros(b.shape, F32)
            for s in range(PEER_TOPK):
                code_b = code_b + jnp.where(tb[s] > b, 1.0, 0.0)
            code_t = [jnp.zeros(tau.shape, F32)]
            for s in range(1, PEER_TOPK):
                code_t.append(jnp.where(tb[s] == tb[s - 1], code_t[s - 1], float(s)))
            q = jnp.full(a.shape, -1.0, F32)
            for r in reversed(range(PEER_TOPK)):
                q_r = jnp.full(tau.shape, -1.0, F32)
                for s in range(PEER_TOPK // (r + 1)):
                    q_r = jnp.maximum(q_r, jnp.where(ta[r] + tb[s] >= tau, code_t[s], -1.0))
                q = jnp.where(a == ta[r], q_r, q)
            q_ref[h, :, lanes] = q
            code_ref[h, :, lanes] = code_b.astype(BF16)
            ea_ref[h, :, lanes] = jnp.exp(a - ta[0]) / zsum
            eb_ref[h, :, lanes] = jnp.exp(b - tb[0]).astype(BF16)
        return carry

    lax.fori_loop(0, PEER_HEADS, body, 0)


def peer_select(st, tm):
    H, _, n, T = st.shape
    kern = functools.partial(_peer_select_kernel, tm=tm)
    return pl.pallas_call(
        kern,
        grid=(T // tm,),
        in_specs=[pl.BlockSpec((H, 2, n, tm), lambda t: (0, 0, 0, t))],
        out_specs=[pl.BlockSpec((H, n, tm), lambda t: (0, 0, t))] * 4,
        out_shape=[jax.ShapeDtypeStruct((H, n, T), dt) for dt in (F32, F32, BF16, BF16)],
        scratch_shapes=[pltpu.VMEM((_CAND_ROWS, 128), F32)],
        compiler_params=_params("parallel"),
        name="peer_select",
    )(st)


def _gelu_tanh(x):
    k = -2.0 * math.sqrt(2.0 / math.pi) * LOG2E
    return x / (1.0 + jnp.exp2(x * (k + (k * 0.044715) * (x * x))))


def _peer_kernel(x_ref, u_ref, vt_ref, ea_ref, q_ref, eb_ref, code_ref, res_ref, gate_ref, o_ref,
                 acc_ref, act_ref, g_ref, *, ni, tm):
    c = pl.program_id(1)

    @pl.when(c == 0)
    def _():
        acc_ref[...] = jnp.zeros_like(acc_ref)

    act_ref[...] = _gelu_tanh(lax.dot_general(u_ref[...], x_ref[...], (((1,), (1,)), ((), ())),
                                              preferred_element_type=F32))

    for ts in range(tm // 128):
        lanes = slice(ts * 128, (ts + 1) * 128)
        for ii in range(ni):
            rows = slice(ii * N_KEYS, (ii + 1) * N_KEYS)
            w = jnp.zeros((N_KEYS, 128), BF16)
            for h in range(PEER_HEADS):
                sel = code_ref[h, :, lanes] <= q_ref[h, ii:ii + 1, lanes].astype(BF16)
                gate = ea_ref[h, ii:ii + 1, lanes].astype(BF16) * eb_ref[h, :, lanes]
                w = w + jnp.where(sel, gate, jnp.zeros_like(gate))
            g_ref[rows, lanes] = w * act_ref[rows, lanes].astype(BF16)

    acc_ref[...] += jnp.dot(vt_ref[...], g_ref[...], preferred_element_type=F32)

    @pl.when(c == pl.num_programs(1) - 1)
    def _():
        o_ref[...] = res_ref[...] + gate_ref[0] * acc_ref[...].T


def peer_dense(x, u, vt, eat, qt, ebt, codet, res, gate, rows_per_seg, tm, ni):
    T, D = x.shape
    E = u.shape[0]
    ec = ni * N_KEYS
    kern = functools.partial(_peer_kernel, ni=ni, tm=tm)
    nseg = gate.shape[0]
    blocks_per_seg = rows_per_seg // tm
    rows_of_chunk = pl.BlockSpec((PEER_HEADS, ni, tm), lambda t, c: (0, c, t))
    return pl.pallas_call(
        kern,
        grid=(T // tm, E // ec),
        in_specs=[pl.BlockSpec((tm, D), lambda t, c: (t, 0)),
                  pl.BlockSpec((ec, D), lambda t, c: (c, 0)),
                  pl.BlockSpec((D, ec), lambda t, c: (0, c)),
                  rows_of_chunk,
                  rows_of_chunk,
                  pl.BlockSpec((PEER_HEADS, N_KEYS, tm), lambda t, c: (0, 0, t)),
                  pl.BlockSpec((PEER_HEADS, N_KEYS, tm), lambda t, c: (0, 0, t)),
                  pl.BlockSpec((tm, D), lambda t, c: (t, 0)),
                  pl.BlockSpec((1, 1, D), lambda t, c: (jnp.minimum(t // blocks_per_seg, nseg - 1), 0, 0))],
        out_specs=pl.BlockSpec((tm, D), lambda t, c: (t, 0)),
        out_shape=jax.ShapeDtypeStruct((T, D), F32),
        scratch_shapes=[pltpu.VMEM((D, tm), F32),
                        pltpu.VMEM((ec, tm), F32),
                        pltpu.VMEM((ec, tm), BF16)],
        compiler_params=_params("parallel", "arbitrary"),
        name="peer_dense",
    )(x, u, vt, eat, qt, ebt, codet, res, gate.reshape(nseg, 1, D))


def peer(qp, h_bf, keys, u_bf, vt_bf, res, gate, rows_per_seg, tm):
    T = qp.shape[0]
    dk = keys.shape[-1]
    kflat = keys.reshape(PEER_HEADS * 2, N_KEYS, dk).astype(BF16)
    st = keys_times_qT(kflat, qp, tm).reshape(PEER_HEADS, 2, N_KEYS, T)
    eat, qt, ebt, codet = peer_select(st, tm)
    return peer_dense(h_bf, u_bf, vt_bf, eat, qt, ebt, codet, res, gate, rows_per_seg, tm, 8)


def rmsnorm(x, g):
    return x * lax.rsqrt(jnp.mean(x * x, axis=-1, keepdims=True) + EPS) * g


def axial_rope(n_tok, dim):
    rows = n_tok // GRID_W
    row = jnp.repeat(jnp.arange(rows, dtype=F32), GRID_W)
    col = jnp.tile(jnp.arange(GRID_W, dtype=F32), rows)
    axis_dim = dim // 2
    inv_freq = ROPE_THETA ** (-jnp.arange(0, axis_dim, 2, dtype=F32) / axis_dim)
    ang = jnp.concatenate([row[:, None] * inv_freq, col[:, None] * inv_freq], axis=-1)
    return jnp.cos(ang), jnp.sin(ang)


def apply_rope(x, cos, sin):
    shape = (1, x.shape[1]) + (1,) * (x.ndim - 3) + (cos.shape[-1],)
    c, s = cos.reshape(shape), sin.reshape(shape)
    x1, x2 = jnp.split(x, 2, axis=-1)
    return jnp.concatenate([x1 * c - x2 * s, x1 * s + x2 * c], axis=-1)


def short_conv(u, w, b):
    up = jnp.pad(u, ((0, 0), (1, 1), (0, 0)))
    return up[:, :-2] * w[0] + up[:, 1:-1] * w[1] + up[:, 2:] * w[2] + b


def hyena_filters(n_tok, w1, b1, w2, b2, w3, b3, freq):
    hp = lax.Precision.HIGHEST
    t = jnp.linspace(0.0, 1.0, n_tok, dtype=F32)[:, None]
    w = (2.0 * math.pi / n_tok) * jnp.arange(n_tok, dtype=F32)[:, None]
    f = jnp.linspace(1e-4, HY_BANDS - 1, HY_BANDS, dtype=F32)[None, :]
    feat = jnp.concatenate([t, jnp.cos(f * w), -jnp.sin(f * w)], axis=-1)
    hdn = jnp.sin(freq * (jnp.dot(feat, w1, precision=hp) + b1))
    hdn = jnp.sin(freq * (jnp.dot(hdn, w2, precision=hp) + b2))
    filt = (jnp.dot(hdn, w3, precision=hp) + b3).reshape(n_tok, HY_ORDER, 2, HY_CH)
    deltas = jnp.abs(jnp.linspace(math.log(HY_TARGET) / HY_SLOW, math.log(HY_TARGET) / HY_FAST, HY_CH,
                                  dtype=F32))
    filt = filt * jnp.exp(-t[:, :, None, None] * deltas)
    fwd, bwd = filt[:, :, 0], filt[:, :, 1]
    kfull = jnp.concatenate([fwd, jnp.zeros_like(fwd[:1]), bwd[1:][::-1]], axis=0)
    return kfull / jnp.sum(jnp.abs(kfull), axis=0, keepdims=True)


def hyena_filters_t(n_tok, w1, b1, w2, b2, w3, b3, freq):
    hp = lax.Precision.HIGHEST
    t = jnp.linspace(0.0, 1.0, n_tok, dtype=F32)[:, None]
    w = (2.0 * math.pi / n_tok) * jnp.arange(n_tok, dtype=F32)[:, None]
    f = jnp.linspace(1e-4, HY_BANDS - 1, HY_BANDS, dtype=F32)[None, :]
    feat = jnp.concatenate([t, jnp.cos(f * w), -jnp.sin(f * w)], axis=-1)
    hdn = jnp.sin(freq * (jnp.dot(feat, w1, precision=hp) + b1))
    hdn = jnp.sin(freq * (jnp.dot(hdn, w2, precision=hp) + b2))
    deltas = jnp.abs(jnp.linspace(math.log(HY_TARGET) / HY_SLOW, math.log(HY_TARGET) / HY_FAST, HY_CH,
                                  dtype=F32))
    w3t = jnp.transpose(w3.reshape(-1, HY_ORDER, 2, HY_CH), (2, 1, 3, 0))
    b3t = jnp.transpose(b3.reshape(HY_ORDER, 2, HY_CH), (1, 0, 2))[..., None]

    def half(d, hidden, times):
        decay = jnp.exp(-deltas[:, None] * times[None, :])
        return (jnp.einsum('ock,nk->ocn', w3t[d], hidden, precision=hp) + b3t[d]) * decay

    fwd = half(0, hdn, t[:, 0])
    bwd_rev = half(1, hdn[::-1], t[::-1, 0])
    kfull = jnp.concatenate([fwd, jnp.zeros_like(fwd[..., :1]), bwd_rev[..., :n_tok - 1]], axis=-1)
    kfull = kfull / jnp.sum(jnp.abs(kfull), axis=-1, keepdims=True)
    return kfull.reshape(HY_ORDER * HY_CH, 2 * n_tok)


def long_conv(z, kf):
    n = z.shape[1]
    zf = jnp.fft.rfft(z, n=2 * n, axis=1)
    hf = jnp.fft.rfft(kf, n=2 * n, axis=0)
    return jnp.fft.irfft(zf * hf[None], n=2 * n, axis=1)[:, :n]


def hyena_small(u, conv_w, conv_b, filter_params, bias):
    n = u.shape[1]
    u = short_conv(u, conv_w, conv_b)
    v, x1, x2 = jnp.split(u, 3, axis=-1)
    kfull = hyena_filters(n, *filter_params)
    z = x1 * (long_conv(v, kfull[:, 0]) + bias[0] * v)
    return x2 * (long_conv(z, kfull[:, 1]) + bias[1] * z)


def _hi_lo(x):
    hi = x.astype(BF16)
    return hi, (x - hi.astype(F32)).astype(BF16)


def _dot3(a, b):
    d = lambda x, y: jnp.dot(x, y, preferred_element_type=F32)
    return d(a[0], b[0]) + d(a[1], b[0]) + d(a[0], b[1])


def _dft_constants(n_tok):
    n = 2 * n_tok
    n1 = n // 128
    a1 = 2.0 * np.pi * np.outer(np.arange(n1), np.arange(n1)) / n1
    a2 = 2.0 * np.pi * np.outer(np.arange(128), np.arange(128)) / 128
    at = 2.0 * np.pi * np.outer(np.arange(n1), np.arange(128)) / n
    c1, s1, c2, s2 = np.cos(a1), np.sin(a1), np.cos(a2), np.sin(a2)
    pair = lambda m: _hi_lo(jnp.asarray(m, F32))
    return dict(
        f1_half=pair(np.concatenate([c1[:, :n1 // 2], -s1[:, :n1 // 2]], axis=0)),
        f1_full=pair(np.concatenate([c1, -s1], axis=0)),
        m_fwd=pair(np.block([[c2, -s2], [s2, c2]])),
        m_inv=pair(np.block([[c2, s2], [-s2, c2]])),
        g_half=pair(np.concatenate([c1[:n1 // 2], -s1[:n1 // 2]], axis=1)),
        tr=jnp.asarray(np.cos(at), F32), ti=jnp.asarray(-np.sin(at), F32))


def _dft_fwd(seqs, f1, tr, ti, m_fwd):
    n1 = tr.shape[0]
    y = _dot3(f1, _hi_lo(jnp.concatenate(seqs, axis=1)))
    rows = []
    for k in range(len(seqs)):
        yr, yi = y[:n1, k * 128:(k + 1) * 128], y[n1:, k * 128:(k + 1) * 128]
        rows.append(jnp.concatenate([yr * tr - yi * ti, yr * ti + yi * tr], axis=1))
    return _dot3(_hi_lo(jnp.concatenate(rows, axis=0)), m_fwd)


def _dft_inv_half(p, g_half, tr, ti, m_inv):
    n1 = tr.shape[0]
    u = _dot3(_hi_lo(p), m_inv)
    cols = []
    for k in range(p.shape[0] // n1):
        ur, ui = u[k * n1:(k + 1) * n1, :128], u[k * n1:(k + 1) * n1, 128:]
        cols.append(jnp.concatenate([ur * tr + ui * ti, ui * tr - ur * ti], axis=0))
    return _dot3(g_half, _hi_lo(jnp.concatenate(cols, axis=1))) * (1.0 / (n1 * 128))


def _pairs(refs):
    return (refs[0][...], refs[1][...])


def _spectrum_kernel(a_ref, f1h, f1l, tr_ref, ti_ref, mh, ml, o_ref):
    cb, n1 = a_ref.shape[0], tr_ref.shape[0]
    x = _dft_fwd([a_ref[k] for k in range(cb)], _pairs((f1h, f1l)), tr_ref[...], ti_ref[...],
                 _pairs((mh, ml)))
    o_ref[...] = x.reshape(cb, n1, 256)


def filter_spectrum(kf, consts, cb):
    items, n1, _ = kf.shape
    full = lambda shape: pl.BlockSpec(shape, lambda i: (0,) * len(shape))
    f1, m = consts["f1_full"], consts["m_fwd"]
    return pl.pallas_call(
        _spectrum_kernel,
        grid=(items // cb,),
        in_specs=[pl.BlockSpec((cb, n1, 128), lambda i: (i, 0, 0)),
                  full(f1[0].shape), full(f1[1].shape), full((n1, 128)), full((n1, 128)),
                  full(m[0].shape), full(m[1].shape)],
        out_specs=pl.BlockSpec((cb, n1, 256), lambda i: (i, 0, 0)),
        out_shape=jax.ShapeDtypeStruct((items, n1, 256), F32),
        compiler_params=_params("parallel"),
        name="filter_spectrum",
    )(kf, f1[0], f1[1], consts["tr"], consts["ti"], m[0], m[1])


def _conv_gate_kernel(bias_ref, u_ref, g_ref, h_ref, f1h, f1l, tr_ref, ti_ref, mfh, mfl, mih, mil, gh, gl,
                      o_ref, *, cb):
    tr, ti = tr_ref[...], ti_ref[...]
    n1 = tr.shape[0]
    c0 = pl.program_id(1) * cb
    u = [u_ref[0, k] for k in range(cb)]
    x = _dft_fwd(u, _pairs((f1h, f1l)), tr, ti, _pairs((mfh, mfl)))
    prod = []
    for k in range(cb):
        xr, xi = x[k * n1:(k + 1) * n1, :128], x[k * n1:(k + 1) * n1, 128:]
        hr, hi = h_ref[k, :, :128], h_ref[k, :, 128:]
        prod.append(jnp.concatenate([xr * hr - xi * hi, xr * hi + xi * hr], axis=1))
    y = _dft_inv_half(jnp.concatenate(prod, axis=0), _pairs((gh, gl)), tr, ti, _pairs((mih, mil)))
    for k in range(cb):
        o_ref[0, k] = g_ref[0, k] * (y[:, k * 128:(k + 1) * 128] + bias_ref[c0 + k] * u[k])


def conv_gate(u, gate, spec, bias, consts, cb):
    B, C, half, _ = u.shape
    n1 = 2 * half
    full = lambda shape: pl.BlockSpec(shape, lambda b, c: (0,) * len(shape))
    seq = pl.BlockSpec((1, cb, half, 128), lambda b, c: (b, c, 0, 0))
    mats = [*consts["f1_half"], consts["tr"], consts["ti"], *consts["m_fwd"], *consts["m_inv"],
            *consts["g_half"]]
    return pl.pallas_call(
        functools.partial(_conv_gate_kernel, cb=cb),
        grid=(B, C // cb),
        in_specs=[pl.BlockSpec(memory_space=pltpu.SMEM), seq, seq,
                  pl.BlockSpec((cb, n1, 256), lambda b, c: (c, 0, 0))] + [full(m.shape) for m in mats],
        out_specs=seq,
        out_shape=jax.ShapeDtypeStruct(u.shape, F32),
        compiler_params=_params("parallel", "arbitrary"),
        name="conv_gate",
    )(bias.astype(F32), u, gate, spec, *mats)


def hyena(u, conv_w, conv_b, filter_params, bias):
    B, n, _ = u.shape
    half = n // 128
    consts = _dft_constants(n)
    u = short_conv(u, conv_w, conv_b)
    seqs = jnp.transpose(u, (0, 2, 1)).reshape(B, 3, HY_CH, half, 128)
    v, x1, x2 = seqs[:, 0], seqs[:, 1], seqs[:, 2]
    kf = hyena_filters_t(n, *filter_params).reshape(HY_ORDER * HY_CH, 2 * half, 128)
    spec = filter_spectrum(kf, consts, HY_CH_BLOCK).reshape(HY_ORDER, HY_CH, 2 * half, 256)
    z = conv_gate(v, x1, spec[0], bias[0], consts, HY_CH_BLOCK)
    o = conv_gate(z, x2, spec[1], bias[1], consts, HY_CH_BLOCK)
    return jnp.transpose(o.reshape(B, HY_CH, n), (0, 2, 1))


def _head_major_t(t):
    B, L, H, d = t.shape
    return jnp.transpose(t, (0, 2, 3, 1)).reshape(B * H, d, L)


def _key_chunk(n_keys):
    for c in range(KEY_CHUNK_CAP, 0, -128):
        if n_keys % c == 0:
            return c
    raise ValueError(f"no key chunk for {n_keys}")


def kernel(x, c, ctx, c_ctx, w_mod, b_mod, g_norm_mix, w_in, g_qk_diff, lambda_diff, g_qk_win, sink_win, g_qk_glob, hy_conv_w, hy_conv_b, hy_w1, hy_b1, hy_w2, hy_b2, hy_w3, hy_b3, hy_freq, hy_bias, g_mix_out, w_out, g_norm_ffn, peer_wq, peer_keys, peer_u, peer_v):
    B, L, D = x.shape
    C = ctx.shape[1]
    depth = w_mod.shape[0]
    TM = 512
    n_lat = B * L
    n_ctx = B * C
    assert L % TM == 0 and n_ctx % TM == 0
    nseg = B + 1

    def rope_rows(dim):
        cos, sin = axial_rope(L, dim)
        return (jnp.concatenate([jnp.tile(cos, (B, 1)), jnp.ones((n_ctx, dim // 2), F32)], axis=0),
                jnp.concatenate([jnp.tile(sin, (B, 1)), jnp.zeros((n_ctx, dim // 2), F32)], axis=0))

    rope_rows_half = rope_rows(DIFF_QK_DIM)
    rope_rows_full = rope_rows(HEAD_DIM)
    sc = jnp.concatenate([jax.nn.silu(c), jax.nn.silu(c_ctx)[None]], axis=0)
    sc = jnp.pad(sc, ((0, 8 - nseg), (0, 0))).astype(BF16)

    xall = jnp.concatenate([x.reshape(n_lat, D), ctx.reshape(n_ctx, D)], axis=0)

    for i in range(depth):
        want_ctx = i < depth - 1
        lambda_init = 0.8 - 0.6 * math.exp(-0.3 * i)
        mod = mm(sc, w_mod[i].astype(BF16), 8, 1024)[:nseg] + b_mod[i]
        sh1, s1, g1, sh2, s2, g2 = jnp.split(mod, N_MOD, axis=-1)

        groups = normmod_mm(xall, g_norm_mix[i], sh1, s1, w_in[i].astype(BF16), L, TM, F32, False,
                            widths=(W_DIFF, W_WIN, W_GLOB, W_HY))
        ga_, gw_, gg_, gh_ = groups
        ph, phc = gh_[:n_lat].reshape(B, L, -1), gh_[n_lat:].reshape(B, C, -1)

        def gains(g_qk, n_q, n_k, d):
            return jnp.concatenate([jnp.tile(g_qk[0], n_q) * (d ** -0.5 * LOG2E), jnp.tile(g_qk[1], n_k)])

        def split_rows(t, *shape):
            return t[:n_lat].reshape(B, L, *shape), t[n_lat:].reshape(B, C, *shape)

        def rows_of(t):
            lat, cx = t[:n_lat].reshape(B, L, -1), t[n_lat:].reshape(B, C, -1)
            return lat, cx, jnp.concatenate([cx, lat], axis=1)

        qka, qkac, qka_keys = rows_of(head_prep(ga_, 2 * GROUP_W, DIFF_QK_DIM,
                                                gains(g_qk_diff[i], 2 * DIFF_HEADS, 2 * DIFF_HEADS, DIFF_QK_DIM),
                                                *rope_rows_half, TM))
        va, vac = split_rows(ga_[:, 2 * GROUP_W:].astype(BF16), DIFF_HEADS, HEAD_DIM)
        lam_vec = lambda_diff[i]
        lam = (jnp.exp(jnp.sum(lam_vec[0] * lam_vec[1])) - jnp.exp(jnp.sum(lam_vec[2] * lam_vec[3]))
               + lambda_init)
        coefs = jnp.stack([jnp.ones((), F32), -lam])
        vva = jnp.concatenate([vac, va], axis=1)
        oa = dense_attention(qka, qka_keys, GROUP_W, _head_major_t(vva), coefs, None,
                             2, DIFF_HEADS, 1, DIFF_QK_DIM, ATTN_TQ_DIFF, _key_chunk(C + L))

        def gqa_parts(g_, g_qk, n_q, n_kv):
            wq, wk = n_q * HEAD_DIM, n_kv * HEAD_DIM
            qk = head_prep(g_, wq + wk, HEAD_DIM, gains(g_qk, n_q, n_kv, HEAD_DIM), *rope_rows_full, TM)
            v, vc = split_rows(g_[:, wq + wk:].astype(BF16), n_kv, HEAD_DIM)
            return rows_of(qk) + (v, vc)

        WQ = WIN_HEADS * HEAD_DIM
        WK = WIN_KV_HEADS * HEAD_DIM
        qkw, qkwc, _, vw, vwc = gqa_parts(gw_, g_qk_win[i], WIN_HEADS, WIN_KV_HEADS)
        zblk = jnp.zeros((B, BLOCK, WK), BF16)
        kpad = jnp.concatenate([qkwc[..., WQ:], zblk, qkw[..., WQ:], zblk], axis=1)
        vpad = jnp.concatenate([vwc.reshape(B, C, WK), zblk, vw.reshape(B, L, WK), zblk], axis=1)
        Rw = WIN_HEADS // WIN_KV_HEADS
        ob = window_attention(qkw, kpad, vpad, sink_win[i], WIN_KV_HEADS, Rw, C, L)

        qkg, qkgc, qkg_keys, vg, vgc = gqa_parts(gg_, g_qk_glob[i], GLOB_HEADS, GLOB_KV_HEADS)
        Rg = GLOB_HEADS // GLOB_KV_HEADS
        one = jnp.ones((1,), F32)
        vvg = _head_major_t(jnp.concatenate([vgc, vg], axis=1))
        og = dense_attention(qkg, qkg_keys, GLOB_HEADS * HEAD_DIM, vvg, one, None,
                             1, GLOB_KV_HEADS, Rg, HEAD_DIM, ATTN_TQ_GLOB, _key_chunk(C + L))

        filt = (hy_w1[i], hy_b1[i], hy_w2[i], hy_b2[i], hy_w3[i], hy_b3[i], hy_freq[i])
        oh = hyena(ph, hy_conv_w[i], hy_conv_b[i], filt, hy_bias[i])

        mixed = [t.reshape(n_lat, -1) for t in (oa, ob, og, oh)]
        if want_ctx:
            oac = dense_attention(qkac, qkac, GROUP_W, _head_major_t(vac), coefs, None,
                                  2, DIFF_HEADS, 1, DIFF_QK_DIM, C, C)
            obc = dense_attention(qkwc, qkwc, WQ, _head_major_t(vwc), one, sink_win[i],
                                  1, WIN_KV_HEADS, Rw, HEAD_DIM, C, C)
            ogc = dense_attention(qkgc, qkgc, GLOB_HEADS * HEAD_DIM, _head_major_t(vgc), one, None,
                                  1, GLOB_KV_HEADS, Rg, HEAD_DIM, C, C)
            ohc = hyena_small(phc, hy_conv_w[i], hy_conv_b[i], filt, hy_bias[i])
            mixed = [jnp.concatenate([m_, t.reshape(n_ctx, -1)], axis=0)
                     for m_, t in zip(mixed, (oac, obc, ogc, ohc))]

        head_scale = jnp.where(jnp.arange(N_OUT_HEADS) < DIFF_HEADS, 1.0 - lambda_init, 1.0)
        gain_out = (g_mix_out[i].reshape(N_OUT_HEADS, HEAD_DIM) * head_scale[:, None]).reshape(D)
        xcur = merge_residual(mixed, HEAD_DIM, gain_out, w_out[i].astype(BF16), xall, g1, L, TM)

        qp, h2 = normmod_mm(xcur, g_norm_ffn[i], sh2, s2, peer_wq[i].astype(BF16), L, TM, BF16, True)
        xall = peer(qp, h2, peer_keys[i], peer_u[i].astype(BF16), jnp.transpose(peer_v[i].astype(BF16)),
                    xcur, g2, L, TM)

    return xall[:n_lat].reshape(B, L, D)
```

```python
import functools
import math

import jax
import jax.numpy as jnp
import numpy as np
from jax import lax
from jax.experimental import pallas as pl
from jax.experimental.pallas import tpu as pltpu

F32 = jnp.float32
BF16 = jnp.bfloat16

GRID_W = 64
HEAD_DIM = 64
BLOCK = 128
WINDOW = 128
ROPE_THETA = 10000.0
EPS = 1e-6
NEG_INF = -1e30
N_MOD = 6
DIFF_HEADS = 4
DIFF_QK_DIM = 32
WIN_HEADS = 4
WIN_KV_HEADS = 2
GLOB_HEADS = 4
GLOB_KV_HEADS = 2
N_OUT_HEADS = 16
HY_CH = 256
HY_ORDER = 2
HY_BANDS = 16
HY_TARGET = 1e-2
HY_FAST = 0.3
HY_SLOW = 1.5
W_DIFF = 768
W_WIN = 512
W_GLOB = 512
W_HY = 768
GROUP_W = 256
PEER_HEADS = 8
N_KEYS = 128
PEER_TOPK = 16

VMEM_LIMIT = 56 * 1024 * 1024
ATTN_TQ_DIFF = 512
ATTN_TQ_GLOB = 256
KEY_CHUNK_CAP = 1408
HY_CH_BLOCK = 16
WIN_QUERY_BLOCKS = 4


def _params(*sem):
    return pltpu.CompilerParams(dimension_semantics=sem, vmem_limit_bytes=VMEM_LIMIT)


def _mm_kernel(a_ref, b_ref, o_ref):
    o_ref[...] = jnp.dot(a_ref[...], b_ref[...], preferred_element_type=F32)


def mm(a, b, tm, tn):
    M, K = a.shape
    N = b.shape[1]
    return pl.pallas_call(
        _mm_kernel,
        grid=(M // tm, N // tn),
        in_specs=[pl.BlockSpec((tm, K), lambda i, j: (i, 0)),
                  pl.BlockSpec((K, tn), lambda i, j: (0, j))],
        out_specs=pl.BlockSpec((tm, tn), lambda i, j: (i, j)),
        out_shape=jax.ShapeDtypeStruct((M, N), F32),
        compiler_params=_params("parallel", "arbitrary"),
        name="mm",
    )(a, b)


def _seg_spec(rows_per_seg, tm, nseg, D):
    blocks_per_seg = rows_per_seg // tm
    return pl.BlockSpec((1, 1, D), lambda i: (jnp.minimum(i // blocks_per_seg, nseg - 1), 0, 0))


def _normmod_mm_kernel(x_ref, g_ref, sh_ref, sc_ref, w_ref, *outs, widths, with_h):
    x = x_ref[...]
    y = x * lax.rsqrt(jnp.mean(x * x, axis=-1, keepdims=True) + EPS) * g_ref[...]
    h = (y * (1.0 + sc_ref[0]) + sh_ref[0]).astype(BF16)
    if with_h:
        outs[-1][...] = h
    r = jnp.dot(h, w_ref[...], preferred_element_type=F32)
    lo = 0
    for o_ref, n in zip(outs, widths):
        o_ref[...] = r[:, lo:lo + n].astype(o_ref.dtype)
        lo += n


def normmod_mm(x, g, shift, scale, w, rows_per_seg, tm, out_dtype, with_h, widths=None):
    M, D = x.shape
    N = w.shape[1]
    widths = (N,) if widths is None else tuple(widths)
    nseg = shift.shape[0]
    seg = _seg_spec(rows_per_seg, tm, nseg, D)
    row_block = lambda n: pl.BlockSpec((tm, n), lambda i: (i, 0))
    out_specs = [row_block(n) for n in widths]
    out_shape = [jax.ShapeDtypeStruct((M, n), out_dtype) for n in widths]
    if with_h:
        out_specs.append(row_block(D))
        out_shape.append(jax.ShapeDtypeStruct((M, D), BF16))
    return pl.pallas_call(
        functools.partial(_normmod_mm_kernel, widths=widths, with_h=with_h),
        grid=(M // tm,),
        in_specs=[row_block(D), pl.BlockSpec((1, D), lambda i: (0, 0)), seg, seg,
                  pl.BlockSpec((D, N), lambda i: (0, 0))],
        out_specs=out_specs,
        out_shape=out_shape,
        compiler_params=_params("parallel"),
        name="normmod_mm",
    )(x, g.reshape(1, D), shift.reshape(nseg, 1, D), scale.reshape(nseg, 1, D), w)


def _merge_residual_kernel(*refs, n_in, d):
    in_refs, (gain_ref, bd_ref, w_ref, x_ref, gate_ref, o_ref) = refs[:n_in], refs[n_in:]
    bd = bd_ref[...]
    parts, lo = [], 0
    for r in in_refs:
        o = r[...]
        ms = sum(jnp.dot(p, bd, preferred_element_type=F32) for p in _hi_lo(o * o))
        parts.append((o * lax.rsqrt(ms + EPS) * gain_ref[:, lo:lo + o.shape[1]]).astype(BF16))
        lo += o.shape[1]
    a = jnp.concatenate(parts, axis=1)
    o_ref[...] = x_ref[...] + gate_ref[0] * jnp.dot(a, w_ref[...], preferred_element_type=F32)


def merge_residual(outs, d, gain, w, x, gate, rows_per_seg, tm):
    M, width = outs[0].shape
    K, N = w.shape
    nseg = gate.shape[0]
    col = np.arange(width)
    bd = jnp.asarray((col[:, None] // d == col[None, :] // d) / d, BF16)
    row_block = lambda n: pl.BlockSpec((tm, n), lambda i: (i, 0))
    return pl.pallas_call(
        functools.partial(_merge_residual_kernel, n_in=len(outs), d=d),
        grid=(M // tm,),
        in_specs=[row_block(width)] * len(outs) + [
            pl.BlockSpec((1, K), lambda i: (0, 0)), pl.BlockSpec((width, width), lambda i: (0, 0)),
            pl.BlockSpec((K, N), lambda i: (0, 0)), row_block(N), _seg_spec(rows_per_seg, tm, nseg, N)],
        out_specs=row_block(N),
        out_shape=jax.ShapeDtypeStruct((M, N), F32),
        compiler_params=_params("parallel"),
        name="merge_residual",
    )(*outs, gain.reshape(1, K).astype(F32), bd, w, x, gate.reshape(nseg, 1, N))


def _scores_kernel(k_ref, q_ref, o_ref):
    d = k_ref.shape[-1]
    for p in range(k_ref.shape[0]):
        o_ref[p] = lax.dot_general(k_ref[p], q_ref[:, p * d:(p + 1) * d], (((1,), (1,)), ((), ())),
                                   preferred_element_type=F32)


def keys_times_qT(keys, q, tn):
    P, n, d = keys.shape
    T = q.shape[0]
    return pl.pallas_call(
        _scores_kernel,
        grid=(T // tn,),
        in_specs=[pl.BlockSpec((P, n, d), lambda j: (0, 0, 0)),
                  pl.BlockSpec((tn, P * d), lambda j: (j, 0))],
        out_specs=pl.BlockSpec((P, n, tn), lambda j: (0, 0, j)),
        out_shape=jax.ShapeDtypeStruct((P, n, T), F32),
        compiler_params=_params("parallel"),
        name="peer_scores_t",
    )(keys, q)


def _head_prep_kernel(t_ref, cos_ref, sin_ref, gain_ref, bd_ref, pm_ref, ex_ref, o_ref):
    d3 = lambda parts, m: sum(jnp.dot(p, m, preferred_element_type=F32) for p in parts)

    def split3(v):
        hi, lo = _hi_lo(v)
        rest = v - hi.astype(F32) - lo.astype(F32)
        return hi, lo, rest.astype(BF16)

    x = t_ref[...]
    ms = d3(_hi_lo(x * x), bd_ref[...])
    y = x * lax.rsqrt(ms + EPS) * gain_ref[...]
    ex = ex_ref[...]
    cos = d3(split3(cos_ref[...]), ex)
    sin = d3(split3(sin_ref[...]), ex)
    o_ref[...] = (y * cos + d3(_hi_lo(y), pm_ref[...]) * sin).astype(BF16)


def head_prep(t, width, d, gain, cos_rows, sin_rows, tm):
    rows = t.shape[0]
    hd = d // 2
    col = np.arange(width)
    same_head = (col[:, None] // d) == (col[None, :] // d)
    bd = jnp.asarray(same_head / d, BF16)
    src = np.where((col % d) < hd, col + hd, col - hd)
    pm = np.zeros((width, width), np.float32)
    pm[src, col] = np.where((col % d) < hd, -1.0, 1.0)
    ex = np.zeros((hd, width), np.float32)
    ex[col % hd, col] = 1.0
    full = lambda a: pl.BlockSpec(a.shape, lambda i: (0,) * a.ndim)
    consts = (gain.reshape(1, width).astype(F32), bd, jnp.asarray(pm, BF16), jnp.asarray(ex, BF16))
    return pl.pallas_call(
        _head_prep_kernel,
        grid=(rows // tm,),
        in_specs=[pl.BlockSpec((tm, width), lambda i: (i, 0)),
                  pl.BlockSpec((tm, hd), lambda i: (i, 0)),
                  pl.BlockSpec((tm, hd), lambda i: (i, 0))] + [full(c) for c in consts],
        out_specs=pl.BlockSpec((tm, width), lambda i: (i, 0)),
        out_shape=jax.ShapeDtypeStruct((rows, width), BF16),
        compiler_params=_params("parallel"),
        name="head_prep",
    )(t, cos_rows, sin_rows, *consts)


LOG2E = math.log2(math.e)


def _dense_attn_kernel(coef_ref, sink_ref, q_ref, k_ref, vt_ref, o_ref, sa_ref, sb_ref, *, nbr, R, G,
                       HP, d, tq, tk, nk, has_sink):
    cols = R * tq
    dv = vt_ref.shape[1]
    kw = k_ref.shape[-1]
    g0 = (pl.program_id(0) % (G // HP)) * HP
    nt = (((1,), (1,)), ((), ()))
    bufs = (sa_ref, sb_ref)
    lane = lax.broadcasted_iota(jnp.int32, (1, kw), 1)
    heads = []
    for hp in range(HP):
        out = jnp.zeros((dv, cols), F32)
        for br in range(nbr):
            k_lo = (hp * nbr + br) * d
            parts = []
            for r in range(R):
                q_lo = ((hp * R + r) * nbr + br) * d
                blk = q_ref[0, :, (q_lo // kw) * kw:(q_lo // kw + 1) * kw]
                if (q_lo - k_lo) % kw:
                    sh = (q_lo - k_lo) % kw
                    blk = jnp.concatenate([blk[:, sh:], blk[:, :sh]], axis=1)
                parts.append(jnp.where((lane >= k_lo) & (lane < k_lo + d), blk, jnp.zeros_like(blk)))
            q = parts[0] if R == 1 else jnp.concatenate(parts, axis=0)
            if has_sink:
                m = jnp.concatenate([jnp.full((1, tq), sink_ref[(g0 + hp) * R + r] * LOG2E, F32)
                                     for r in range(R)], axis=1)
                l = jnp.ones((1, cols), F32)
            else:
                m = jnp.full((1, cols), NEG_INF, F32)
                l = jnp.zeros((1, cols), F32)
            acc = jnp.zeros((dv, cols), F32)
            bufs[0][...] = lax.dot_general(k_ref[0, 0:tk, :], q, nt, preferred_element_type=F32)
            for j in range(nk):
                if j + 1 < nk:
                    bufs[(j + 1) % 2][...] = lax.dot_general(k_ref[0, (j + 1) * tk:(j + 2) * tk, :], q,
                                                             nt, preferred_element_type=F32)
                s = bufs[j % 2][...]
                m_new = jnp.maximum(m, jnp.max(s, axis=0, keepdims=True))
                alpha = jnp.exp2(m - m_new)
                p = jnp.exp2(s - m_new)
                l = alpha * l + jnp.sum(p, axis=0, keepdims=True)
                acc = alpha * acc + jnp.dot(vt_ref[hp, :, j * tk:(j + 1) * tk], p.astype(BF16),
                                            preferred_element_type=F32)
                m = m_new
            out = out + coef_ref[br] * (acc / l)
        out_t = out.T
        heads += [out_t[r * tq:(r + 1) * tq] for r in range(R)]
    o_ref[0] = jnp.concatenate(heads, axis=1)


def dense_attention(qsrc, ksrc, k_lane0, vt, coefs, sink, nbr, G, R, d, tq, tk):
    B, Lq, _ = qsrc.shape
    Lk = ksrc.shape[1]
    dv = vt.shape[1]
    HP = 128 // (nbr * d)
    assert G % HP == 0 and (HP * R * dv) % 128 == 0 and k_lane0 % 128 == 0
    has_sink = sink is not None
    if sink is None:
        sink = jnp.zeros((1,), F32)
    kern = functools.partial(_dense_attn_kernel, nbr=nbr, R=R, G=G, HP=HP, d=d, tq=tq, tk=tk, nk=Lk // tk,
                             has_sink=has_sink)
    gp = G // HP
    qw = HP * R * nbr * d
    return pl.pallas_call(
        kern,
        grid=(B * gp, Lq // tq),
        in_specs=[pl.BlockSpec(memory_space=pltpu.SMEM),
                  pl.BlockSpec(memory_space=pltpu.SMEM),
                  pl.BlockSpec((1, tq, qw), lambda b, i: (b // gp, i, b % gp)),
                  pl.BlockSpec((1, Lk, 128), lambda b, i: (b // gp, 0, k_lane0 // 128 + b % gp)),
                  pl.BlockSpec((HP, dv, Lk), lambda b, i: (b, 0, 0))],
        out_specs=pl.BlockSpec((1, tq, HP * R * dv), lambda b, i: (b // gp, i, b % gp)),
        out_shape=jax.ShapeDtypeStruct((B, Lq, G * R * dv), F32),
        scratch_shapes=[pltpu.VMEM((tk, R * tq), F32), pltpu.VMEM((tk, R * tq), F32)],
        compiler_params=_params("parallel", "arbitrary"),
        name="dense_attn",
    )(coefs.astype(F32), sink.astype(F32), qsrc, ksrc, vt)


def _window_attn_kernel(sink_ref, q_ref, k_ref, v_ref, o_ref, *, R, G, C, L, QB):
    rows = R * BLOCK
    kw = k_ref.shape[-1]
    d = kw // G
    kc = k_ref[0, 0:C, :]
    vc = v_ref[0, 0:C, :]
    nt = (((1,), (1,)), ((), ()))
    qi = lax.broadcasted_iota(jnp.int32, (rows, 3 * BLOCK), 0) & (BLOCK - 1)
    kj = lax.broadcasted_iota(jnp.int32, (rows, 3 * BLOCK), 1)
    in_window = jnp.abs(kj - BLOCK - qi) <= WINDOW
    lane = lax.broadcasted_iota(jnp.int32, (1, kw), 1)
    for qb in range(QB):
        n = pl.program_id(1) * QB + qb
        start = pl.multiple_of(C + n * BLOCK, BLOCK)
        kl = k_ref[0, pl.ds(start, 3 * BLOCK), :]
        vl = v_ref[0, pl.ds(start, 3 * BLOCK), :]
        kpos = (n - 1) * BLOCK + kj
        valid = in_window & (kpos >= 0) & (kpos < L)
        heads = []
        for g in range(G):
            parts = []
            for r in range(R):
                q_lo, k_lo = (g * R + r) * d, g * d
                blk = q_ref[0, qb * BLOCK:(qb + 1) * BLOCK, (q_lo // kw) * kw:(q_lo // kw + 1) * kw]
                if (q_lo - k_lo) % kw:
                    sh = (q_lo - k_lo) % kw
                    blk = jnp.concatenate([blk[:, sh:], blk[:, :sh]], axis=1)
                parts.append(jnp.where((lane >= k_lo) & (lane < k_lo + d), blk, jnp.zeros_like(blk)))
            q = jnp.concatenate(parts, axis=0)
            sink = jnp.concatenate([jnp.full((BLOCK, 1), sink_ref[g * R + r] * LOG2E, F32)
                                    for r in range(R)], axis=0)
            s_ctx = lax.dot_general(q, kc, nt, preferred_element_type=F32)
            s_loc = jnp.where(valid, lax.dot_general(q, kl, nt, preferred_element_type=F32), NEG_INF)
            m = jnp.maximum(jnp.maximum(jnp.max(s_ctx, axis=-1, keepdims=True),
                                        jnp.max(s_loc, axis=-1, keepdims=True)), sink)
            e_ctx = jnp.exp2(s_ctx - m)
            e_loc = jnp.exp2(s_loc - m)
            den = (jnp.sum(e_ctx, axis=-1, keepdims=True) + jnp.sum(e_loc, axis=-1, keepdims=True)
                   + jnp.exp2(sink - m))
            inv = 1.0 / den
            o = (jnp.dot((e_ctx * inv).astype(BF16), vc, preferred_element_type=F32)
                 + jnp.dot((e_loc * inv).astype(BF16), vl, preferred_element_type=F32))
            heads += [o[r * BLOCK:(r + 1) * BLOCK, g * d:(g + 1) * d] for r in range(R)]
        o_ref[0, qb * BLOCK:(qb + 1) * BLOCK, :] = jnp.concatenate(heads, axis=1)


def window_attention(qsrc, kpad, vpad, sink, G, R, C, L):
    B, _, _ = qsrc.shape
    Lp, kw = kpad.shape[1], kpad.shape[2]
    d = kw // G
    QB = WIN_QUERY_BLOCKS
    kern = functools.partial(_window_attn_kernel, R=R, G=G, C=C, L=L, QB=QB)
    return pl.pallas_call(
        kern,
        grid=(B, L // (BLOCK * QB)),
        in_specs=[pl.BlockSpec(memory_space=pltpu.SMEM),
                  pl.BlockSpec((1, QB * BLOCK, G * R * d), lambda b, i: (b, i, 0)),
                  pl.BlockSpec((1, Lp, kw), lambda b, i: (b, 0, 0)),
                  pl.BlockSpec((1, Lp, kw), lambda b, i: (b, 0, 0))],
        out_specs=pl.BlockSpec((1, QB * BLOCK, G * R * d), lambda b, i: (b, i, 0)),
        out_shape=jax.ShapeDtypeStruct((B, L, G * R * d), F32),
        compiler_params=_params("parallel", "arbitrary"),
        name="window_attn",
    )(sink.astype(F32), qsrc, kpad, vpad)


_CAND_PAIRS = tuple((r, s) for r in range(PEER_TOPK) for s in range(PEER_TOPK)
                    if (r + 1) * (s + 1) <= PEER_TOPK)
_CAND_ROWS = -(-len(_CAND_PAIRS) // 8) * 8


def _top_rows(vs, n):
    vs = list(vs)
    iota = lax.broadcasted_iota(jnp.int32, vs[0].shape, 0).astype(F32)
    rows = [[] for _ in vs]
    for r in range(n):
        for i, v in enumerate(vs):
            m = jnp.max(v, axis=0, keepdims=True)
            rows[i].append(m)
            if r + 1 < n:
                first = jnp.min(jnp.where(v == m, iota, float(v.shape[0])), axis=0, keepdims=True)
                vs[i] = jnp.where(iota == first, -jnp.inf, v)
    return rows


def _peer_select_kernel(s_ref, ea_ref, q_ref, eb_ref, code_ref, c_ref, *, tm):
    n_lane = tm // 128

    def body(h, carry):
        tiles = [slice(t * 128, (t + 1) * 128) for t in range(n_lane)]
        ab = [s_ref[h, p, :, lanes] for lanes in tiles for p in (0, 1)]
        tops = _top_rows(ab, PEER_TOPK)
        for t, lanes in enumerate(tiles):
            a, b, ta, tb = ab[2 * t], ab[2 * t + 1], tops[2 * t], tops[2 * t + 1]
            c_ref[...] = jnp.full(c_ref.shape, -jnp.inf, F32)
            for k, (r, s) in enumerate(_CAND_PAIRS):
                c_ref[k:k + 1, :] = ta[r] + tb[s]
            best, = _top_rows((c_ref[...],), PEER_TOPK)
            tau = best[PEER_TOPK - 1]
            zsum = jnp.zeros_like(best[0])
            for bk in best:
                zsum = zsum + jnp.exp(bk - best[0])
            code_b = jnp.zeros(b.shape, F32)
            for s in range(PEER_TOPK):
                code_b = code_b + jnp.where(tb[s] > b, 1.0, 0.0)
            code_t = [jnp.zeros(tau.shape, F32)]
            for s in range(1, PEER_TOPK):
                code_t.append(jnp.where(tb[s] == tb[s - 1], code_t[s - 1], float(s)))
            q = jnp.full(a.shape, -1.0, F32)
            for r in reversed(range(PEER_TOPK)):
                q_r = jnp.full(tau.shape, -1.0, F32)
                for s in range(PEER_TOPK // (r + 1)):
                    q_r = jnp.maximum(q_r, jnp.where(ta[r] + tb[s] >= tau, code_t[s], -1.0))
                q = jnp.where(a == ta[r], q_r, q)
            q_ref[h, :, lanes] = q
            code_ref[h, :, lanes] = code_b.astype(BF16)
            ea_ref[h, :, lanes] = jnp.exp(a - ta[0]) / zsum
            eb_ref[h, :, lanes] = jnp.exp(b - tb[0]).astype(BF16)
        return carry

    lax.fori_loop(0, PEER_HEADS, body, 0)


def peer_select(st, tm):
    H, _, n, T = st.shape
    kern = functools.partial(_peer_select_kernel, tm=tm)
    return pl.pallas_call(
        kern,
        grid=(T // tm,),
        in_specs=[pl.BlockSpec((H, 2, n, tm), lambda t: (0, 0, 0, t))],
        out_specs=[pl.BlockSpec((H, n, tm), lambda t: (0, 0, t))] * 4,
        out_shape=[jax.ShapeDtypeStruct((H, n, T), dt) for dt in (F32, F32, BF16, BF16)],
        scratch_shapes=[pltpu.VMEM((_CAND_ROWS, 128), F32)],
        compiler_params=_params("parallel"),
        name="peer_select",
    )(st)


def _gelu_tanh(x):
    k = -2.0 * math.sqrt(2.0 / math.pi) * LOG2E
    return x / (1.0 + jnp.exp2(x * (k + (k * 0.044715) * (x * x))))


def _peer_kernel(x_ref, u_ref, vt_ref, ea_ref, q_ref, eb_ref, code_ref, res_ref, gate_ref, o_ref,
                 acc_ref, act_ref, g_ref, *, ni, tm):
    c = pl.program_id(1)

    @pl.when(c == 0)
    def _():
        acc_ref[...] = jnp.zeros_like(acc_ref)

    act_ref[...] = _gelu_tanh(lax.dot_general(u_ref[...], x_ref[...], (((1,), (1,)), ((), ())),
                                              preferred_element_type=F32))

    for ts in range(tm // 128):
        lanes = slice(ts * 128, (ts + 1) * 128)
        for ii in range(ni):
            rows = slice(ii * N_KEYS, (ii + 1) * N_KEYS)
            w = jnp.zeros((N_KEYS, 128), BF16)
            for h in range(PEER_HEADS):
                sel = code_ref[h, :, lanes] <= q_ref[h, ii:ii + 1, lanes].astype(BF16)
                gate = ea_ref[h, ii:ii + 1, lanes].astype(BF16) * eb_ref[h, :, lanes]
                w = w + jnp.where(sel, gate, jnp.zeros_like(gate))
            g_ref[rows, lanes] = w * act_ref[rows, lanes].astype(BF16)

    acc_ref[...] += jnp.dot(vt_ref[...], g_ref[...], preferred_element_type=F32)

    @pl.when(c == pl.num_programs(1) - 1)
    def _():
        o_ref[...] = res_ref[...] + gate_ref[0] * acc_ref[...].T


def peer_dense(x, u, vt, eat, qt, ebt, codet, res, gate, rows_per_seg, tm, ni):
    T, D = x.shape
    E = u.shape[0]
    ec = ni * N_KEYS
    kern = functools.partial(_peer_kernel, ni=ni, tm=tm)
    nseg = gate.shape[0]
    blocks_per_seg = rows_per_seg // tm
    rows_of_chunk = pl.BlockSpec((PEER_HEADS, ni, tm), lambda t, c: (0, c, t))
    return pl.pallas_call(
        kern,
        grid=(T // tm, E // ec),
        in_specs=[pl.BlockSpec((tm, D), lambda t, c: (t, 0)),
                  pl.BlockSpec((ec, D), lambda t, c: (c, 0)),
                  pl.BlockSpec((D, ec), lambda t, c: (0, c)),
                  rows_of_chunk,
                  rows_of_chunk,
                  pl.BlockSpec((PEER_HEADS, N_KEYS, tm), lambda t, c: (0, 0, t)),
                  pl.BlockSpec((PEER_HEADS, N_KEYS, tm), lambda t, c: (0, 0, t)),
                  pl.BlockSpec((tm, D), lambda t, c: (t, 0)),
                  pl.BlockSpec((1, 1, D), lambda t, c: (jnp.minimum(t // blocks_per_seg, nseg - 1), 0, 0))],
        out_specs=pl.BlockSpec((tm, D), lambda t, c: (t, 0)),
        out_shape=jax.ShapeDtypeStruct((T, D), F32),
        scratch_shapes=[pltpu.VMEM((D, tm), F32),
                        pltpu.VMEM((ec, tm), F32),
                        pltpu.VMEM((ec, tm), BF16)],
        compiler_params=_params("parallel", "arbitrary"),
        name="peer_dense",
    )(x, u, vt, eat, qt, ebt, codet, res, gate.reshape(nseg, 1, D))


def peer(qp, h_bf, keys, u_bf, vt_bf, res, gate, rows_per_seg, tm):
    T = qp.shape[0]
    dk = keys.shape[-1]
    kflat = keys.reshape(PEER_HEADS * 2, N_KEYS, dk).astype(BF16)
    st = keys_times_qT(kflat, qp, tm).reshape(PEER_HEADS, 2, N_KEYS, T)
    eat, qt, ebt, codet = peer_select(st, tm)
    return peer_dense(h_bf, u_bf, vt_bf, eat, qt, ebt, codet, res, gate, rows_per_seg, tm, 8)


def axial_rope(n_tok, dim):
    rows = n_tok // GRID_W
    row = jnp.repeat(jnp.arange(rows, dtype=F32), GRID_W)
    col = jnp.tile(jnp.arange(GRID_W, dtype=F32), rows)
    axis_dim = dim // 2
    inv_freq = ROPE_THETA ** (-jnp.arange(0, axis_dim, 2, dtype=F32) / axis_dim)
    ang = jnp.concatenate([row[:, None] * inv_freq, col[:, None] * inv_freq], axis=-1)
    return jnp.cos(ang), jnp.sin(ang)


def short_conv(u, w, b):
    up = jnp.pad(u, ((0, 0), (1, 1), (0, 0)))
    return up[:, :-2] * w[0] + up[:, 1:-1] * w[1] + up[:, 2:] * w[2] + b


def hyena_filters(n_tok, w1, b1, w2, b2, w3, b3, freq):
    hp = lax.Precision.HIGHEST
    t = jnp.linspace(0.0, 1.0, n_tok, dtype=F32)[:, None]
    w = (2.0 * math.pi / n_tok) * jnp.arange(n_tok, dtype=F32)[:, None]
    f = jnp.linspace(1e-4, HY_BANDS - 1, HY_BANDS, dtype=F32)[None, :]
    feat = jnp.concatenate([t, jnp.cos(f * w), -jnp.sin(f * w)], axis=-1)
    hdn = jnp.sin(freq * (jnp.dot(feat, w1, precision=hp) + b1))
    hdn = jnp.sin(freq * (jnp.dot(hdn, w2, precision=hp) + b2))
    filt = (jnp.dot(hdn, w3, precision=hp) + b3).reshape(n_tok, HY_ORDER, 2, HY_CH)
    deltas = jnp.abs(jnp.linspace(math.log(HY_TARGET) / HY_SLOW, math.log(HY_TARGET) / HY_FAST, HY_CH,
                                  dtype=F32))
    filt = filt * jnp.exp(-t[:, :, None, None] * deltas)
    fwd, bwd = filt[:, :, 0], filt[:, :, 1]
    kfull = jnp.concatenate([fwd, jnp.zeros_like(fwd[:1]), bwd[1:][::-1]], axis=0)
    return kfull / jnp.sum(jnp.abs(kfull), axis=0, keepdims=True)


def hyena_filters_t(n_tok, w1, b1, w2, b2, w3, b3, freq):
    hp = lax.Precision.HIGHEST
    t = jnp.linspace(0.0, 1.0, n_tok, dtype=F32)[:, None]
    w = (2.0 * math.pi / n_tok) * jnp.arange(n_tok, dtype=F32)[:, None]
    f = jnp.linspace(1e-4, HY_BANDS - 1, HY_BANDS, dtype=F32)[None, :]
    feat = jnp.concatenate([t, jnp.cos(f * w), -jnp.sin(f * w)], axis=-1)
    hdn = jnp.sin(freq * (jnp.dot(feat, w1, precision=hp) + b1))
    hdn = jnp.sin(freq * (jnp.dot(hdn, w2, precision=hp) + b2))
    deltas = jnp.abs(jnp.linspace(math.log(HY_TARGET) / HY_SLOW, math.log(HY_TARGET) / HY_FAST, HY_CH,
                                  dtype=F32))
    w3t = jnp.transpose(w3.reshape(-1, HY_ORDER, 2, HY_CH), (2, 1, 3, 0))
    b3t = jnp.transpose(b3.reshape(HY_ORDER, 2, HY_CH), (1, 0, 2))[..., None]

    def half(d, hidden, times):
        decay = jnp.exp(-deltas[:, None] * times[None, :])
        return (jnp.einsum('ock,nk->ocn', w3t[d], hidden, precision=hp) + b3t[d]) * decay

    fwd = half(0, hdn, t[:, 0])
    bwd_rev = half(1, hdn[::-1], t[::-1, 0])
    kfull = jnp.concatenate([fwd, jnp.zeros_like(fwd[..., :1]), bwd_rev[..., :n_tok - 1]], axis=-1)
    kfull = kfull / jnp.sum(jnp.abs(kfull), axis=-1, keepdims=True)
    return kfull.reshape(HY_ORDER * HY_CH, 2 * n_tok)


def long_conv(z, kf):
    n = z.shape[1]
    zf = jnp.fft.rfft(z, n=2 * n, axis=1)
    hf = jnp.fft.rfft(kf, n=2 * n, axis=0)
    return jnp.fft.irfft(zf * hf[None], n=2 * n, axis=1)[:, :n]


def hyena_small(u, conv_w, conv_b, filter_params, bias):
    n = u.shape[1]
    u = short_conv(u, conv_w, conv_b)
    v, x1, x2 = jnp.split(u, 3, axis=-1)
    kfull = hyena_filters(n, *filter_params)
    z = x1 * (long_conv(v, kfull[:, 0]) + bias[0] * v)
    return x2 * (long_conv(z, kfull[:, 1]) + bias[1] * z)


def _hi_lo(x):
    hi = x.astype(BF16)
    return hi, (x - hi.astype(F32)).astype(BF16)


def _dot3(a, b):
    d = lambda x, y: jnp.dot(x, y, preferred_element_type=F32)
    return d(a[0], b[0]) + d(a[1], b[0]) + d(a[0], b[1])


def _dft_constants(n_tok):
    n = 2 * n_tok
    n1 = n // 128
    a1 = 2.0 * np.pi * np.outer(np.arange(n1), np.arange(n1)) / n1
    a2 = 2.0 * np.pi * np.outer(np.arange(128), np.arange(128)) / 128
    at = 2.0 * np.pi * np.outer(np.arange(n1), np.arange(128)) / n
    c1, s1, c2, s2 = np.cos(a1), np.sin(a1), np.cos(a2), np.sin(a2)
    pair = lambda m: _hi_lo(jnp.asarray(m, F32))
    return dict(
        f1_half=pair(np.concatenate([c1[:, :n1 // 2], -s1[:, :n1 // 2]], axis=0)),
        f1_full=pair(np.concatenate([c1, -s1], axis=0)),
        m_fwd=pair(np.block([[c2, -s2], [s2, c2]])),
        m_inv=pair(np.block([[c2, s2], [-s2, c2]])),
        g_half=pair(np.concatenate([c1[:n1 // 2], -s1[:n1 // 2]], axis=1)),
        tr=jnp.asarray(np.cos(at), F32), ti=jnp.asarray(-np.sin(at), F32))


def _dft_fwd(seqs, f1, tr, ti, m_fwd):
    n1 = tr.shape[0]
    y = _dot3(f1, _hi_lo(jnp.concatenate(seqs, axis=1)))
    rows = []
    for k in range(len(seqs)):
        yr, yi = y[:n1, k * 128:(k + 1) * 128], y[n1:, k * 128:(k + 1) * 128]
        rows.append(jnp.concatenate([yr * tr - yi * ti, yr * ti + yi * tr], axis=1))
    return _dot3(_hi_lo(jnp.concatenate(rows, axis=0)), m_fwd)


def _dft_inv_half(p, g_half, tr, ti, m_inv):
    n1 = tr.shape[0]
    u = _dot3(_hi_lo(p), m_inv)
    cols = []
    for k in range(p.shape[0] // n1):
        ur, ui = u[k * n1:(k + 1) * n1, :128], u[k * n1:(k + 1) * n1, 128:]
        cols.append(jnp.concatenate([ur * tr + ui * ti, ui * tr - ur * ti], axis=0))
    return _dot3(g_half, _hi_lo(jnp.concatenate(cols, axis=1))) * (1.0 / (n1 * 128))


def _pairs(refs):
    return (refs[0][...], refs[1][...])


def _spectrum_kernel(a_ref, f1h, f1l, tr_ref, ti_ref, mh, ml, o_ref):
    cb, n1 = a_ref.shape[0], tr_ref.shape[0]
    x = _dft_fwd([a_ref[k] for k in range(cb)], _pairs((f1h, f1l)), tr_ref[...], ti_ref[...],
                 _pairs((mh, ml)))
    o_ref[...] = x.reshape(cb, n1, 256)


def filter_spectrum(kf, consts, cb):
    items, n1, _ = kf.shape
    full = lambda shape: pl.BlockSpec(shape, lambda i: (0,) * len(shape))
    f1, m = consts["f1_full"], consts["m_fwd"]
    return pl.pallas_call(
        _spectrum_kernel,
        grid=(items // cb,),
        in_specs=[pl.BlockSpec((cb, n1, 128), lambda i: (i, 0, 0)),
                  full(f1[0].shape), full(f1[1].shape), full((n1, 128)), full((n1, 128)),
                  full(m[0].shape), full(m[1].shape)],
        out_specs=pl.BlockSpec((cb, n1, 256), lambda i: (i, 0, 0)),
        out_shape=jax.ShapeDtypeStruct((items, n1, 256), F32),
        compiler_params=_params("parallel"),
        name="filter_spectrum",
    )(kf, f1[0], f1[1], consts["tr"], consts["ti"], m[0], m[1])


def _conv_gate_kernel(bias_ref, u_ref, g_ref, h_ref, f1h, f1l, tr_ref, ti_ref, mfh, mfl, mih, mil, gh, gl,
                      o_ref, *, cb):
    tr, ti = tr_ref[...], ti_ref[...]
    n1 = tr.shape[0]
    c0 = pl.program_id(1) * cb
    u = [u_ref[0, k] for k in range(cb)]
    x = _dft_fwd(u, _pairs((f1h, f1l)), tr, ti, _pairs((mfh, mfl)))
    prod = []
    for k in range(cb):
        xr, xi = x[k * n1:(k + 1) * n1, :128], x[k * n1:(k + 1) * n1, 128:]
        hr, hi = h_ref[k, :, :128], h_ref[k, :, 128:]
        prod.append(jnp.concatenate([xr * hr - xi * hi, xr * hi + xi * hr], axis=1))
    y = _dft_inv_half(jnp.concatenate(prod, axis=0), _pairs((gh, gl)), tr, ti, _pairs((mih, mil)))
    for k in range(cb):
        o_ref[0, k] = g_ref[0, k] * (y[:, k * 128:(k + 1) * 128] + bias_ref[c0 + k] * u[k])


def conv_gate(u, gate, spec, bias, consts, cb):
    B, C, half, _ = u.shape
    n1 = 2 * half
    full = lambda shape: pl.BlockSpec(shape, lambda b, c: (0,) * len(shape))
    seq = pl.BlockSpec((1, cb, half, 128), lambda b, c: (b, c, 0, 0))
    mats = [*consts["f1_half"], consts["tr"], consts["ti"], *consts["m_fwd"], *consts["m_inv"],
            *consts["g_half"]]
    return pl.pallas_call(
        functools.partial(_conv_gate_kernel, cb=cb),
        grid=(B, C // cb),
        in_specs=[pl.BlockSpec(memory_space=pltpu.SMEM), seq, seq,
                  pl.BlockSpec((cb, n1, 256), lambda b, c: (c, 0, 0))] + [full(m.shape) for m in mats],
        out_specs=seq,
        out_shape=jax.ShapeDtypeStruct(u.shape, F32),
        compiler_params=_params("parallel", "arbitrary"),
        name="conv_gate",
    )(bias.astype(F32), u, gate, spec, *mats)


def hyena(u, conv_w, conv_b, filter_params, bias):
    B, n, _ = u.shape
    half = n // 128
    consts = _dft_constants(n)
    u = short_conv(u, conv_w, conv_b)
    seqs = jnp.transpose(u, (0, 2, 1)).reshape(B, 3, HY_CH, half, 128)
    v, x1, x2 = seqs[:, 0], seqs[:, 1], seqs[:, 2]
    kf = hyena_filters_t(n, *filter_params).reshape(HY_ORDER * HY_CH, 2 * half, 128)
    spec = filter_spectrum(kf, consts, HY_CH_BLOCK).reshape(HY_ORDER, HY_CH, 2 * half, 256)
    z = conv_gate(v, x1, spec[0], bias[0], consts, HY_CH_BLOCK)
    o = conv_gate(z, x2, spec[1], bias[1], consts, HY_CH_BLOCK)
    return jnp.transpose(o.reshape(B, HY_CH, n), (0, 2, 1))


def _head_major_t(t):
    B, L, H, d = t.shape
    return jnp.transpose(t, (0, 2, 3, 1)).reshape(B * H, d, L)


def _key_chunk(n_keys):
    for c in range(KEY_CHUNK_CAP, 0, -128):
        if n_keys % c == 0:
            return c
    raise ValueError(f"no key chunk for {n_keys}")


def kernel(x, c, ctx, c_ctx, w_mod, b_mod, g_norm_mix, w_in, g_qk_diff, lambda_diff, g_qk_win, sink_win, g_qk_glob, hy_conv_w, hy_conv_b, hy_w1, hy_b1, hy_w2, hy_b2, hy_w3, hy_b3, hy_freq, hy_bias, g_mix_out, w_out, g_norm_ffn, peer_wq, peer_keys, peer_u, peer_v):
    B, L, D = x.shape
    C = ctx.shape[1]
    depth = w_mod.shape[0]
    TM = 512
    n_lat = B * L
    n_ctx = B * C
    assert L % TM == 0 and n_ctx % TM == 0
    nseg = B + 1

    def rope_rows(dim):
        cos, sin = axial_rope(L, dim)
        return (jnp.concatenate([jnp.tile(cos, (B, 1)), jnp.ones((n_ctx, dim // 2), F32)], axis=0),
                jnp.concatenate([jnp.tile(sin, (B, 1)), jnp.zeros((n_ctx, dim // 2), F32)], axis=0))

    rope_rows_half = rope_rows(DIFF_QK_DIM)
    rope_rows_full = rope_rows(HEAD_DIM)
    sc = jnp.concatenate([jax.nn.silu(c), jax.nn.silu(c_ctx)[None]], axis=0)
    sc = jnp.pad(sc, ((0, 8 - nseg), (0, 0))).astype(BF16)

    xall = jnp.concatenate([x.reshape(n_lat, D), ctx.reshape(n_ctx, D)], axis=0)

    for i in range(depth):
        want_ctx = i < depth - 1
        lambda_init = 0.8 - 0.6 * math.exp(-0.3 * i)
        mod = mm(sc, w_mod[i].astype(BF16), 8, 1024)[:nseg] + b_mod[i]
        sh1, s1, g1, sh2, s2, g2 = jnp.split(mod, N_MOD, axis=-1)

        groups = normmod_mm(xall, g_norm_mix[i], sh1, s1, w_in[i].astype(BF16), L, TM, F32, False,
                            widths=(W_DIFF, W_WIN, W_GLOB, W_HY))
        ga_, gw_, gg_, gh_ = groups
        ph, phc = gh_[:n_lat].reshape(B, L, -1), gh_[n_lat:].reshape(B, C, -1)

        def gains(g_qk, n_q, n_k, d):
            return jnp.concatenate([jnp.tile(g_qk[0], n_q) * (d ** -0.5 * LOG2E), jnp.tile(g_qk[1], n_k)])

        def split_rows(t, *shape):
            return t[:n_lat].reshape(B, L, *shape), t[n_lat:].reshape(B, C, *shape)

        def rows_of(t):
            lat, cx = t[:n_lat].reshape(B, L, -1), t[n_lat:].reshape(B, C, -1)
            return lat, cx, jnp.concatenate([cx, lat], axis=1)

        qka, qkac, qka_keys = rows_of(head_prep(ga_, 2 * GROUP_W, DIFF_QK_DIM,
                                                gains(g_qk_diff[i], 2 * DIFF_HEADS, 2 * DIFF_HEADS, DIFF_QK_DIM),
                                                *rope_rows_half, TM))
        va, vac = split_rows(ga_[:, 2 * GROUP_W:].astype(BF16), DIFF_HEADS, HEAD_DIM)
        lam_vec = lambda_diff[i]
        lam = (jnp.exp(jnp.sum(lam_vec[0] * lam_vec[1])) - jnp.exp(jnp.sum(lam_vec[2] * lam_vec[3]))
               + lambda_init)
        coefs = jnp.stack([jnp.ones((), F32), -lam])
        vva = jnp.concatenate([vac, va], axis=1)
        oa = dense_attention(qka, qka_keys, GROUP_W, _head_major_t(vva), coefs, None,
                             2, DIFF_HEADS, 1, DIFF_QK_DIM, ATTN_TQ_DIFF, _key_chunk(C + L))

        def gqa_parts(g_, g_qk, n_q, n_kv):
            wq, wk = n_q * HEAD_DIM, n_kv * HEAD_DIM
            qk = head_prep(g_, wq + wk, HEAD_DIM, gains(g_qk, n_q, n_kv, HEAD_DIM), *rope_rows_full, TM)
            v, vc = split_rows(g_[:, wq + wk:].astype(BF16), n_kv, HEAD_DIM)
            return rows_of(qk) + (v, vc)

        WQ = WIN_HEADS * HEAD_DIM
        WK = WIN_KV_HEADS * HEAD_DIM
        qkw, qkwc, _, vw, vwc = gqa_parts(gw_, g_qk_win[i], WIN_HEADS, WIN_KV_HEADS)
        zblk = jnp.zeros((B, BLOCK, WK), BF16)
        kpad = jnp.concatenate([qkwc[..., WQ:], zblk, qkw[..., WQ:], zblk], axis=1)
        vpad = jnp.concatenate([vwc.reshape(B, C, WK), zblk, vw.reshape(B, L, WK), zblk], axis=1)
        Rw = WIN_HEADS // WIN_KV_HEADS
        ob = window_attention(qkw, kpad, vpad, sink_win[i], WIN_KV_HEADS, Rw, C, L)

        qkg, qkgc, qkg_keys, vg, vgc = gqa_parts(gg_, g_qk_glob[i], GLOB_HEADS, GLOB_KV_HEADS)
        Rg = GLOB_HEADS // GLOB_KV_HEADS
        one = jnp.ones((1,), F32)
        vvg = _head_major_t(jnp.concatenate([vgc, vg], axis=1))
        og = dense_attention(qkg, qkg_keys, GLOB_HEADS * HEAD_DIM, vvg, one, None,
                             1, GLOB_KV_HEADS, Rg, HEAD_DIM, ATTN_TQ_GLOB, _key_chunk(C + L))

        filt = (hy_w1[i], hy_b1[i], hy_w2[i], hy_b2[i], hy_w3[i], hy_b3[i], hy_freq[i])
        oh = hyena(ph, hy_conv_w[i], hy_conv_b[i], filt, hy_bias[i])

        mixed = [t.reshape(n_lat, -1) for t in (oa, ob, og, oh)]
        if want_ctx:
            oac = dense_attention(qkac, qkac, GROUP_W, _head_major_t(vac), coefs, None,
                                  2, DIFF_HEADS, 1, DIFF_QK_DIM, C, C)
            obc = dense_attention(qkwc, qkwc, WQ, _head_major_t(vwc), one, sink_win[i],
                                  1, WIN_KV_HEADS, Rw, HEAD_DIM, C, C)
            ogc = dense_attention(qkgc, qkgc, GLOB_HEADS * HEAD_DIM, _head_major_t(vgc), one, None,
                                  1, GLOB_KV_HEADS, Rg, HEAD_DIM, C, C)
            ohc = hyena_small(phc, hy_conv_w[i], hy_conv_b[i], filt, hy_bias[i])
            mixed = [jnp.concatenate([m_, t.reshape(n_ctx, -1)], axis=0)
                     for m_, t in zip(mixed, (oac, obc, ogc, ohc))]

        head_scale = jnp.where(jnp.arange(N_OUT_HEADS) < DIFF_HEADS, 1.0 - lambda_init, 1.0)
        gain_out = (g_mix_out[i].reshape(N_OUT_HEADS, HEAD_DIM) * head_scale[:, None]).reshape(D)
        xcur = merge_residual(mixed, HEAD_DIM, gain_out, w_out[i].astype(BF16), xall, g1, L, TM)

        qp, h2 = normmod_mm(xcur, g_norm_ffn[i], sh2, s2, peer_wq[i].astype(BF16), L, TM, BF16, True)
        xall = peer(qp, h2, peer_keys[i], peer_u[i].astype(BF16), jnp.transpose(peer_v[i].astype(BF16)),
                    xcur, g2, L, TM)

    return xall[:n_lat].reshape(B, L, D)
```

```python
import functools
import math

import jax
import jax.numpy as jnp
import numpy as np
from jax import lax
from jax.experimental import pallas as pl
from jax.experimental.pallas import tpu as pltpu

F32 = jnp.float32
BF16 = jnp.bfloat16

GRID_W = 64
HEAD_DIM = 64
BLOCK = 128
WINDOW = 128
ROPE_THETA = 10000.0
EPS = 1e-6
NEG_INF = -1e30
N_MOD = 6
DIFF_HEADS = 4
DIFF_QK_DIM = 32
WIN_HEADS = 4
WIN_KV_HEADS = 2
GLOB_HEADS = 4
GLOB_KV_HEADS = 2
N_OUT_HEADS = 16
HY_CH = 256
HY_ORDER = 2
HY_BANDS = 16
HY_TARGET = 1e-2
HY_FAST = 0.3
HY_SLOW = 1.5
W_DIFF = 768
W_WIN = 512
W_GLOB = 512
W_HY = 768
GROUP_W = 256
PEER_HEADS = 8
N_KEYS = 128
PEER_TOPK = 16

VMEM_LIMIT = 56 * 1024 * 1024
ATTN_TQ_DIFF = 512
ATTN_TQ_GLOB = 256
KEY_CHUNK_CAP = 1408
HY_CH_BLOCK = 16
WIN_QUERY_BLOCKS = 4
PEER_CHUNK_ROWS = 16


def _params(*sem):
    return pltpu.CompilerParams(dimension_semantics=sem, vmem_limit_bytes=VMEM_LIMIT)


def _mm_kernel(a_ref, b_ref, o_ref):
    o_ref[...] = jnp.dot(a_ref[...], b_ref[...], preferred_element_type=F32)


def mm(a, b, tm, tn):
    M, K = a.shape
    N = b.shape[1]
    return pl.pallas_call(
        _mm_kernel,
        grid=(M // tm, N // tn),
        in_specs=[pl.BlockSpec((tm, K), lambda i, j: (i, 0)),
                  pl.BlockSpec((K, tn), lambda i, j: (0, j))],
        out_specs=pl.BlockSpec((tm, tn), lambda i, j: (i, j)),
        out_shape=jax.ShapeDtypeStruct((M, N), F32),
        compiler_params=_params("parallel", "arbitrary"),
        name="mm",
    )(a, b)


def _seg_spec(rows_per_seg, tm, nseg, D):
    blocks_per_seg = rows_per_seg // tm
    return pl.BlockSpec((1, 1, D), lambda i: (jnp.minimum(i // blocks_per_seg, nseg - 1), 0, 0))


def _normmod_mm_kernel(x_ref, g_ref, sh_ref, sc_ref, w_ref, *outs, widths, with_h):
    x = x_ref[...]
    y = x * lax.rsqrt(jnp.mean(x * x, axis=-1, keepdims=True) + EPS) * g_ref[...]
    h = (y * (1.0 + sc_ref[0]) + sh_ref[0]).astype(BF16)
    if with_h:
        outs[-1][...] = h
    r = jnp.dot(h, w_ref[...], preferred_element_type=F32)
    lo = 0
    for o_ref, n in zip(outs, widths):
        o_ref[...] = r[:, lo:lo + n].astype(o_ref.dtype)
        lo += n


def normmod_mm(x, g, shift, scale, w, rows_per_seg, tm, out_dtype, with_h, widths=None):
    M, D = x.shape
    N = w.shape[1]
    widths = (N,) if widths is None else tuple(widths)
    nseg = shift.shape[0]
    seg = _seg_spec(rows_per_seg, tm, nseg, D)
    row_block = lambda n: pl.BlockSpec((tm, n), lambda i: (i, 0))
    out_specs = [row_block(n) for n in widths]
    out_shape = [jax.ShapeDtypeStruct((M, n), out_dtype) for n in widths]
    if with_h:
        out_specs.append(row_block(D))
        out_shape.append(jax.ShapeDtypeStruct((M, D), BF16))
    return pl.pallas_call(
        functools.partial(_normmod_mm_kernel, widths=widths, with_h=with_h),
        grid=(M // tm,),
        in_specs=[row_block(D), pl.BlockSpec((1, D), lambda i: (0, 0)), seg, seg,
                  pl.BlockSpec((D, N), lambda i: (0, 0))],
        out_specs=out_specs,
        out_shape=out_shape,
        compiler_params=_params("parallel"),
        name="normmod_mm",
    )(x, g.reshape(1, D), shift.reshape(nseg, 1, D), scale.reshape(nseg, 1, D), w)


def _merge_residual_kernel(*refs, n_in, d):
    in_refs, (gain_ref, bd_ref, w_ref, x_ref, gate_ref, o_ref) = refs[:n_in], refs[n_in:]
    bd = bd_ref[...]
    parts, lo = [], 0
    for r in in_refs:
        o = r[...]
        ms = sum(jnp.dot(p, bd, preferred_element_type=F32) for p in _hi_lo(o * o))
        parts.append((o * lax.rsqrt(ms + EPS) * gain_ref[:, lo:lo + o.shape[1]]).astype(BF16))
        lo += o.shape[1]
    a = jnp.concatenate(parts, axis=1)
    o_ref[...] = x_ref[...] + gate_ref[0] * jnp.dot(a, w_ref[...], preferred_element_type=F32)


def merge_residual(outs, d, gain, w, x, gate, rows_per_seg, tm):
    M, width = outs[0].shape
    K, N = w.shape
    nseg = gate.shape[0]
    col = np.arange(width)
    bd = jnp.asarray((col[:, None] // d == col[None, :] // d) / d, BF16)
    row_block = lambda n: pl.BlockSpec((tm, n), lambda i: (i, 0))
    return pl.pallas_call(
        functools.partial(_merge_residual_kernel, n_in=len(outs), d=d),
        grid=(M // tm,),
        in_specs=[row_block(width)] * len(outs) + [
            pl.BlockSpec((1, K), lambda i: (0, 0)), pl.BlockSpec((width, width), lambda i: (0, 0)),
            pl.BlockSpec((K, N), lambda i: (0, 0)), row_block(N), _seg_spec(rows_per_seg, tm, nseg, N)],
        out_specs=row_block(N),
        out_shape=jax.ShapeDtypeStruct((M, N), F32),
        compiler_params=_params("parallel"),
        name="merge_residual",
    )(*outs, gain.reshape(1, K).astype(F32), bd, w, x, gate.reshape(nseg, 1, N))


def _scores_kernel(k_ref, q_ref, o_ref):
    d = k_ref.shape[-1]
    for p in range(k_ref.shape[0]):
        o_ref[p] = lax.dot_general(k_ref[p], q_ref[:, p * d:(p + 1) * d], (((1,), (1,)), ((), ())),
                                   preferred_element_type=F32)


def keys_times_qT(keys, q, tn):
    P, n, d = keys.shape
    T = q.shape[0]
    return pl.pallas_call(
        _scores_kernel,
        grid=(T // tn,),
        in_specs=[pl.BlockSpec((P, n, d), lambda j: (0, 0, 0)),
                  pl.BlockSpec((tn, P * d), lambda j: (j, 0))],
        out_specs=pl.BlockSpec((P, n, tn), lambda j: (0, 0, j)),
        out_shape=jax.ShapeDtypeStruct((P, n, T), F32),
        compiler_params=_params("parallel"),
        name="peer_scores_t",
    )(keys, q)


def _head_prep_kernel(t_ref, cos_ref, sin_ref, gain_ref, bd_ref, pm_ref, ex_ref, o_ref):
    d3 = lambda parts, m: sum(jnp.dot(p, m, preferred_element_type=F32) for p in parts)

    def split3(v):
        hi, lo = _hi_lo(v)
        rest = v - hi.astype(F32) - lo.astype(F32)
        return hi, lo, rest.astype(BF16)

    x = t_ref[...]
    ms = d3(_hi_lo(x * x), bd_ref[...])
    y = x * lax.rsqrt(ms + EPS) * gain_ref[...]
    ex = ex_ref[...]
    cos = d3(split3(cos_ref[...]), ex)
    sin = d3(split3(sin_ref[...]), ex)
    o_ref[...] = (y * cos + d3(_hi_lo(y), pm_ref[...]) * sin).astype(BF16)


def head_prep(t, width, d, gain, cos_rows, sin_rows, tm):
    rows = t.shape[0]
    hd = d // 2
    col = np.arange(width)
    same_head = (col[:, None] // d) == (col[None, :] // d)
    bd = jnp.asarray(same_head / d, BF16)
    src = np.where((col % d) < hd, col + hd, col - hd)
    pm = np.zeros((width, width), np.float32)
    pm[src, col] = np.where((col % d) < hd, -1.0, 1.0)
    ex = np.zeros((hd, width), np.float32)
    ex[col % hd, col] = 1.0
    full = lambda a: pl.BlockSpec(a.shape, lambda i: (0,) * a.ndim)
    consts = (gain.reshape(1, width).astype(F32), bd, jnp.asarray(pm, BF16), jnp.asarray(ex, BF16))
    return pl.pallas_call(
        _head_prep_kernel,
        grid=(rows // tm,),
        in_specs=[pl.BlockSpec((tm, width), lambda i: (i, 0)),
                  pl.BlockSpec((tm, hd), lambda i: (i, 0)),
                  pl.BlockSpec((tm, hd), lambda i: (i, 0))] + [full(c) for c in consts],
        out_specs=pl.BlockSpec((tm, width), lambda i: (i, 0)),
        out_shape=jax.ShapeDtypeStruct((rows, width), BF16),
        compiler_params=_params("parallel"),
        name="head_prep",
    )(t, cos_rows, sin_rows, *consts)


LOG2E = math.log2(math.e)


def _dense_attn_kernel(coef_ref, sink_ref, q_ref, k_ref, vt_ref, o_ref, sa_ref, sb_ref, *, nbr, R, G,
                       HP, d, tq, tk, nk, has_sink):
    cols = R * tq
    dv = vt_ref.shape[1]
    kw = k_ref.shape[-1]
    g0 = (pl.program_id(0) % (G // HP)) * HP
    nt = (((1,), (1,)), ((), ()))
    bufs = (sa_ref, sb_ref)
    lane = lax.broadcasted_iota(jnp.int32, (1, kw), 1)
    heads = []
    for hp in range(HP):
        out = jnp.zeros((dv, cols), F32)
        for br in range(nbr):
            k_lo = (hp * nbr + br) * d
            parts = []
            for r in range(R):
                q_lo = ((hp * R + r) * nbr + br) * d
                blk = q_ref[0, :, (q_lo // kw) * kw:(q_lo // kw + 1) * kw]
                if (q_lo - k_lo) % kw:
                    sh = (q_lo - k_lo) % kw
                    blk = jnp.concatenate([blk[:, sh:], blk[:, :sh]], axis=1)
                parts.append(jnp.where((lane >= k_lo) & (lane < k_lo + d), blk, jnp.zeros_like(blk)))
            q = parts[0] if R == 1 else jnp.concatenate(parts, axis=0)
            if has_sink:
                m = jnp.concatenate([jnp.full((1, tq), sink_ref[(g0 + hp) * R + r] * LOG2E, F32)
                                     for r in range(R)], axis=1)
                l = jnp.ones((1, cols), F32)
            else:
                m = jnp.full((1, cols), NEG_INF, F32)
                l = jnp.zeros((1, cols), F32)
            acc = jnp.zeros((dv, cols), F32)
            bufs[0][...] = lax.dot_general(k_ref[0, 0:tk, :], q, nt, preferred_element_type=F32)
            for j in range(nk):
                if j + 1 < nk:
                    bufs[(j + 1) % 2][...] = lax.dot_general(k_ref[0, (j + 1) * tk:(j + 2) * tk, :], q,
                                                             nt, preferred_element_type=F32)
                s = bufs[j % 2][...]
                m_new = jnp.maximum(m, jnp.max(s, axis=0, keepdims=True))
                alpha = jnp.exp2(m - m_new)
                p = jnp.exp2(s - m_new)
                l = alpha * l + jnp.sum(p, axis=0, keepdims=True)
                acc = alpha * acc + jnp.dot(vt_ref[hp, :, j * tk:(j + 1) * tk], p.astype(BF16),
                                            preferred_element_type=F32)
                m = m_new
            out = out + coef_ref[br] * (acc / l)
        out_t = out.T
        heads += [out_t[r * tq:(r + 1) * tq] for r in range(R)]
    o_ref[0] = jnp.concatenate(heads, axis=1)


def dense_attention(qsrc, ksrc, k_lane0, vt, coefs, sink, nbr, G, R, d, tq, tk):
    B, Lq, _ = qsrc.shape
    Lk = ksrc.shape[1]
    dv = vt.shape[1]
    HP = 128 // (nbr * d)
    assert G % HP == 0 and (HP * R * dv) % 128 == 0 and k_lane0 % 128 == 0
    has_sink = sink is not None
    if sink is None:
        sink = jnp.zeros((1,), F32)
    kern = functools.partial(_dense_attn_kernel, nbr=nbr, R=R, G=G, HP=HP, d=d, tq=tq, tk=tk, nk=Lk // tk,
                             has_sink=has_sink)
    gp = G // HP
    qw = HP * R * nbr * d
    return pl.pallas_call(
        kern,
        grid=(B * gp, Lq // tq),
        in_specs=[pl.BlockSpec(memory_space=pltpu.SMEM),
                  pl.BlockSpec(memory_space=pltpu.SMEM),
                  pl.BlockSpec((1, tq, qw), lambda b, i: (b // gp, i, b % gp)),
                  pl.BlockSpec((1, Lk, 128), lambda b, i: (b // gp, 0, k_lane0 // 128 + b % gp)),
                  pl.BlockSpec((HP, dv, Lk), lambda b, i: (b, 0, 0))],
        out_specs=pl.BlockSpec((1, tq, HP * R * dv), lambda b, i: (b // gp, i, b % gp)),
        out_shape=jax.ShapeDtypeStruct((B, Lq, G * R * dv), F32),
        scratch_shapes=[pltpu.VMEM((tk, R * tq), F32), pltpu.VMEM((tk, R * tq), F32)],
        compiler_params=_params("parallel", "arbitrary"),
        name="dense_attn",
    )(coefs.astype(F32), sink.astype(F32), qsrc, ksrc, vt)


def _window_attn_kernel(sink_ref, q_ref, k_ref, v_ref, o_ref, *, R, G, C, L, QB):
    rows = R * BLOCK
    kw = k_ref.shape[-1]
    d = kw // G
    kc = k_ref[0, 0:C, :]
    vc = v_ref[0, 0:C, :]
    nt = (((1,), (1,)), ((), ()))
    qi = lax.broadcasted_iota(jnp.int32, (rows, 3 * BLOCK), 0) & (BLOCK - 1)
    kj = lax.broadcasted_iota(jnp.int32, (rows, 3 * BLOCK), 1)
    in_window = jnp.abs(kj - BLOCK - qi) <= WINDOW
    lane = lax.broadcasted_iota(jnp.int32, (1, kw), 1)
    for qb in range(QB):
        n = pl.program_id(1) * QB + qb
        start = pl.multiple_of(C + n * BLOCK, BLOCK)
        kl = k_ref[0, pl.ds(start, 3 * BLOCK), :]
        vl = v_ref[0, pl.ds(start, 3 * BLOCK), :]
        kpos = (n - 1) * BLOCK + kj
        valid = in_window & (kpos >= 0) & (kpos < L)
        heads = []
        for g in range(G):
            parts = []
            for r in range(R):
                q_lo, k_lo = (g * R + r) * d, g * d
                blk = q_ref[0, qb * BLOCK:(qb + 1) * BLOCK, (q_lo // kw) * kw:(q_lo // kw + 1) * kw]
                if (q_lo - k_lo) % kw:
                    sh = (q_lo - k_lo) % kw
                    blk = jnp.concatenate([blk[:, sh:], blk[:, :sh]], axis=1)
                parts.append(jnp.where((lane >= k_lo) & (lane < k_lo + d), blk, jnp.zeros_like(blk)))
            q = jnp.concatenate(parts, axis=0)
            sink = jnp.concatenate([jnp.full((BLOCK, 1), sink_ref[g * R + r] * LOG2E, F32)
                                    for r in range(R)], axis=0)
            s_ctx = lax.dot_general(q, kc, nt, preferred_element_type=F32)
            s_loc = jnp.where(valid, lax.dot_general(q, kl, nt, preferred_element_type=F32), NEG_INF)
            m = jnp.maximum(jnp.maximum(jnp.max(s_ctx, axis=-1, keepdims=True),
                                        jnp.max(s_loc, axis=-1, keepdims=True)), sink)
            e_ctx = jnp.exp2(s_ctx - m)
            e_loc = jnp.exp2(s_loc - m)
            den = (jnp.sum(e_ctx, axis=-1, keepdims=True) + jnp.sum(e_loc, axis=-1, keepdims=True)
                   + jnp.exp2(sink - m))
            inv = 1.0 / den
            o = (jnp.dot((e_ctx * inv).astype(BF16), vc, preferred_element_type=F32)
                 + jnp.dot((e_loc * inv).astype(BF16), vl, preferred_element_type=F32))
            heads += [o[r * BLOCK:(r + 1) * BLOCK, g * d:(g + 1) * d] for r in range(R)]
        o_ref[0, qb * BLOCK:(qb + 1) * BLOCK, :] = jnp.concatenate(heads, axis=1)


def window_attention(qsrc, kpad, vpad, sink, G, R, C, L):
    B, _, _ = qsrc.shape
    Lp, kw = kpad.shape[1], kpad.shape[2]
    d = kw // G
    QB = WIN_QUERY_BLOCKS
    kern = functools.partial(_window_attn_kernel, R=R, G=G, C=C, L=L, QB=QB)
    return pl.pallas_call(
        kern,
        grid=(B, L // (BLOCK * QB)),
        in_specs=[pl.BlockSpec(memory_space=pltpu.SMEM),
                  pl.BlockSpec((1, QB * BLOCK, G * R * d), lambda b, i: (b, i, 0)),
                  pl.BlockSpec((1, Lp, kw), lambda b, i: (b, 0, 0)),
                  pl.BlockSpec((1, Lp, kw), lambda b, i: (b, 0, 0))],
        out_specs=pl.BlockSpec((1, QB * BLOCK, G * R * d), lambda b, i: (b, i, 0)),
        out_shape=jax.ShapeDtypeStruct((B, L, G * R * d), F32),
        compiler_params=_params("parallel", "arbitrary"),
        name="window_attn",
    )(sink.astype(F32), qsrc, kpad, vpad)


_CAND_PAIRS = tuple((r, s) for r in range(PEER_TOPK) for s in range(PEER_TOPK)
                    if (r + 1) * (s + 1) <= PEER_TOPK)
_CAND_ROWS = -(-len(_CAND_PAIRS) // 8) * 8


def _top_rows(vs, n):
    vs = list(vs)
    iota = lax.broadcasted_iota(jnp.int32, vs[0].shape, 0).astype(F32)
    rows = [[] for _ in vs]
    for r in range(n):
        for i, v in enumerate(vs):
            m = jnp.max(v, axis=0, keepdims=True)
            rows[i].append(m)
            if r + 1 < n:
                first = jnp.min(jnp.where(v == m, iota, float(v.shape[0])), axis=0, keepdims=True)
                vs[i] = jnp.where(iota == first, -jnp.inf, v)
    return rows


def _peer_select_kernel(s_ref, ea_ref, q_ref, eb_ref, code_ref, c_ref, *, tm):
    n_lane = tm // 128

    def body(h, carry):
        tiles = [slice(t * 128, (t + 1) * 128) for t in range(n_lane)]
        ab = [s_ref[h, p, :, lanes] for lanes in tiles for p in (0, 1)]
        tops = _top_rows(ab, PEER_TOPK)
        for t, lanes in enumerate(tiles):
            a, b, ta, tb = ab[2 * t], ab[2 * t + 1], tops[2 * t], tops[2 * t + 1]
            c_ref[...] = jnp.full(c_ref.shape, -jnp.inf, F32)
            for k, (r, s) in enumerate(_CAND_PAIRS):
                c_ref[k:k + 1, :] = ta[r] + tb[s]
            best, = _top_rows((c_ref[...],), PEER_TOPK)
            tau = best[PEER_TOPK - 1]
            zsum = jnp.zeros_like(best[0])
            for bk in best:
                zsum = zsum + jnp.exp(bk - best[0])
            code_b = jnp.zeros(b.shape, F32)
            for s in range(PEER_TOPK):
                code_b = code_b + jnp.where(tb[s] > b, 1.0, 0.0)
            code_t = [jnp.zeros(tau.shape, F32)]
            for s in range(1, PEER_TOPK):
                code_t.append(jnp.where(tb[s] == tb[s - 1], code_t[s - 1], float(s)))
            q = jnp.full(a.shape, -1.0, F32)
            for r in reversed(range(PEER_TOPK)):
                q_r = jnp.full(tau.shape, -1.0, F32)
                for s in range(PEER_TOPK // (r + 1)):
                    q_r = jnp.maximum(q_r, jnp.where(ta[r] + tb[s] >= tau, code_t[s], -1.0))
                q = jnp.where(a == ta[r], q_r, q)
            q_ref[h, :, lanes] = q
            code_ref[h, :, lanes] = code_b.astype(BF16)
            ea_ref[h, :, lanes] = jnp.exp(a - ta[0]) / zsum
            eb_ref[h, :, lanes] = jnp.exp(b - tb[0]).astype(BF16)
        return carry

    lax.fori_loop(0, PEER_HEADS, body, 0)


def peer_select(st, tm):
    H, _, n, T = st.shape
    kern = functools.partial(_peer_select_kernel, tm=tm)
    return pl.pallas_call(
        kern,
        grid=(T // tm,),
        in_specs=[pl.BlockSpec((H, 2, n, tm), lambda t: (0, 0, 0, t))],
        out_specs=[pl.BlockSpec((H, n, tm), lambda t: (0, 0, t))] * 4,
        out_shape=[jax.ShapeDtypeStruct((H, n, T), dt) for dt in (F32, F32, BF16, BF16)],
        scratch_shapes=[pltpu.VMEM((_CAND_ROWS, 128), F32)],
        compiler_params=_params("parallel"),
        name="peer_select",
    )(st)


def _gelu_tanh(x):
    k = -2.0 * math.sqrt(2.0 / math.pi) * LOG2E
    return x / (1.0 + jnp.exp2(x * (k + (k * 0.044715) * (x * x))))


def _peer_kernel(x_ref, u_ref, vt_ref, ea_ref, q_ref, eb_ref, code_ref, res_ref, gate_ref, o_ref,
                 acc_ref, act_ref, g_ref, *, ni, tm):
    c = pl.program_id(1)

    @pl.when(c == 0)
    def _():
        acc_ref[...] = jnp.zeros_like(acc_ref)

    act_ref[...] = _gelu_tanh(lax.dot_general(u_ref[...], x_ref[...], (((1,), (1,)), ((), ())),
                                              preferred_element_type=F32))

    for ts in range(tm // 128):
        lanes = slice(ts * 128, (ts + 1) * 128)
        for ii in range(ni):
            rows = slice(ii * N_KEYS, (ii + 1) * N_KEYS)
            w = jnp.zeros((N_KEYS, 128), BF16)
            for h in range(PEER_HEADS):
                sel = code_ref[h, :, lanes] <= q_ref[h, ii:ii + 1, lanes].astype(BF16)
                gate = ea_ref[h, ii:ii + 1, lanes].astype(BF16) * eb_ref[h, :, lanes]
                w = w + jnp.where(sel, gate, jnp.zeros_like(gate))
            g_ref[rows, lanes] = w * act_ref[rows, lanes].astype(BF16)

    acc_ref[...] += jnp.dot(vt_ref[...], g_ref[...], preferred_element_type=F32)

    @pl.when(c == pl.num_programs(1) - 1)
    def _():
        o_ref[...] = res_ref[...] + gate_ref[0] * acc_ref[...].T


def peer_dense(x, u, vt, eat, qt, ebt, codet, res, gate, rows_per_seg, tm, ni):
    T, D = x.shape
    E = u.shape[0]
    ec = ni * N_KEYS
    kern = functools.partial(_peer_kernel, ni=ni, tm=tm)
    nseg = gate.shape[0]
    blocks_per_seg = rows_per_seg // tm
    rows_of_chunk = pl.BlockSpec((PEER_HEADS, ni, tm), lambda t, c: (0, c, t))
    return pl.pallas_call(
        kern,
        grid=(T // tm, E // ec),
        in_specs=[pl.BlockSpec((tm, D), lambda t, c: (t, 0)),
                  pl.BlockSpec((ec, D), lambda t, c: (c, 0)),
                  pl.BlockSpec((D, ec), lambda t, c: (0, c)),
                  rows_of_chunk,
                  rows_of_chunk,
                  pl.BlockSpec((PEER_HEADS, N_KEYS, tm), lambda t, c: (0, 0, t)),
                  pl.BlockSpec((PEER_HEADS, N_KEYS, tm), lambda t, c: (0, 0, t)),
                  pl.BlockSpec((tm, D), lambda t, c: (t, 0)),
                  pl.BlockSpec((1, 1, D), lambda t, c: (jnp.minimum(t // blocks_per_seg, nseg - 1), 0, 0))],
        out_specs=pl.BlockSpec((tm, D), lambda t, c: (t, 0)),
        out_shape=jax.ShapeDtypeStruct((T, D), F32),
        scratch_shapes=[pltpu.VMEM((D, tm), F32),
                        pltpu.VMEM((ec, tm), F32),
                        pltpu.VMEM((ec, tm), BF16)],
        compiler_params=_params("parallel", "arbitrary"),
        name="peer_dense",
    )(x, u, vt, eat, qt, ebt, codet, res, gate.reshape(nseg, 1, D))


def peer(qp, h_bf, keys, u_bf, vt_bf, res, gate, rows_per_seg, tm):
    T = qp.shape[0]
    dk = keys.shape[-1]
    kflat = keys.reshape(PEER_HEADS * 2, N_KEYS, dk).astype(BF16)
    st = keys_times_qT(kflat, qp, tm).reshape(PEER_HEADS, 2, N_KEYS, T)
    eat, qt, ebt, codet = peer_select(st, tm)
    return peer_dense(h_bf, u_bf, vt_bf, eat, qt, ebt, codet, res, gate, rows_per_seg, tm, PEER_CHUNK_ROWS)


def axial_rope(n_tok, dim):
    rows = n_tok // GRID_W
    row = jnp.repeat(jnp.arange(rows, dtype=F32), GRID_W)
    col = jnp.tile(jnp.arange(GRID_W, dtype=F32), rows)
    axis_dim = dim // 2
    inv_freq = ROPE_THETA ** (-jnp.arange(0, axis_dim, 2, dtype=F32) / axis_dim)
    ang = jnp.concatenate([row[:, None] * inv_freq, col[:, None] * inv_freq], axis=-1)
    return jnp.cos(ang), jnp.sin(ang)


def short_conv(u, w, b):
    up = jnp.pad(u, ((0, 0), (1, 1), (0, 0)))
    return up[:, :-2] * w[0] + up[:, 1:-1] * w[1] + up[:, 2:] * w[2] + b


def hyena_filters(n_tok, w1, b1, w2, b2, w3, b3, freq):
    hp = lax.Precision.HIGHEST
    t = jnp.linspace(0.0, 1.0, n_tok, dtype=F32)[:, None]
    w = (2.0 * math.pi / n_tok) * jnp.arange(n_tok, dtype=F32)[:, None]
    f = jnp.linspace(1e-4, HY_BANDS - 1, HY_BANDS, dtype=F32)[None, :]
    feat = jnp.concatenate([t, jnp.cos(f * w), -jnp.sin(f * w)], axis=-1)
    hdn = jnp.sin(freq * (jnp.dot(feat, w1, precision=hp) + b1))
    hdn = jnp.sin(freq * (jnp.dot(hdn, w2, precision=hp) + b2))
    filt = (jnp.dot(hdn, w3, precision=hp) + b3).reshape(n_tok, HY_ORDER, 2, HY_CH)
    deltas = jnp.abs(jnp.linspace(math.log(HY_TARGET) / HY_SLOW, math.log(HY_TARGET) / HY_FAST, HY_CH,
                                  dtype=F32))
    filt = filt * jnp.exp(-t[:, :, None, None] * deltas)
    fwd, bwd = filt[:, :, 0], filt[:, :, 1]
    kfull = jnp.concatenate([fwd, jnp.zeros_like(fwd[:1]), bwd[1:][::-1]], axis=0)
    return kfull / jnp.sum(jnp.abs(kfull), axis=0, keepdims=True)


def hyena_filters_t(n_tok, w1, b1, w2, b2, w3, b3, freq):
    hp = lax.Precision.HIGHEST
    t = jnp.linspace(0.0, 1.0, n_tok, dtype=F32)[:, None]
    w = (2.0 * math.pi / n_tok) * jnp.arange(n_tok, dtype=F32)[:, None]
    f = jnp.linspace(1e-4, HY_BANDS - 1, HY_BANDS, dtype=F32)[None, :]
    feat = jnp.concatenate([t, jnp.cos(f * w), -jnp.sin(f * w)], axis=-1)
    hdn = jnp.sin(freq * (jnp.dot(feat, w1, precision=hp) + b1))
    hdn = jnp.sin(freq * (jnp.dot(hdn, w2, precision=hp) + b2))
    deltas = jnp.abs(jnp.linspace(math.log(HY_TARGET) / HY_SLOW, math.log(HY_TARGET) / HY_FAST, HY_CH,
                                  dtype=F32))
    w3t = jnp.transpose(w3.reshape(-1, HY_ORDER, 2, HY_CH), (2, 1, 3, 0))
    b3t = jnp.transpose(b3.reshape(HY_ORDER, 2, HY_CH), (1, 0, 2))[..., None]

    def half(d, hidden, times):
        decay = jnp.exp(-deltas[:, None] * times[None, :])
        return (jnp.einsum('ock,nk->ocn', w3t[d], hidden, precision=hp) + b3t[d]) * decay

    fwd = half(0, hdn, t[:, 0])
    bwd_rev = half(1, hdn[::-1], t[::-1, 0])
    kfull = jnp.concatenate([fwd, jnp.zeros_like(fwd[..., :1]), bwd_rev[..., :n_tok - 1]], axis=-1)
    kfull = kfull / jnp.sum(jnp.abs(kfull), axis=-1, keepdims=True)
    return kfull.reshape(HY_ORDER * HY_CH, 2 * n_tok)


def long_conv(z, kf):
    n = z.shape[1]
    zf = jnp.fft.rfft(z, n=2 * n, axis=1)
    hf = jnp.fft.rfft(kf, n=2 * n, axis=0)
    return jnp.fft.irfft(zf * hf[None], n=2 * n, axis=1)[:, :n]


def hyena_small(u, conv_w, conv_b, filter_params, bias):
    n = u.shape[1]
    u = short_conv(u, conv_w, conv_b)
    v, x1, x2 = jnp.split(u, 3, axis=-1)
    kfull = hyena_filters(n, *filter_params)
    z = x1 * (long_conv(v, kfull[:, 0]) + bias[0] * v)
    return x2 * (long_conv(z, kfull[:, 1]) + bias[1] * z)


def _hi_lo(x):
    hi = x.astype(BF16)
    return hi, (x - hi.astype(F32)).astype(BF16)


def _dot3(a, b):
    d = lambda x, y: jnp.dot(x, y, preferred_element_type=F32)
    return d(a[0], b[0]) + d(a[1], b[0]) + d(a[0], b[1])


def _dft_constants(n_tok):
    n = 2 * n_tok
    n1 = n // 128
    a1 = 2.0 * np.pi * np.outer(np.arange(n1), np.arange(n1)) / n1
    a2 = 2.0 * np.pi * np.outer(np.arange(128), np.arange(128)) / 128
    at = 2.0 * np.pi * np.outer(np.arange(n1), np.arange(128)) / n
    c1, s1, c2, s2 = np.cos(a1), np.sin(a1), np.cos(a2), np.sin(a2)
    pair = lambda m: _hi_lo(jnp.asarray(m, F32))
    return dict(
        f1_half=pair(np.concatenate([c1[:, :n1 // 2], -s1[:, :n1 // 2]], axis=0)),
        f1_full=pair(np.concatenate([c1, -s1], axis=0)),
        m_fwd=pair(np.block([[c2, -s2], [s2, c2]])),
        m_inv=pair(np.block([[c2, s2], [-s2, c2]])),
        g_half=pair(np.concatenate([c1[:n1 // 2], -s1[:n1 // 2]], axis=1)),
        tr=jnp.asarray(np.cos(at), F32), ti=jnp.asarray(-np.sin(at), F32))


def _dft_fwd(seqs, f1, tr, ti, m_fwd):
    n1 = tr.shape[0]
    y = _dot3(f1, _hi_lo(jnp.concatenate(seqs, axis=1)))
    rows = []
    for k in range(len(seqs)):
        yr, yi = y[:n1, k * 128:(k + 1) * 128], y[n1:, k * 128:(k + 1) * 128]
        rows.append(jnp.concatenate([yr * tr - yi * ti, yr * ti + yi * tr], axis=1))
    return _dot3(_hi_lo(jnp.concatenate(rows, axis=0)), m_fwd)


def _dft_inv_half(p, g_half, tr, ti, m_inv):
    n1 = tr.shape[0]
    u = _dot3(_hi_lo(p), m_inv)
    cols = []
    for k in range(p.shape[0] // n1):
        ur, ui = u[k * n1:(k + 1) * n1, :128], u[k * n1:(k + 1) * n1, 128:]
        cols.append(jnp.concatenate([ur * tr + ui * ti, ui * tr - ur * ti], axis=0))
    return _dot3(g_half, _hi_lo(jnp.concatenate(cols, axis=1))) * (1.0 / (n1 * 128))


def _pairs(refs):
    return (refs[0][...], refs[1][...])


def _spectrum_kernel(a_ref, f1h, f1l, tr_ref, ti_ref, mh, ml, o_ref):
    cb, n1 = a_ref.shape[0], tr_ref.shape[0]
    x = _dft_fwd([a_ref[k] for k in range(cb)], _pairs((f1h, f1l)), tr_ref[...], ti_ref[...],
                 _pairs((mh, ml)))
    o_ref[...] = x.reshape(cb, n1, 256)


def filter_spectrum(kf, consts, cb):
    items, n1, _ = kf.shape
    full = lambda shape: pl.BlockSpec(shape, lambda i: (0,) * len(shape))
    f1, m = consts["f1_full"], consts["m_fwd"]
    return pl.pallas_call(
        _spectrum_kernel,
        grid=(items // cb,),
        in_specs=[pl.BlockSpec((cb, n1, 128), lambda i: (i, 0, 0)),
                  full(f1[0].shape), full(f1[1].shape), full((n1, 128)), full((n1, 128)),
                  full(m[0].shape), full(m[1].shape)],
        out_specs=pl.BlockSpec((cb, n1, 256), lambda i: (i, 0, 0)),
        out_shape=jax.ShapeDtypeStruct((items, n1, 256), F32),
        compiler_params=_params("parallel"),
        name="filter_spectrum",
    )(kf, f1[0], f1[1], consts["tr"], consts["ti"], m[0], m[1])


def _conv_gate_kernel(bias_ref, u_ref, g_ref, h_ref, f1h, f1l, tr_ref, ti_ref, mfh, mfl, mih, mil, gh, gl,
                      o_ref, *, cb):
    tr, ti = tr_ref[...], ti_ref[...]
    n1 = tr.shape[0]
    c0 = pl.program_id(1) * cb
    u = [u_ref[0, k] for k in range(cb)]
    x = _dft_fwd(u, _pairs((f1h, f1l)), tr, ti, _pairs((mfh, mfl)))
    prod = []
    for k in range(cb):
        xr, xi = x[k * n1:(k + 1) * n1, :128], x[k * n1:(k + 1) * n1, 128:]
        hr, hi = h_ref[k, :, :128], h_ref[k, :, 128:]
        prod.append(jnp.concatenate([xr * hr - xi * hi, xr * hi + xi * hr], axis=1))
    y = _dft_inv_half(jnp.concatenate(prod, axis=0), _pairs((gh, gl)), tr, ti, _pairs((mih, mil)))
    for k in range(cb):
        o_ref[0, k] = g_ref[0, k] * (y[:, k * 128:(k + 1) * 128] + bias_ref[c0 + k] * u[k])


def conv_gate(u, gate, spec, bias, consts, cb):
    B, C, half, _ = u.shape
    n1 = 2 * half
    full = lambda shape: pl.BlockSpec(shape, lambda b, c: (0,) * len(shape))
    seq = pl.BlockSpec((1, cb, half, 128), lambda b, c: (b, c, 0, 0))
    mats = [*consts["f1_half"], consts["tr"], consts["ti"], *consts["m_fwd"], *consts["m_inv"],
            *consts["g_half"]]
    return pl.pallas_call(
        functools.partial(_conv_gate_kernel, cb=cb),
        grid=(B, C // cb),
        in_specs=[pl.BlockSpec(memory_space=pltpu.SMEM), seq, seq,
                  pl.BlockSpec((cb, n1, 256), lambda b, c: (c, 0, 0))] + [full(m.shape) for m in mats],
        out_specs=seq,
        out_shape=jax.ShapeDtypeStruct(u.shape, F32),
        compiler_params=_params("parallel", "arbitrary"),
        name="conv_gate",
    )(bias.astype(F32), u, gate, spec, *mats)


def hyena(u, conv_w, conv_b, filter_params, bias):
    B, n, _ = u.shape
    half = n // 128
    consts = _dft_constants(n)
    u = short_conv(u, conv_w, conv_b)
    seqs = jnp.transpose(u, (0, 2, 1)).reshape(B, 3, HY_CH, half, 128)
    v, x1, x2 = seqs[:, 0], seqs[:, 1], seqs[:, 2]
    kf = hyena_filters_t(n, *filter_params).reshape(HY_ORDER * HY_CH, 2 * half, 128)
    spec = filter_spectrum(kf, consts, HY_CH_BLOCK).reshape(HY_ORDER, HY_CH, 2 * half, 256)
    z = conv_gate(v, x1, spec[0], bias[0], consts, HY_CH_BLOCK)
    o = conv_gate(z, x2, spec[1], bias[1], consts, HY_CH_BLOCK)
    return jnp.transpose(o.reshape(B, HY_CH, n), (0, 2, 1))


def _head_major_t(t):
    B, L, H, d = t.shape
    return jnp.transpose(t, (0, 2, 3, 1)).reshape(B * H, d, L)


def _key_chunk(n_keys):
    for c in range(KEY_CHUNK_CAP, 0, -128):
        if n_keys % c == 0:
            return c
    raise ValueError(f"no key chunk for {n_keys}")


def kernel(x, c, ctx, c_ctx, w_mod, b_mod, g_norm_mix, w_in, g_qk_diff, lambda_diff, g_qk_win, sink_win, g_qk_glob, hy_conv_w, hy_conv_b, hy_w1, hy_b1, hy_w2, hy_b2, hy_w3, hy_b3, hy_freq, hy_bias, g_mix_out, w_out, g_norm_ffn, peer_wq, peer_keys, peer_u, peer_v):
    B, L, D = x.shape
    C = ctx.shape[1]
    depth = w_mod.shape[0]
    TM = 512
    n_lat = B * L
    n_ctx = B * C
    assert L % TM == 0 and n_ctx % TM == 0
    nseg = B + 1

    def rope_rows(dim):
        cos, sin = axial_rope(L, dim)
        return (jnp.concatenate([jnp.tile(cos, (B, 1)), jnp.ones((n_ctx, dim // 2), F32)], axis=0),
                jnp.concatenate([jnp.tile(sin, (B, 1)), jnp.zeros((n_ctx, dim // 2), F32)], axis=0))

    rope_rows_half = rope_rows(DIFF_QK_DIM)
    rope_rows_full = rope_rows(HEAD_DIM)
    sc = jnp.concatenate([jax.nn.silu(c), jax.nn.silu(c_ctx)[None]], axis=0)
    sc = jnp.pad(sc, ((0, 8 - nseg), (0, 0))).astype(BF16)

    xall = jnp.concatenate([x.reshape(n_lat, D), ctx.reshape(n_ctx, D)], axis=0)

    for i in range(depth):
        want_ctx = i < depth - 1
        lambda_init = 0.8 - 0.6 * math.exp(-0.3 * i)
        mod = mm(sc, w_mod[i].astype(BF16), 8, 1024)[:nseg] + b_mod[i]
        sh1, s1, g1, sh2, s2, g2 = jnp.split(mod, N_MOD, axis=-1)

        groups = normmod_mm(xall, g_norm_mix[i], sh1, s1, w_in[i].astype(BF16), L, TM, F32, False,
                            widths=(W_DIFF, W_WIN, W_GLOB, W_HY))
        ga_, gw_, gg_, gh_ = groups
        ph, phc = gh_[:n_lat].reshape(B, L, -1), gh_[n_lat:].reshape(B, C, -1)

        def gains(g_qk, n_q, n_k, d):
            return jnp.concatenate([jnp.tile(g_qk[0], n_q) * (d ** -0.5 * LOG2E), jnp.tile(g_qk[1], n_k)])

        def split_rows(t, *shape):
            return t[:n_lat].reshape(B, L, *shape), t[n_lat:].reshape(B, C, *shape)

        def rows_of(t):
            lat, cx = t[:n_lat].reshape(B, L, -1), t[n_lat:].reshape(B, C, -1)
            return lat, cx, jnp.concatenate([cx, lat], axis=1)

        qka, qkac, qka_keys = rows_of(head_prep(ga_, 2 * GROUP_W, DIFF_QK_DIM,
                                                gains(g_qk_diff[i], 2 * DIFF_HEADS, 2 * DIFF_HEADS, DIFF_QK_DIM),
                                                *rope_rows_half, TM))
        va, vac = split_rows(ga_[:, 2 * GROUP_W:].astype(BF16), DIFF_HEADS, HEAD_DIM)
        lam_vec = lambda_diff[i]
        lam = (jnp.exp(jnp.sum(lam_vec[0] * lam_vec[1])) - jnp.exp(jnp.sum(lam_vec[2] * lam_vec[3]))
               + lambda_init)
        coefs = jnp.stack([jnp.ones((), F32), -lam])
        vva = jnp.concatenate([vac, va], axis=1)
        oa = dense_attention(qka, qka_keys, GROUP_W, _head_major_t(vva), coefs, None,
                             2, DIFF_HEADS, 1, DIFF_QK_DIM, ATTN_TQ_DIFF, _key_chunk(C + L))

        def gqa_parts(g_, g_qk, n_q, n_kv):
            wq, wk = n_q * HEAD_DIM, n_kv * HEAD_DIM
            qk = head_prep(g_, wq + wk, HEAD_DIM, gains(g_qk, n_q, n_kv, HEAD_DIM), *rope_rows_full, TM)
            v, vc = split_rows(g_[:, wq + wk:].astype(BF16), n_kv, HEAD_DIM)
            return rows_of(qk) + (v, vc)

        WQ = WIN_HEADS * HEAD_DIM
        WK = WIN_KV_HEADS * HEAD_DIM
        qkw, qkwc, _, vw, vwc = gqa_parts(gw_, g_qk_win[i], WIN_HEADS, WIN_KV_HEADS)
        zblk = jnp.zeros((B, BLOCK, WK), BF16)
        kpad = jnp.concatenate([qkwc[..., WQ:], zblk, qkw[..., WQ:], zblk], axis=1)
        vpad = jnp.concatenate([vwc.reshape(B, C, WK), zblk, vw.reshape(B, L, WK), zblk], axis=1)
        Rw = WIN_HEADS // WIN_KV_HEADS
        ob = window_attention(qkw, kpad, vpad, sink_win[i], WIN_KV_HEADS, Rw, C, L)

        qkg, qkgc, qkg_keys, vg, vgc = gqa_parts(gg_, g_qk_glob[i], GLOB_HEADS, GLOB_KV_HEADS)
        Rg = GLOB_HEADS // GLOB_KV_HEADS
        one = jnp.ones((1,), F32)
        vvg = _head_major_t(jnp.concatenate([vgc, vg], axis=1))
        og = dense_attention(qkg, qkg_keys, GLOB_HEADS * HEAD_DIM, vvg, one, None,
                             1, GLOB_KV_HEADS, Rg, HEAD_DIM, ATTN_TQ_GLOB, _key_chunk(C + L))

        filt = (hy_w1[i], hy_b1[i], hy_w2[i], hy_b2[i], hy_w3[i], hy_b3[i], hy_freq[i])
        oh = hyena(ph, hy_conv_w[i], hy_conv_b[i], filt, hy_bias[i])

        mixed = [t.reshape(n_lat, -1) for t in (oa, ob, og, oh)]
        if want_ctx:
            oac = dense_attention(qkac, qkac, GROUP_W, _head_major_t(vac), coefs, None,
                                  2, DIFF_HEADS, 1, DIFF_QK_DIM, C, C)
            obc = dense_attention(qkwc, qkwc, WQ, _head_major_t(vwc), one, sink_win[i],
                                  1, WIN_KV_HEADS, Rw, HEAD_DIM, C, C)
            ogc = dense_attention(qkgc, qkgc, GLOB_HEADS * HEAD_DIM, _head_major_t(vgc), one, None,
                                  1, GLOB_KV_HEADS, Rg, HEAD_DIM, C, C)
            ohc = hyena_small(phc, hy_conv_w[i], hy_conv_b[i], filt, hy_bias[i])
            mixed = [jnp.concatenate([m_, t.reshape(n_ctx, -1)], axis=0)
                     for m_, t in zip(mixed, (oac, obc, ogc, ohc))]

        head_scale = jnp.where(jnp.arange(N_OUT_HEADS) < DIFF_HEADS, 1.0 - lambda_init, 1.0)
        gain_out = (g_mix_out[i].reshape(N_OUT_HEADS, HEAD_DIM) * head_scale[:, None]).reshape(D)
        xcur = merge_residual(mixed, HEAD_DIM, gain_out, w_out[i].astype(BF16), xall, g1, L, TM)

        qp, h2 = normmod_mm(xcur, g_norm_ffn[i], sh2, s2, peer_wq[i].astype(BF16), L, TM, BF16, True)
        xall = peer(qp, h2, peer_keys[i], peer_u[i].astype(BF16), jnp.transpose(peer_v[i].astype(BF16)),
                    xcur, g2, L, TM)

    return xall[:n_lat].reshape(B, L, D)
```

```python
import functools
import math

import jax
import jax.numpy as jnp
import numpy as np
from jax import lax
from jax.experimental import pallas as pl
from jax.experimental.pallas import tpu as pltpu

F32 = jnp.float32
BF16 = jnp.bfloat16

GRID_W = 64
HEAD_DIM = 64
BLOCK = 128
WINDOW = 128
ROPE_THETA = 10000.0
EPS = 1e-6
NEG_INF = -1e30
N_MOD = 6
DIFF_HEADS = 4
DIFF_QK_DIM = 32
WIN_HEADS = 4
WIN_KV_HEADS = 2
GLOB_HEADS = 4
GLOB_KV_HEADS = 2
N_OUT_HEADS = 16
HY_CH = 256
HY_ORDER = 2
HY_BANDS = 16
HY_TARGET = 1e-2
HY_FAST = 0.3
HY_SLOW = 1.5
W_DIFF = 768
W_WIN = 512
W_GLOB = 512
W_HY = 768
GROUP_W = 256
PEER_HEADS = 8
N_KEYS = 128
PEER_TOPK = 16

VMEM_LIMIT = 56 * 1024 * 1024
ATTN_TQ_DIFF = 512
ATTN_TQ_GLOB = 256
KEY_CHUNK_CAP = 1408
HY_CH_BLOCK = 16
WIN_QUERY_BLOCKS = 4


def _params(*sem):
    return pltpu.CompilerParams(dimension_semantics=sem, vmem_limit_bytes=VMEM_LIMIT)


def _mm_kernel(a_ref, b_ref, o_ref):
    o_ref[...] = jnp.dot(a_ref[...], b_ref[...], preferred_element_type=F32)


def mm(a, b, tm, tn):
    M, K = a.shape
    N = b.shape[1]
    return pl.pallas_call(
        _mm_kernel,
        grid=(M // tm, N // tn),
        in_specs=[pl.BlockSpec((tm, K), lambda i, j: (i, 0)),
                  pl.BlockSpec((K, tn), lambda i, j: (0, j))],
        out_specs=pl.BlockSpec((tm, tn), lambda i, j: (i, j)),
        out_shape=jax.ShapeDtypeStruct((M, N), F32),
        compiler_params=_params("parallel", "arbitrary"),
        name="mm",
    )(a, b)


def _seg_spec(rows_per_seg, tm, nseg, D):
    blocks_per_seg = rows_per_seg // tm
    return pl.BlockSpec((1, 1, D), lambda i: (jnp.minimum(i // blocks_per_seg, nseg - 1), 0, 0))


def _normmod_mm_kernel(x_ref, g_ref, sh_ref, sc_ref, w_ref, *outs, widths, with_h):
    x = x_ref[...]
    y = x * lax.rsqrt(jnp.mean(x * x, axis=-1, keepdims=True) + EPS) * g_ref[...]
    h = (y * (1.0 + sc_ref[0]) + sh_ref[0]).astype(BF16)
    if with_h:
        outs[-1][...] = h
    r = jnp.dot(h, w_ref[...], preferred_element_type=F32)
    lo = 0
    for o_ref, n in zip(outs, widths):
        o_ref[...] = r[:, lo:lo + n].astype(o_ref.dtype)
        lo += n


def normmod_mm(x, g, shift, scale, w, rows_per_seg, tm, out_dtype, with_h, widths=None):
    M, D = x.shape
    N = w.shape[1]
    widths = (N,) if widths is None else tuple(widths)
    nseg = shift.shape[0]
    seg = _seg_spec(rows_per_seg, tm, nseg, D)
    row_block = lambda n: pl.BlockSpec((tm, n), lambda i: (i, 0))
    out_specs = [row_block(n) for n in widths]
    out_shape = [jax.ShapeDtypeStruct((M, n), out_dtype) for n in widths]
    if with_h:
        out_specs.append(row_block(D))
        out_shape.append(jax.ShapeDtypeStruct((M, D), BF16))
    return pl.pallas_call(
        functools.partial(_normmod_mm_kernel, widths=widths, with_h=with_h),
        grid=(M // tm,),
        in_specs=[row_block(D), pl.BlockSpec((1, D), lambda i: (0, 0)), seg, seg,
                  pl.BlockSpec((D, N), lambda i: (0, 0))],
        out_specs=out_specs,
        out_shape=out_shape,
        compiler_params=_params("parallel"),
        name="normmod_mm",
    )(x, g.reshape(1, D), shift.reshape(nseg, 1, D), scale.reshape(nseg, 1, D), w)


def _merge_residual_kernel(*refs, n_in, d):
    in_refs, (gain_ref, bd_ref, w_ref, x_ref, gate_ref, o_ref) = refs[:n_in], refs[n_in:]
    bd = bd_ref[...]
    parts, lo = [], 0
    for r in in_refs:
        o = r[...]
        ms = sum(jnp.dot(p, bd, preferred_element_type=F32) for p in _hi_lo(o * o))
        parts.append((o * lax.rsqrt(ms + EPS) * gain_ref[:, lo:lo + o.shape[1]]).astype(BF16))
        lo += o.shape[1]
    a = jnp.concatenate(parts, axis=1)
    o_ref[...] = x_ref[...] + gate_ref[0] * jnp.dot(a, w_ref[...], preferred_element_type=F32)


def merge_residual(outs, d, gain, w, x, gate, rows_per_seg, tm):
    M, width = outs[0].shape
    K, N = w.shape
    nseg = gate.shape[0]
    col = np.arange(width)
    bd = jnp.asarray((col[:, None] // d == col[None, :] // d) / d, BF16)
    row_block = lambda n: pl.BlockSpec((tm, n), lambda i: (i, 0))
    return pl.pallas_call(
        functools.partial(_merge_residual_kernel, n_in=len(outs), d=d),
        grid=(M // tm,),
        in_specs=[row_block(width)] * len(outs) + [
            pl.BlockSpec((1, K), lambda i: (0, 0)), pl.BlockSpec((width, width), lambda i: (0, 0)),
            pl.BlockSpec((K, N), lambda i: (0, 0)), row_block(N), _seg_spec(rows_per_seg, tm, nseg, N)],
        out_specs=row_block(N),
        out_shape=jax.ShapeDtypeStruct((M, N), F32),
        compiler_params=_params("parallel"),
        name="merge_residual",
    )(*outs, gain.reshape(1, K).astype(F32), bd, w, x, gate.reshape(nseg, 1, N))


def _head_prep_kernel(t_ref, cos_ref, sin_ref, gain_ref, bd_ref, pm_ref, ex_ref, o_ref):
    d3 = lambda parts, m: sum(jnp.dot(p, m, preferred_element_type=F32) for p in parts)

    def split3(v):
        hi, lo = _hi_lo(v)
        rest = v - hi.astype(F32) - lo.astype(F32)
        return hi, lo, rest.astype(BF16)

    x = t_ref[...]
    ms = d3(_hi_lo(x * x), bd_ref[...])
    y = x * lax.rsqrt(ms + EPS) * gain_ref[...]
    ex = ex_ref[...]
    cos = d3(split3(cos_ref[...]), ex)
    sin = d3(split3(sin_ref[...]), ex)
    o_ref[...] = (y * cos + d3(_hi_lo(y), pm_ref[...]) * sin).astype(BF16)


def head_prep(t, width, d, gain, cos_rows, sin_rows, tm):
    rows = t.shape[0]
    hd = d // 2
    col = np.arange(width)
    same_head = (col[:, None] // d) == (col[None, :] // d)
    bd = jnp.asarray(same_head / d, BF16)
    src = np.where((col % d) < hd, col + hd, col - hd)
    pm = np.zeros((width, width), np.float32)
    pm[src, col] = np.where((col % d) < hd, -1.0, 1.0)
    ex = np.zeros((hd, width), np.float32)
    ex[col % hd, col] = 1.0
    full = lambda a: pl.BlockSpec(a.shape, lambda i: (0,) * a.ndim)
    consts = (gain.reshape(1, width).astype(F32), bd, jnp.asarray(pm, BF16), jnp.asarray(ex, BF16))
    return pl.pallas_call(
        _head_prep_kernel,
        grid=(rows // tm,),
        in_specs=[pl.BlockSpec((tm, width), lambda i: (i, 0)),
                  pl.BlockSpec((tm, hd), lambda i: (i, 0)),
                  pl.BlockSpec((tm, hd), lambda i: (i, 0))] + [full(c) for c in consts],
        out_specs=pl.BlockSpec((tm, width), lambda i: (i, 0)),
        out_shape=jax.ShapeDtypeStruct((rows, width), BF16),
        compiler_params=_params("parallel"),
        name="head_prep",
    )(t, cos_rows, sin_rows, *consts)


LOG2E = math.log2(math.e)


def _dense_attn_kernel(coef_ref, sink_ref, q_ref, k_ref, vt_ref, o_ref, sa_ref, sb_ref, *, nbr, R, G,
                       HP, d, tq, tk, nk, has_sink):
    cols = R * tq
    dv = vt_ref.shape[1]
    kw = k_ref.shape[-1]
    g0 = (pl.program_id(0) % (G // HP)) * HP
    nt = (((1,), (1,)), ((), ()))
    bufs = (sa_ref, sb_ref)
    lane = lax.broadcasted_iota(jnp.int32, (1, kw), 1)
    heads = []
    for hp in range(HP):
        out = jnp.zeros((dv, cols), F32)
        for br in range(nbr):
            k_lo = (hp * nbr + br) * d
            parts = []
            for r in range(R):
                q_lo = ((hp * R + r) * nbr + br) * d
                blk = q_ref[0, :, (q_lo // kw) * kw:(q_lo // kw + 1) * kw]
                if (q_lo - k_lo) % kw:
                    sh = (q_lo - k_lo) % kw
                    blk = jnp.concatenate([blk[:, sh:], blk[:, :sh]], axis=1)
                parts.append(jnp.where((lane >= k_lo) & (lane < k_lo + d), blk, jnp.zeros_like(blk)))
            q = parts[0] if R == 1 else jnp.concatenate(parts, axis=0)
            if has_sink:
                m = jnp.concatenate([jnp.full((1, tq), sink_ref[(g0 + hp) * R + r] * LOG2E, F32)
                                     for r in range(R)], axis=1)
                l = jnp.ones((1, cols), F32)
            else:
                m = jnp.full((1, cols), NEG_INF, F32)
                l = jnp.zeros((1, cols), F32)
            acc = jnp.zeros((dv, cols), F32)
            bufs[0][...] = lax.dot_general(k_ref[0, 0:tk, :], q, nt, preferred_element_type=F32)
            for j in range(nk):
                if j + 1 < nk:
                    bufs[(j + 1) % 2][...] = lax.dot_general(k_ref[0, (j + 1) * tk:(j + 2) * tk, :], q,
                                                             nt, preferred_element_type=F32)
                s = bufs[j % 2][...]
                m_new = jnp.maximum(m, jnp.max(s, axis=0, keepdims=True))
                alpha = jnp.exp2(m - m_new)
                p = jnp.exp2(s - m_new)
                l = alpha * l + jnp.sum(p, axis=0, keepdims=True)
                acc = alpha * acc + jnp.dot(vt_ref[hp, :, j * tk:(j + 1) * tk], p.astype(BF16),
                                            preferred_element_type=F32)
                m = m_new
            out = out + coef_ref[br] * (acc / l)
        out_t = out.T
        heads += [out_t[r * tq:(r + 1) * tq] for r in range(R)]
    o_ref[0] = jnp.concatenate(heads, axis=1)


def dense_attention(qsrc, ksrc, k_lane0, vt, coefs, sink, nbr, G, R, d, tq, tk):
    B, Lq, _ = qsrc.shape
    Lk = ksrc.shape[1]
    dv = vt.shape[1]
    HP = 128 // (nbr * d)
    assert G % HP == 0 and (HP * R * dv) % 128 == 0 and k_lane0 % 128 == 0
    has_sink = sink is not None
    if sink is None:
        sink = jnp.zeros((1,), F32)
    kern = functools.partial(_dense_attn_kernel, nbr=nbr, R=R, G=G, HP=HP, d=d, tq=tq, tk=tk, nk=Lk // tk,
                             has_sink=has_sink)
    gp = G // HP
    qw = HP * R * nbr * d
    return pl.pallas_call(
        kern,
        grid=(B * gp, Lq // tq),
        in_specs=[pl.BlockSpec(memory_space=pltpu.SMEM),
                  pl.BlockSpec(memory_space=pltpu.SMEM),
                  pl.BlockSpec((1, tq, qw), lambda b, i: (b // gp, i, b % gp)),
                  pl.BlockSpec((1, Lk, 128), lambda b, i: (b // gp, 0, k_lane0 // 128 + b % gp)),
                  pl.BlockSpec((HP, dv, Lk), lambda b, i: (b, 0, 0))],
        out_specs=pl.BlockSpec((1, tq, HP * R * dv), lambda b, i: (b // gp, i, b % gp)),
        out_shape=jax.ShapeDtypeStruct((B, Lq, G * R * dv), F32),
        scratch_shapes=[pltpu.VMEM((tk, R * tq), F32), pltpu.VMEM((tk, R * tq), F32)],
        compiler_params=_params("parallel", "arbitrary"),
        name="dense_attn",
    )(coefs.astype(F32), sink.astype(F32), qsrc, ksrc, vt)


def _window_attn_kernel(sink_ref, q_ref, k_ref, v_ref, o_ref, *, R, G, C, L, QB):
    rows = R * BLOCK
    kw = k_ref.shape[-1]
    d = kw // G
    kc = k_ref[0, 0:C, :]
    vc = v_ref[0, 0:C, :]
    nt = (((1,), (1,)), ((), ()))
    qi = lax.broadcasted_iota(jnp.int32, (rows, 3 * BLOCK), 0) & (BLOCK - 1)
    kj = lax.broadcasted_iota(jnp.int32, (rows, 3 * BLOCK), 1)
    in_window = jnp.abs(kj - BLOCK - qi) <= WINDOW
    lane = lax.broadcasted_iota(jnp.int32, (1, kw), 1)
    for qb in range(QB):
        n = pl.program_id(1) * QB + qb
        start = pl.multiple_of(C + n * BLOCK, BLOCK)
        kl = k_ref[0, pl.ds(start, 3 * BLOCK), :]
        vl = v_ref[0, pl.ds(start, 3 * BLOCK), :]
        kpos = (n - 1) * BLOCK + kj
        valid = in_window & (kpos >= 0) & (kpos < L)
        heads = []
        for g in range(G):
            parts = []
            for r in range(R):
                q_lo, k_lo = (g * R + r) * d, g * d
                blk = q_ref[0, qb * BLOCK:(qb + 1) * BLOCK, (q_lo // kw) * kw:(q_lo // kw + 1) * kw]
                if (q_lo - k_lo) % kw:
                    sh = (q_lo - k_lo) % kw
                    blk = jnp.concatenate([blk[:, sh:], blk[:, :sh]], axis=1)
                parts.append(jnp.where((lane >= k_lo) & (lane < k_lo + d), blk, jnp.zeros_like(blk)))
            q = jnp.concatenate(parts, axis=0)
            sink = jnp.concatenate([jnp.full((BLOCK, 1), sink_ref[g * R + r] * LOG2E, F32)
                                    for r in range(R)], axis=0)
            s_ctx = lax.dot_general(q, kc, nt, preferred_element_type=F32)
            s_loc = jnp.where(valid, lax.dot_general(q, kl, nt, preferred_element_type=F32), NEG_INF)
            m = jnp.maximum(jnp.maximum(jnp.max(s_ctx, axis=-1, keepdims=True),
                                        jnp.max(s_loc, axis=-1, keepdims=True)), sink)
            e_ctx = jnp.exp2(s_ctx - m)
            e_loc = jnp.exp2(s_loc - m)
            den = (jnp.sum(e_ctx, axis=-1, keepdims=True) + jnp.sum(e_loc, axis=-1, keepdims=True)
                   + jnp.exp2(sink - m))
            inv = 1.0 / den
            o = (jnp.dot((e_ctx * inv).astype(BF16), vc, preferred_element_type=F32)
                 + jnp.dot((e_loc * inv).astype(BF16), vl, preferred_element_type=F32))
            heads += [o[r * BLOCK:(r + 1) * BLOCK, g * d:(g + 1) * d] for r in range(R)]
        o_ref[0, qb * BLOCK:(qb + 1) * BLOCK, :] = jnp.concatenate(heads, axis=1)


def window_attention(qsrc, kpad, vpad, sink, G, R, C, L):
    B, _, _ = qsrc.shape
    Lp, kw = kpad.shape[1], kpad.shape[2]
    d = kw // G
    QB = WIN_QUERY_BLOCKS
    kern = functools.partial(_window_attn_kernel, R=R, G=G, C=C, L=L, QB=QB)
    return pl.pallas_call(
        kern,
        grid=(B, L // (BLOCK * QB)),
        in_specs=[pl.BlockSpec(memory_space=pltpu.SMEM),
                  pl.BlockSpec((1, QB * BLOCK, G * R * d), lambda b, i: (b, i, 0)),
                  pl.BlockSpec((1, Lp, kw), lambda b, i: (b, 0, 0)),
                  pl.BlockSpec((1, Lp, kw), lambda b, i: (b, 0, 0))],
        out_specs=pl.BlockSpec((1, QB * BLOCK, G * R * d), lambda b, i: (b, i, 0)),
        out_shape=jax.ShapeDtypeStruct((B, L, G * R * d), F32),
        compiler_params=_params("parallel", "arbitrary"),
        name="window_attn",
    )(sink.astype(F32), qsrc, kpad, vpad)


_CAND_PAIRS = tuple((r, s) for r in range(PEER_TOPK) for s in range(PEER_TOPK)
                    if (r + 1) * (s + 1) <= PEER_TOPK)
_CAND_ROWS = -(-len(_CAND_PAIRS) // 8) * 8


def _top_rows(vs, n):
    vs = list(vs)
    iota = lax.broadcasted_iota(jnp.int32, vs[0].shape, 0).astype(F32)
    rows = [[] for _ in vs]
    for r in range(n):
        for i, v in enumerate(vs):
            m = jnp.max(v, axis=0, keepdims=True)
            rows[i].append(m)
            if r + 1 < n:
                first = jnp.min(jnp.where(v == m, iota, float(v.shape[0])), axis=0, keepdims=True)
                vs[i] = jnp.where(iota == first, -jnp.inf, v)
    return rows


def _peer_select_kernel(k_ref, qp_ref, ea_ref, q_ref, eb_ref, code_ref, s_ref, c_ref, *, tm):
    n_lane = tm // 128
    dk = k_ref.shape[-1]
    for p in range(k_ref.shape[0]):
        s_ref[p // 2, p % 2] = lax.dot_general(k_ref[p], qp_ref[:, p * dk:(p + 1) * dk],
                                               (((1,), (1,)), ((), ())), preferred_element_type=F32)

    def body(h, carry):
        tiles = [slice(t * 128, (t + 1) * 128) for t in range(n_lane)]
        ab = [s_ref[h, p, :, lanes] for lanes in tiles for p in (0, 1)]
        tops = _top_rows(ab, PEER_TOPK)
        for t, lanes in enumerate(tiles):
            a, b, ta, tb = ab[2 * t], ab[2 * t + 1], tops[2 * t], tops[2 * t + 1]
            c_ref[...] = jnp.full(c_ref.shape, -jnp.inf, F32)
            for k, (r, s) in enumerate(_CAND_PAIRS):
                c_ref[k:k + 1, :] = ta[r] + tb[s]
            best, = _top_rows((c_ref[...],), PEER_TOPK)
            tau = best[PEER_TOPK - 1]
            zsum = jnp.zeros_like(best[0])
            for bk in best:
                zsum = zsum + jnp.exp(bk - best[0])
            code_b = jnp.zeros(b.shape, F32)
            for s in range(PEER_TOPK):
                code_b = code_b + jnp.where(tb[s] > b, 1.0, 0.0)
            code_t = [jnp.zeros(tau.shape, F32)]
            for s in range(1, PEER_TOPK):
                code_t.append(jnp.where(tb[s] == tb[s - 1], code_t[s - 1], float(s)))
            q = jnp.full(a.shape, -1.0, F32)
            for r in reversed(range(PEER_TOPK)):
                q_r = jnp.full(tau.shape, -1.0, F32)
                for s in range(PEER_TOPK // (r + 1)):
                    q_r = jnp.maximum(q_r, jnp.where(ta[r] + tb[s] >= tau, code_t[s], -1.0))
                q = jnp.where(a == ta[r], q_r, q)
            q_ref[h, :, lanes] = q
            code_ref[h, :, lanes] = code_b.astype(BF16)
            ea_ref[h, :, lanes] = jnp.exp(a - ta[0]) / zsum
            eb_ref[h, :, lanes] = jnp.exp(b - tb[0]).astype(BF16)
        return carry

    lax.fori_loop(0, PEER_HEADS, body, 0)


def peer_select(keys, qp, tm):
    P, n, dk = keys.shape
    H = P // 2
    T = qp.shape[0]
    kern = functools.partial(_peer_select_kernel, tm=tm)
    return pl.pallas_call(
        kern,
        grid=(T // tm,),
        in_specs=[pl.BlockSpec((P, n, dk), lambda t: (0, 0, 0)),
                  pl.BlockSpec((tm, P * dk), lambda t: (t, 0))],
        out_specs=[pl.BlockSpec((H, n, tm), lambda t: (0, 0, t))] * 4,
        out_shape=[jax.ShapeDtypeStruct((H, n, T), dt) for dt in (F32, F32, BF16, BF16)],
        scratch_shapes=[pltpu.VMEM((H, 2, n, tm), F32), pltpu.VMEM((_CAND_ROWS, 128), F32)],
        compiler_params=_params("parallel"),
        name="peer_select",
    )(keys, qp)


def _gelu_tanh(x):
    k = -2.0 * math.sqrt(2.0 / math.pi) * LOG2E
    return x / (1.0 + jnp.exp2(x * (k + (k * 0.044715) * (x * x))))


def _peer_kernel(x_ref, u_ref, vt_ref, ea_ref, q_ref, eb_ref, code_ref, res_ref, gate_ref, o_ref,
                 acc_ref, act_ref, g_ref, *, ni, tm):
    c = pl.program_id(1)

    @pl.when(c == 0)
    def _():
        acc_ref[...] = jnp.zeros_like(acc_ref)

    act_ref[...] = _gelu_tanh(lax.dot_general(u_ref[...], x_ref[...], (((1,), (1,)), ((), ())),
                                              preferred_element_type=F32))

    for ts in range(tm // 128):
        lanes = slice(ts * 128, (ts + 1) * 128)
        for ii in range(ni):
            rows = slice(ii * N_KEYS, (ii + 1) * N_KEYS)
            w = jnp.zeros((N_KEYS, 128), BF16)
            for h in range(PEER_HEADS):
                sel = code_ref[h, :, lanes] <= q_ref[h, ii:ii + 1, lanes].astype(BF16)
                gate = ea_ref[h, ii:ii + 1, lanes].astype(BF16) * eb_ref[h, :, lanes]
                w = w + jnp.where(sel, gate, jnp.zeros_like(gate))
            g_ref[rows, lanes] = w * act_ref[rows, lanes].astype(BF16)

    acc_ref[...] += jnp.dot(vt_ref[...], g_ref[...], preferred_element_type=F32)

    @pl.when(c == pl.num_programs(1) - 1)
    def _():
        o_ref[...] = res_ref[...] + gate_ref[0] * acc_ref[...].T


def peer_dense(x, u, vt, eat, qt, ebt, codet, res, gate, rows_per_seg, tm, ni):
    T, D = x.shape
    E = u.shape[0]
    ec = ni * N_KEYS
    kern = functools.partial(_peer_kernel, ni=ni, tm=tm)
    nseg = gate.shape[0]
    blocks_per_seg = rows_per_seg // tm
    rows_of_chunk = pl.BlockSpec((PEER_HEADS, ni, tm), lambda t, c: (0, c, t))
    return pl.pallas_call(
        kern,
        grid=(T // tm, E // ec),
        in_specs=[pl.BlockSpec((tm, D), lambda t, c: (t, 0)),
                  pl.BlockSpec((ec, D), lambda t, c: (c, 0)),
                  pl.BlockSpec((D, ec), lambda t, c: (0, c)),
                  rows_of_chunk,
                  rows_of_chunk,
                  pl.BlockSpec((PEER_HEADS, N_KEYS, tm), lambda t, c: (0, 0, t)),
                  pl.BlockSpec((PEER_HEADS, N_KEYS, tm), lambda t, c: (0, 0, t)),
                  pl.BlockSpec((tm, D), lambda t, c: (t, 0)),
                  pl.BlockSpec((1, 1, D), lambda t, c: (jnp.minimum(t // blocks_per_seg, nseg - 1), 0, 0))],
        out_specs=pl.BlockSpec((tm, D), lambda t, c: (t, 0)),
        out_shape=jax.ShapeDtypeStruct((T, D), F32),
        scratch_shapes=[pltpu.VMEM((D, tm), F32),
                        pltpu.VMEM((ec, tm), F32),
                        pltpu.VMEM((ec, tm), BF16)],
        compiler_params=_params("parallel", "arbitrary"),
        name="peer_dense",
    )(x, u, vt, eat, qt, ebt, codet, res, gate.reshape(nseg, 1, D))


def peer(qp, h_bf, keys, u_bf, vt_bf, res, gate, rows_per_seg, tm):
    kflat = keys.reshape(PEER_HEADS * 2, N_KEYS, keys.shape[-1]).astype(BF16)
    eat, qt, ebt, codet = peer_select(kflat, qp, tm)
    return peer_dense(h_bf, u_bf, vt_bf, eat, qt, ebt, codet, res, gate, rows_per_seg, tm, 8)


def axial_rope(n_tok, dim):
    rows = n_tok // GRID_W
    row = jnp.repeat(jnp.arange(rows, dtype=F32), GRID_W)
    col = jnp.tile(jnp.arange(GRID_W, dtype=F32), rows)
    axis_dim = dim // 2
    inv_freq = ROPE_THETA ** (-jnp.arange(0, axis_dim, 2, dtype=F32) / axis_dim)
    ang = jnp.concatenate([row[:, None] * inv_freq, col[:, None] * inv_freq], axis=-1)
    return jnp.cos(ang), jnp.sin(ang)


def short_conv(u, w, b):
    up = jnp.pad(u, ((0, 0), (1, 1), (0, 0)))
    return up[:, :-2] * w[0] + up[:, 1:-1] * w[1] + up[:, 2:] * w[2] + b


def hyena_filters(n_tok, w1, b1, w2, b2, w3, b3, freq):
    hp = lax.Precision.HIGHEST
    t = jnp.linspace(0.0, 1.0, n_tok, dtype=F32)[:, None]
    w = (2.0 * math.pi / n_tok) * jnp.arange(n_tok, dtype=F32)[:, None]
    f = jnp.linspace(1e-4, HY_BANDS - 1, HY_BANDS, dtype=F32)[None, :]
    feat = jnp.concatenate([t, jnp.cos(f * w), -jnp.sin(f * w)], axis=-1)
    hdn = jnp.sin(freq * (jnp.dot(feat, w1, precision=hp) + b1))
    hdn = jnp.sin(freq * (jnp.dot(hdn, w2, precision=hp) + b2))
    filt = (jnp.dot(hdn, w3, precision=hp) + b3).reshape(n_tok, HY_ORDER, 2, HY_CH)
    deltas = jnp.abs(jnp.linspace(math.log(HY_TARGET) / HY_SLOW, math.log(HY_TARGET) / HY_FAST, HY_CH,
                                  dtype=F32))
    filt = filt * jnp.exp(-t[:, :, None, None] * deltas)
    fwd, bwd = filt[:, :, 0], filt[:, :, 1]
    kfull = jnp.concatenate([fwd, jnp.zeros_like(fwd[:1]), bwd[1:][::-1]], axis=0)
    return kfull / jnp.sum(jnp.abs(kfull), axis=0, keepdims=True)


def hyena_filters_t(n_tok, w1, b1, w2, b2, w3, b3, freq):
    hp = lax.Precision.HIGHEST
    t = jnp.linspace(0.0, 1.0, n_tok, dtype=F32)[:, None]
    w = (2.0 * math.pi / n_tok) * jnp.arange(n_tok, dtype=F32)[:, None]
    f = jnp.linspace(1e-4, HY_BANDS - 1, HY_BANDS, dtype=F32)[None, :]
    feat = jnp.concatenate([t, jnp.cos(f * w), -jnp.sin(f * w)], axis=-1)
    hdn = jnp.sin(freq * (jnp.dot(feat, w1, precision=hp) + b1))
    hdn = jnp.sin(freq * (jnp.dot(hdn, w2, precision=hp) + b2))
    deltas = jnp.abs(jnp.linspace(math.log(HY_TARGET) / HY_SLOW, math.log(HY_TARGET) / HY_FAST, HY_CH,
                                  dtype=F32))
    w3t = jnp.transpose(w3.reshape(-1, HY_ORDER, 2, HY_CH), (2, 1, 3, 0))
    b3t = jnp.transpose(b3.reshape(HY_ORDER, 2, HY_CH), (1, 0, 2))[..., None]

    def half(d, hidden, times):
        decay = jnp.exp(-deltas[:, None] * times[None, :])
        return (jnp.einsum('ock,nk->ocn', w3t[d], hidden, precision=hp) + b3t[d]) * decay

    fwd = half(0, hdn, t[:, 0])
    bwd_rev = half(1, hdn[::-1], t[::-1, 0])
    kfull = jnp.concatenate([fwd, jnp.zeros_like(fwd[..., :1]), bwd_rev[..., :n_tok - 1]], axis=-1)
    kfull = kfull / jnp.sum(jnp.abs(kfull), axis=-1, keepdims=True)
    return kfull.reshape(HY_ORDER * HY_CH, 2 * n_tok)


def long_conv(z, kf):
    n = z.shape[1]
    zf = jnp.fft.rfft(z, n=2 * n, axis=1)
    hf = jnp.fft.rfft(kf, n=2 * n, axis=0)
    return jnp.fft.irfft(zf * hf[None], n=2 * n, axis=1)[:, :n]


def hyena_small(u, conv_w, conv_b, filter_params, bias):
    n = u.shape[1]
    u = short_conv(u, conv_w, conv_b)
    v, x1, x2 = jnp.split(u, 3, axis=-1)
    kfull = hyena_filters(n, *filter_params)
    z = x1 * (long_conv(v, kfull[:, 0]) + bias[0] * v)
    return x2 * (long_conv(z, kfull[:, 1]) + bias[1] * z)


def _hi_lo(x):
    hi = x.astype(BF16)
    return hi, (x - hi.astype(F32)).astype(BF16)


def _dot3(a, b):
    d = lambda x, y: jnp.dot(x, y, preferred_element_type=F32)
    return d(a[0], b[0]) + d(a[1], b[0]) + d(a[0], b[1])


def _dft_constants(n_tok):
    n = 2 * n_tok
    n1 = n // 128
    a1 = 2.0 * np.pi * np.outer(np.arange(n1), np.arange(n1)) / n1
    a2 = 2.0 * np.pi * np.outer(np.arange(128), np.arange(128)) / 128
    at = 2.0 * np.pi * np.outer(np.arange(n1), np.arange(128)) / n
    c1, s1, c2, s2 = np.cos(a1), np.sin(a1), np.cos(a2), np.sin(a2)
    pair = lambda m: _hi_lo(jnp.asarray(m, F32))
    return dict(
        f1_half=pair(np.concatenate([c1[:, :n1 // 2], -s1[:, :n1 // 2]], axis=0)),
        f1_full=pair(np.concatenate([c1, -s1], axis=0)),
        m_fwd=pair(np.block([[c2, -s2], [s2, c2]])),
        m_inv=pair(np.block([[c2, s2], [-s2, c2]])),
        g_half=pair(np.concatenate([c1[:n1 // 2], -s1[:n1 // 2]], axis=1)),
        tr=jnp.asarray(np.cos(at), F32), ti=jnp.asarray(-np.sin(at), F32))


def _dft_fwd(seqs, f1, tr, ti, m_fwd):
    n1 = tr.shape[0]
    y = _dot3(f1, _hi_lo(jnp.concatenate(seqs, axis=1)))
    rows = []
    for k in range(len(seqs)):
        yr, yi = y[:n1, k * 128:(k + 1) * 128], y[n1:, k * 128:(k + 1) * 128]
        rows.append(jnp.concatenate([yr * tr - yi * ti, yr * ti + yi * tr], axis=1))
    return _dot3(_hi_lo(jnp.concatenate(rows, axis=0)), m_fwd)


def _dft_inv_half(p, g_half, tr, ti, m_inv):
    n1 = tr.shape[0]
    u = _dot3(_hi_lo(p), m_inv)
    cols = []
    for k in range(p.shape[0] // n1):
        ur, ui = u[k * n1:(k + 1) * n1, :128], u[k * n1:(k + 1) * n1, 128:]
        cols.append(jnp.concatenate([ur * tr + ui * ti, ui * tr - ur * ti], axis=0))
    return _dot3(g_half, _hi_lo(jnp.concatenate(cols, axis=1))) * (1.0 / (n1 * 128))


def _pairs(refs):
    return (refs[0][...], refs[1][...])


def _spectrum_kernel(a_ref, f1h, f1l, tr_ref, ti_ref, mh, ml, o_ref):
    cb, n1 = a_ref.shape[0], tr_ref.shape[0]
    x = _dft_fwd([a_ref[k] for k in range(cb)], _pairs((f1h, f1l)), tr_ref[...], ti_ref[...],
                 _pairs((mh, ml)))
    o_ref[...] = x.reshape(cb, n1, 256)


def filter_spectrum(kf, consts, cb):
    items, n1, _ = kf.shape
    full = lambda shape: pl.BlockSpec(shape, lambda i: (0,) * len(shape))
    f1, m = consts["f1_full"], consts["m_fwd"]
    return pl.pallas_call(
        _spectrum_kernel,
        grid=(items // cb,),
        in_specs=[pl.BlockSpec((cb, n1, 128), lambda i: (i, 0, 0)),
                  full(f1[0].shape), full(f1[1].shape), full((n1, 128)), full((n1, 128)),
                  full(m[0].shape), full(m[1].shape)],
        out_specs=pl.BlockSpec((cb, n1, 256), lambda i: (i, 0, 0)),
        out_shape=jax.ShapeDtypeStruct((items, n1, 256), F32),
        compiler_params=_params("parallel"),
        name="filter_spectrum",
    )(kf, f1[0], f1[1], consts["tr"], consts["ti"], m[0], m[1])


def _conv_gate_kernel(bias_ref, u_ref, g_ref, h_ref, f1h, f1l, tr_ref, ti_ref, mfh, mfl, mih, mil, gh, gl,
                      o_ref, *, cb):
    tr, ti = tr_ref[...], ti_ref[...]
    n1 = tr.shape[0]
    c0 = pl.program_id(1) * cb
    u = [u_ref[0, k] for k in range(cb)]
    x = _dft_fwd(u, _pairs((f1h, f1l)), tr, ti, _pairs((mfh, mfl)))
    prod = []
    for k in range(cb):
        xr, xi = x[k * n1:(k + 1) * n1, :128], x[k * n1:(k + 1) * n1, 128:]
        hr, hi = h_ref[k, :, :128], h_ref[k, :, 128:]
        prod.append(jnp.concatenate([xr * hr - xi * hi, xr * hi + xi * hr], axis=1))
    y = _dft_inv_half(jnp.concatenate(prod, axis=0), _pairs((gh, gl)), tr, ti, _pairs((mih, mil)))
    for k in range(cb):
        o_ref[0, k] = g_ref[0, k] * (y[:, k * 128:(k + 1) * 128] + bias_ref[c0 + k] * u[k])


def conv_gate(u, gate, spec, bias, consts, cb):
    B, C, half, _ = u.shape
    n1 = 2 * half
    full = lambda shape: pl.BlockSpec(shape, lambda b, c: (0,) * len(shape))
    seq = pl.BlockSpec((1, cb, half, 128), lambda b, c: (b, c, 0, 0))
    mats = [*consts["f1_half"], consts["tr"], consts["ti"], *consts["m_fwd"], *consts["m_inv"],
            *consts["g_half"]]
    return pl.pallas_call(
        functools.partial(_conv_gate_kernel, cb=cb),
        grid=(B, C // cb),
        in_specs=[pl.BlockSpec(memory_space=pltpu.SMEM), seq, seq,
                  pl.BlockSpec((cb, n1, 256), lambda b, c: (c, 0, 0))] + [full(m.shape) for m in mats],
        out_specs=seq,
        out_shape=jax.ShapeDtypeStruct(u.shape, F32),
        compiler_params=_params("parallel", "arbitrary"),
        name="conv_gate",
    )(bias.astype(F32), u, gate, spec, *mats)


def hyena(u, conv_w, conv_b, filter_params, bias):
    B, n, _ = u.shape
    half = n // 128
    consts = _dft_constants(n)
    u = short_conv(u, conv_w, conv_b)
    seqs = jnp.transpose(u, (0, 2, 1)).reshape(B, 3, HY_CH, half, 128)
    v, x1, x2 = seqs[:, 0], seqs[:, 1], seqs[:, 2]
    kf = hyena_filters_t(n, *filter_params).reshape(HY_ORDER * HY_CH, 2 * half, 128)
    spec = filter_spectrum(kf, consts, HY_CH_BLOCK).reshape(HY_ORDER, HY_CH, 2 * half, 256)
    z = conv_gate(v, x1, spec[0], bias[0], consts, HY_CH_BLOCK)
    o = conv_gate(z, x2, spec[1], bias[1], consts, HY_CH_BLOCK)
    return jnp.transpose(o.reshape(B, HY_CH, n), (0, 2, 1))


def _head_major_t(t):
    B, L, H, d = t.shape
    return jnp.transpose(t, (0, 2, 3, 1)).reshape(B * H, d, L)


def _key_chunk(n_keys):
    for c in range(KEY_CHUNK_CAP, 0, -128):
        if n_keys % c == 0:
            return c
    raise ValueError(f"no key chunk for {n_keys}")


def kernel(x, c, ctx, c_ctx, w_mod, b_mod, g_norm_mix, w_in, g_qk_diff, lambda_diff, g_qk_win, sink_win, g_qk_glob, hy_conv_w, hy_conv_b, hy_w1, hy_b1, hy_w2, hy_b2, hy_w3, hy_b3, hy_freq, hy_bias, g_mix_out, w_out, g_norm_ffn, peer_wq, peer_keys, peer_u, peer_v):
    B, L, D = x.shape
    C = ctx.shape[1]
    depth = w_mod.shape[0]
    TM = 512
    n_lat = B * L
    n_ctx = B * C
    assert L % TM == 0 and n_ctx % TM == 0
    nseg = B + 1

    def rope_rows(dim):
        cos, sin = axial_rope(L, dim)
        return (jnp.concatenate([jnp.tile(cos, (B, 1)), jnp.ones((n_ctx, dim // 2), F32)], axis=0),
                jnp.concatenate([jnp.tile(sin, (B, 1)), jnp.zeros((n_ctx, dim // 2), F32)], axis=0))

    rope_rows_half = rope_rows(DIFF_QK_DIM)
    rope_rows_full = rope_rows(HEAD_DIM)
    sc = jnp.concatenate([jax.nn.silu(c), jax.nn.silu(c_ctx)[None]], axis=0)
    sc = jnp.pad(sc, ((0, 8 - nseg), (0, 0))).astype(BF16)

    xall = jnp.concatenate([x.reshape(n_lat, D), ctx.reshape(n_ctx, D)], axis=0)

    for i in range(depth):
        want_ctx = i < depth - 1
        lambda_init = 0.8 - 0.6 * math.exp(-0.3 * i)
        mod = mm(sc, w_mod[i].astype(BF16), 8, 1024)[:nseg] + b_mod[i]
        sh1, s1, g1, sh2, s2, g2 = jnp.split(mod, N_MOD, axis=-1)

        groups = normmod_mm(xall, g_norm_mix[i], sh1, s1, w_in[i].astype(BF16), L, TM, F32, False,
                            widths=(W_DIFF, W_WIN, W_GLOB, W_HY))
        ga_, gw_, gg_, gh_ = groups
        ph, phc = gh_[:n_lat].reshape(B, L, -1), gh_[n_lat:].reshape(B, C, -1)

        def gains(g_qk, n_q, n_k, d):
            return jnp.concatenate([jnp.tile(g_qk[0], n_q) * (d ** -0.5 * LOG2E), jnp.tile(g_qk[1], n_k)])

        def split_rows(t, *shape):
            return t[:n_lat].reshape(B, L, *shape), t[n_lat:].reshape(B, C, *shape)

        def rows_of(t):
            lat, cx = t[:n_lat].reshape(B, L, -1), t[n_lat:].reshape(B, C, -1)
            return lat, cx, jnp.concatenate([cx, lat], axis=1)

        qka, qkac, qka_keys = rows_of(head_prep(ga_, 2 * GROUP_W, DIFF_QK_DIM,
                                                gains(g_qk_diff[i], 2 * DIFF_HEADS, 2 * DIFF_HEADS, DIFF_QK_DIM),
                                                *rope_rows_half, TM))
        va, vac = split_rows(ga_[:, 2 * GROUP_W:].astype(BF16), DIFF_HEADS, HEAD_DIM)
        lam_vec = lambda_diff[i]
        lam = (jnp.exp(jnp.sum(lam_vec[0] * lam_vec[1])) - jnp.exp(jnp.sum(lam_vec[2] * lam_vec[3]))
               + lambda_init)
        coefs = jnp.stack([jnp.ones((), F32), -lam])
        vva = jnp.concatenate([vac, va], axis=1)
        oa = dense_attention(qka, qka_keys, GROUP_W, _head_major_t(vva), coefs, None,
                             2, DIFF_HEADS, 1, DIFF_QK_DIM, ATTN_TQ_DIFF, _key_chunk(C + L))

        def gqa_parts(g_, g_qk, n_q, n_kv):
            wq, wk = n_q * HEAD_DIM, n_kv * HEAD_DIM
            qk = head_prep(g_, wq + wk, HEAD_DIM, gains(g_qk, n_q, n_kv, HEAD_DIM), *rope_rows_full, TM)
            v, vc = split_rows(g_[:, wq + wk:].astype(BF16), n_kv, HEAD_DIM)
            return rows_of(qk) + (v, vc)

        WQ = WIN_HEADS * HEAD_DIM
        WK = WIN_KV_HEADS * HEAD_DIM
        qkw, qkwc, _, vw, vwc = gqa_parts(gw_, g_qk_win[i], WIN_HEADS, WIN_KV_HEADS)
        zblk = jnp.zeros((B, BLOCK, WK), BF16)
        kpad = jnp.concatenate([qkwc[..., WQ:], zblk, qkw[..., WQ:], zblk], axis=1)
        vpad = jnp.concatenate([vwc.reshape(B, C, WK), zblk, vw.reshape(B, L, WK), zblk], axis=1)
        Rw = WIN_HEADS // WIN_KV_HEADS
        ob = window_attention(qkw, kpad, vpad, sink_win[i], WIN_KV_HEADS, Rw, C, L)

        qkg, qkgc, qkg_keys, vg, vgc = gqa_parts(gg_, g_qk_glob[i], GLOB_HEADS, GLOB_KV_HEADS)
        Rg = GLOB_HEADS // GLOB_KV_HEADS
        one = jnp.ones((1,), F32)
        vvg = _head_major_t(jnp.concatenate([vgc, vg], axis=1))
        og = dense_attention(qkg, qkg_keys, GLOB_HEADS * HEAD_DIM, vvg, one, None,
                             1, GLOB_KV_HEADS, Rg, HEAD_DIM, ATTN_TQ_GLOB, _key_chunk(C + L))

        filt = (hy_w1[i], hy_b1[i], hy_w2[i], hy_b2[i], hy_w3[i], hy_b3[i], hy_freq[i])
        oh = hyena(ph, hy_conv_w[i], hy_conv_b[i], filt, hy_bias[i])

        mixed = [t.reshape(n_lat, -1) for t in (oa, ob, og, oh)]
        if want_ctx:
            oac = dense_attention(qkac, qkac, GROUP_W, _head_major_t(vac), coefs, None,
                                  2, DIFF_HEADS, 1, DIFF_QK_DIM, C, C)
            obc = dense_attention(qkwc, qkwc, WQ, _head_major_t(vwc), one, sink_win[i],
                                  1, WIN_KV_HEADS, Rw, HEAD_DIM, C, C)
            ogc = dense_attention(qkgc, qkgc, GLOB_HEADS * HEAD_DIM, _head_major_t(vgc), one, None,
                                  1, GLOB_KV_HEADS, Rg, HEAD_DIM, C, C)
            ohc = hyena_small(phc, hy_conv_w[i], hy_conv_b[i], filt, hy_bias[i])
            mixed = [jnp.concatenate([m_, t.reshape(n_ctx, -1)], axis=0)
                     for m_, t in zip(mixed, (oac, obc, ogc, ohc))]

        head_scale = jnp.where(jnp.arange(N_OUT_HEADS) < DIFF_HEADS, 1.0 - lambda_init, 1.0)
        gain_out = (g_mix_out[i].reshape(N_OUT_HEADS, HEAD_DIM) * head_scale[:, None]).reshape(D)
        xcur = merge_residual(mixed, HEAD_DIM, gain_out, w_out[i].astype(BF16), xall, g1, L, TM)

        qp, h2 = normmod_mm(xcur, g_norm_ffn[i], sh2, s2, peer_wq[i].astype(BF16), L, TM, BF16, True)
        xall = peer(qp, h2, peer_keys[i], peer_u[i].astype(BF16), jnp.transpose(peer_v[i].astype(BF16)),
                    xcur, g2, L, TM)

    return xall[:n_lat].reshape(B, L, D)
```
